```python
import math
import jax, jax.numpy as jnp
from jax import lax
import numpy as np

D_MODEL = 1024
BATCH = 2
SEQ = 16384
DEPTH = 1

GRID_W = 64
CTX_LEN = 256
N_HEADS = 8
N_KV_HEADS = 4
GROUP = N_HEADS // N_KV_HEADS
HEAD_DIM = 128
AXIS_DIM = HEAD_DIM // 2
ATTN_WIDTH = N_HEADS * HEAD_DIM
KV_WIDTH = N_KV_HEADS * HEAD_DIM
Q_BLOCK = 128
ROPE_THETA = 10000.0
ATTN_SCALE = HEAD_DIM ** -0.5
CONV_WIDTH = 512
CONV_KERNEL = 31
CONV_PAD = CONV_KERNEL // 2
N_BRANCH = 2
Q_END = ATTN_WIDTH
K_END = Q_END + KV_WIDTH
KV_END = K_END + KV_WIDTH
U_END = KV_END + 2 * CONV_WIDTH
IN_WIDTH = U_END + N_BRANCH * D_MODEL
N_EXPERTS = 256
TOP_K = 8
N_GROUPS = 8
TOPK_GROUPS = 4
EXPERTS_PER_GROUP = N_EXPERTS // N_GROUPS
EXPERT_HIDDEN = 256
SHARED_HIDDEN = 256
ROUTED_SCALE = 2.5
DISPATCH_BLOCK = 128
EPS = 1e-6

kernel_name = "hybrid_gqa_conformer_moe_dit_block"


def _rmsnorm(x, g):
    xf = x.astype(jnp.float32)
    y = xf * lax.rsqrt(jnp.mean(xf * xf, axis=-1, keepdims=True) + EPS)
    return (y * g.astype(jnp.float32)).astype(x.dtype)


def _layernorm(x, g, b):
    xf = x.astype(jnp.float32)
    mu = jnp.mean(xf, axis=-1, keepdims=True)
    var = jnp.mean(jnp.square(xf - mu), axis=-1, keepdims=True)
    y = (xf - mu) * lax.rsqrt(var + EPS) * g.astype(jnp.float32) + b.astype(jnp.float32)
    return y.astype(x.dtype)


def _modulate(h, shift, scale):
    return h * (1 + scale) + shift


def _rope_tables(seq):
    rows = seq // GRID_W
    pos_row = jnp.repeat(jnp.arange(rows, dtype=jnp.float32), GRID_W)
    pos_col = jnp.tile(jnp.arange(GRID_W, dtype=jnp.float32), rows)
    inv_freq = ROPE_THETA ** (-jnp.arange(0, AXIS_DIM, 2, dtype=jnp.float32) / AXIS_DIM)
    ang = jnp.stack([pos_row[:, None] * inv_freq, pos_col[:, None] * inv_freq], axis=1)
    return jnp.cos(ang), jnp.sin(ang)


def _apply_rope(h, cos, sin):
    shp = h.shape
    hf = h.astype(jnp.float32).reshape(*shp[:-1], 2, 2, AXIS_DIM // 2)
    x1, x2 = hf[..., 0, :], hf[..., 1, :]
    cb, sb = cos[None, :, None], sin[None, :, None]
    out = jnp.stack([x1 * cb - x2 * sb, x2 * cb + x1 * sb], axis=-2)
    return out.reshape(shp).astype(h.dtype)


def _heads(p, n_heads, g):
    return _rmsnorm(p.reshape(*p.shape[:-1], n_heads, HEAD_DIM), g)


def _attend(q, k, v):
    b, lq = q.shape[:2]
    qg = q.astype(jnp.float32).reshape(b, lq, N_KV_HEADS, GROUP, HEAD_DIM)
    s = jnp.einsum('bqkgd,bskd->bkgqs', qg, k.astype(jnp.float32)) * ATTN_SCALE
    p = jax.nn.softmax(s, axis=-1)
    o = jnp.einsum('bkgqs,bskd->bqkgd', p, v.astype(jnp.float32))
    return o.reshape(b, lq, ATTN_WIDTH).astype(q.dtype)


def _latent_attention(q, k_all, v_all):
    b, s = q.shape[:2]
    nb = s // Q_BLOCK
    qb = q.reshape(b, nb, Q_BLOCK, N_HEADS, HEAD_DIM).swapaxes(0, 1)
    o = lax.map(lambda qi: _attend(qi, k_all, v_all), qb)
    return o.swapaxes(0, 1).reshape(b, s, ATTN_WIDTH)


def _conv_module(u, w_dw, b_dw, ln_g, ln_b):
    a, gt = jnp.split(u, 2, axis=-1)
    h = a * jax.nn.sigmoid(gt)
    h = lax.conv_general_dilated(h, w_dw[:, None, :].astype(h.dtype), window_strides=(1,),
                                 padding=[(CONV_PAD, CONV_PAD)],
                                 dimension_numbers=('NWC', 'WIO', 'NWC'),
                                 feature_group_count=CONV_WIDTH) + b_dw
    return jax.nn.silu(_layernorm(h, ln_g, ln_b))


def _merge(o_attn, u, g, w_dw, b_dw, ln_g, ln_b, w_attn_proj, w_conv_proj, w_out):
    y_attn = o_attn @ w_attn_proj
    y_conv = _conv_module(u, w_dw, b_dw, ln_g, ln_b) @ w_conv_proj
    g_attn, g_conv = jnp.split(jax.nn.sigmoid(g), 2, axis=-1)
    return (g_attn * y_attn + g_conv * y_conv) @ w_out


def _swiglu(t, w_gu, w_dn):
    gt, up = jnp.split(t @ w_gu, 2, axis=-1)
    return (jax.nn.silu(gt) * up) @ w_dn


def _routed_experts(t, idx, wts, w_gu, w_dn):
    n, d = t.shape
    nk = n * TOP_K
    n_blocks = -(-(nk + N_EXPERTS * (DISPATCH_BLOCK - 1)) // DISPATCH_BLOCK)
    cap = n_blocks * DISPATCH_BLOCK
    flat_e = idx.reshape(-1)
    flat_tok = jnp.arange(nk, dtype=jnp.int32) // TOP_K
    flat_w = wts.reshape(-1)
    order = jnp.argsort(flat_e)
    e_sorted = flat_e[order]
    counts = jnp.bincount(flat_e, length=N_EXPERTS)
    padded = (counts + DISPATCH_BLOCK - 1) // DISPATCH_BLOCK * DISPATCH_BLOCK
    starts = jnp.cumsum(counts) - counts
    pends = jnp.cumsum(padded)
    pstarts = pends - padded
    dest = pstarts[e_sorted] + (jnp.arange(nk, dtype=jnp.int32) - starts[e_sorted])
    tok_buf = jnp.full((cap,), n, jnp.int32).at[dest].set(flat_tok[order])
    w_buf = jnp.zeros((cap,), jnp.float32).at[dest].set(flat_w[order])
    blk_e = jnp.minimum(jnp.searchsorted(pends, jnp.arange(n_blocks) * DISPATCH_BLOCK, side='right'),
                        N_EXPERTS - 1)
    t_pad = jnp.concatenate([t, jnp.zeros((1, d), t.dtype)], axis=0)

    def body(acc, blk):
        tok, w, e = blk
        yb = _swiglu(t_pad[tok], w_gu[e], w_dn[e])
        return acc.at[tok].add(yb.astype(jnp.float32) * w[:, None]), None

    acc, _ = lax.scan(body, jnp.zeros((n + 1, d), jnp.float32),
                      (tok_buf.reshape(n_blocks, DISPATCH_BLOCK),
                       w_buf.reshape(n_blocks, DISPATCH_BLOCK), blk_e))
    return acc[:n].astype(t.dtype)


def _moe(h, w_router, router_bias, w_gu, w_dn, w_sh_gu, w_sh_dn):
    shp = h.shape
    t = h.reshape(-1, shp[-1])
    n = t.shape[0]
    scores = jax.nn.sigmoid((t @ w_router).astype(jnp.float32))
    biased = scores + router_bias.astype(jnp.float32)
    grp_score = lax.top_k(biased.reshape(n, N_GROUPS, EXPERTS_PER_GROUP), 2)[0].sum(-1)
    _, grp_idx = lax.top_k(grp_score, TOPK_GROUPS)
    grp_mask = (grp_idx[..., None] == jnp.arange(N_GROUPS)).any(axis=1)
    masked = jnp.where(jnp.repeat(grp_mask, EXPERTS_PER_GROUP, axis=1), biased, -jnp.inf)
    _, idx = lax.top_k(masked, TOP_K)
    wts = jnp.take_along_axis(scores, idx, axis=1)
    wts = wts / jnp.sum(wts, axis=-1, keepdims=True) * ROUTED_SCALE
    routed = _routed_experts(t, idx, wts, w_gu, w_dn)
    shared = _swiglu(t, w_sh_gu, w_sh_dn)
    return (shared + routed).reshape(shp)


def setup_inputs(seed: int = 0) -> dict:
    key = jax.random.key(seed)
    ks = jax.random.split(key, 26)
    f32 = jnp.float32

    def nrm(k, shape, scale):
        return jax.random.normal(k, shape, f32) * scale

    L, D, E = DEPTH, D_MODEL, N_EXPERTS
    return {
        "x": nrm(ks[0], (BATCH, SEQ, D), 1.0),
        "c": nrm(ks[1], (BATCH, D), 1.0),
        "ctx": nrm(ks[2], (BATCH, CTX_LEN, D), 1.0),
        "c_ctx": nrm(ks[3], (D,), 1.0),
        "w_mod": nrm(ks[4], (L, D, 6 * D), 0.5 * D ** -0.5),
        "b_mod": nrm(ks[5], (L, 6 * D), 0.02),
        "norm1_g": 1.0 + nrm(ks[6], (L, D), 0.02),
        "w_in": nrm(ks[7], (L, D, IN_WIDTH), D ** -0.5),
        "q_norm_g": 1.0 + nrm(ks[8], (L, HEAD_DIM), 0.02),
        "k_norm_g": 1.0 + nrm(ks[9], (L, HEAD_DIM), 0.02),
        "w_dw": nrm(ks[10], (L, CONV_KERNEL, CONV_WIDTH), CONV_KERNEL ** -0.5),
        "b_dw": nrm(ks[11], (L, CONV_WIDTH), 0.02),
        "conv_ln_g": 1.0 + nrm(ks[12], (L, CONV_WIDTH), 0.02),
        "conv_ln_b": nrm(ks[13], (L, CONV_WIDTH), 0.02),
        "w_attn_proj": nrm(ks[14], (L, ATTN_WIDTH, D), ATTN_WIDTH ** -0.5),
        "w_conv_proj": nrm(ks[15], (L, CONV_WIDTH, D), CONV_WIDTH ** -0.5),
        "w_out": nrm(ks[16], (L, D, D), D ** -0.5),
        "norm2_g": 1.0 + nrm(ks[17], (L, D), 0.02),
        "w_router": nrm(ks[18], (L, D, E), D ** -0.5),
        "router_bias": nrm(ks[19], (L, E), 0.01),
        "w_exp_gu": nrm(ks[20], (L, E, D, 2 * EXPERT_HIDDEN), D ** -0.5),
        "w_exp_dn": nrm(ks[21], (L, E, EXPERT_HIDDEN, D), EXPERT_HIDDEN ** -0.5),
        "w_sh_gu": nrm(ks[22], (L, D, 2 * SHARED_HIDDEN), D ** -0.5),
        "w_sh_dn": nrm(ks[23], (L, SHARED_HIDDEN, D), SHARED_HIDDEN ** -0.5),
        "final_g": 1.0 + nrm(ks[24], (D,), 0.02),
    }


def reference(x, c, ctx, c_ctx, w_mod, b_mod, norm1_g, w_in, q_norm_g, k_norm_g, w_dw, b_dw,
              conv_ln_g, conv_ln_b, w_attn_proj, w_conv_proj, w_out, norm2_g, w_router,
              router_bias, w_exp_gu, w_exp_dn, w_sh_gu, w_sh_dn, final_g):
    cos, sin = _rope_tables(x.shape[1])
    for i in range(DEPTH):
        mod_x = (jax.nn.silu(c) @ w_mod[i] + b_mod[i])[:, None, :]
        mod_c = jax.nn.silu(c_ctx) @ w_mod[i] + b_mod[i]
        sh1, sc1, gt1, sh2, sc2, gt2 = jnp.split(mod_x, 6, axis=-1)
        csh1, csc1, cgt1, csh2, csc2, cgt2 = jnp.split(mod_c, 6, axis=-1)

        hc = _modulate(_rmsnorm(ctx, norm1_g[i]), csh1, csc1)
        kv_c = hc @ w_in[i][:, Q_END:KV_END]
        kc = _heads(kv_c[..., :KV_WIDTH], N_KV_HEADS, k_norm_g[i])
        vc = kv_c[..., KV_WIDTH:].reshape(*kv_c.shape[:-1], N_KV_HEADS, HEAD_DIM)

        hx = _modulate(_rmsnorm(x, norm1_g[i]), sh1, sc1)
        px = hx @ w_in[i]
        qx = _apply_rope(_heads(px[..., :Q_END], N_HEADS, q_norm_g[i]), cos, sin)
        kx = _apply_rope(_heads(px[..., Q_END:K_END], N_KV_HEADS, k_norm_g[i]), cos, sin)
        vx = px[..., K_END:KV_END].reshape(*px.shape[:-1], N_KV_HEADS, HEAD_DIM)
        k_all = jnp.concatenate([kc, kx], axis=1)
        v_all = jnp.concatenate([vc, vx], axis=1)
        o_x = _latent_attention(qx, k_all, v_all)
        mix_x = _merge(o_x, px[..., KV_END:U_END], px[..., U_END:], w_dw[i], b_dw[i],
                       conv_ln_g[i], conv_ln_b[i], w_attn_proj[i], w_conv_proj[i], w_out[i])
        x = x + gt1 * mix_x

        if i < DEPTH - 1:
            qc = _heads(hc @ w_in[i][:, :Q_END], N_HEADS, q_norm_g[i])
            rest_c = hc @ w_in[i][:, KV_END:]
            o_c = _attend(qc, kc, vc)
            mix_c = _merge(o_c, rest_c[..., :2 * CONV_WIDTH], rest_c[..., 2 * CONV_WIDTH:], w_dw[i],
                           b_dw[i], conv_ln_g[i], conv_ln_b[i], w_attn_proj[i], w_conv_proj[i],
                           w_out[i])
            ctx = ctx + cgt1 * mix_c
            hc2 = _modulate(_rmsnorm(ctx, norm2_g[i]), csh2, csc2)
            ctx = ctx + cgt2 * _moe(hc2, w_router[i], router_bias[i], w_exp_gu[i], w_exp_dn[i],
                                    w_sh_gu[i], w_sh_dn[i])

        hx2 = _modulate(_rmsnorm(x, norm2_g[i]), sh2, sc2)
        x = x + gt2 * _moe(hx2, w_router[i], router_bias[i], w_exp_gu[i], w_exp_dn[i],
                           w_sh_gu[i], w_sh_dn[i])
    return _rmsnorm(x, final_g)
```

```python
import functools
import math

import jax
import jax.numpy as jnp
from jax import lax
from jax.experimental import pallas as pl
from jax.experimental.pallas import tpu as pltpu

F32 = jnp.float32
BF16 = jnp.bfloat16
U32 = jnp.uint32
I32 = jnp.int32

GRID_W = 64
N_HEADS = 8
N_KV_HEADS = 4
GROUP = N_HEADS // N_KV_HEADS
HEAD_DIM = 128
AXIS_DIM = HEAD_DIM // 2
ATTN_WIDTH = N_HEADS * HEAD_DIM
KV_WIDTH = N_KV_HEADS * HEAD_DIM
ROPE_THETA = 10000.0
ATTN_SCALE = HEAD_DIM ** -0.5
CONV_WIDTH = 512
CONV_KERNEL = 31
CONV_PAD = CONV_KERNEL // 2
N_EXPERTS = 256
TOP_K = 8
N_GROUPS = 8
TOPK_GROUPS = 4
EXPERTS_PER_GROUP = N_EXPERTS // N_GROUPS
EXPERT_HIDDEN = 256
SHARED_HIDDEN = 256
ROUTED_SCALE = 2.5
EPS = 1e-6
LOG2E = 1.4426950408889634

LANES = 128
SUBLANES = 8
VMEM_LIMIT_BYTES = 56 * 1024 * 1024

HALO_ROWS = 16
EXPERT_BLOCK = 256
HIGHEST = lax.Precision.HIGHEST


def _params(sem):
    return pltpu.CompilerParams(dimension_semantics=sem, vmem_limit_bytes=VMEM_LIMIT_BYTES)


def _sigmoid(x):
    return 1.0 / (1.0 + jnp.exp(-x))


def _pack_bf16_pair(lo, hi):
    lo_b = pltpu.bitcast(lo.astype(BF16).astype(F32), U32)
    hi_b = pltpu.bitcast(hi.astype(BF16).astype(F32), U32)
    return (lo_b >> 16) | (hi_b & jnp.uint32(0xFFFF0000))


def _unpack_bf16_pair(u):
    lo = pltpu.bitcast(u << 16, F32)
    hi = pltpu.bitcast(u & jnp.uint32(0xFFFF0000), F32)
    return lo, hi


def _mod_kernel(cc_ref, w_ref, b_ref, o_ref):
    cc = cc_ref[...]
    s = cc * _sigmoid(cc)
    o_ref[...] = jnp.dot(s, w_ref[...], precision=HIGHEST, preferred_element_type=F32) + b_ref[...]


def _modulation(cc, w_mod, b_mod):
    d, n = w_mod.shape
    tn = n // 4
    return pl.pallas_call(
        _mod_kernel,
        grid=(n // tn,),
        in_specs=[pl.BlockSpec((SUBLANES, d), lambda j: (0, 0)),
                  pl.BlockSpec((d, tn), lambda j: (0, j)),
                  pl.BlockSpec((1, tn), lambda j: (0, j))],
        out_specs=pl.BlockSpec((SUBLANES, tn), lambda j: (0, j)),
        out_shape=jax.ShapeDtypeStruct((SUBLANES, n), F32),
        compiler_params=_params(("arbitrary",)),
        name="mod",
    )(cc, w_mod, b_mod)


def _norm_modulate(x, g, sh, sc):
    ms = jnp.mean(x * x, axis=-1, keepdims=True)
    return (x * lax.rsqrt(ms + EPS) * g) * (1.0 + sc) + sh


def _head_norm(p, gain):
    r = lax.rsqrt(jnp.mean(p * p, axis=-1, keepdims=True) + EPS)
    return p * r * gain


def _ctx_kv_kernel(x_ref, sh_ref, sc_ref, g1_ref, w_ref, gk_ref, k_ref, v_ref):
    h = _norm_modulate(x_ref[0], g1_ref[...], sh_ref[...], sc_ref[...]).astype(BF16)
    pk = jnp.dot(h, w_ref[:, :KV_WIDTH], preferred_element_type=F32)
    for j in range(N_KV_HEADS):
        sl = slice(j * HEAD_DIM, (j + 1) * HEAD_DIM)
        k_ref[0, :, sl] = _head_norm(pk[:, sl], gk_ref[...]).astype(BF16)
    v_ref[0] = jnp.dot(h, w_ref[:, KV_WIDTH:], preferred_element_type=F32).astype(BF16)


def _ctx_kv(ctx, csh, csc, g1, w_kv, gk):
    b, lc, d = ctx.shape
    vec = lambda: pl.BlockSpec((1, d), lambda i: (0, 0))
    return pl.pallas_call(
        _ctx_kv_kernel,
        grid=(b,),
        in_specs=[pl.BlockSpec((1, lc, d), lambda i: (i, 0, 0)), vec(), vec(), vec(),
                  pl.BlockSpec((d, 2 * KV_WIDTH), lambda i: (0, 0)),
                  pl.BlockSpec((1, HEAD_DIM), lambda i: (0, 0))],
        out_specs=[pl.BlockSpec((1, lc, KV_WIDTH), lambda i: (i, 0, 0)),
                   pl.BlockSpec((1, lc, KV_WIDTH), lambda i: (i, 0, 0))],
        out_shape=[jax.ShapeDtypeStruct((b, lc, KV_WIDTH), BF16)] * 2,
        compiler_params=_params(("arbitrary",)),
        name="ctx_kv",
    )(ctx, csh, csc, g1, w_kv, gk)


def _in_proj_kernel(x_ref, sh_ref, sc_ref, g1_ref, w_ref, gq_ref, gk_ref, cos_ref, sin_ref,
                    q_ref, k_ref, v_ref, h_ref, g_ref):
    h = _norm_modulate(x_ref[0], g1_ref[...], sh_ref[0], sc_ref[0]).astype(BF16)
    cos = cos_ref[...]
    sin = sin_ref[...]
    lane = lax.broadcasted_iota(I32, cos.shape, 1)
    upper = (lane & (AXIS_DIM // 2)) != 0

    def rope(p):
        swapped = jnp.where(upper, pltpu.roll(p, AXIS_DIM // 2, 1),
                            pltpu.roll(p, HEAD_DIM - AXIS_DIM // 2, 1))
        return p * cos + swapped * sin

    q_end = ATTN_WIDTH
    k_end = q_end + KV_WIDTH
    v_end = k_end + KV_WIDTH
    u_end = v_end + 2 * CONV_WIDTH
    pq = jnp.dot(h, w_ref[:, :q_end], preferred_element_type=F32)
    for j in range(N_HEADS):
        sl = slice(j * HEAD_DIM, (j + 1) * HEAD_DIM)
        q_ref[0, :, sl] = (rope(_head_norm(pq[:, sl], gq_ref[...])) * (ATTN_SCALE * LOG2E)).astype(BF16)
    pk = jnp.dot(h, w_ref[:, q_end:k_end], preferred_element_type=F32)
    for j in range(N_KV_HEADS):
        sl = slice(j * HEAD_DIM, (j + 1) * HEAD_DIM)
        k_ref[0, :, sl] = rope(_head_norm(pk[:, sl], gk_ref[...])).astype(BF16)
    v_ref[0] = jnp.dot(h, w_ref[:, k_end:v_end], preferred_element_type=F32).astype(BF16)
    u = jnp.dot(h, w_ref[:, v_end:u_end], preferred_element_type=F32)
    h_ref[0] = (u[:, :CONV_WIDTH] * _sigmoid(u[:, CONV_WIDTH:])).astype(BF16)
    g_ref[0] = _sigmoid(jnp.dot(h, w_ref[:, u_end:], preferred_element_type=F32)).astype(BF16)


def _in_proj(x, sh1, sc1, g1, w_in, gq, gk, cos_t, sin_t, tm):
    b, s, d = x.shape
    n_in = w_in.shape[1]
    bvec = lambda: pl.BlockSpec((1, 1, d), lambda bi, i: (bi, 0, 0))
    tok = lambda w: pl.BlockSpec((1, tm, w), lambda bi, i: (bi, i, 0))
    return pl.pallas_call(
        _in_proj_kernel,
        grid=(b, s // tm),
        in_specs=[tok(d), bvec(), bvec(),
                  pl.BlockSpec((1, d), lambda bi, i: (0, 0)),
                  pl.BlockSpec((d, n_in), lambda bi, i: (0, 0)),
                  pl.BlockSpec((1, HEAD_DIM), lambda bi, i: (0, 0)),
                  pl.BlockSpec((1, HEAD_DIM), lambda bi, i: (0, 0)),
                  pl.BlockSpec((tm, HEAD_DIM), lambda bi, i: (i, 0)),
                  pl.BlockSpec((tm, HEAD_DIM), lambda bi, i: (i, 0))],
        out_specs=[tok(ATTN_WIDTH), tok(KV_WIDTH), tok(KV_WIDTH), tok(CONV_WIDTH), tok(2 * d)],
        out_shape=[jax.ShapeDtypeStruct((b, s, ATTN_WIDTH), BF16),
                   jax.ShapeDtypeStruct((b, s, KV_WIDTH), BF16),
                   jax.ShapeDtypeStruct((b, s, KV_WIDTH), BF16),
                   jax.ShapeDtypeStruct((b, s, CONV_WIDTH), BF16),
                   jax.ShapeDtypeStruct((b, s, 2 * d), BF16)],
        compiler_params=_params(("arbitrary", "arbitrary")),
        name="in_proj",
    )(x, sh1, sc1, g1, w_in, gq, gk, cos_t, sin_t)


def _attn_kernel(q_ref, k_ref, v_ref, o_ref, m_sc, l_sc, acc_sc, *, tk):
    tq = q_ref.shape[1]
    nk = k_ref.shape[1] // tk
    q = jnp.concatenate([q_ref[0, :, :HEAD_DIM], q_ref[0, :, HEAD_DIM:]], axis=0)
    m_sc[...] = jnp.full(m_sc.shape, -jnp.inf, F32)
    l_sc[...] = jnp.zeros(l_sc.shape, F32)
    acc_sc[...] = jnp.zeros(acc_sc.shape, F32)

    def body(j, carry):
        off = pl.multiple_of(j * tk, tk)
        k = k_ref[0, pl.ds(off, tk), :]
        v = v_ref[0, pl.ds(off, tk), :]
        s = lax.dot_general(q, k, (((1,), (1,)), ((), ())), preferred_element_type=F32)
        m_prev = m_sc[...]
        m_new = jnp.maximum(m_prev, jnp.max(s, axis=-1, keepdims=True))
        alpha = jnp.exp2(m_prev - m_new)
        p = jnp.exp2(s - m_new)
        l_sc[...] = alpha * l_sc[...] + jnp.sum(p, axis=-1, keepdims=True)
        acc_sc[...] = alpha * acc_sc[...] + jnp.dot(p.astype(BF16), v, preferred_element_type=F32)
        m_sc[...] = m_new
        return carry

    lax.fori_loop(0, nk, body, 0)
    o = acc_sc[...] / l_sc[...]
    o_ref[0, :, :HEAD_DIM] = o[:tq].astype(BF16)
    o_ref[0, :, HEAD_DIM:] = o[tq:].astype(BF16)


def _attention(q, k_all, v_all, tq, tk):
    b, s, _ = q.shape
    lk = k_all.shape[1]
    gw = GROUP * HEAD_DIM
    return pl.pallas_call(
        functools.partial(_attn_kernel, tk=tk),
        grid=(b, N_KV_HEADS, s // tq),
        in_specs=[pl.BlockSpec((1, tq, gw), lambda bi, h, i: (bi, i, h)),
                  pl.BlockSpec((1, lk, HEAD_DIM), lambda bi, h, i: (bi, 0, h)),
                  pl.BlockSpec((1, lk, HEAD_DIM), lambda bi, h, i: (bi, 0, h))],
        out_specs=pl.BlockSpec((1, tq, gw), lambda bi, h, i: (bi, i, h)),
        out_shape=jax.ShapeDtypeStruct((b, s, ATTN_WIDTH), BF16),
        scratch_shapes=[pltpu.VMEM((GROUP * tq, 1), F32), pltpu.VMEM((GROUP * tq, 1), F32),
                        pltpu.VMEM((GROUP * tq, HEAD_DIM), F32)],
        compiler_params=_params(("arbitrary", "arbitrary", "arbitrary")),
        name="attn",
    )(q, k_all, v_all)


def _merge_kernel(o_ref, hp_ref, hc_ref, hn_ref, g_ref, x_ref, gt1_ref, sh2_ref, sc2_ref,
                  wdw_ref, bdw_ref, lng_ref, lnb_ref, wap_ref, wcp_ref, wout_ref, g2_ref, wrt_ref,
                  x1_ref, hx_ref, lg_ref, hcat):
    i = pl.program_id(1)
    n_i = pl.num_programs(1)
    tm = hc_ref.shape[1]
    d = x_ref.shape[2]
    prev = hp_ref[0].astype(F32)
    nxt = hn_ref[0].astype(F32)
    hcat[0:HALO_ROWS, :] = jnp.where(i > 0, prev, jnp.zeros_like(prev))
    hcat[HALO_ROWS:HALO_ROWS + tm, :] = hc_ref[0].astype(F32)
    hcat[HALO_ROWS + tm:, :] = jnp.where(i < n_i - 1, nxt, jnp.zeros_like(nxt))
    acc = jnp.zeros((tm, CONV_WIDTH), F32) + bdw_ref[...]
    base = HALO_ROWS - CONV_PAD
    for j in range(CONV_KERNEL):
        acc = acc + hcat[base + j:base + j + tm, :] * wdw_ref[j:j + 1, :]
    mu = jnp.mean(acc, axis=-1, keepdims=True)
    cen = acc - mu
    var = jnp.mean(cen * cen, axis=-1, keepdims=True)
    ln = cen * lax.rsqrt(var + EPS) * lng_ref[...] + lnb_ref[...]
    act = (ln * _sigmoid(ln)).astype(BF16)
    y_conv = jnp.dot(act, wcp_ref[...], preferred_element_type=F32)
    y_attn = jnp.dot(o_ref[0], wap_ref[...], preferred_element_type=F32)
    z = g_ref[0, :, :d].astype(F32) * y_attn + g_ref[0, :, d:].astype(F32) * y_conv
    mix = jnp.dot(z.astype(BF16), wout_ref[...], preferred_element_type=F32)
    x1 = x_ref[0] + gt1_ref[0] * mix
    x1_ref[0] = x1
    hx = _norm_modulate(x1, g2_ref[...], sh2_ref[0], sc2_ref[0])
    half = d // 2
    hx_ref[...] = _pack_bf16_pair(hx[:, :half], hx[:, half:])
    lg_ref[...] = lax.dot_general(wrt_ref[...], hx, (((1,), (1,)), ((), ())),
                                  precision=HIGHEST, preferred_element_type=F32)


def _merge(o, hglu, g, x, gt1, sh2, sc2, w_dw, b_dw, ln_g, ln_b, w_ap, w_cp, w_out, g2, w_rt, tm):
    b, s, d = x.shape
    nt = s // tm
    hb = tm // HALO_ROWS
    n_halo = s // HALO_ROWS
    bvec = lambda: pl.BlockSpec((1, 1, d), lambda bi, i: (bi, 0, 0))
    full = lambda a: pl.BlockSpec(a.shape, lambda bi, i: (0,) * a.ndim)
    tok = lambda w: pl.BlockSpec((1, tm, w), lambda bi, i: (bi, i, 0))
    return pl.pallas_call(
        _merge_kernel,
        grid=(b, nt),
        in_specs=[tok(ATTN_WIDTH),
                  pl.BlockSpec((1, HALO_ROWS, CONV_WIDTH), lambda bi, i: (bi, jnp.maximum(i * hb - 1, 0), 0)),
                  tok(CONV_WIDTH),
                  pl.BlockSpec((1, HALO_ROWS, CONV_WIDTH),
                               lambda bi, i: (bi, jnp.minimum((i + 1) * hb, n_halo - 1), 0)),
                  tok(2 * d), tok(d), bvec(), bvec(), bvec(),
                  full(w_dw), full(b_dw), full(ln_g), full(ln_b), full(w_ap), full(w_cp), full(w_out),
                  full(g2), full(w_rt)],
        out_specs=[tok(d),
                   pl.BlockSpec((tm, d // 2), lambda bi, i: (bi * nt + i, 0)),
                   pl.BlockSpec((N_EXPERTS, tm), lambda bi, i: (0, bi * nt + i))],
        out_shape=[jax.ShapeDtypeStruct((b, s, d), F32),
                   jax.ShapeDtypeStruct((b * s, d // 2), U32),
                   jax.ShapeDtypeStruct((N_EXPERTS, b * s), F32)],
        scratch_shapes=[pltpu.VMEM((tm + 2 * HALO_ROWS, CONV_WIDTH), F32)],
        compiler_params=_params(("arbitrary", "arbitrary")),
        name="merge",
    )(o, hglu, hglu, hglu, g, x, gt1, sh2, sc2, w_dw, b_dw, ln_g, ln_b, w_ap, w_cp, w_out, g2, w_rt)


def _sub_allreduce(x, op):
    for s in (4, 2, 1):
        x = op(x, pltpu.roll(x, s, 0))
    return x


def _route_kernel(lg_ref, bias_ref, tri_ref, e_ref, w_ref, r_ref, cnt_ref, run_sc):
    step = pl.program_id(0)
    n_strips = lg_ref.shape[1] // LANES
    nv = N_EXPERTS // SUBLANES
    gv = EXPERTS_PER_GROUP // SUBLANES

    @pl.when(step == 0)
    def _():
        run_sc[...] = jnp.zeros(run_sc.shape, F32)

    row = (lax.broadcasted_iota(I32, (nv, SUBLANES, LANES), 0) * SUBLANES
           + lax.broadcasted_iota(I32, (nv, SUBLANES, LANES), 1))
    sub = lax.broadcasted_iota(I32, (SUBLANES, LANES), 0)
    bias = bias_ref[...].reshape(nv, SUBLANES, LANES)
    neg_inf = jnp.float32(-jnp.inf)

    for st in range(n_strips):
        lanes = slice(st * LANES, (st + 1) * LANES)
        scores = _sigmoid(lg_ref[:, lanes]).reshape(nv, SUBLANES, LANES)
        biased = scores + bias
        gscore = []
        for g in range(N_GROUPS):
            m1 = biased[g * gv]
            m2 = jnp.full((SUBLANES, LANES), neg_inf, F32)
            for t in range(1, gv):
                v = biased[g * gv + t]
                m2 = jnp.maximum(m2, jnp.minimum(m1, v))
                m1 = jnp.maximum(m1, v)
            for s in (4, 2, 1):
                p1 = pltpu.roll(m1, s, 0)
                p2 = pltpu.roll(m2, s, 0)
                m2 = jnp.maximum(jnp.minimum(m1, p1), jnp.maximum(m2, p2))
                m1 = jnp.maximum(m1, p1)
            gscore.append(m1 + m2)
        masked = []
        for g in range(N_GROUPS):
            beaten = jnp.zeros((SUBLANES, LANES), I32)
            for o in range(N_GROUPS):
                if o == g:
                    continue
                wins = (gscore[o] > gscore[g]) | ((gscore[o] == gscore[g]) & (o < g))
                beaten = beaten + wins.astype(I32)
            keep = beaten < TOPK_GROUPS
            for t in range(gv):
                masked.append(jnp.where(keep, biased[g * gv + t], neg_inf))
        cand = jnp.stack(masked, axis=0)
        sel = jnp.zeros((nv, SUBLANES, LANES), jnp.bool_)
        picks, pick_scores = [], []
        for _ in range(TOP_K):
            mx = _sub_allreduce(jnp.max(cand, axis=0), jnp.maximum)
            idx = _sub_allreduce(jnp.min(jnp.where(cand == mx, row, N_EXPERTS), axis=0), jnp.minimum)
            hit = row == idx
            pick_scores.append(_sub_allreduce(jnp.sum(jnp.where(hit, scores, 0.0), axis=0), jnp.add))
            picks.append(idx)
            sel = sel | hit
            cand = jnp.where(hit, neg_inf, cand)
        sel_b = sel.astype(F32).astype(BF16).reshape(N_EXPERTS, LANES)
        before = jnp.dot(sel_b, tri_ref[0], preferred_element_type=F32)
        total = jnp.dot(sel_b, tri_ref[1], preferred_element_type=F32)
        rank_all = (before + run_sc[...]).reshape(nv, SUBLANES, LANES)
        run_sc[...] = run_sc[...] + total
        denom = pick_scores[0]
        for kk in range(1, TOP_K):
            denom = denom + pick_scores[kk]
        e_out = jnp.zeros((SUBLANES, LANES), I32)
        w_out = jnp.zeros((SUBLANES, LANES), F32)
        r_out = jnp.zeros((SUBLANES, LANES), I32)
        for kk in range(TOP_K):
            rk = _sub_allreduce(jnp.sum(jnp.where(row == picks[kk], rank_all, 0.0), axis=0), jnp.add)
            e_out = jnp.where(sub == kk, picks[kk], e_out)
            w_out = jnp.where(sub == kk, pick_scores[kk] / denom * ROUTED_SCALE, w_out)
            r_out = jnp.where(sub == kk, rk.astype(I32), r_out)
        e_ref[:, lanes] = e_out
        w_ref[:, lanes] = w_out
        r_ref[:, lanes] = r_out

    cnt_ref[...] = run_sc[...].astype(I32)


def _route(logits_t, bias, tb):
    n = logits_t.shape[1]
    iota_r = lax.broadcasted_iota(I32, (LANES, LANES), 0)
    iota_c = lax.broadcasted_iota(I32, (LANES, LANES), 1)
    tri = jnp.stack([(iota_r < iota_c), jnp.ones((LANES, LANES), jnp.bool_)]).astype(BF16)
    bias_b = jnp.broadcast_to(bias.reshape(N_EXPERTS, 1), (N_EXPERTS, LANES)).astype(F32)
    tokrow = lambda dt: jax.ShapeDtypeStruct((TOP_K, n), dt)
    return pl.pallas_call(
        _route_kernel,
        grid=(n // tb,),
        in_specs=[pl.BlockSpec((N_EXPERTS, tb), lambda i: (0, i)),
                  pl.BlockSpec((N_EXPERTS, LANES), lambda i: (0, 0)),
                  pl.BlockSpec((2, LANES, LANES), lambda i: (0, 0, 0))],
        out_specs=[pl.BlockSpec((TOP_K, tb), lambda i: (0, i)),
                   pl.BlockSpec((TOP_K, tb), lambda i: (0, i)),
                   pl.BlockSpec((TOP_K, tb), lambda i: (0, i)),
                   pl.BlockSpec((N_EXPERTS, LANES), lambda i: (0, 0))],
        out_shape=[tokrow(I32), tokrow(F32), tokrow(I32),
                   jax.ShapeDtypeStruct((N_EXPERTS, LANES), I32)],
        scratch_shapes=[pltpu.VMEM((N_EXPERTS, LANES), F32)],
        compiler_params=_params(("arbitrary",)),
        name="route",
    )(logits_t, bias_b, tri)


def _row_copy(src, dst, sem):
    return pltpu.make_async_copy(src, dst, sem)


def _dispatch_kernel(pstart_ref, e_ref, r_ref, hx_ref, xs_in_ref, xs_ref, sem):
    del xs_in_ref
    tm = hx_ref.shape[0]

    def copy(t, kk):
        slot = pstart_ref[e_ref[0, 0, kk * tm + t]] + r_ref[0, 0, kk * tm + t]
        return _row_copy(hx_ref.at[pl.ds(t, 1), :], xs_ref.at[pl.ds(slot, 1), :], sem)

    def issue(t, carry):
        for kk in range(TOP_K):
            copy(t, kk).start()
        return carry

    def drain(t, carry):
        for kk in range(TOP_K):
            copy(t, kk).wait()
        return carry

    lax.fori_loop(0, tm, issue, 0)
    lax.fori_loop(0, tm, drain, 0)


def _dispatch(pstart, e_blk, r_blk, hx, cap, tm):
    n, w = hx.shape
    idx = lambda: pl.BlockSpec((1, 1, TOP_K * tm), lambda i, ps: (i, 0, 0), memory_space=pltpu.SMEM)
    xs0 = jnp.zeros((cap, w), U32)
    return pl.pallas_call(
        _dispatch_kernel,
        grid_spec=pltpu.PrefetchScalarGridSpec(
            num_scalar_prefetch=1,
            grid=(n // tm,),
            in_specs=[idx(), idx(),
                      pl.BlockSpec((tm, w), lambda i, ps: (i, 0)),
                      pl.BlockSpec(memory_space=pl.ANY)],
            out_specs=pl.BlockSpec(memory_space=pl.ANY),
            scratch_shapes=[pltpu.SemaphoreType.DMA(())]),
        out_shape=jax.ShapeDtypeStruct((cap, w), U32),
        input_output_aliases={4: 0},
        compiler_params=_params(("arbitrary",)),
        name="dispatch",
    )(pstart, e_blk, r_blk, hx, xs0)


def _experts_kernel(blk_e_ref, n_real_ref, xs_ref, wgu_ref, wdn_ref, ys_ref, wgu_sc, wdn_sc):
    i = pl.program_id(0)
    real = i < n_real_ref[0]

    @pl.when(real)
    def _():
        prev = blk_e_ref[jnp.maximum(i - 1, 0)]

        @pl.when((i == 0) | (blk_e_ref[i] != prev))
        def _():
            wgu_sc[...] = wgu_ref[0].astype(BF16)
            wdn_sc[...] = wdn_ref[0].astype(BF16)

        lo, hi = _unpack_bf16_pair(xs_ref[...])
        half = lo.shape[1]
        gu = (jnp.dot(lo.astype(BF16), wgu_sc[:half, :], preferred_element_type=F32)
              + jnp.dot(hi.astype(BF16), wgu_sc[half:, :], preferred_element_type=F32))
        gt = gu[:, :EXPERT_HIDDEN]
        act = (gt * _sigmoid(gt) * gu[:, EXPERT_HIDDEN:]).astype(BF16)
        y = jnp.dot(act, wdn_sc[...], preferred_element_type=F32)
        ys_ref[...] = _pack_bf16_pair(y[:, :half], y[:, half:])

    @pl.when(jnp.logical_not(real))
    def _():
        ys_ref[...] = jnp.zeros(ys_ref.shape, U32)


def _experts(blk_e, n_real, xs, w_gu, w_dn):
    cap, w = xs.shape
    n_e, d, h2 = w_gu.shape
    return pl.pallas_call(
        _experts_kernel,
        grid_spec=pltpu.PrefetchScalarGridSpec(
            num_scalar_prefetch=2,
            grid=(cap // EXPERT_BLOCK,),
            in_specs=[pl.BlockSpec((EXPERT_BLOCK, w), lambda i, be, nr: (i, 0)),
                      pl.BlockSpec((1, d, h2), lambda i, be, nr: (be[i], 0, 0)),
                      pl.BlockSpec((1, h2 // 2, d), lambda i, be, nr: (be[i], 0, 0))],
            out_specs=pl.BlockSpec((EXPERT_BLOCK, w), lambda i, be, nr: (i, 0)),
            scratch_shapes=[pltpu.VMEM((d, h2), BF16), pltpu.VMEM((h2 // 2, d), BF16)]),
        out_shape=jax.ShapeDtypeStruct((cap, w), U32),
        compiler_params=_params(("arbitrary",)),
        name="experts",
    )(blk_e, n_real, xs, w_gu, w_dn)


def _combine_kernel(pstart_ref, e_ref, r_ref, wt_ref, hx_ref, x1_ref, gt2_ref, wsg_ref, wsd_ref, fg_ref,
                    ys_ref, o_ref, gbuf, sem):
    tm = hx_ref.shape[0]

    def copy(t, kk):
        slot = pstart_ref[e_ref[0, 0, kk * tm + t]] + r_ref[0, 0, kk * tm + t]
        return _row_copy(ys_ref.at[pl.ds(slot, 1), :], gbuf.at[kk, pl.ds(t, 1), :], sem)

    def issue(t, carry):
        for kk in range(TOP_K):
            copy(t, kk).start()
        return carry

    def drain(t, carry):
        for kk in range(TOP_K):
            copy(t, kk).wait()
        return carry

    lax.fori_loop(0, tm, issue, 0)
    lo, hi = _unpack_bf16_pair(hx_ref[...])
    half = lo.shape[1]
    gu = (jnp.dot(lo.astype(BF16), wsg_ref[:half, :], preferred_element_type=F32)
          + jnp.dot(hi.astype(BF16), wsg_ref[half:, :], preferred_element_type=F32))
    gt = gu[:, :SHARED_HIDDEN]
    act = (gt * _sigmoid(gt) * gu[:, SHARED_HIDDEN:]).astype(BF16)
    y = jnp.dot(act, wsd_ref[...], preferred_element_type=F32)
    y_lo = y[:, :half]
    y_hi = y[:, half:]
    lax.fori_loop(0, tm, drain, 0)
    for kk in range(TOP_K):
        r_lo, r_hi = _unpack_bf16_pair(gbuf[kk])
        wk = wt_ref[:, kk:kk + 1]
        y_lo = y_lo + wk * r_lo
        y_hi = y_hi + wk * r_hi
    x2_lo = x1_ref[:, :half] + gt2_ref[0, :, :half] * y_lo
    x2_hi = x1_ref[:, half:] + gt2_ref[0, :, half:] * y_hi
    ms = (jnp.sum(x2_lo * x2_lo, axis=-1, keepdims=True)
          + jnp.sum(x2_hi * x2_hi, axis=-1, keepdims=True)) / (2 * half)
    inv = lax.rsqrt(ms + EPS)
    o_ref[:, :half] = x2_lo * inv * fg_ref[:, :half]
    o_ref[:, half:] = x2_hi * inv * fg_ref[:, half:]


def _combine(pstart, e_blk, r_blk, wt, hx, x1, gt2, w_sg, w_sd, fg, ys, tm, tiles_per_batch):
    n, w = hx.shape
    d = 2 * w
    idx = lambda: pl.BlockSpec((1, 1, TOP_K * tm), lambda i, ps: (i, 0, 0), memory_space=pltpu.SMEM)
    full = lambda a: pl.BlockSpec(a.shape, lambda i, ps: (0,) * a.ndim)
    return pl.pallas_call(
        _combine_kernel,
        grid_spec=pltpu.PrefetchScalarGridSpec(
            num_scalar_prefetch=1,
            grid=(n // tm,),
            in_specs=[idx(), idx(),
                      pl.BlockSpec((tm, TOP_K), lambda i, ps: (i, 0)),
                      pl.BlockSpec((tm, w), lambda i, ps: (i, 0)),
                      pl.BlockSpec((tm, d), lambda i, ps: (i, 0)),
                      pl.BlockSpec((1, 1, d), lambda i, ps: (i // tiles_per_batch, 0, 0)),
                      full(w_sg), full(w_sd), full(fg),
                      pl.BlockSpec(memory_space=pl.ANY)],
            out_specs=pl.BlockSpec((tm, d), lambda i, ps: (i, 0)),
            scratch_shapes=[pltpu.VMEM((TOP_K, tm, w), U32), pltpu.SemaphoreType.DMA(())]),
        out_shape=jax.ShapeDtypeStruct((n, d), F32),
        compiler_params=_params(("arbitrary",)),
        name="combine",
    )(pstart, e_blk, r_blk, wt, hx, x1, gt2, w_sg, w_sd, fg, ys)


def _rope_tables(seq):
    rows = seq // GRID_W
    pos_row = jnp.repeat(jnp.arange(rows, dtype=F32), GRID_W)
    pos_col = jnp.tile(jnp.arange(GRID_W, dtype=F32), rows)
    inv_freq = ROPE_THETA ** (-jnp.arange(0, AXIS_DIM, 2, dtype=F32) / AXIS_DIM)
    ar = pos_row[:, None] * inv_freq
    ac = pos_col[:, None] * inv_freq
    cos_t = jnp.concatenate([jnp.cos(ar), jnp.cos(ar), jnp.cos(ac), jnp.cos(ac)], axis=1)
    sin_t = jnp.concatenate([-jnp.sin(ar), jnp.sin(ar), -jnp.sin(ac), jnp.sin(ac)], axis=1)
    return cos_t, sin_t


def _tile(n, want):
    t = min(n, want)
    assert n % t == 0, (n, want)
    return t


def _key_tile(lk, want):
    best = LANES
    for t in range(LANES, want + 1, LANES):
        if lk % t == 0:
            best = t
    assert lk % best == 0, lk
    return best


def kernel(x, c, ctx, c_ctx, w_mod, b_mod, norm1_g, w_in, q_norm_g, k_norm_g, w_dw, b_dw, conv_ln_g, conv_ln_b, w_attn_proj, w_conv_proj, w_out, norm2_g, w_router, router_bias, w_exp_gu, w_exp_dn, w_sh_gu, w_sh_dn, final_g):
    b, s, d = x.shape
    depth = w_mod.shape[0]
    assert depth == 1, "single-layer block"
    n = b * s
    row = lambda v: v.reshape(1, -1)

    cc = jnp.zeros((SUBLANES, d), F32).at[:b].set(c).at[b].set(c_ctx)
    mod = _modulation(cc, w_mod[0], row(b_mod[0]))
    mod_x = mod[:b].reshape(b, 1, 6, d)
    sh1, sc1, gt1, sh2, sc2, gt2 = [mod_x[:, :, j, :] for j in range(6)]
    mod_c = mod[b].reshape(6, d)
    csh1, csc1 = row(mod_c[0]), row(mod_c[1])

    w_in_b = w_in[0].astype(BF16)
    q_end, kv_end = ATTN_WIDTH, ATTN_WIDTH + 2 * KV_WIDTH
    kc, vc = _ctx_kv(ctx, csh1, csc1, row(norm1_g[0]), w_in_b[:, q_end:kv_end], row(k_norm_g[0]))

    cos_t, sin_t = _rope_tables(s)
    q, kx, vx, hglu, gates = _in_proj(x, sh1, sc1, row(norm1_g[0]), w_in_b, row(q_norm_g[0]),
                                      row(k_norm_g[0]), cos_t, sin_t, _tile(s, 512))
    k_all = jnp.concatenate([kc, kx], axis=1)
    v_all = jnp.concatenate([vc, vx], axis=1)
    o = _attention(q, k_all, v_all, _tile(s, 256), _key_tile(k_all.shape[1], 1280))

    x1, hx, logits_t = _merge(o, hglu, gates, x, gt1, sh2, sc2, w_dw[0], row(b_dw[0]), row(conv_ln_g[0]),
                              row(conv_ln_b[0]), w_attn_proj[0].astype(BF16), w_conv_proj[0].astype(BF16),
                              w_out[0].astype(BF16), row(norm2_g[0]), w_router[0].T, _tile(s, 256))

    e_idx, wts, rank, counts = _route(logits_t, router_bias[0], _tile(n, 512))

    cnt = counts[:, 0]
    padded = (cnt + EXPERT_BLOCK - 1) // EXPERT_BLOCK * EXPERT_BLOCK
    pends = jnp.cumsum(padded)
    pstart = (pends - padded).astype(I32)
    n_blocks = (n * TOP_K + N_EXPERTS * (EXPERT_BLOCK - 1)) // EXPERT_BLOCK
    cap = n_blocks * EXPERT_BLOCK
    blk_e = jnp.minimum(jnp.searchsorted(pends, jnp.arange(n_blocks, dtype=I32) * EXPERT_BLOCK, side='right'),
                        N_EXPERTS - 1).astype(I32)
    n_real = (pends[-1:] // EXPERT_BLOCK).astype(I32)

    tm = _tile(n, 256)
    blocked = lambda a: a.reshape(TOP_K, n // tm, tm).transpose(1, 0, 2).reshape(n // tm, 1, TOP_K * tm)
    e_blk, r_blk = blocked(e_idx), blocked(rank)
    xs = _dispatch(pstart, e_blk, r_blk, hx, cap, tm)
    ys = _experts(blk_e, n_real, xs, w_exp_gu[0], w_exp_dn[0])
    out = _combine(pstart, e_blk, r_blk, wts.T, hx, x1.reshape(n, d), gt2, w_sh_gu[0].astype(BF16),
                   w_sh_dn[0].astype(BF16), row(final_g), ys, tm, s // tm)
    return out.reshape(b, s, d)
```

```python
import functools
import math

import jax
import jax.numpy as jnp
from jax import lax
from jax.experimental import pallas as pl
from jax.experimental.pallas import tpu as pltpu

F32 = jnp.float32
BF16 = jnp.bfloat16
U32 = jnp.uint32
I32 = jnp.int32

GRID_W = 64
N_HEADS = 8
N_KV_HEADS = 4
GROUP = N_HEADS // N_KV_HEADS
HEAD_DIM = 128
AXIS_DIM = HEAD_DIM // 2
ATTN_WIDTH = N_HEADS * HEAD_DIM
KV_WIDTH = N_KV_HEADS * HEAD_DIM
ROPE_THETA = 10000.0
ATTN_SCALE = HEAD_DIM ** -0.5
CONV_WIDTH = 512
CONV_KERNEL = 31
CONV_PAD = CONV_KERNEL // 2
N_EXPERTS = 256
TOP_K = 8
N_GROUPS = 8
TOPK_GROUPS = 4
EXPERTS_PER_GROUP = N_EXPERTS // N_GROUPS
EXPERT_HIDDEN = 256
SHARED_HIDDEN = 256
ROUTED_SCALE = 2.5
EPS = 1e-6
LOG2E = 1.4426950408889634

LANES = 128
SUBLANES = 8
VMEM_LIMIT_BYTES = 56 * 1024 * 1024

HALO_ROWS = 16
EXPERT_BLOCK = 256
HIGHEST = lax.Precision.HIGHEST


def _params(sem):
    return pltpu.CompilerParams(dimension_semantics=sem, vmem_limit_bytes=VMEM_LIMIT_BYTES)


def _sigmoid(x):
    return 1.0 / (1.0 + jnp.exp(-x))


def _pack_bf16_pair(lo, hi):
    lo_b = pltpu.bitcast(lo.astype(BF16).astype(F32), U32)
    hi_b = pltpu.bitcast(hi.astype(BF16).astype(F32), U32)
    return (lo_b >> 16) | (hi_b & jnp.uint32(0xFFFF0000))


def _unpack_bf16_pair(u):
    lo = pltpu.bitcast(u << 16, F32)
    hi = pltpu.bitcast(u & jnp.uint32(0xFFFF0000), F32)
    return lo, hi


def _mod_kernel(cc_ref, w_ref, b_ref, o_ref):
    cc = cc_ref[...]
    s = cc * _sigmoid(cc)
    o_ref[...] = jnp.dot(s, w_ref[...], precision=HIGHEST, preferred_element_type=F32) + b_ref[...]


def _modulation(cc, w_mod, b_mod):
    d, n = w_mod.shape
    tn = n // 4
    return pl.pallas_call(
        _mod_kernel,
        grid=(n // tn,),
        in_specs=[pl.BlockSpec((SUBLANES, d), lambda j: (0, 0)),
                  pl.BlockSpec((d, tn), lambda j: (0, j)),
                  pl.BlockSpec((1, tn), lambda j: (0, j))],
        out_specs=pl.BlockSpec((SUBLANES, tn), lambda j: (0, j)),
        out_shape=jax.ShapeDtypeStruct((SUBLANES, n), F32),
        compiler_params=_params(("arbitrary",)),
        name="mod",
    )(cc, w_mod, b_mod)


def _norm_modulate(x, g, sh, sc):
    ms = jnp.mean(x * x, axis=-1, keepdims=True)
    return (x * lax.rsqrt(ms + EPS) * g) * (1.0 + sc) + sh


def _head_norm(p, gain):
    r = lax.rsqrt(jnp.mean(p * p, axis=-1, keepdims=True) + EPS)
    return p * r * gain


def _dot_nt(a, b):
    return lax.dot_general(a, b, (((1,), (1,)), ((), ())), preferred_element_type=F32)


def _ctx_kv_kernel(x_ref, sh_ref, sc_ref, g1_ref, wk_ref, wvt_ref, gk_ref, k_ref, vt_ref):
    h = _norm_modulate(x_ref[0], g1_ref[...], sh_ref[...], sc_ref[...]).astype(BF16)
    pk = jnp.dot(h, wk_ref[...], preferred_element_type=F32)
    for j in range(N_KV_HEADS):
        sl = slice(j * HEAD_DIM, (j + 1) * HEAD_DIM)
        k_ref[0, :, sl] = _head_norm(pk[:, sl], gk_ref[...]).astype(BF16)
    vt_ref[0] = _dot_nt(wvt_ref[...], h).astype(BF16)


def _ctx_kv(ctx, csh, csc, g1, w_k, w_vt, gk):
    b, lc, d = ctx.shape
    vec = lambda: pl.BlockSpec((1, d), lambda i: (0, 0))
    return pl.pallas_call(
        _ctx_kv_kernel,
        grid=(b,),
        in_specs=[pl.BlockSpec((1, lc, d), lambda i: (i, 0, 0)), vec(), vec(), vec(),
                  pl.BlockSpec((d, KV_WIDTH), lambda i: (0, 0)),
                  pl.BlockSpec((KV_WIDTH, d), lambda i: (0, 0)),
                  pl.BlockSpec((1, HEAD_DIM), lambda i: (0, 0))],
        out_specs=[pl.BlockSpec((1, lc, KV_WIDTH), lambda i: (i, 0, 0)),
                   pl.BlockSpec((1, KV_WIDTH, lc), lambda i: (i, 0, 0))],
        out_shape=[jax.ShapeDtypeStruct((b, lc, KV_WIDTH), BF16),
                   jax.ShapeDtypeStruct((b, KV_WIDTH, lc), BF16)],
        compiler_params=_params(("arbitrary",)),
        name="ctx_kv",
    )(ctx, csh, csc, g1, w_k, w_vt, gk)


def _in_proj_kernel(x_ref, sh_ref, sc_ref, g1_ref, w_ref, wvt_ref, gq_ref, gk_ref, cos_ref, sin_ref,
                    q_ref, k_ref, vt_ref, h_ref, g_ref):
    h = _norm_modulate(x_ref[0], g1_ref[...], sh_ref[0], sc_ref[0]).astype(BF16)
    cos = cos_ref[...]
    sin = sin_ref[...]
    lane = lax.broadcasted_iota(I32, cos.shape, 1)
    upper = (lane & (AXIS_DIM // 2)) != 0

    def rope(p):
        swapped = jnp.where(upper, pltpu.roll(p, AXIS_DIM // 2, 1),
                            pltpu.roll(p, HEAD_DIM - AXIS_DIM // 2, 1))
        return p * cos + swapped * sin

    q_end = ATTN_WIDTH
    k_end = q_end + KV_WIDTH
    v_end = k_end + KV_WIDTH
    u_end = v_end + 2 * CONV_WIDTH
    pq = jnp.dot(h, w_ref[:, :q_end], preferred_element_type=F32)
    for j in range(N_HEADS):
        sl = slice(j * HEAD_DIM, (j + 1) * HEAD_DIM)
        q_ref[0, :, sl] = (rope(_head_norm(pq[:, sl], gq_ref[...])) * (ATTN_SCALE * LOG2E)).astype(BF16)
    pk = jnp.dot(h, w_ref[:, q_end:k_end], preferred_element_type=F32)
    for j in range(N_KV_HEADS):
        sl = slice(j * HEAD_DIM, (j + 1) * HEAD_DIM)
        k_ref[0, :, sl] = rope(_head_norm(pk[:, sl], gk_ref[...])).astype(BF16)
    vt_ref[0] = _dot_nt(wvt_ref[...], h).astype(BF16)
    u = jnp.dot(h, w_ref[:, v_end:u_end], preferred_element_type=F32)
    h_ref[0] = (u[:, :CONV_WIDTH] * _sigmoid(u[:, CONV_WIDTH:])).astype(BF16)
    g_ref[0] = _sigmoid(jnp.dot(h, w_ref[:, u_end:], preferred_element_type=F32)).astype(BF16)


def _in_proj(x, sh1, sc1, g1, w_in, w_vt, gq, gk, cos_t, sin_t, tm):
    b, s, d = x.shape
    n_in = w_in.shape[1]
    bvec = lambda: pl.BlockSpec((1, 1, d), lambda bi, i: (bi, 0, 0))
    tok = lambda w: pl.BlockSpec((1, tm, w), lambda bi, i: (bi, i, 0))
    return pl.pallas_call(
        _in_proj_kernel,
        grid=(b, s // tm),
        in_specs=[tok(d), bvec(), bvec(),
                  pl.BlockSpec((1, d), lambda bi, i: (0, 0)),
                  pl.BlockSpec((d, n_in), lambda bi, i: (0, 0)),
                  pl.BlockSpec((KV_WIDTH, d), lambda bi, i: (0, 0)),
                  pl.BlockSpec((1, HEAD_DIM), lambda bi, i: (0, 0)),
                  pl.BlockSpec((1, HEAD_DIM), lambda bi, i: (0, 0)),
                  pl.BlockSpec((tm, HEAD_DIM), lambda bi, i: (i, 0)),
                  pl.BlockSpec((tm, HEAD_DIM), lambda bi, i: (i, 0))],
        out_specs=[tok(ATTN_WIDTH), tok(KV_WIDTH),
                   pl.BlockSpec((1, KV_WIDTH, tm), lambda bi, i: (bi, 0, i)),
                   tok(CONV_WIDTH), tok(2 * d)],
        out_shape=[jax.ShapeDtypeStruct((b, s, ATTN_WIDTH), BF16),
                   jax.ShapeDtypeStruct((b, s, KV_WIDTH), BF16),
                   jax.ShapeDtypeStruct((b, KV_WIDTH, s), BF16),
                   jax.ShapeDtypeStruct((b, s, CONV_WIDTH), BF16),
                   jax.ShapeDtypeStruct((b, s, 2 * d), BF16)],
        compiler_params=_params(("arbitrary", "arbitrary")),
        name="in_proj",
    )(x, sh1, sc1, g1, w_in, w_vt, gq, gk, cos_t, sin_t)


def _sub_allreduce(x, op):
    for s in (4, 2, 1):
        x = op(x, pltpu.roll(x, s, 0))
    return x


def _attn_kernel(q_ref, kc_ref, vct_ref, k_ref, vt_ref, o_ref, s0, s1, m_sc, l_sc, acc_sc, *, tk):
    tq = q_ref.shape[1]
    m_cols = GROUP * tq
    nk = k_ref.shape[1] // tk
    qf = q_ref[0].astype(F32).T
    qt = jnp.concatenate([qf[:HEAD_DIM], qf[HEAD_DIM:]], axis=1).astype(BF16)

    def scores(k):
        return jnp.dot(k, qt, preferred_element_type=F32)

    def absorb(st, vt):
        n = st.shape[0]
        s3 = st.reshape(n // SUBLANES, SUBLANES, m_cols)
        m_prev = m_sc[...]
        m_new = jnp.maximum(m_prev, _sub_allreduce(jnp.max(s3, axis=0), jnp.maximum))
        alpha = jnp.exp2(m_prev - m_new)
        p3 = jnp.exp2(s3 - m_new[None])
        l_sc[...] = alpha * l_sc[...] + _sub_allreduce(jnp.sum(p3, axis=0), jnp.add)
        pv = jnp.dot(vt, p3.reshape(n, m_cols).astype(BF16), preferred_element_type=F32)
        acc_sc[...] = alpha[0:1] * acc_sc[...] + pv
        m_sc[...] = m_new

    def kchunk(j):
        return k_ref[0, pl.ds(pl.multiple_of(j * tk, tk), tk), :]

    def vchunk(j):
        return vt_ref[0, :, pl.ds(pl.multiple_of(j * tk, tk), tk)]

    m_sc[...] = jnp.full(m_sc.shape, -jnp.inf, F32)
    l_sc[...] = jnp.zeros(l_sc.shape, F32)
    acc_sc[...] = jnp.zeros(acc_sc.shape, F32)
    s0[...] = scores(kchunk(0))
    absorb(scores(kc_ref[0]), vct_ref[0])

    def body(i, carry):
        j = 2 * i
        s1[...] = scores(kchunk(j + 1))
        absorb(s0[...], vchunk(j))
        s0[...] = scores(kchunk(j + 2))
        absorb(s1[...], vchunk(j + 1))
        return carry

    lax.fori_loop(0, nk // 2 - 1, body, 0)
    s1[...] = scores(kchunk(nk - 1))
    absorb(s0[...], vchunk(nk - 2))
    absorb(s1[...], vchunk(nk - 1))

    o = (acc_sc[...] / l_sc[0:1]).T
    o_ref[0, :, :HEAD_DIM] = o[:tq].astype(BF16)
    o_ref[0, :, HEAD_DIM:] = o[tq:].astype(BF16)


def _attention(q, kc, vct, kx, vxt, tq, tk):
    b, s, _ = q.shape
    lc = kc.shape[1]
    assert s % (2 * tk) == 0
    gw = GROUP * HEAD_DIM
    m_cols = GROUP * tq
    kv = lambda l: pl.BlockSpec((1, l, HEAD_DIM), lambda bi, h, i: (bi, 0, h))
    kvt = lambda l: pl.BlockSpec((1, HEAD_DIM, l), lambda bi, h, i: (bi, h, 0))
    return pl.pallas_call(
        functools.partial(_attn_kernel, tk=tk),
        grid=(b, N_KV_HEADS, s // tq),
        in_specs=[pl.BlockSpec((1, tq, gw), lambda bi, h, i: (bi, i, h)), kv(lc), kvt(lc), kv(s), kvt(s)],
        out_specs=pl.BlockSpec((1, tq, gw), lambda bi, h, i: (bi, i, h)),
        out_shape=jax.ShapeDtypeStruct((b, s, ATTN_WIDTH), BF16),
        scratch_shapes=[pltpu.VMEM((tk, m_cols), F32), pltpu.VMEM((tk, m_cols), F32),
                        pltpu.VMEM((SUBLANES, m_cols), F32), pltpu.VMEM((SUBLANES, m_cols), F32),
                        pltpu.VMEM((HEAD_DIM, m_cols), F32)],
        compiler_params=_params(("arbitrary", "arbitrary", "arbitrary")),
        name="attn",
    )(q, kc, vct, kx, vxt)


def _merge_kernel(o_ref, hp_ref, hc_ref, hn_ref, g_ref, x_ref, gt1_ref, sh2_ref, sc2_ref,
                  wdw_ref, bdw_ref, lng_ref, lnb_ref, wap_ref, wcp_ref, wout_ref, g2_ref, wrt_ref,
                  x1_ref, hx_ref, lg_ref, hcat):
    i = pl.program_id(1)
    n_i = pl.num_programs(1)
    tm = hc_ref.shape[1]
    d = x_ref.shape[2]
    prev = hp_ref[0].astype(F32)
    nxt = hn_ref[0].astype(F32)
    hcat[0:HALO_ROWS, :] = jnp.where(i > 0, prev, jnp.zeros_like(prev))
    hcat[HALO_ROWS:HALO_ROWS + tm, :] = hc_ref[0].astype(F32)
    hcat[HALO_ROWS + tm:, :] = jnp.where(i < n_i - 1, nxt, jnp.zeros_like(nxt))
    acc = jnp.zeros((tm, CONV_WIDTH), F32) + bdw_ref[...]
    base = HALO_ROWS - CONV_PAD
    for j in range(CONV_KERNEL):
        acc = acc + hcat[base + j:base + j + tm, :] * wdw_ref[j:j + 1, :]
    mu = jnp.mean(acc, axis=-1, keepdims=True)
    cen = acc - mu
    var = jnp.mean(cen * cen, axis=-1, keepdims=True)
    ln = cen * lax.rsqrt(var + EPS) * lng_ref[...] + lnb_ref[...]
    act = (ln * _sigmoid(ln)).astype(BF16)
    y_conv = jnp.dot(act, wcp_ref[...], preferred_element_type=F32)
    y_attn = jnp.dot(o_ref[0], wap_ref[...], preferred_element_type=F32)
    z = g_ref[0, :, :d].astype(F32) * y_attn + g_ref[0, :, d:].astype(F32) * y_conv
    mix = jnp.dot(z.astype(BF16), wout_ref[...], preferred_element_type=F32)
    x1 = x_ref[0] + gt1_ref[0] * mix
    x1_ref[0] = x1
    hx = _norm_modulate(x1, g2_ref[...], sh2_ref[0], sc2_ref[0])
    half = d // 2
    hx_ref[...] = _pack_bf16_pair(hx[:, :half], hx[:, half:])
    lg_ref[...] = lax.dot_general(wrt_ref[...], hx, (((1,), (1,)), ((), ())),
                                  precision=HIGHEST, preferred_element_type=F32)


def _merge(o, hglu, g, x, gt1, sh2, sc2, w_dw, b_dw, ln_g, ln_b, w_ap, w_cp, w_out, g2, w_rt, tm):
    b, s, d = x.shape
    nt = s // tm
    hb = tm // HALO_ROWS
    n_halo = s // HALO_ROWS
    bvec = lambda: pl.BlockSpec((1, 1, d), lambda bi, i: (bi, 0, 0))
    full = lambda a: pl.BlockSpec(a.shape, lambda bi, i: (0,) * a.ndim)
    tok = lambda w: pl.BlockSpec((1, tm, w), lambda bi, i: (bi, i, 0))
    return pl.pallas_call(
        _merge_kernel,
        grid=(b, nt),
        in_specs=[tok(ATTN_WIDTH),
                  pl.BlockSpec((1, HALO_ROWS, CONV_WIDTH), lambda bi, i: (bi, jnp.maximum(i * hb - 1, 0), 0)),
                  tok(CONV_WIDTH),
                  pl.BlockSpec((1, HALO_ROWS, CONV_WIDTH),
                               lambda bi, i: (bi, jnp.minimum((i + 1) * hb, n_halo - 1), 0)),
                  tok(2 * d), tok(d), bvec(), bvec(), bvec(),
                  full(w_dw), full(b_dw), full(ln_g), full(ln_b), full(w_ap), full(w_cp), full(w_out),
                  full(g2), full(w_rt)],
        out_specs=[tok(d),
                   pl.BlockSpec((tm, d // 2), lambda bi, i: (bi * nt + i, 0)),
                   pl.BlockSpec((N_EXPERTS, tm), lambda bi, i: (0, bi * nt + i))],
        out_shape=[jax.ShapeDtypeStruct((b, s, d), F32),
                   jax.ShapeDtypeStruct((b * s, d // 2), U32),
                   jax.ShapeDtypeStruct((N_EXPERTS, b * s), F32)],
        scratch_shapes=[pltpu.VMEM((tm + 2 * HALO_ROWS, CONV_WIDTH), F32)],
        compiler_params=_params(("arbitrary", "arbitrary")),
        name="merge",
    )(o, hglu, hglu, hglu, g, x, gt1, sh2, sc2, w_dw, b_dw, ln_g, ln_b, w_ap, w_cp, w_out, g2, w_rt)


def _route_kernel(lg_ref, bias_ref, tri_ref, e_ref, w_ref, r_ref, cnt_ref, run_sc):
    step = pl.program_id(0)
    n_strips = lg_ref.shape[1] // LANES
    nv = N_EXPERTS // SUBLANES
    gv = EXPERTS_PER_GROUP // SUBLANES

    @pl.when(step == 0)
    def _():
        run_sc[...] = jnp.zeros(run_sc.shape, F32)

    row = (lax.broadcasted_iota(I32, (nv, SUBLANES, LANES), 0) * SUBLANES
           + lax.broadcasted_iota(I32, (nv, SUBLANES, LANES), 1))
    sub = lax.broadcasted_iota(I32, (SUBLANES, LANES), 0)
    bias = bias_ref[...].reshape(nv, SUBLANES, LANES)
    neg_inf = jnp.float32(-jnp.inf)

    for st in range(n_strips):
        lanes = slice(st * LANES, (st + 1) * LANES)
        scores = _sigmoid(lg_ref[:, lanes]).reshape(nv, SUBLANES, LANES)
        biased = scores + bias
        gscore = []
        for g in range(N_GROUPS):
            m1 = biased[g * gv]
            m2 = jnp.full((SUBLANES, LANES), neg_inf, F32)
            for t in range(1, gv):
                v = biased[g * gv + t]
                m2 = jnp.maximum(m2, jnp.minimum(m1, v))
                m1 = jnp.maximum(m1, v)
            for s in (4, 2, 1):
                p1 = pltpu.roll(m1, s, 0)
                p2 = pltpu.roll(m2, s, 0)
                m2 = jnp.maximum(jnp.minimum(m1, p1), jnp.maximum(m2, p2))
                m1 = jnp.maximum(m1, p1)
            gscore.append(m1 + m2)
        masked = []
        for g in range(N_GROUPS):
            beaten = jnp.zeros((SUBLANES, LANES), I32)
            for o in range(N_GROUPS):
                if o == g:
                    continue
                wins = (gscore[o] > gscore[g]) | ((gscore[o] == gscore[g]) & (o < g))
                beaten = beaten + wins.astype(I32)
            keep = beaten < TOPK_GROUPS
            for t in range(gv):
                masked.append(jnp.where(keep, biased[g * gv + t], neg_inf))
        cand = jnp.stack(masked, axis=0)
        sel = jnp.zeros((nv, SUBLANES, LANES), jnp.bool_)
        picks, pick_scores = [], []
        for _ in range(TOP_K):
            mx = _sub_allreduce(jnp.max(cand, axis=0), jnp.maximum)
            idx = _sub_allreduce(jnp.min(jnp.where(cand == mx, row, N_EXPERTS), axis=0), jnp.minimum)
            hit = row == idx
            pick_scores.append(_sub_allreduce(jnp.sum(jnp.where(hit, scores, 0.0), axis=0), jnp.add))
            picks.append(idx)
            sel = sel | hit
            cand = jnp.where(hit, neg_inf, cand)
        sel_b = sel.astype(F32).astype(BF16).reshape(N_EXPERTS, LANES)
        before = jnp.dot(sel_b, tri_ref[0], preferred_element_type=F32)
        total = jnp.dot(sel_b, tri_ref[1], preferred_element_type=F32)
        rank_all = (before + run_sc[...]).reshape(nv, SUBLANES, LANES)
        run_sc[...] = run_sc[...] + total
        denom = pick_scores[0]
        for kk in range(1, TOP_K):
            denom = denom + pick_scores[kk]
        e_out = jnp.zeros((SUBLANES, LANES), I32)
        w_out = jnp.zeros((SUBLANES, LANES), F32)
        r_out = jnp.zeros((SUBLANES, LANES), I32)
        for kk in range(TOP_K):
            rk = _sub_allreduce(jnp.sum(jnp.where(row == picks[kk], rank_all, 0.0), axis=0), jnp.add)
            e_out = jnp.where(sub == kk, picks[kk], e_out)
            w_out = jnp.where(sub == kk, pick_scores[kk] / denom * ROUTED_SCALE, w_out)
            r_out = jnp.where(sub == kk, rk.astype(I32), r_out)
        e_ref[:, lanes] = e_out
        w_ref[:, lanes] = w_out
        r_ref[:, lanes] = r_out

    cnt_ref[...] = run_sc[...].astype(I32)


def _route(logits_t, bias, tb):
    n = logits_t.shape[1]
    iota_r = lax.broadcasted_iota(I32, (LANES, LANES), 0)
    iota_c = lax.broadcasted_iota(I32, (LANES, LANES), 1)
    tri = jnp.stack([(iota_r < iota_c), jnp.ones((LANES, LANES), jnp.bool_)]).astype(BF16)
    bias_b = jnp.broadcast_to(bias.reshape(N_EXPERTS, 1), (N_EXPERTS, LANES)).astype(F32)
    tokrow = lambda dt: jax.ShapeDtypeStruct((TOP_K, n), dt)
    return pl.pallas_call(
        _route_kernel,
        grid=(n // tb,),
        in_specs=[pl.BlockSpec((N_EXPERTS, tb), lambda i: (0, i)),
                  pl.BlockSpec((N_EXPERTS, LANES), lambda i: (0, 0)),
                  pl.BlockSpec((2, LANES, LANES), lambda i: (0, 0, 0))],
        out_specs=[pl.BlockSpec((TOP_K, tb), lambda i: (0, i)),
                   pl.BlockSpec((TOP_K, tb), lambda i: (0, i)),
                   pl.BlockSpec((TOP_K, tb), lambda i: (0, i)),
                   pl.BlockSpec((N_EXPERTS, LANES), lambda i: (0, 0))],
        out_shape=[tokrow(I32), tokrow(F32), tokrow(I32),
                   jax.ShapeDtypeStruct((N_EXPERTS, LANES), I32)],
        scratch_shapes=[pltpu.VMEM((N_EXPERTS, LANES), F32)],
        compiler_params=_params(("arbitrary",)),
        name="route",
    )(logits_t, bias_b, tri)


def _row_copy(src, dst, sem):
    return pltpu.make_async_copy(src, dst, sem)


def _dispatch_kernel(pstart_ref, e_ref, r_ref, hx_ref, xs_in_ref, xs_ref, sem):
    del xs_in_ref
    tm = hx_ref.shape[0]

    def copy(t, kk):
        slot = pstart_ref[e_ref[0, 0, kk * tm + t]] + r_ref[0, 0, kk * tm + t]
        return _row_copy(hx_ref.at[pl.ds(t, 1), :], xs_ref.at[pl.ds(slot, 1), :], sem)

    def issue(t, carry):
        for kk in range(TOP_K):
            copy(t, kk).start()
        return carry

    def drain(t, carry):
        for kk in range(TOP_K):
            copy(t, kk).wait()
        return carry

    lax.fori_loop(0, tm, issue, 0)
    lax.fori_loop(0, tm, drain, 0)


def _dispatch(pstart, e_blk, r_blk, hx, cap, tm):
    n, w = hx.shape
    idx = lambda: pl.BlockSpec((1, 1, TOP_K * tm), lambda i, ps: (i, 0, 0), memory_space=pltpu.SMEM)
    xs0 = jnp.zeros((cap, w), U32)
    return pl.pallas_call(
        _dispatch_kernel,
        grid_spec=pltpu.PrefetchScalarGridSpec(
            num_scalar_prefetch=1,
            grid=(n // tm,),
            in_specs=[idx(), idx(),
                      pl.BlockSpec((tm, w), lambda i, ps: (i, 0)),
                      pl.BlockSpec(memory_space=pl.ANY)],
            out_specs=pl.BlockSpec(memory_space=pl.ANY),
            scratch_shapes=[pltpu.SemaphoreType.DMA(())]),
        out_shape=jax.ShapeDtypeStruct((cap, w), U32),
        input_output_aliases={4: 0},
        compiler_params=_params(("arbitrary",)),
        name="dispatch",
    )(pstart, e_blk, r_blk, hx, xs0)


def _experts_kernel(blk_e_ref, n_real_ref, xs_ref, wgu_ref, wdn_ref, ys_ref, wgu_sc, wdn_sc):
    i = pl.program_id(0)
    real = i < n_real_ref[0]

    @pl.when(real)
    def _():
        prev = blk_e_ref[jnp.maximum(i - 1, 0)]

        @pl.when((i == 0) | (blk_e_ref[i] != prev))
        def _():
            wgu_sc[...] = wgu_ref[0].astype(BF16)
            wdn_sc[...] = wdn_ref[0].astype(BF16)

        lo, hi = _unpack_bf16_pair(xs_ref[...])
        half = lo.shape[1]
        gu = (jnp.dot(lo.astype(BF16), wgu_sc[:half, :], preferred_element_type=F32)
              + jnp.dot(hi.astype(BF16), wgu_sc[half:, :], preferred_element_type=F32))
        gt = gu[:, :EXPERT_HIDDEN]
        act = (gt * _sigmoid(gt) * gu[:, EXPERT_HIDDEN:]).astype(BF16)
        y = jnp.dot(act, wdn_sc[...], preferred_element_type=F32)
        ys_ref[...] = _pack_bf16_pair(y[:, :half], y[:, half:])

    @pl.when(jnp.logical_not(real))
    def _():
        ys_ref[...] = jnp.zeros(ys_ref.shape, U32)


def _experts(blk_e, n_real, xs, w_gu, w_dn):
    cap, w = xs.shape
    n_e, d, h2 = w_gu.shape
    return pl.pallas_call(
        _experts_kernel,
        grid_spec=pltpu.PrefetchScalarGridSpec(
            num_scalar_prefetch=2,
            grid=(cap // EXPERT_BLOCK,),
            in_specs=[pl.BlockSpec((EXPERT_BLOCK, w), lambda i, be, nr: (i, 0)),
                      pl.BlockSpec((1, d, h2), lambda i, be, nr: (be[i], 0, 0)),
                      pl.BlockSpec((1, h2 // 2, d), lambda i, be, nr: (be[i], 0, 0))],
            out_specs=pl.BlockSpec((EXPERT_BLOCK, w), lambda i, be, nr: (i, 0)),
            scratch_shapes=[pltpu.VMEM((d, h2), BF16), pltpu.VMEM((h2 // 2, d), BF16)]),
        out_shape=jax.ShapeDtypeStruct((cap, w), U32),
        compiler_params=_params(("arbitrary",)),
        name="experts",
    )(blk_e, n_real, xs, w_gu, w_dn)


def _combine_kernel(pstart_ref, e_ref, r_ref, wt_ref, hx_ref, x1_ref, gt2_ref, wsg_ref, wsd_ref, fg_ref,
                    ys_ref, o_ref, gbuf, sem):
    tm = hx_ref.shape[0]

    def copy(t, kk):
        slot = pstart_ref[e_ref[0, 0, kk * tm + t]] + r_ref[0, 0, kk * tm + t]
        return _row_copy(ys_ref.at[pl.ds(slot, 1), :], gbuf.at[kk, pl.ds(t, 1), :], sem)

    def issue(t, carry):
        for kk in range(TOP_K):
            copy(t, kk).start()
        return carry

    def drain(t, carry):
        for kk in range(TOP_K):
            copy(t, kk).wait()
        return carry

    lax.fori_loop(0, tm, issue, 0)
    lo, hi = _unpack_bf16_pair(hx_ref[...])
    half = lo.shape[1]
    gu = (jnp.dot(lo.astype(BF16), wsg_ref[:half, :], preferred_element_type=F32)
          + jnp.dot(hi.astype(BF16), wsg_ref[half:, :], preferred_element_type=F32))
    gt = gu[:, :SHARED_HIDDEN]
    act = (gt * _sigmoid(gt) * gu[:, SHARED_HIDDEN:]).astype(BF16)
    y = jnp.dot(act, wsd_ref[...], preferred_element_type=F32)
    y_lo = y[:, :half]
    y_hi = y[:, half:]
    lax.fori_loop(0, tm, drain, 0)
    for kk in range(TOP_K):
        r_lo, r_hi = _unpack_bf16_pair(gbuf[kk])
        wk = wt_ref[:, kk:kk + 1]
        y_lo = y_lo + wk * r_lo
        y_hi = y_hi + wk * r_hi
    x2_lo = x1_ref[:, :half] + gt2_ref[0, :, :half] * y_lo
    x2_hi = x1_ref[:, half:] + gt2_ref[0, :, half:] * y_hi
    ms = (jnp.sum(x2_lo * x2_lo, axis=-1, keepdims=True)
          + jnp.sum(x2_hi * x2_hi, axis=-1, keepdims=True)) / (2 * half)
    inv = lax.rsqrt(ms + EPS)
    o_ref[:, :half] = x2_lo * inv * fg_ref[:, :half]
    o_ref[:, half:] = x2_hi * inv * fg_ref[:, half:]


def _combine(pstart, e_blk, r_blk, wt, hx, x1, gt2, w_sg, w_sd, fg, ys, tm, tiles_per_batch):
    n, w = hx.shape
    d = 2 * w
    idx = lambda: pl.BlockSpec((1, 1, TOP_K * tm), lambda i, ps: (i, 0, 0), memory_space=pltpu.SMEM)
    full = lambda a: pl.BlockSpec(a.shape, lambda i, ps: (0,) * a.ndim)
    return pl.pallas_call(
        _combine_kernel,
        grid_spec=pltpu.PrefetchScalarGridSpec(
            num_scalar_prefetch=1,
            grid=(n // tm,),
            in_specs=[idx(), idx(),
                      pl.BlockSpec((tm, TOP_K), lambda i, ps: (i, 0)),
                      pl.BlockSpec((tm, w), lambda i, ps: (i, 0)),
                      pl.BlockSpec((tm, d), lambda i, ps: (i, 0)),
                      pl.BlockSpec((1, 1, d), lambda i, ps: (i // tiles_per_batch, 0, 0)),
                      full(w_sg), full(w_sd), full(fg),
                      pl.BlockSpec(memory_space=pl.ANY)],
            out_specs=pl.BlockSpec((tm, d), lambda i, ps: (i, 0)),
            scratch_shapes=[pltpu.VMEM((TOP_K, tm, w), U32), pltpu.SemaphoreType.DMA(())]),
        out_shape=jax.ShapeDtypeStruct((n, d), F32),
        compiler_params=_params(("arbitrary",)),
        name="combine",
    )(pstart, e_blk, r_blk, wt, hx, x1, gt2, w_sg, w_sd, fg, ys)


def _rope_tables(seq):
    rows = seq // GRID_W
    pos_row = jnp.repeat(jnp.arange(rows, dtype=F32), GRID_W)
    pos_col = jnp.tile(jnp.arange(GRID_W, dtype=F32), rows)
    inv_freq = ROPE_THETA ** (-jnp.arange(0, AXIS_DIM, 2, dtype=F32) / AXIS_DIM)
    ar = pos_row[:, None] * inv_freq
    ac = pos_col[:, None] * inv_freq
    cos_t = jnp.concatenate([jnp.cos(ar), jnp.cos(ar), jnp.cos(ac), jnp.cos(ac)], axis=1)
    sin_t = jnp.concatenate([-jnp.sin(ar), jnp.sin(ar), -jnp.sin(ac), jnp.sin(ac)], axis=1)
    return cos_t, sin_t


def _tile(n, want):
    t = min(n, want)
    assert n % t == 0, (n, want)
    return t


def kernel(x, c, ctx, c_ctx, w_mod, b_mod, norm1_g, w_in, q_norm_g, k_norm_g, w_dw, b_dw, conv_ln_g, conv_ln_b, w_attn_proj, w_conv_proj, w_out, norm2_g, w_router, router_bias, w_exp_gu, w_exp_dn, w_sh_gu, w_sh_dn, final_g):
    b, s, d = x.shape
    depth = w_mod.shape[0]
    assert depth == 1, "single-layer block"
    n = b * s
    row = lambda v: v.reshape(1, -1)

    cc = jnp.zeros((SUBLANES, d), F32).at[:b].set(c).at[b].set(c_ctx)
    mod = _modulation(cc, w_mod[0], row(b_mod[0]))
    mod_x = mod[:b].reshape(b, 1, 6, d)
    sh1, sc1, gt1, sh2, sc2, gt2 = [mod_x[:, :, j, :] for j in range(6)]
    mod_c = mod[b].reshape(6, d)
    csh1, csc1 = row(mod_c[0]), row(mod_c[1])

    w_in_b = w_in[0].astype(BF16)
    q_end, kv_end = ATTN_WIDTH, ATTN_WIDTH + 2 * KV_WIDTH
    k_end = q_end + KV_WIDTH
    w_vt = w_in_b[:, k_end:kv_end].T
    kc, vct = _ctx_kv(ctx, csh1, csc1, row(norm1_g[0]), w_in_b[:, q_end:k_end], w_vt, row(k_norm_g[0]))

    cos_t, sin_t = _rope_tables(s)
    q, kx, vxt, hglu, gates = _in_proj(x, sh1, sc1, row(norm1_g[0]), w_in_b, w_vt, row(q_norm_g[0]),
                                       row(k_norm_g[0]), cos_t, sin_t, _tile(s, 512))
    o = _attention(q, kc, vct, kx, vxt, _tile(s, 512), _tile(s // 2, 1024))

    x1, hx, logits_t = _merge(o, hglu, gates, x, gt1, sh2, sc2, w_dw[0], row(b_dw[0]), row(conv_ln_g[0]),
                              row(conv_ln_b[0]), w_attn_proj[0].astype(BF16), w_conv_proj[0].astype(BF16),
                              w_out[0].astype(BF16), row(norm2_g[0]), w_router[0].T, _tile(s, 256))

    e_idx, wts, rank, counts = _route(logits_t, router_bias[0], _tile(n, 512))

    cnt = counts[:, 0]
    padded = (cnt + EXPERT_BLOCK - 1) // EXPERT_BLOCK * EXPERT_BLOCK
    pends = jnp.cumsum(padded)
    pstart = (pends - padded).astype(I32)
    n_blocks = (n * TOP_K + N_EXPERTS * (EXPERT_BLOCK - 1)) // EXPERT_BLOCK
    cap = n_blocks * EXPERT_BLOCK
    blk_row0 = jnp.arange(n_blocks, dtype=I32) * EXPERT_BLOCK
    blk_e = jnp.minimum(jnp.sum(pends[None, :] <= blk_row0[:, None], axis=1), N_EXPERTS - 1).astype(I32)
    n_real = (pends[-1:] // EXPERT_BLOCK).astype(I32)

    tm = _tile(n, 256)
    blocked = lambda a: a.reshape(TOP_K, n // tm, tm).transpose(1, 0, 2).reshape(n // tm, 1, TOP_K * tm)
    e_blk, r_blk = blocked(e_idx), blocked(rank)
    xs = _dispatch(pstart, e_blk, r_blk, hx, cap, tm)
    ys = _experts(blk_e, n_real, xs, w_exp_gu[0], w_exp_dn[0])
    out = _combine(pstart, e_blk, r_blk, wts.T, hx, x1.reshape(n, d), gt2, w_sh_gu[0].astype(BF16),
                   w_sh_dn[0].astype(BF16), row(final_g), ys, tm, s // tm)
    return out.reshape(b, s, d)
```

```python
import functools
import math

import jax
import jax.numpy as jnp
from jax import lax
from jax.experimental import pallas as pl
from jax.experimental.pallas import tpu as pltpu
from jax.experimental.pallas import tpu_sc as plsc

F32 = jnp.float32
BF16 = jnp.bfloat16
U32 = jnp.uint32
I32 = jnp.int32

GRID_W = 64
N_HEADS = 8
N_KV_HEADS = 4
GROUP = N_HEADS // N_KV_HEADS
HEAD_DIM = 128
AXIS_DIM = HEAD_DIM // 2
ATTN_WIDTH = N_HEADS * HEAD_DIM
KV_WIDTH = N_KV_HEADS * HEAD_DIM
ROPE_THETA = 10000.0
ATTN_SCALE = HEAD_DIM ** -0.5
CONV_WIDTH = 512
CONV_KERNEL = 31
CONV_PAD = CONV_KERNEL // 2
N_EXPERTS = 256
TOP_K = 8
N_GROUPS = 8
TOPK_GROUPS = 4
EXPERTS_PER_GROUP = N_EXPERTS // N_GROUPS
EXPERT_HIDDEN = 256
SHARED_HIDDEN = 256
ROUTED_SCALE = 2.5
EPS = 1e-6
LOG2E = 1.4426950408889634

LANES = 128
SUBLANES = 8
VMEM_LIMIT_BYTES = 56 * 1024 * 1024

HALO_ROWS = 16
EXPERT_BLOCK = 512
SC_SUBCORES = 16
SC_WORKERS = 2 * SC_SUBCORES
SC_WINDOW = 128
HIGHEST = lax.Precision.HIGHEST


def _params(sem):
    return pltpu.CompilerParams(dimension_semantics=sem, vmem_limit_bytes=VMEM_LIMIT_BYTES)


def _sigmoid(x):
    return 1.0 / (1.0 + jnp.exp(-x))


def _pack_bf16_pair(lo, hi):
    lo_b = pltpu.bitcast(lo.astype(BF16).astype(F32), U32)
    hi_b = pltpu.bitcast(hi.astype(BF16).astype(F32), U32)
    return (lo_b >> 16) | (hi_b & jnp.uint32(0xFFFF0000))


def _unpack_bf16_pair(u):
    lo = pltpu.bitcast(u << 16, F32)
    hi = pltpu.bitcast(u & jnp.uint32(0xFFFF0000), F32)
    return lo, hi


def _mod_kernel(cc_ref, w_ref, b_ref, o_ref):
    cc = cc_ref[...]
    s = cc * _sigmoid(cc)
    o_ref[...] = jnp.dot(s, w_ref[...], precision=HIGHEST, preferred_element_type=F32) + b_ref[...]


def _modulation(cc, w_mod, b_mod):
    d, n = w_mod.shape
    tn = n // 4
    return pl.pallas_call(
        _mod_kernel,
        grid=(n // tn,),
        in_specs=[pl.BlockSpec((SUBLANES, d), lambda j: (0, 0)),
                  pl.BlockSpec((d, tn), lambda j: (0, j)),
                  pl.BlockSpec((1, tn), lambda j: (0, j))],
        out_specs=pl.BlockSpec((SUBLANES, tn), lambda j: (0, j)),
        out_shape=jax.ShapeDtypeStruct((SUBLANES, n), F32),
        compiler_params=_params(("arbitrary",)),
        name="mod",
    )(cc, w_mod, b_mod)


def _norm_modulate(x, g, sh, sc):
    ms = jnp.mean(x * x, axis=-1, keepdims=True)
    return (x * lax.rsqrt(ms + EPS) * g) * (1.0 + sc) + sh


def _head_norm(p, gain):
    r = lax.rsqrt(jnp.mean(p * p, axis=-1, keepdims=True) + EPS)
    return p * r * gain


def _dot_nt(a, b):
    return lax.dot_general(a, b, (((1,), (1,)), ((), ())), preferred_element_type=F32)


def _ctx_kv_kernel(x_ref, sh_ref, sc_ref, g1_ref, wk_ref, wvt_ref, gk_ref, k_ref, vt_ref):
    h = _norm_modulate(x_ref[0], g1_ref[...], sh_ref[...], sc_ref[...]).astype(BF16)
    pk = jnp.dot(h, wk_ref[...], preferred_element_type=F32)
    for j in range(N_KV_HEADS):
        sl = slice(j * HEAD_DIM, (j + 1) * HEAD_DIM)
        k_ref[0, :, sl] = _head_norm(pk[:, sl], gk_ref[...]).astype(BF16)
    vt_ref[0] = _dot_nt(wvt_ref[...], h).astype(BF16)


def _ctx_kv(ctx, csh, csc, g1, w_k, w_vt, gk):
    b, lc, d = ctx.shape
    vec = lambda: pl.BlockSpec((1, d), lambda i: (0, 0))
    return pl.pallas_call(
        _ctx_kv_kernel,
        grid=(b,),
        in_specs=[pl.BlockSpec((1, lc, d), lambda i: (i, 0, 0)), vec(), vec(), vec(),
                  pl.BlockSpec((d, KV_WIDTH), lambda i: (0, 0)),
                  pl.BlockSpec((KV_WIDTH, d), lambda i: (0, 0)),
                  pl.BlockSpec((1, HEAD_DIM), lambda i: (0, 0))],
        out_specs=[pl.BlockSpec((1, lc, KV_WIDTH), lambda i: (i, 0, 0)),
                   pl.BlockSpec((1, KV_WIDTH, lc), lambda i: (i, 0, 0))],
        out_shape=[jax.ShapeDtypeStruct((b, lc, KV_WIDTH), BF16),
                   jax.ShapeDtypeStruct((b, KV_WIDTH, lc), BF16)],
        compiler_params=_params(("arbitrary",)),
        name="ctx_kv",
    )(ctx, csh, csc, g1, w_k, w_vt, gk)


def _in_proj_kernel(x_ref, sh_ref, sc_ref, g1_ref, w_ref, wvt_ref, gq_ref, gk_ref, cos_ref, sin_ref,
                    q_ref, k_ref, vt_ref, h_ref, g_ref):
    h = _norm_modulate(x_ref[0], g1_ref[...], sh_ref[0], sc_ref[0]).astype(BF16)
    cos = cos_ref[...]
    sin = sin_ref[...]
    lane = lax.broadcasted_iota(I32, cos.shape, 1)
    upper = (lane & (AXIS_DIM // 2)) != 0

    def rope(p):
        swapped = jnp.where(upper, pltpu.roll(p, AXIS_DIM // 2, 1),
                            pltpu.roll(p, HEAD_DIM - AXIS_DIM // 2, 1))
        return p * cos + swapped * sin

    q_end = ATTN_WIDTH
    k_end = q_end + KV_WIDTH
    v_end = k_end + KV_WIDTH
    u_end = v_end + 2 * CONV_WIDTH
    pq = jnp.dot(h, w_ref[:, :q_end], preferred_element_type=F32)
    for j in range(N_HEADS):
        sl = slice(j * HEAD_DIM, (j + 1) * HEAD_DIM)
        q_ref[0, :, sl] = (rope(_head_norm(pq[:, sl], gq_ref[...])) * (ATTN_SCALE * LOG2E)).astype(BF16)
    pk = jnp.dot(h, w_ref[:, q_end:k_end], preferred_element_type=F32)
    for j in range(N_KV_HEADS):
        sl = slice(j * HEAD_DIM, (j + 1) * HEAD_DIM)
        k_ref[0, :, sl] = rope(_head_norm(pk[:, sl], gk_ref[...])).astype(BF16)
    vt_ref[0] = _dot_nt(wvt_ref[...], h).astype(BF16)
    u = jnp.dot(h, w_ref[:, v_end:u_end], preferred_element_type=F32)
    h_ref[0] = (u[:, :CONV_WIDTH] * _sigmoid(u[:, CONV_WIDTH:])).astype(BF16)
    g_ref[0] = _sigmoid(jnp.dot(h, w_ref[:, u_end:], preferred_element_type=F32)).astype(BF16)


def _in_proj(x, sh1, sc1, g1, w_in, w_vt, gq, gk, cos_t, sin_t, tm):
    b, s, d = x.shape
    n_in = w_in.shape[1]
    bvec = lambda: pl.BlockSpec((1, 1, d), lambda bi, i: (bi, 0, 0))
    tok = lambda w: pl.BlockSpec((1, tm, w), lambda bi, i: (bi, i, 0))
    return pl.pallas_call(
        _in_proj_kernel,
        grid=(b, s // tm),
        in_specs=[tok(d), bvec(), bvec(),
                  pl.BlockSpec((1, d), lambda bi, i: (0, 0)),
                  pl.BlockSpec((d, n_in), lambda bi, i: (0, 0)),
                  pl.BlockSpec((KV_WIDTH, d), lambda bi, i: (0, 0)),
                  pl.BlockSpec((1, HEAD_DIM), lambda bi, i: (0, 0)),
                  pl.BlockSpec((1, HEAD_DIM), lambda bi, i: (0, 0)),
                  pl.BlockSpec((tm, HEAD_DIM), lambda bi, i: (i, 0)),
                  pl.BlockSpec((tm, HEAD_DIM), lambda bi, i: (i, 0))],
        out_specs=[tok(ATTN_WIDTH), tok(KV_WIDTH),
                   pl.BlockSpec((1, KV_WIDTH, tm), lambda bi, i: (bi, 0, i)),
                   tok(CONV_WIDTH), tok(2 * d)],
        out_shape=[jax.ShapeDtypeStruct((b, s, ATTN_WIDTH), BF16),
                   jax.ShapeDtypeStruct((b, s, KV_WIDTH), BF16),
                   jax.ShapeDtypeStruct((b, KV_WIDTH, s), BF16),
                   jax.ShapeDtypeStruct((b, s, CONV_WIDTH), BF16),
                   jax.ShapeDtypeStruct((b, s, 2 * d), BF16)],
        compiler_params=_params(("arbitrary", "arbitrary")),
        name="in_proj",
    )(x, sh1, sc1, g1, w_in, w_vt, gq, gk, cos_t, sin_t)


def _sub_allreduce(x, op):
    for s in (4, 2, 1):
        x = op(x, pltpu.roll(x, s, 0))
    return x


def _attn_kernel(q_ref, kc_ref, vct_ref, k_ref, vt_ref, o_ref, s0, s1, m_sc, l_sc, acc_sc, *, tk):
    tq = q_ref.shape[1]
    m_cols = GROUP * tq
    nk = k_ref.shape[1] // tk
    qf = q_ref[0].astype(F32).T
    qt = jnp.concatenate([qf[:HEAD_DIM], qf[HEAD_DIM:]], axis=1).astype(BF16)

    def scores(k):
        return jnp.dot(k, qt, preferred_element_type=F32)

    def absorb(st, vt):
        n = st.shape[0]
        s3 = st.reshape(n // SUBLANES, SUBLANES, m_cols)
        m_prev = m_sc[...]
        m_new = jnp.maximum(m_prev, _sub_allreduce(jnp.max(s3, axis=0), jnp.maximum))
        alpha = jnp.exp2(m_prev - m_new)
        p3 = jnp.exp2(s3 - m_new[None])
        l_sc[...] = alpha * l_sc[...] + _sub_allreduce(jnp.sum(p3, axis=0), jnp.add)
        pv = jnp.dot(vt, p3.reshape(n, m_cols).astype(BF16), preferred_element_type=F32)
        acc_sc[...] = alpha[0:1] * acc_sc[...] + pv
        m_sc[...] = m_new

    def kchunk(j):
        return k_ref[0, pl.ds(pl.multiple_of(j * tk, tk), tk), :]

    def vchunk(j):
        return vt_ref[0, :, pl.ds(pl.multiple_of(j * tk, tk), tk)]

    m_sc[...] = jnp.full(m_sc.shape, -jnp.inf, F32)
    l_sc[...] = jnp.zeros(l_sc.shape, F32)
    acc_sc[...] = jnp.zeros(acc_sc.shape, F32)
    s0[...] = scores(kchunk(0))
    absorb(scores(kc_ref[0]), vct_ref[0])

    def body(i, carry):
        j = 2 * i
        s1[...] = scores(kchunk(j + 1))
        absorb(s0[...], vchunk(j))
        s0[...] = scores(kchunk(j + 2))
        absorb(s1[...], vchunk(j + 1))
        return carry

    lax.fori_loop(0, nk // 2 - 1, body, 0)
    s1[...] = scores(kchunk(nk - 1))
    absorb(s0[...], vchunk(nk - 2))
    absorb(s1[...], vchunk(nk - 1))

    o = (acc_sc[...] / l_sc[0:1]).T
    o_ref[0, :, :HEAD_DIM] = o[:tq].astype(BF16)
    o_ref[0, :, HEAD_DIM:] = o[tq:].astype(BF16)


def _attention(q, kc, vct, kx, vxt, tq, tk):
    b, s, _ = q.shape
    lc = kc.shape[1]
    assert s % (2 * tk) == 0
    gw = GROUP * HEAD_DIM
    m_cols = GROUP * tq
    kv = lambda l: pl.BlockSpec((1, l, HEAD_DIM), lambda bi, h, i: (bi, 0, h))
    kvt = lambda l: pl.BlockSpec((1, HEAD_DIM, l), lambda bi, h, i: (bi, h, 0))
    return pl.pallas_call(
        functools.partial(_attn_kernel, tk=tk),
        grid=(b, N_KV_HEADS, s // tq),
        in_specs=[pl.BlockSpec((1, tq, gw), lambda bi, h, i: (bi, i, h)), kv(lc), kvt(lc), kv(s), kvt(s)],
        out_specs=pl.BlockSpec((1, tq, gw), lambda bi, h, i: (bi, i, h)),
        out_shape=jax.ShapeDtypeStruct((b, s, ATTN_WIDTH), BF16),
        scratch_shapes=[pltpu.VMEM((tk, m_cols), F32), pltpu.VMEM((tk, m_cols), F32),
                        pltpu.VMEM((SUBLANES, m_cols), F32), pltpu.VMEM((SUBLANES, m_cols), F32),
                        pltpu.VMEM((HEAD_DIM, m_cols), F32)],
        compiler_params=_params(("arbitrary", "arbitrary", "arbitrary")),
        name="attn",
    )(q, kc, vct, kx, vxt)


def _merge_kernel(o_ref, hp_ref, hc_ref, hn_ref, g_ref, x_ref, gt1_ref, sh2_ref, sc2_ref,
                  wdw_ref, bdw_ref, lng_ref, lnb_ref, wap_ref, wcp_ref, wout_ref, g2_ref, wrt_ref,
                  x1_ref, hx_ref, lg_ref, hcat):
    i = pl.program_id(1)
    n_i = pl.num_programs(1)
    tm = hc_ref.shape[1]
    d = x_ref.shape[2]
    prev = hp_ref[0].astype(F32)
    nxt = hn_ref[0].astype(F32)
    hcat[0:HALO_ROWS, :] = jnp.where(i > 0, prev, jnp.zeros_like(prev))
    hcat[HALO_ROWS:HALO_ROWS + tm, :] = hc_ref[0].astype(F32)
    hcat[HALO_ROWS + tm:, :] = jnp.where(i < n_i - 1, nxt, jnp.zeros_like(nxt))
    acc = jnp.zeros((tm, CONV_WIDTH), F32) + bdw_ref[...]
    base = HALO_ROWS - CONV_PAD
    for j in range(CONV_KERNEL):
        acc = acc + hcat[base + j:base + j + tm, :] * wdw_ref[j:j + 1, :]
    mu = jnp.mean(acc, axis=-1, keepdims=True)
    cen = acc - mu
    var = jnp.mean(cen * cen, axis=-1, keepdims=True)
    ln = cen * lax.rsqrt(var + EPS) * lng_ref[...] + lnb_ref[...]
    act = (ln * _sigmoid(ln)).astype(BF16)
    y_conv = jnp.dot(act, wcp_ref[...], preferred_element_type=F32)
    y_attn = jnp.dot(o_ref[0], wap_ref[...], preferred_element_type=F32)
    z = g_ref[0, :, :d].astype(F32) * y_attn + g_ref[0, :, d:].astype(F32) * y_conv
    mix = jnp.dot(z.astype(BF16), wout_ref[...], preferred_element_type=F32)
    x1 = x_ref[0] + gt1_ref[0] * mix
    x1_ref[0] = x1
    hx = _norm_modulate(x1, g2_ref[...], sh2_ref[0], sc2_ref[0])
    half = d // 2
    hx_ref[...] = pltpu.bitcast(_pack_bf16_pair(hx[:, :half], hx[:, half:]), I32)
    lg_ref[...] = lax.dot_general(wrt_ref[...], hx, (((1,), (1,)), ((), ())),
                                  precision=HIGHEST, preferred_element_type=F32)


def _merge(o, hglu, g, x, gt1, sh2, sc2, w_dw, b_dw, ln_g, ln_b, w_ap, w_cp, w_out, g2, w_rt, tm):
    b, s, d = x.shape
    nt = s // tm
    hb = tm // HALO_ROWS
    n_halo = s // HALO_ROWS
    bvec = lambda: pl.BlockSpec((1, 1, d), lambda bi, i: (bi, 0, 0))
    full = lambda a: pl.BlockSpec(a.shape, lambda bi, i: (0,) * a.ndim)
    tok = lambda w: pl.BlockSpec((1, tm, w), lambda bi, i: (bi, i, 0))
    return pl.pallas_call(
        _merge_kernel,
        grid=(b, nt),
        in_specs=[tok(ATTN_WIDTH),
                  pl.BlockSpec((1, HALO_ROWS, CONV_WIDTH), lambda bi, i: (bi, jnp.maximum(i * hb - 1, 0), 0)),
                  tok(CONV_WIDTH),
                  pl.BlockSpec((1, HALO_ROWS, CONV_WIDTH),
                               lambda bi, i: (bi, jnp.minimum((i + 1) * hb, n_halo - 1), 0)),
                  tok(2 * d), tok(d), bvec(), bvec(), bvec(),
                  full(w_dw), full(b_dw), full(ln_g), full(ln_b), full(w_ap), full(w_cp), full(w_out),
                  full(g2), full(w_rt)],
        out_specs=[tok(d),
                   pl.BlockSpec((tm, d // 2), lambda bi, i: (bi * nt + i, 0)),
                   pl.BlockSpec((N_EXPERTS, tm), lambda bi, i: (0, bi * nt + i))],
        out_shape=[jax.ShapeDtypeStruct((b, s, d), F32),
                   jax.ShapeDtypeStruct((b * s, d // 2), I32),
                   jax.ShapeDtypeStruct((N_EXPERTS, b * s), F32)],
        scratch_shapes=[pltpu.VMEM((tm + 2 * HALO_ROWS, CONV_WIDTH), F32)],
        compiler_params=_params(("arbitrary", "arbitrary")),
        name="merge",
    )(o, hglu, hglu, hglu, g, x, gt1, sh2, sc2, w_dw, b_dw, ln_g, ln_b, w_ap, w_cp, w_out, g2, w_rt)


def _route_kernel(lg_ref, bias_ref, tri_ref, e_ref, w_ref, r_ref, cnt_ref, run_sc):
    step = pl.program_id(0)
    n_strips = lg_ref.shape[1] // LANES
    nv = N_EXPERTS // SUBLANES
    gv = EXPERTS_PER_GROUP // SUBLANES

    @pl.when(step == 0)
    def _():
        run_sc[...] = jnp.zeros(run_sc.shape, F32)

    row = (lax.broadcasted_iota(I32, (nv, SUBLANES, LANES), 0) * SUBLANES
           + lax.broadcasted_iota(I32, (nv, SUBLANES, LANES), 1))
    sub = lax.broadcasted_iota(I32, (SUBLANES, LANES), 0)
    bias = bias_ref[...].reshape(nv, SUBLANES, LANES)
    neg_inf = jnp.float32(-jnp.inf)

    for st in range(n_strips):
        lanes = slice(st * LANES, (st + 1) * LANES)
        scores = _sigmoid(lg_ref[:, lanes]).reshape(nv, SUBLANES, LANES)
        biased = scores + bias
        gscore = []
        for g in range(N_GROUPS):
            m1 = biased[g * gv]
            m2 = jnp.full((SUBLANES, LANES), neg_inf, F32)
            for t in range(1, gv):
                v = biased[g * gv + t]
                m2 = jnp.maximum(m2, jnp.minimum(m1, v))
                m1 = jnp.maximum(m1, v)
            for s in (4, 2, 1):
                p1 = pltpu.roll(m1, s, 0)
                p2 = pltpu.roll(m2, s, 0)
                m2 = jnp.maximum(jnp.minimum(m1, p1), jnp.maximum(m2, p2))
                m1 = jnp.maximum(m1, p1)
            gscore.append(m1 + m2)
        masked = []
        for g in range(N_GROUPS):
            beaten = jnp.zeros((SUBLANES, LANES), I32)
            for o in range(N_GROUPS):
                if o == g:
                    continue
                wins = (gscore[o] > gscore[g]) | ((gscore[o] == gscore[g]) & (o < g))
                beaten = beaten + wins.astype(I32)
            keep = beaten < TOPK_GROUPS
            for t in range(gv):
                masked.append(jnp.where(keep, biased[g * gv + t], neg_inf))
        cand = jnp.stack(masked, axis=0)
        sel = jnp.zeros((nv, SUBLANES, LANES), jnp.bool_)
        picks, pick_scores = [], []
        for _ in range(TOP_K):
            mx = _sub_allreduce(jnp.max(cand, axis=0), jnp.maximum)
            idx = _sub_allreduce(jnp.min(jnp.where(cand == mx, row, N_EXPERTS), axis=0), jnp.minimum)
            hit = row == idx
            pick_scores.append(_sub_allreduce(jnp.sum(jnp.where(hit, scores, 0.0), axis=0), jnp.add))
            picks.append(idx)
            sel = sel | hit
            cand = jnp.where(hit, neg_inf, cand)
        sel_b = sel.astype(F32).astype(BF16).reshape(N_EXPERTS, LANES)
        before = jnp.dot(sel_b, tri_ref[0], preferred_element_type=F32)
        total = jnp.dot(sel_b, tri_ref[1], preferred_element_type=F32)
        rank_all = (before + run_sc[...]).reshape(nv, SUBLANES, LANES)
        run_sc[...] = run_sc[...] + total
        denom = pick_scores[0]
        for kk in range(1, TOP_K):
            denom = denom + pick_scores[kk]
        e_out = jnp.zeros((SUBLANES, LANES), I32)
        w_out = jnp.zeros((SUBLANES, LANES), F32)
        r_out = jnp.zeros((SUBLANES, LANES), I32)
        for kk in range(TOP_K):
            rk = _sub_allreduce(jnp.sum(jnp.where(row == picks[kk], rank_all, 0.0), axis=0), jnp.add)
            e_out = jnp.where(sub == kk, picks[kk], e_out)
            w_out = jnp.where(sub == kk, pick_scores[kk] / denom * ROUTED_SCALE, w_out)
            r_out = jnp.where(sub == kk, rk.astype(I32), r_out)
        e_ref[:, lanes] = e_out
        w_ref[:, lanes] = w_out
        r_ref[:, lanes] = r_out

    cnt_ref[...] = run_sc[...].astype(I32)


def _route(logits_t, bias, tb):
    n = logits_t.shape[1]
    iota_r = lax.broadcasted_iota(I32, (LANES, LANES), 0)
    iota_c = lax.broadcasted_iota(I32, (LANES, LANES), 1)
    tri = jnp.stack([(iota_r < iota_c), jnp.ones((LANES, LANES), jnp.bool_)]).astype(BF16)
    bias_b = jnp.broadcast_to(bias.reshape(N_EXPERTS, 1), (N_EXPERTS, LANES)).astype(F32)
    tokrow = lambda dt: jax.ShapeDtypeStruct((TOP_K, n), dt)
    return pl.pallas_call(
        _route_kernel,
        grid=(n // tb,),
        in_specs=[pl.BlockSpec((N_EXPERTS, tb), lambda i: (0, i)),
                  pl.BlockSpec((N_EXPERTS, LANES), lambda i: (0, 0)),
                  pl.BlockSpec((2, LANES, LANES), lambda i: (0, 0, 0))],
        out_specs=[pl.BlockSpec((TOP_K, tb), lambda i: (0, i)),
                   pl.BlockSpec((TOP_K, tb), lambda i: (0, i)),
                   pl.BlockSpec((TOP_K, tb), lambda i: (0, i)),
                   pl.BlockSpec((N_EXPERTS, LANES), lambda i: (0, 0))],
        out_shape=[tokrow(I32), tokrow(F32), tokrow(I32),
                   jax.ShapeDtypeStruct((N_EXPERTS, LANES), I32)],
        scratch_shapes=[pltpu.VMEM((N_EXPERTS, LANES), F32)],
        compiler_params=_params(("arbitrary",)),
        name="route",
    )(logits_t, bias_b, tri)


def _slots_kernel(pstart_ref, e_ref, r_ref, o_ref):
    e = e_ref[...]

    def body(x, acc):
        return acc + jnp.where(e == x, pstart_ref[x], 0)

    o_ref[...] = lax.fori_loop(0, N_EXPERTS, body, r_ref[...])


def _slots(pstart, e_idx, rank, tb):
    n = e_idx.shape[1]
    blk = lambda: pl.BlockSpec((TOP_K, tb), lambda i, ps: (0, i))
    return pl.pallas_call(
        _slots_kernel,
        grid_spec=pltpu.PrefetchScalarGridSpec(
            num_scalar_prefetch=1, grid=(n // tb,), in_specs=[blk(), blk()], out_specs=blk()),
        out_shape=jax.ShapeDtypeStruct((TOP_K, n), I32),
        compiler_params=_params(("arbitrary",)),
        name="slots",
    )(pstart, e_idx, rank)


def _sc_mesh():
    return plsc.VectorSubcoreMesh(core_axis_name="c", subcore_axis_name="s")


def _sc_worker():
    return lax.axis_index("c") * SC_SUBCORES + lax.axis_index("s")


def _sc_dispatch(hx, pos, cap):
    n, w = hx.shape
    per_worker = n // SC_WINDOW // SC_WORKERS
    assert per_worker * SC_WINDOW * SC_WORKERS == n

    @pl.kernel(out_type=jax.ShapeDtypeStruct((cap, w), hx.dtype), mesh=_sc_mesh(),
               scratch_types=[pltpu.VMEM((SC_WINDOW, w), hx.dtype), pltpu.VMEM((TOP_K, SC_WINDOW), I32),
                              pltpu.SemaphoreType.DMA])
    def scatter_rows(x_hbm, i_hbm, o_hbm, xbuf, ibuf, sem):
        wid = _sc_worker()

        @pl.loop(0, per_worker)
        def _(j):
            row0 = (wid * per_worker + j) * SC_WINDOW
            pltpu.sync_copy(x_hbm.at[pl.ds(row0, SC_WINDOW)], xbuf)
            pltpu.sync_copy(i_hbm.at[:, pl.ds(row0, SC_WINDOW)], ibuf)
            copies = [pltpu.async_copy(xbuf, o_hbm.at[ibuf.at[kk]], sem) for kk in range(TOP_K)]
            for cp in copies:
                cp.wait()

    return scatter_rows(hx, pos)


def _sc_gather(ys, pos):
    n = pos.shape[1]
    w = ys.shape[1]
    per_worker = n // SC_WINDOW // SC_WORKERS
    assert per_worker * SC_WINDOW * SC_WORKERS == n

    @pl.kernel(out_type=jax.ShapeDtypeStruct((TOP_K, n, w), ys.dtype), mesh=_sc_mesh(),
               scratch_types=[pltpu.VMEM((SC_WINDOW, w), ys.dtype), pltpu.VMEM((TOP_K, SC_WINDOW), I32)])
    def gather_rows(y_hbm, i_hbm, o_hbm, ybuf, ibuf):
        wid = _sc_worker()

        @pl.loop(0, per_worker)
        def _(j):
            row0 = (wid * per_worker + j) * SC_WINDOW
            pltpu.sync_copy(i_hbm.at[:, pl.ds(row0, SC_WINDOW)], ibuf)
            for kk in range(TOP_K):
                pltpu.sync_copy(y_hbm.at[ibuf.at[kk]], ybuf)
                pltpu.sync_copy(ybuf, o_hbm.at[kk, pl.ds(row0, SC_WINDOW)])

    return gather_rows(ys, pos)


def _experts_kernel(blk_e_ref, valid_ref, xs_ref, wgu_ref, wdn_ref, ys_ref, wgu_sc, wdn_sc):
    i = pl.program_id(0)
    valid = valid_ref[i]

    @pl.when(valid > 0)
    def _():
        prev = blk_e_ref[jnp.maximum(i - 1, 0)]

        @pl.when((i == 0) | (blk_e_ref[i] != prev))
        def _():
            wgu_sc[...] = wgu_ref[0].astype(BF16)
            wdn_sc[...] = wdn_ref[0].astype(BF16)

        rows = lax.broadcasted_iota(I32, xs_ref.shape, 0)
        xu = jnp.where(rows < valid, pltpu.bitcast(xs_ref[...], U32), jnp.uint32(0))
        lo, hi = _unpack_bf16_pair(xu)
        half = lo.shape[1]
        gu = (jnp.dot(lo.astype(BF16), wgu_sc[:half, :], preferred_element_type=F32)
              + jnp.dot(hi.astype(BF16), wgu_sc[half:, :], preferred_element_type=F32))
        gt = gu[:, :EXPERT_HIDDEN]
        act = (gt * _sigmoid(gt) * gu[:, EXPERT_HIDDEN:]).astype(BF16)
        y = jnp.dot(act, wdn_sc[...], preferred_element_type=F32)
        ys_ref[...] = pltpu.bitcast(_pack_bf16_pair(y[:, :half], y[:, half:]), I32)

    @pl.when(valid <= 0)
    def _():
        ys_ref[...] = jnp.zeros(ys_ref.shape, I32)


def _experts(blk_e, blk_valid, xs, w_gu, w_dn):
    cap, w = xs.shape
    n_e, d, h2 = w_gu.shape
    return pl.pallas_call(
        _experts_kernel,
        grid_spec=pltpu.PrefetchScalarGridSpec(
            num_scalar_prefetch=2,
            grid=(cap // EXPERT_BLOCK,),
            in_specs=[pl.BlockSpec((EXPERT_BLOCK, w), lambda i, be, bv: (i, 0)),
                      pl.BlockSpec((1, d, h2), lambda i, be, bv: (be[i], 0, 0)),
                      pl.BlockSpec((1, h2 // 2, d), lambda i, be, bv: (be[i], 0, 0))],
            out_specs=pl.BlockSpec((EXPERT_BLOCK, w), lambda i, be, bv: (i, 0)),
            scratch_shapes=[pltpu.VMEM((d, h2), BF16), pltpu.VMEM((h2 // 2, d), BF16)]),
        out_shape=jax.ShapeDtypeStruct((cap, w), I32),
        compiler_params=_params(("arbitrary",)),
        name="experts",
    )(blk_e, blk_valid, xs, w_gu, w_dn)


def _combine_kernel(yg_ref, wt_ref, hx_ref, x1_ref, gt2_ref, wsg_ref, wsd_ref, fg_ref, o_ref):
    lo, hi = _unpack_bf16_pair(pltpu.bitcast(hx_ref[...], U32))
    half = lo.shape[1]
    gu = (jnp.dot(lo.astype(BF16), wsg_ref[:half, :], preferred_element_type=F32)
          + jnp.dot(hi.astype(BF16), wsg_ref[half:, :], preferred_element_type=F32))
    gt = gu[:, :SHARED_HIDDEN]
    act = (gt * _sigmoid(gt) * gu[:, SHARED_HIDDEN:]).astype(BF16)
    y = jnp.dot(act, wsd_ref[...], preferred_element_type=F32)
    y_lo = y[:, :half]
    y_hi = y[:, half:]
    for kk in range(TOP_K):
        r_lo, r_hi = _unpack_bf16_pair(pltpu.bitcast(yg_ref[kk], U32))
        wk = wt_ref[:, kk:kk + 1]
        y_lo = y_lo + wk * r_lo
        y_hi = y_hi + wk * r_hi
    x2_lo = x1_ref[:, :half] + gt2_ref[0, :, :half] * y_lo
    x2_hi = x1_ref[:, half:] + gt2_ref[0, :, half:] * y_hi
    ms = (jnp.sum(x2_lo * x2_lo, axis=-1, keepdims=True)
          + jnp.sum(x2_hi * x2_hi, axis=-1, keepdims=True)) / (2 * half)
    inv = lax.rsqrt(ms + EPS)
    o_ref[:, :half] = x2_lo * inv * fg_ref[:, :half]
    o_ref[:, half:] = x2_hi * inv * fg_ref[:, half:]


def _combine(yg, wt, hx, x1, gt2, w_sg, w_sd, fg, tm, tiles_per_batch):
    n, w = hx.shape
    d = 2 * w
    full = lambda a: pl.BlockSpec(a.shape, lambda i: (0,) * a.ndim)
    return pl.pallas_call(
        _combine_kernel,
        grid=(n // tm,),
        in_specs=[pl.BlockSpec((TOP_K, tm, w), lambda i: (0, i, 0)),
                  pl.BlockSpec((tm, TOP_K), lambda i: (i, 0)),
                  pl.BlockSpec((tm, w), lambda i: (i, 0)),
                  pl.BlockSpec((tm, d), lambda i: (i, 0)),
                  pl.BlockSpec((1, 1, d), lambda i: (i // tiles_per_batch, 0, 0)),
                  full(w_sg), full(w_sd), full(fg)],
        out_specs=pl.BlockSpec((tm, d), lambda i: (i, 0)),
        out_shape=jax.ShapeDtypeStruct((n, d), F32),
        compiler_params=_params(("arbitrary",)),
        name="combine",
    )(yg, wt, hx, x1, gt2, w_sg, w_sd, fg)


def _rope_tables(seq):
    rows = seq // GRID_W
    pos_row = jnp.repeat(jnp.arange(rows, dtype=F32), GRID_W)
    pos_col = jnp.tile(jnp.arange(GRID_W, dtype=F32), rows)
    inv_freq = ROPE_THETA ** (-jnp.arange(0, AXIS_DIM, 2, dtype=F32) / AXIS_DIM)
    ar = pos_row[:, None] * inv_freq
    ac = pos_col[:, None] * inv_freq
    cos_t = jnp.concatenate([jnp.cos(ar), jnp.cos(ar), jnp.cos(ac), jnp.cos(ac)], axis=1)
    sin_t = jnp.concatenate([-jnp.sin(ar), jnp.sin(ar), -jnp.sin(ac), jnp.sin(ac)], axis=1)
    return cos_t, sin_t


def _tile(n, want):
    t = min(n, want)
    assert n % t == 0, (n, want)
    return t


def kernel(x, c, ctx, c_ctx, w_mod, b_mod, norm1_g, w_in, q_norm_g, k_norm_g, w_dw, b_dw, conv_ln_g, conv_ln_b, w_attn_proj, w_conv_proj, w_out, norm2_g, w_router, router_bias, w_exp_gu, w_exp_dn, w_sh_gu, w_sh_dn, final_g):
    b, s, d = x.shape
    depth = w_mod.shape[0]
    assert depth == 1, "single-layer block"
    n = b * s
    row = lambda v: v.reshape(1, -1)

    cc = jnp.zeros((SUBLANES, d), F32).at[:b].set(c).at[b].set(c_ctx)
    mod = _modulation(cc, w_mod[0], row(b_mod[0]))
    mod_x = mod[:b].reshape(b, 1, 6, d)
    sh1, sc1, gt1, sh2, sc2, gt2 = [mod_x[:, :, j, :] for j in range(6)]
    mod_c = mod[b].reshape(6, d)
    csh1, csc1 = row(mod_c[0]), row(mod_c[1])

    w_in_b = w_in[0].astype(BF16)
    q_end, kv_end = ATTN_WIDTH, ATTN_WIDTH + 2 * KV_WIDTH
    k_end = q_end + KV_WIDTH
    w_vt = w_in_b[:, k_end:kv_end].T
    kc, vct = _ctx_kv(ctx, csh1, csc1, row(norm1_g[0]), w_in_b[:, q_end:k_end], w_vt, row(k_norm_g[0]))

    cos_t, sin_t = _rope_tables(s)
    q, kx, vxt, hglu, gates = _in_proj(x, sh1, sc1, row(norm1_g[0]), w_in_b, w_vt, row(q_norm_g[0]),
                                       row(k_norm_g[0]), cos_t, sin_t, _tile(s, 512))
    o = _attention(q, kc, vct, kx, vxt, _tile(s, 1024), _tile(s // 2, 1024))

    x1, hx, logits_t = _merge(o, hglu, gates, x, gt1, sh2, sc2, w_dw[0], row(b_dw[0]), row(conv_ln_g[0]),
                              row(conv_ln_b[0]), w_attn_proj[0].astype(BF16), w_conv_proj[0].astype(BF16),
                              w_out[0].astype(BF16), row(norm2_g[0]), w_router[0].T, _tile(s, 256))

    e_idx, wts, rank, counts = _route(logits_t, router_bias[0], _tile(n, 512))

    cnt = counts[:, 0]
    padded = (cnt + EXPERT_BLOCK - 1) // EXPERT_BLOCK * EXPERT_BLOCK
    pends = jnp.cumsum(padded)
    pstart = (pends - padded).astype(I32)
    n_blocks = (n * TOP_K + N_EXPERTS * (EXPERT_BLOCK - 1)) // EXPERT_BLOCK
    cap = n_blocks * EXPERT_BLOCK
    blk_row0 = jnp.arange(n_blocks, dtype=I32) * EXPERT_BLOCK
    blk_e = jnp.minimum(jnp.sum(pends[None, :] <= blk_row0[:, None], axis=1), N_EXPERTS - 1).astype(I32)
    blk_valid = jnp.clip(pstart[blk_e] + cnt[blk_e] - blk_row0, 0, EXPERT_BLOCK).astype(I32)

    pos = _slots(pstart, e_idx, rank, _tile(n, 4096))
    xs = _sc_dispatch(hx, pos, cap)
    ys = _experts(blk_e, blk_valid, xs, w_exp_gu[0], w_exp_dn[0])
    yg = _sc_gather(ys, pos)
    tm = _tile(n, 256)
    out = _combine(yg, wts.T, hx, x1.reshape(n, d), gt2, w_sh_gu[0].astype(BF16),
                   w_sh_dn[0].astype(BF16), row(final_g), tm, s // tm)
    return out.reshape(b, s, d)
```

```python
import functools
import math

import jax
import jax.numpy as jnp
from jax import lax
from jax.experimental import pallas as pl
from jax.experimental.pallas import tpu as pltpu
from jax.experimental.pallas import tpu_sc as plsc

F32 = jnp.float32
BF16 = jnp.bfloat16
U32 = jnp.uint32
I32 = jnp.int32

GRID_W = 64
N_HEADS = 8
N_KV_HEADS = 4
GROUP = N_HEADS // N_KV_HEADS
HEAD_DIM = 128
AXIS_DIM = HEAD_DIM // 2
ATTN_WIDTH = N_HEADS * HEAD_DIM
KV_WIDTH = N_KV_HEADS * HEAD_DIM
ROPE_THETA = 10000.0
ATTN_SCALE = HEAD_DIM ** -0.5
CONV_WIDTH = 512
CONV_KERNEL = 31
CONV_PAD = CONV_KERNEL // 2
N_EXPERTS = 256
TOP_K = 8
N_GROUPS = 8
TOPK_GROUPS = 4
EXPERTS_PER_GROUP = N_EXPERTS // N_GROUPS
EXPERT_HIDDEN = 256
SHARED_HIDDEN = 256
ROUTED_SCALE = 2.5
EPS = 1e-6
LOG2E = 1.4426950408889634

LANES = 128
SUBLANES = 8
VMEM_LIMIT_BYTES = 56 * 1024 * 1024

HALO_ROWS = 16
CONV_ROWS = 64
EXPERT_BLOCK = 512
SC_SUBCORES = 16
SC_WORKERS = 2 * SC_SUBCORES
SC_WINDOW = 128
COMBINE_CHUNKS = 4
HIGHEST = lax.Precision.HIGHEST


def _params(sem):
    return pltpu.CompilerParams(dimension_semantics=sem, vmem_limit_bytes=VMEM_LIMIT_BYTES)


def _sigmoid(x):
    return 1.0 / (1.0 + jnp.exp(-x))


def _pack_bf16_pair(lo, hi):
    lo_b = pltpu.bitcast(lo.astype(BF16).astype(F32), U32)
    hi_b = pltpu.bitcast(hi.astype(BF16).astype(F32), U32)
    return (lo_b >> 16) | (hi_b & jnp.uint32(0xFFFF0000))


def _unpack_bf16_pair(u):
    lo = pltpu.bitcast(u << 16, F32)
    hi = pltpu.bitcast(u & jnp.uint32(0xFFFF0000), F32)
    return lo, hi


def _mod_kernel(cc_ref, w_ref, b_ref, o_ref):
    cc = cc_ref[...]
    s = cc * _sigmoid(cc)
    o_ref[...] = jnp.dot(s, w_ref[...], precision=HIGHEST, preferred_element_type=F32) + b_ref[...]


def _modulation(cc, w_mod, b_mod):
    d, n = w_mod.shape
    tn = n // 4
    return pl.pallas_call(
        _mod_kernel,
        grid=(n // tn,),
        in_specs=[pl.BlockSpec((SUBLANES, d), lambda j: (0, 0)),
                  pl.BlockSpec((d, tn), lambda j: (0, j)),
                  pl.BlockSpec((1, tn), lambda j: (0, j))],
        out_specs=pl.BlockSpec((SUBLANES, tn), lambda j: (0, j)),
        out_shape=jax.ShapeDtypeStruct((SUBLANES, n), F32),
        compiler_params=_params(("arbitrary",)),
        name="mod",
    )(cc, w_mod, b_mod)


def _norm_modulate(x, g, sh, sc):
    ms = jnp.mean(x * x, axis=-1, keepdims=True)
    return (x * lax.rsqrt(ms + EPS) * g) * (1.0 + sc) + sh


def _head_norm(p, gain):
    r = lax.rsqrt(jnp.mean(p * p, axis=-1, keepdims=True) + EPS)
    return p * r * gain


def _dot_nt(a, b):
    return lax.dot_general(a, b, (((1,), (1,)), ((), ())), preferred_element_type=F32)


def _ctx_kv_kernel(x_ref, sh_ref, sc_ref, g1_ref, wk_ref, wvt_ref, gk_ref, k_ref, vt_ref):
    h = _norm_modulate(x_ref[0], g1_ref[...], sh_ref[...], sc_ref[...]).astype(BF16)
    pk = jnp.dot(h, wk_ref[...], preferred_element_type=F32)
    for j in range(N_KV_HEADS):
        sl = slice(j * HEAD_DIM, (j + 1) * HEAD_DIM)
        k_ref[0, :, sl] = _head_norm(pk[:, sl], gk_ref[...]).astype(BF16)
    vt_ref[0] = _dot_nt(wvt_ref[...], h).astype(BF16)


def _ctx_kv(ctx, csh, csc, g1, w_k, w_vt, gk):
    b, lc, d = ctx.shape
    vec = lambda: pl.BlockSpec((1, d), lambda i: (0, 0))
    return pl.pallas_call(
        _ctx_kv_kernel,
        grid=(b,),
        in_specs=[pl.BlockSpec((1, lc, d), lambda i: (i, 0, 0)), vec(), vec(), vec(),
                  pl.BlockSpec((d, KV_WIDTH), lambda i: (0, 0)),
                  pl.BlockSpec((KV_WIDTH, d), lambda i: (0, 0)),
                  pl.BlockSpec((1, HEAD_DIM), lambda i: (0, 0))],
        out_specs=[pl.BlockSpec((1, lc, KV_WIDTH), lambda i: (i, 0, 0)),
                   pl.BlockSpec((1, KV_WIDTH, lc), lambda i: (i, 0, 0))],
        out_shape=[jax.ShapeDtypeStruct((b, lc, KV_WIDTH), BF16),
                   jax.ShapeDtypeStruct((b, KV_WIDTH, lc), BF16)],
        compiler_params=_params(("arbitrary",)),
        name="ctx_kv",
    )(ctx, csh, csc, g1, w_k, w_vt, gk)


def _in_proj_kernel(x_ref, sh_ref, sc_ref, g1_ref, w_ref, wvt_ref, gq_ref, gk_ref, cos_ref, sin_ref,
                    q_ref, k_ref, vt_ref, h_ref, g_ref):
    h = _norm_modulate(x_ref[0], g1_ref[...], sh_ref[0], sc_ref[0]).astype(BF16)
    cos = cos_ref[...]
    sin = sin_ref[...]
    lane = lax.broadcasted_iota(I32, cos.shape, 1)
    upper = (lane & (AXIS_DIM // 2)) != 0

    def rope(p):
        swapped = jnp.where(upper, pltpu.roll(p, AXIS_DIM // 2, 1),
                            pltpu.roll(p, HEAD_DIM - AXIS_DIM // 2, 1))
        return p * cos + swapped * sin

    q_end = ATTN_WIDTH
    k_end = q_end + KV_WIDTH
    v_end = k_end + KV_WIDTH
    u_end = v_end + 2 * CONV_WIDTH
    pq = jnp.dot(h, w_ref[:, :q_end], preferred_element_type=F32)
    for j in range(N_HEADS):
        sl = slice(j * HEAD_DIM, (j + 1) * HEAD_DIM)
        q_ref[0, :, sl] = (rope(_head_norm(pq[:, sl], gq_ref[...])) * (ATTN_SCALE * LOG2E)).astype(BF16)
    pk = jnp.dot(h, w_ref[:, q_end:k_end], preferred_element_type=F32)
    for j in range(N_KV_HEADS):
        sl = slice(j * HEAD_DIM, (j + 1) * HEAD_DIM)
        k_ref[0, :, sl] = rope(_head_norm(pk[:, sl], gk_ref[...])).astype(BF16)
    vt_ref[0] = _dot_nt(wvt_ref[...], h).astype(BF16)
    u = jnp.dot(h, w_ref[:, v_end:u_end], preferred_element_type=F32)
    h_ref[0] = (u[:, :CONV_WIDTH] * _sigmoid(u[:, CONV_WIDTH:])).astype(BF16)
    g_ref[0] = _sigmoid(jnp.dot(h, w_ref[:, u_end:], preferred_element_type=F32)).astype(BF16)


def _in_proj(x, sh1, sc1, g1, w_in, w_vt, gq, gk, cos_t, sin_t, tm):
    b, s, d = x.shape
    n_in = w_in.shape[1]
    bvec = lambda: pl.BlockSpec((1, 1, d), lambda bi, i: (bi, 0, 0))
    tok = lambda w: pl.BlockSpec((1, tm, w), lambda bi, i: (bi, i, 0))
    return pl.pallas_call(
        _in_proj_kernel,
        grid=(b, s // tm),
        in_specs=[tok(d), bvec(), bvec(),
                  pl.BlockSpec((1, d), lambda bi, i: (0, 0)),
                  pl.BlockSpec((d, n_in), lambda bi, i: (0, 0)),
                  pl.BlockSpec((KV_WIDTH, d), lambda bi, i: (0, 0)),
                  pl.BlockSpec((1, HEAD_DIM), lambda bi, i: (0, 0)),
                  pl.BlockSpec((1, HEAD_DIM), lambda bi, i: (0, 0)),
                  pl.BlockSpec((tm, HEAD_DIM), lambda bi, i: (i, 0)),
                  pl.BlockSpec((tm, HEAD_DIM), lambda bi, i: (i, 0))],
        out_specs=[tok(ATTN_WIDTH), tok(KV_WIDTH),
                   pl.BlockSpec((1, KV_WIDTH, tm), lambda bi, i: (bi, 0, i)),
                   tok(CONV_WIDTH), tok(2 * d)],
        out_shape=[jax.ShapeDtypeStruct((b, s, ATTN_WIDTH), BF16),
                   jax.ShapeDtypeStruct((b, s, KV_WIDTH), BF16),
                   jax.ShapeDtypeStruct((b, KV_WIDTH, s), BF16),
                   jax.ShapeDtypeStruct((b, s, CONV_WIDTH), BF16),
                   jax.ShapeDtypeStruct((b, s, 2 * d), BF16)],
        compiler_params=_params(("arbitrary", "arbitrary")),
        name="in_proj",
    )(x, sh1, sc1, g1, w_in, w_vt, gq, gk, cos_t, sin_t)


def _sub_allreduce(x, op):
    for s in (4, 2, 1):
        x = op(x, pltpu.roll(x, s, 0))
    return x


def _attn_kernel(q_ref, kc_ref, vct_ref, k_ref, vt_ref, o_ref, s0, s1, x0, x1, m_sc, l_sc, acc_sc, *, tk):
    tq = q_ref.shape[1]
    m_cols = GROUP * tq
    nk = k_ref.shape[1] // tk
    qf = q_ref[0].astype(F32).T
    qt = jnp.concatenate([qf[:HEAD_DIM], qf[HEAD_DIM:]], axis=1).astype(BF16)

    def scores(k):
        st = jnp.dot(k, qt, preferred_element_type=F32)
        return st, jnp.max(st.reshape(st.shape[0] // SUBLANES, SUBLANES, m_cols), axis=0)

    def absorb(st, mx, vt):
        n = st.shape[0]
        s3 = st.reshape(n // SUBLANES, SUBLANES, m_cols)
        m_prev = m_sc[...]
        m_new = jnp.maximum(m_prev, _sub_allreduce(mx, jnp.maximum))
        alpha = jnp.exp2(m_prev - m_new)
        p3 = jnp.exp2(s3 - m_new[None])
        l_sc[...] = alpha * l_sc[...] + _sub_allreduce(jnp.sum(p3, axis=0), jnp.add)
        pv = jnp.dot(vt, p3.reshape(n, m_cols).astype(BF16), preferred_element_type=F32)
        acc_sc[...] = alpha[0:1] * acc_sc[...] + pv
        m_sc[...] = m_new

    def kchunk(j):
        return k_ref[0, pl.ds(pl.multiple_of(j * tk, tk), tk), :]

    def vchunk(j):
        return vt_ref[0, :, pl.ds(pl.multiple_of(j * tk, tk), tk)]

    m_sc[...] = jnp.full(m_sc.shape, -jnp.inf, F32)
    l_sc[...] = jnp.zeros(l_sc.shape, F32)
    acc_sc[...] = jnp.zeros(acc_sc.shape, F32)
    s0[...], x0[...] = scores(kchunk(0))

    def body(i, carry):
        j = 2 * i
        s1[...], x1[...] = scores(kchunk(j + 1))
        absorb(s0[...], x0[...], vchunk(j))
        s0[...], x0[...] = scores(kchunk(j + 2))
        absorb(s1[...], x1[...], vchunk(j + 1))
        return carry

    lax.fori_loop(0, nk // 2 - 1, body, 0)
    s1[...], x1[...] = scores(kchunk(nk - 1))
    absorb(s0[...], x0[...], vchunk(nk - 2))
    sc, xc = scores(kc_ref[0])
    absorb(s1[...], x1[...], vchunk(nk - 1))
    absorb(sc, xc, vct_ref[0])

    o = (acc_sc[...] / l_sc[0:1]).T
    o_ref[0, :, :HEAD_DIM] = o[:tq].astype(BF16)
    o_ref[0, :, HEAD_DIM:] = o[tq:].astype(BF16)


def _attention(q, kc, vct, kx, vxt, tq, tk):
    b, s, _ = q.shape
    lc = kc.shape[1]
    assert s % (2 * tk) == 0
    gw = GROUP * HEAD_DIM
    m_cols = GROUP * tq
    kv = lambda l: pl.BlockSpec((1, l, HEAD_DIM), lambda bi, h, i: (bi, 0, h))
    kvt = lambda l: pl.BlockSpec((1, HEAD_DIM, l), lambda bi, h, i: (bi, h, 0))
    return pl.pallas_call(
        functools.partial(_attn_kernel, tk=tk),
        grid=(b, N_KV_HEADS, s // tq),
        in_specs=[pl.BlockSpec((1, tq, gw), lambda bi, h, i: (bi, i, h)), kv(lc), kvt(lc), kv(s), kvt(s)],
        out_specs=pl.BlockSpec((1, tq, gw), lambda bi, h, i: (bi, i, h)),
        out_shape=jax.ShapeDtypeStruct((b, s, ATTN_WIDTH), BF16),
        scratch_shapes=[pltpu.VMEM((tk, m_cols), F32), pltpu.VMEM((tk, m_cols), F32),
                        pltpu.VMEM((SUBLANES, m_cols), F32), pltpu.VMEM((SUBLANES, m_cols), F32),
                        pltpu.VMEM((SUBLANES, m_cols), F32), pltpu.VMEM((SUBLANES, m_cols), F32),
                        pltpu.VMEM((HEAD_DIM, m_cols), F32)],
        compiler_params=_params(("arbitrary", "arbitrary", "arbitrary")),
        name="attn",
    )(q, kc, vct, kx, vxt)


def _merge_kernel(o_ref, hp_ref, hc_ref, hn_ref, g_ref, x_ref, gt1_ref, sh2_ref, sc2_ref,
                  wdw_ref, bdw_ref, lng_ref, lnb_ref, wap_ref, wcp_ref, wout_ref, g2_ref, wrt_ref,
                  x1_ref, hx_ref, lg_ref, hcat, act_sc, shift_sc):
    i = pl.program_id(1)
    n_i = pl.num_programs(1)
    tm = hc_ref.shape[1]
    d = x_ref.shape[2]
    prev = hp_ref[0].astype(F32)
    nxt = hn_ref[0].astype(F32)
    hcat[0:HALO_ROWS, :] = jnp.where(i > 0, prev, jnp.zeros_like(prev))
    hcat[HALO_ROWS:HALO_ROWS + tm, :] = hc_ref[0].astype(F32)
    hcat[HALO_ROWS + tm:, :] = jnp.where(i < n_i - 1, nxt, jnp.zeros_like(nxt))
    base = HALO_ROWS - CONV_PAD
    reach = (base + CONV_KERNEL - 1) // SUBLANES * SUBLANES
    for r0 in range(0, tm, CONV_ROWS):
        acc = jnp.zeros((CONV_ROWS, CONV_WIDTH), F32) + bdw_ref[...]
        for res in range(SUBLANES):
            shift_sc[res] = hcat[r0 + res:r0 + res + CONV_ROWS + reach, :]
            for off in range(res, base + CONV_KERNEL, SUBLANES):
                j = off - base
                if 0 <= j < CONV_KERNEL:
                    a0 = off - res
                    acc = acc + shift_sc[res, a0:a0 + CONV_ROWS, :] * wdw_ref[j:j + 1, :]
        mu = jnp.mean(acc, axis=-1, keepdims=True)
        cen = acc - mu
        var = jnp.mean(cen * cen, axis=-1, keepdims=True)
        ln = cen * lax.rsqrt(var + EPS) * lng_ref[...] + lnb_ref[...]
        act_sc[r0:r0 + CONV_ROWS, :] = (ln * _sigmoid(ln)).astype(BF16)
    y_conv = jnp.dot(act_sc[...], wcp_ref[...], preferred_element_type=F32)
    y_attn = jnp.dot(o_ref[0], wap_ref[...], preferred_element_type=F32)
    z = g_ref[0, :, :d].astype(F32) * y_attn + g_ref[0, :, d:].astype(F32) * y_conv
    mix = jnp.dot(z.astype(BF16), wout_ref[...], preferred_element_type=F32)
    x1 = x_ref[0] + gt1_ref[0] * mix
    x1_ref[0] = x1
    hx = _norm_modulate(x1, g2_ref[...], sh2_ref[0], sc2_ref[0])
    half = d // 2
    hx_ref[...] = pltpu.bitcast(_pack_bf16_pair(hx[:, :half], hx[:, half:]), I32)
    lg_ref[...] = lax.dot_general(wrt_ref[...], hx, (((1,), (1,)), ((), ())),
                                  precision=HIGHEST, preferred_element_type=F32)


def _merge(o, hglu, g, x, gt1, sh2, sc2, w_dw, b_dw, ln_g, ln_b, w_ap, w_cp, w_out, g2, w_rt, tm):
    b, s, d = x.shape
    nt = s // tm
    hb = tm // HALO_ROWS
    n_halo = s // HALO_ROWS
    bvec = lambda: pl.BlockSpec((1, 1, d), lambda bi, i: (bi, 0, 0))
    full = lambda a: pl.BlockSpec(a.shape, lambda bi, i: (0,) * a.ndim)
    tok = lambda w: pl.BlockSpec((1, tm, w), lambda bi, i: (bi, i, 0))
    return pl.pallas_call(
        _merge_kernel,
        grid=(b, nt),
        in_specs=[tok(ATTN_WIDTH),
                  pl.BlockSpec((1, HALO_ROWS, CONV_WIDTH), lambda bi, i: (bi, jnp.maximum(i * hb - 1, 0), 0)),
                  tok(CONV_WIDTH),
                  pl.BlockSpec((1, HALO_ROWS, CONV_WIDTH),
                               lambda bi, i: (bi, jnp.minimum((i + 1) * hb, n_halo - 1), 0)),
                  tok(2 * d), tok(d), bvec(), bvec(), bvec(),
                  full(w_dw), full(b_dw), full(ln_g), full(ln_b), full(w_ap), full(w_cp), full(w_out),
                  full(g2), full(w_rt)],
        out_specs=[tok(d),
                   pl.BlockSpec((tm, d // 2), lambda bi, i: (bi * nt + i, 0)),
                   pl.BlockSpec((N_EXPERTS, tm), lambda bi, i: (0, bi * nt + i))],
        out_shape=[jax.ShapeDtypeStruct((b, s, d), F32),
                   jax.ShapeDtypeStruct((b * s, d // 2), I32),
                   jax.ShapeDtypeStruct((N_EXPERTS, b * s), F32)],
        scratch_shapes=[pltpu.VMEM((tm + 2 * HALO_ROWS, CONV_WIDTH), F32), pltpu.VMEM((tm, CONV_WIDTH), BF16),
                        pltpu.VMEM((SUBLANES,
                                    CONV_ROWS + (CONV_KERNEL + HALO_ROWS - CONV_PAD - 1) // SUBLANES * SUBLANES,
                                    CONV_WIDTH), F32)],
        compiler_params=_params(("arbitrary", "arbitrary")),
        name="merge",
    )(o, hglu, hglu, hglu, g, x, gt1, sh2, sc2, w_dw, b_dw, ln_g, ln_b, w_ap, w_cp, w_out, g2, w_rt)


def _route_kernel(lg_ref, bias_ref, tri_ref, e_ref, w_ref, r_ref, cnt_ref, run_sc):
    step = pl.program_id(0)
    n_strips = lg_ref.shape[1] // LANES
    nv = N_EXPERTS // SUBLANES
    gv = EXPERTS_PER_GROUP // SUBLANES

    @pl.when(step == 0)
    def _():
        run_sc[...] = jnp.zeros(run_sc.shape, F32)

    row = (lax.broadcasted_iota(I32, (nv, SUBLANES, LANES), 0) * SUBLANES
           + lax.broadcasted_iota(I32, (nv, SUBLANES, LANES), 1))
    sub = lax.broadcasted_iota(I32, (SUBLANES, LANES), 0)
    bias = bias_ref[...].reshape(nv, SUBLANES, LANES)
    neg_inf = jnp.float32(-jnp.inf)

    for st in range(n_strips):
        lanes = slice(st * LANES, (st + 1) * LANES)
        scores = _sigmoid(lg_ref[:, lanes]).reshape(nv, SUBLANES, LANES)
        biased = scores + bias
        gscore = []
        for g in range(N_GROUPS):
            m1 = biased[g * gv]
            m2 = jnp.full((SUBLANES, LANES), neg_inf, F32)
            for t in range(1, gv):
                v = biased[g * gv + t]
                m2 = jnp.maximum(m2, jnp.minimum(m1, v))
                m1 = jnp.maximum(m1, v)
            for s in (4, 2, 1):
                p1 = pltpu.roll(m1, s, 0)
                p2 = pltpu.roll(m2, s, 0)
                m2 = jnp.maximum(jnp.minimum(m1, p1), jnp.maximum(m2, p2))
                m1 = jnp.maximum(m1, p1)
            gscore.append(m1 + m2)
        masked = []
        for g in range(N_GROUPS):
            beaten = jnp.zeros((SUBLANES, LANES), I32)
            for o in range(N_GROUPS):
                if o == g:
                    continue
                wins = (gscore[o] > gscore[g]) | ((gscore[o] == gscore[g]) & (o < g))
                beaten = beaten + wins.astype(I32)
            keep = beaten < TOPK_GROUPS
            for t in range(gv):
                masked.append(jnp.where(keep, biased[g * gv + t], neg_inf))
        cand = jnp.stack(masked, axis=0)
        sel = jnp.zeros((nv, SUBLANES, LANES), jnp.bool_)
        picks, pick_scores = [], []
        for _ in range(TOP_K):
            mx = _sub_allreduce(jnp.max(cand, axis=0), jnp.maximum)
            idx = _sub_allreduce(jnp.min(jnp.where(cand == mx, row, N_EXPERTS), axis=0), jnp.minimum)
            hit = row == idx
            pick_scores.append(_sub_allreduce(jnp.sum(jnp.where(hit, scores, 0.0), axis=0), jnp.add))
            picks.append(idx)
            sel = sel | hit
            cand = jnp.where(hit, neg_inf, cand)
        sel_b = sel.astype(F32).astype(BF16).reshape(N_EXPERTS, LANES)
        before = jnp.dot(sel_b, tri_ref[0], preferred_element_type=F32)
        total = jnp.dot(sel_b, tri_ref[1], preferred_element_type=F32)
        rank_all = (before + run_sc[...]).reshape(nv, SUBLANES, LANES)
        run_sc[...] = run_sc[...] + total
        denom = pick_scores[0]
        for kk in range(1, TOP_K):
            denom = denom + pick_scores[kk]
        e_out = jnp.zeros((SUBLANES, LANES), I32)
        w_out = jnp.zeros((SUBLANES, LANES), F32)
        r_out = jnp.zeros((SUBLANES, LANES), I32)
        for kk in range(TOP_K):
            rk = _sub_allreduce(jnp.sum(jnp.where(row == picks[kk], rank_all, 0.0), axis=0), jnp.add)
            e_out = jnp.where(sub == kk, picks[kk], e_out)
            w_out = jnp.where(sub == kk, pick_scores[kk] / denom * ROUTED_SCALE, w_out)
            r_out = jnp.where(sub == kk, rk.astype(I32), r_out)
        e_ref[:, lanes] = e_out
        w_ref[:, lanes] = w_out
        r_ref[:, lanes] = r_out

    cnt_ref[...] = run_sc[...].astype(I32)


def _route(logits_t, bias, tb):
    n = logits_t.shape[1]
    iota_r = lax.broadcasted_iota(I32, (LANES, LANES), 0)
    iota_c = lax.broadcasted_iota(I32, (LANES, LANES), 1)
    tri = jnp.stack([(iota_r < iota_c), jnp.ones((LANES, LANES), jnp.bool_)]).astype(BF16)
    bias_b = jnp.broadcast_to(bias.reshape(N_EXPERTS, 1), (N_EXPERTS, LANES)).astype(F32)
    tokrow = lambda dt: jax.ShapeDtypeStruct((TOP_K, n), dt)
    return pl.pallas_call(
        _route_kernel,
        grid=(n // tb,),
        in_specs=[pl.BlockSpec((N_EXPERTS, tb), lambda i: (0, i)),
                  pl.BlockSpec((N_EXPERTS, LANES), lambda i: (0, 0)),
                  pl.BlockSpec((2, LANES, LANES), lambda i: (0, 0, 0))],
        out_specs=[pl.BlockSpec((TOP_K, tb), lambda i: (0, i)),
                   pl.BlockSpec((TOP_K, tb), lambda i: (0, i)),
                   pl.BlockSpec((TOP_K, tb), lambda i: (0, i)),
                   pl.BlockSpec((N_EXPERTS, LANES), lambda i: (0, 0))],
        out_shape=[tokrow(I32), tokrow(F32), tokrow(I32),
                   jax.ShapeDtypeStruct((N_EXPERTS, LANES), I32)],
        scratch_shapes=[pltpu.VMEM((N_EXPERTS, LANES), F32)],
        compiler_params=_params(("arbitrary",)),
        name="route",
    )(logits_t, bias_b, tri)


def _slots_kernel(pstart_ref, e_ref, r_ref, o_ref):
    e = e_ref[...]

    def body(x, acc):
        return acc + jnp.where(e == x, pstart_ref[x], 0)

    o_ref[...] = lax.fori_loop(0, N_EXPERTS, body, r_ref[...])


def _slots(pstart, e_idx, rank, tb):
    n = e_idx.shape[1]
    blk = lambda: pl.BlockSpec((TOP_K, tb), lambda i, ps: (0, i))
    return pl.pallas_call(
        _slots_kernel,
        grid_spec=pltpu.PrefetchScalarGridSpec(
            num_scalar_prefetch=1, grid=(n // tb,), in_specs=[blk(), blk()], out_specs=blk()),
        out_shape=jax.ShapeDtypeStruct((TOP_K, n), I32),
        compiler_params=_params(("arbitrary",)),
        name="slots",
    )(pstart, e_idx, rank)


def _sc_mesh():
    return plsc.VectorSubcoreMesh(core_axis_name="c", subcore_axis_name="s")


def _sc_worker():
    return lax.axis_index("c") * SC_SUBCORES + lax.axis_index("s")


def _sc_dispatch(hx, pos, cap):
    n, w = hx.shape
    per_worker = n // SC_WINDOW // SC_WORKERS
    assert per_worker * SC_WINDOW * SC_WORKERS == n

    @pl.kernel(out_type=jax.ShapeDtypeStruct((cap, w), hx.dtype), mesh=_sc_mesh(),
               scratch_types=[pltpu.VMEM((SC_WINDOW, w), hx.dtype), pltpu.VMEM((TOP_K, SC_WINDOW), I32),
                              pltpu.SemaphoreType.DMA])
    def scatter_rows(x_hbm, i_hbm, o_hbm, xbuf, ibuf, sem):
        wid = _sc_worker()

        @pl.loop(0, per_worker)
        def _(j):
            row0 = (wid * per_worker + j) * SC_WINDOW
            pltpu.sync_copy(x_hbm.at[pl.ds(row0, SC_WINDOW)], xbuf)
            pltpu.sync_copy(i_hbm.at[:, pl.ds(row0, SC_WINDOW)], ibuf)
            copies = [pltpu.async_copy(xbuf, o_hbm.at[ibuf.at[kk]], sem) for kk in range(TOP_K)]
            for cp in copies:
                cp.wait()

    return scatter_rows(hx, pos)


def _sc_gather(ys, pos):
    n = pos.shape[1]
    w = ys.shape[1]
    per_worker = n // SC_WINDOW // SC_WORKERS
    assert per_worker * SC_WINDOW * SC_WORKERS == n

    @pl.kernel(out_type=jax.ShapeDtypeStruct((TOP_K, n, w), ys.dtype), mesh=_sc_mesh(),
               scratch_types=[pltpu.VMEM((SC_WINDOW, w), ys.dtype), pltpu.VMEM((TOP_K, SC_WINDOW), I32)])
    def gather_rows(y_hbm, i_hbm, o_hbm, ybuf, ibuf):
        wid = _sc_worker()

        @pl.loop(0, per_worker)
        def _(j):
            row0 = (wid * per_worker + j) * SC_WINDOW
            pltpu.sync_copy(i_hbm.at[:, pl.ds(row0, SC_WINDOW)], ibuf)
            for kk in range(TOP_K):
                pltpu.sync_copy(y_hbm.at[ibuf.at[kk]], ybuf)
                pltpu.sync_copy(ybuf, o_hbm.at[kk, pl.ds(row0, SC_WINDOW)])

    return gather_rows(ys, pos)


def _experts_kernel(blk_e_ref, valid_ref, blk_in_ref, blk_out_ref, xs_ref, wgu_ref, wdn_ref, ys_ref,
                    wgu_sc, wdn_sc):
    del blk_in_ref, blk_out_ref
    i = pl.program_id(0)
    valid = valid_ref[i]

    @pl.when(valid > 0)
    def _():
        prev = blk_e_ref[jnp.maximum(i - 1, 0)]

        @pl.when((i == 0) | (blk_e_ref[i] != prev))
        def _():
            wgu_sc[...] = wgu_ref[0].astype(BF16)
            wdn_sc[...] = wdn_ref[0].astype(BF16)

        rows = lax.broadcasted_iota(I32, xs_ref.shape, 0)
        xu = jnp.where(rows < valid, pltpu.bitcast(xs_ref[...], U32), jnp.uint32(0))
        lo, hi = _unpack_bf16_pair(xu)
        half = lo.shape[1]
        gu = (jnp.dot(lo.astype(BF16), wgu_sc[:half, :], preferred_element_type=F32)
              + jnp.dot(hi.astype(BF16), wgu_sc[half:, :], preferred_element_type=F32))
        gt = gu[:, :EXPERT_HIDDEN]
        act = (gt * _sigmoid(gt) * gu[:, EXPERT_HIDDEN:]).astype(BF16)
        y = jnp.dot(act, wdn_sc[...], preferred_element_type=F32)
        ys_ref[...] = pltpu.bitcast(_pack_bf16_pair(y[:, :half], y[:, half:]), I32)

    @pl.when(valid <= 0)
    def _():
        ys_ref[...] = jnp.zeros(ys_ref.shape, I32)


def _experts(blk_e, blk_valid, xs, w_gu, w_dn):
    cap, w = xs.shape
    n_e, d, h2 = w_gu.shape
    n_blocks = cap // EXPERT_BLOCK
    step = jnp.arange(n_blocks, dtype=I32)
    n_used = jnp.sum((blk_valid > 0).astype(I32))
    blk_in = jnp.minimum(step, jnp.maximum(n_used - 1, 0)).astype(I32)
    blk_out = jnp.where(blk_valid > 0, step, n_blocks).astype(I32)
    return pl.pallas_call(
        _experts_kernel,
        grid_spec=pltpu.PrefetchScalarGridSpec(
            num_scalar_prefetch=4,
            grid=(n_blocks,),
            in_specs=[pl.BlockSpec((EXPERT_BLOCK, w), lambda i, be, bv, bi, bo: (bi[i], 0)),
                      pl.BlockSpec((1, d, h2), lambda i, be, bv, bi, bo: (be[i], 0, 0)),
                      pl.BlockSpec((1, h2 // 2, d), lambda i, be, bv, bi, bo: (be[i], 0, 0))],
            out_specs=pl.BlockSpec((EXPERT_BLOCK, w), lambda i, be, bv, bi, bo: (bo[i], 0)),
            scratch_shapes=[pltpu.VMEM((d, h2), BF16), pltpu.VMEM((h2 // 2, d), BF16)]),
        out_shape=jax.ShapeDtypeStruct((cap + EXPERT_BLOCK, w), I32),
        compiler_params=_params(("arbitrary",)),
        name="experts",
    )(blk_e, blk_valid, blk_in, blk_out, xs, w_gu, w_dn)


def _combine_kernel(yg_ref, wt_ref, hx_ref, x1_ref, gt2_ref, wsg_ref, wsd_ref, fg_ref, *rest):
    o_ref = rest[-1]
    lo, hi = _unpack_bf16_pair(pltpu.bitcast(hx_ref[...], U32))
    half = lo.shape[1]
    gu = (jnp.dot(lo.astype(BF16), wsg_ref[:half, :], preferred_element_type=F32)
          + jnp.dot(hi.astype(BF16), wsg_ref[half:, :], preferred_element_type=F32))
    gt = gu[:, :SHARED_HIDDEN]
    act = (gt * _sigmoid(gt) * gu[:, SHARED_HIDDEN:]).astype(BF16)
    y = jnp.dot(act, wsd_ref[...], preferred_element_type=F32)
    y_lo = y[:, :half]
    y_hi = y[:, half:]
    for kk in range(TOP_K):
        r_lo, r_hi = _unpack_bf16_pair(pltpu.bitcast(yg_ref[kk], U32))
        wk = wt_ref[:, kk:kk + 1]
        y_lo = y_lo + wk * r_lo
        y_hi = y_hi + wk * r_hi
    x2_lo = x1_ref[:, :half] + gt2_ref[0, :, :half] * y_lo
    x2_hi = x1_ref[:, half:] + gt2_ref[0, :, half:] * y_hi
    ms = (jnp.sum(x2_lo * x2_lo, axis=-1, keepdims=True)
          + jnp.sum(x2_hi * x2_hi, axis=-1, keepdims=True)) / (2 * half)
    inv = lax.rsqrt(ms + EPS)
    o_ref[:, :half] = x2_lo * inv * fg_ref[:, :half]
    o_ref[:, half:] = x2_hi * inv * fg_ref[:, half:]


def _combine(yg, wt, hx, x1, gt2, w_sg, w_sd, fg, tm, tiles_per_batch, tile0, out_prev):
    n, w = hx.shape
    d = 2 * w
    full = lambda a: pl.BlockSpec(a.shape, lambda i: (0,) * a.ndim)
    in_specs = [pl.BlockSpec((TOP_K, tm, w), lambda i: (0, i, 0)),
                pl.BlockSpec((tm, TOP_K), lambda i: (tile0 + i, 0)),
                pl.BlockSpec((tm, w), lambda i: (tile0 + i, 0)),
                pl.BlockSpec((tm, d), lambda i: (tile0 + i, 0)),
                pl.BlockSpec((1, 1, d), lambda i: ((tile0 + i) // tiles_per_batch, 0, 0)),
                full(w_sg), full(w_sd), full(fg)]
    args = [yg, wt, hx, x1, gt2, w_sg, w_sd, fg]
    aliases = {}
    if out_prev is not None:
        in_specs.append(pl.BlockSpec(memory_space=pl.ANY))
        args.append(out_prev)
        aliases = {len(args) - 1: 0}
    return pl.pallas_call(
        _combine_kernel,
        grid=(yg.shape[1] // tm,),
        in_specs=in_specs,
        out_specs=pl.BlockSpec((tm, d), lambda i: (tile0 + i, 0)),
        out_shape=jax.ShapeDtypeStruct((n, d), F32),
        input_output_aliases=aliases,
        compiler_params=_params(("arbitrary",)),
        name="combine",
    )(*args)


def _rope_tables(seq):
    rows = seq // GRID_W
    pos_row = jnp.repeat(jnp.arange(rows, dtype=F32), GRID_W)
    pos_col = jnp.tile(jnp.arange(GRID_W, dtype=F32), rows)
    inv_freq = ROPE_THETA ** (-jnp.arange(0, AXIS_DIM, 2, dtype=F32) / AXIS_DIM)
    ar = pos_row[:, None] * inv_freq
    ac = pos_col[:, None] * inv_freq
    cos_t = jnp.concatenate([jnp.cos(ar), jnp.cos(ar), jnp.cos(ac), jnp.cos(ac)], axis=1)
    sin_t = jnp.concatenate([-jnp.sin(ar), jnp.sin(ar), -jnp.sin(ac), jnp.sin(ac)], axis=1)
    return cos_t, sin_t


def _tile(n, want):
    t = min(n, want)
    assert n % t == 0, (n, want)
    return t


def kernel(x, c, ctx, c_ctx, w_mod, b_mod, norm1_g, w_in, q_norm_g, k_norm_g, w_dw, b_dw, conv_ln_g, conv_ln_b, w_attn_proj, w_conv_proj, w_out, norm2_g, w_router, router_bias, w_exp_gu, w_exp_dn, w_sh_gu, w_sh_dn, final_g):
    b, s, d = x.shape
    depth = w_mod.shape[0]
    assert depth == 1, "single-layer block"
    n = b * s
    row = lambda v: v.reshape(1, -1)

    cc = jnp.zeros((SUBLANES, d), F32).at[:b].set(c).at[b].set(c_ctx)
    mod = _modulation(cc, w_mod[0], row(b_mod[0]))
    mod_x = mod[:b].reshape(b, 1, 6, d)
    sh1, sc1, gt1, sh2, sc2, gt2 = [mod_x[:, :, j, :] for j in range(6)]
    mod_c = mod[b].reshape(6, d)
    csh1, csc1 = row(mod_c[0]), row(mod_c[1])

    w_in_b = w_in[0].astype(BF16)
    q_end, kv_end = ATTN_WIDTH, ATTN_WIDTH + 2 * KV_WIDTH
    k_end = q_end + KV_WIDTH
    w_vt = w_in_b[:, k_end:kv_end].T
    kc, vct = _ctx_kv(ctx, csh1, csc1, row(norm1_g[0]), w_in_b[:, q_end:k_end], w_vt, row(k_norm_g[0]))

    cos_t, sin_t = _rope_tables(s)
    q, kx, vxt, hglu, gates = _in_proj(x, sh1, sc1, row(norm1_g[0]), w_in_b, w_vt, row(q_norm_g[0]),
                                       row(k_norm_g[0]), cos_t, sin_t, _tile(s, 512))
    o = _attention(q, kc, vct, kx, vxt, _tile(s, 1024), _tile(s // 2, 1024))

    x1, hx, logits_t = _merge(o, hglu, gates, x, gt1, sh2, sc2, w_dw[0], row(b_dw[0]), row(conv_ln_g[0]),
                              row(conv_ln_b[0]), w_attn_proj[0].astype(BF16), w_conv_proj[0].astype(BF16),
                              w_out[0].astype(BF16), row(norm2_g[0]), w_router[0].T, _tile(s, 256))

    e_idx, wts, rank, counts = _route(logits_t, router_bias[0], _tile(n, 512))

    cnt = counts[:, 0]
    padded = (cnt + EXPERT_BLOCK - 1) // EXPERT_BLOCK * EXPERT_BLOCK
    pends = jnp.cumsum(padded)
    pstart = (pends - padded).astype(I32)
    n_blocks = (n * TOP_K + N_EXPERTS * (EXPERT_BLOCK - 1)) // EXPERT_BLOCK
    cap = n_blocks * EXPERT_BLOCK
    blk_row0 = jnp.arange(n_blocks, dtype=I32) * EXPERT_BLOCK
    blk_e = jnp.minimum(jnp.sum(pends[None, :] <= blk_row0[:, None], axis=1), N_EXPERTS - 1).astype(I32)
    blk_valid = jnp.clip(pstart[blk_e] + cnt[blk_e] - blk_row0, 0, EXPERT_BLOCK).astype(I32)

    pos = _slots(pstart, e_idx, rank, _tile(n, 4096))
    xs = _sc_dispatch(hx, pos, cap)
    ys = _experts(blk_e, blk_valid, xs, w_exp_gu[0], w_exp_dn[0])
    tm = _tile(n, 256)
    chunk = n // COMBINE_CHUNKS if n % (COMBINE_CHUNKS * SC_WINDOW * SC_WORKERS) == 0 else n
    wt_t, x1_2d = wts.T, x1.reshape(n, d)
    w_sg, w_sd = w_sh_gu[0].astype(BF16), w_sh_dn[0].astype(BF16)
    out = None
    for c0 in range(0, n, chunk):
        yg = _sc_gather(ys, pos[:, c0:c0 + chunk])
        out = _combine(yg, wt_t, hx, x1_2d, gt2, w_sg, w_sd, row(final_g), tm, s // tm, c0 // tm, out)
    return out.reshape(b, s, d)
```

```python
import functools
import math

import jax
import jax.numpy as jnp
import numpy as np
from jax import lax
from jax.experimental import pallas as pl
from jax.experimental.pallas import tpu as pltpu
from jax.experimental.pallas import tpu_sc as plsc

F32 = jnp.float32
BF16 = jnp.bfloat16
U32 = jnp.uint32
I32 = jnp.int32

GRID_W = 64
N_HEADS = 8
N_KV_HEADS = 4
GROUP = N_HEADS // N_KV_HEADS
HEAD_DIM = 128
AXIS_DIM = HEAD_DIM // 2
ATTN_WIDTH = N_HEADS * HEAD_DIM
KV_WIDTH = N_KV_HEADS * HEAD_DIM
ROPE_THETA = 10000.0
ATTN_SCALE = HEAD_DIM ** -0.5
CONV_WIDTH = 512
CONV_KERNEL = 31
CONV_PAD = CONV_KERNEL // 2
N_EXPERTS = 256
TOP_K = 8
N_GROUPS = 8
TOPK_GROUPS = 4
EXPERTS_PER_GROUP = N_EXPERTS // N_GROUPS
EXPERT_HIDDEN = 256
SHARED_HIDDEN = 256
ROUTED_SCALE = 2.5
EPS = 1e-6
LOG2E = 1.4426950408889634
SAFE_EXP2_ARG = 64.0

LANES = 128
SUBLANES = 8
VMEM_LIMIT_BYTES = 56 * 1024 * 1024

HALO_ROWS = 16
CONV_ROWS = 64
EXPERT_BLOCK = 512
SC_SUBCORES = 16
SC_WORKERS = 2 * SC_SUBCORES
SC_WINDOW = 128
COMBINE_CHUNKS = 4
HIGHEST = lax.Precision.HIGHEST


def _params(sem):
    return pltpu.CompilerParams(dimension_semantics=sem, vmem_limit_bytes=VMEM_LIMIT_BYTES)


def _sigmoid(x):
    return 1.0 / (1.0 + jnp.exp(-x))


def _pack_bf16_pair(lo, hi):
    lo_b = pltpu.bitcast(lo.astype(BF16).astype(F32), U32)
    hi_b = pltpu.bitcast(hi.astype(BF16).astype(F32), U32)
    return (lo_b >> 16) | (hi_b & jnp.uint32(0xFFFF0000))


def _unpack_bf16_pair(u):
    lo = pltpu.bitcast(u << 16, F32)
    hi = pltpu.bitcast(u & jnp.uint32(0xFFFF0000), F32)
    return lo, hi


def _mod_kernel(cc_ref, w_ref, b_ref, o_ref):
    cc = cc_ref[...]
    s = cc * _sigmoid(cc)
    o_ref[...] = jnp.dot(s, w_ref[...], precision=HIGHEST, preferred_element_type=F32) + b_ref[...]


def _modulation(cc, w_mod, b_mod):
    d, n = w_mod.shape
    tn = n // 4
    return pl.pallas_call(
        _mod_kernel,
        grid=(n // tn,),
        in_specs=[pl.BlockSpec((SUBLANES, d), lambda j: (0, 0)),
                  pl.BlockSpec((d, tn), lambda j: (0, j)),
                  pl.BlockSpec((1, tn), lambda j: (0, j))],
        out_specs=pl.BlockSpec((SUBLANES, tn), lambda j: (0, j)),
        out_shape=jax.ShapeDtypeStruct((SUBLANES, n), F32),
        compiler_params=_params(("arbitrary",)),
        name="mod",
    )(cc, w_mod, b_mod)


def _norm_modulate(x, g, sh, sc):
    ms = jnp.mean(x * x, axis=-1, keepdims=True)
    return (x * lax.rsqrt(ms + EPS) * g) * (1.0 + sc) + sh


def _head_norm(p, gain):
    r = lax.rsqrt(jnp.mean(p * p, axis=-1, keepdims=True) + EPS)
    return p * r * gain


def _dot_nt(a, b):
    return lax.dot_general(a, b, (((1,), (1,)), ((), ())), preferred_element_type=F32)


def _ctx_kv_kernel(x_ref, sh_ref, sc_ref, g1_ref, wk_ref, wvt_ref, gk_ref, k_ref, vt_ref):
    h = _norm_modulate(x_ref[0], g1_ref[...], sh_ref[...], sc_ref[...]).astype(BF16)
    pk = jnp.dot(h, wk_ref[...], preferred_element_type=F32)
    for j in range(N_KV_HEADS):
        sl = slice(j * HEAD_DIM, (j + 1) * HEAD_DIM)
        k_ref[0, :, sl] = _head_norm(pk[:, sl], gk_ref[...]).astype(BF16)
    vt_ref[0] = _dot_nt(wvt_ref[...], h).astype(BF16)


def _ctx_kv(ctx, csh, csc, g1, w_k, w_vt, gk):
    b, lc, d = ctx.shape
    vec = lambda: pl.BlockSpec((1, d), lambda i: (0, 0))
    return pl.pallas_call(
        _ctx_kv_kernel,
        grid=(b,),
        in_specs=[pl.BlockSpec((1, lc, d), lambda i: (i, 0, 0)), vec(), vec(), vec(),
                  pl.BlockSpec((d, KV_WIDTH), lambda i: (0, 0)),
                  pl.BlockSpec((KV_WIDTH, d), lambda i: (0, 0)),
                  pl.BlockSpec((1, HEAD_DIM), lambda i: (0, 0))],
        out_specs=[pl.BlockSpec((1, lc, KV_WIDTH), lambda i: (i, 0, 0)),
                   pl.BlockSpec((1, KV_WIDTH, lc), lambda i: (i, 0, 0))],
        out_shape=[jax.ShapeDtypeStruct((b, lc, KV_WIDTH), BF16),
                   jax.ShapeDtypeStruct((b, KV_WIDTH, lc), BF16)],
        compiler_params=_params(("arbitrary",)),
        name="ctx_kv",
    )(ctx, csh, csc, g1, w_k, w_vt, gk)


def _in_proj_kernel(x_ref, sh_ref, sc_ref, g1_ref, w_ref, wvt_ref, gq_ref, gk_ref, cos_ref, sin_ref,
                    q_ref, k_ref, vt_ref, h_ref, g_ref):
    h = _norm_modulate(x_ref[0], g1_ref[...], sh_ref[0], sc_ref[0]).astype(BF16)
    cos = cos_ref[...]
    sin = sin_ref[...]
    lane = lax.broadcasted_iota(I32, cos.shape, 1)
    upper = (lane & (AXIS_DIM // 2)) != 0

    def rope(p):
        swapped = jnp.where(upper, pltpu.roll(p, AXIS_DIM // 2, 1),
                            pltpu.roll(p, HEAD_DIM - AXIS_DIM // 2, 1))
        return p * cos + swapped * sin

    q_end = ATTN_WIDTH
    k_end = q_end + KV_WIDTH
    v_end = k_end + KV_WIDTH
    u_end = v_end + 2 * CONV_WIDTH
    pq = jnp.dot(h, w_ref[:, :q_end], preferred_element_type=F32)
    for j in range(N_HEADS):
        sl = slice(j * HEAD_DIM, (j + 1) * HEAD_DIM)
        q_ref[0, :, sl] = (rope(_head_norm(pq[:, sl], gq_ref[...])) * (ATTN_SCALE * LOG2E)).astype(BF16)
    pk = jnp.dot(h, w_ref[:, q_end:k_end], preferred_element_type=F32)
    for j in range(N_KV_HEADS):
        sl = slice(j * HEAD_DIM, (j + 1) * HEAD_DIM)
        k_ref[0, :, sl] = rope(_head_norm(pk[:, sl], gk_ref[...])).astype(BF16)
    vt_ref[0] = _dot_nt(wvt_ref[...], h).astype(BF16)
    u = jnp.dot(h, w_ref[:, v_end:u_end], preferred_element_type=F32)
    h_ref[0] = (u[:, :CONV_WIDTH] * _sigmoid(u[:, CONV_WIDTH:])).astype(BF16)
    g_ref[0] = _sigmoid(jnp.dot(h, w_ref[:, u_end:], preferred_element_type=F32)).astype(BF16)


def _in_proj(x, sh1, sc1, g1, w_in, w_vt, gq, gk, cos_t, sin_t, tm):
    b, s, d = x.shape
    n_in = w_in.shape[1]
    bvec = lambda: pl.BlockSpec((1, 1, d), lambda bi, i: (bi, 0, 0))
    tok = lambda w: pl.BlockSpec((1, tm, w), lambda bi, i: (bi, i, 0))
    return pl.pallas_call(
        _in_proj_kernel,
        grid=(b, s // tm),
        in_specs=[tok(d), bvec(), bvec(),
                  pl.BlockSpec((1, d), lambda bi, i: (0, 0)),
                  pl.BlockSpec((d, n_in), lambda bi, i: (0, 0)),
                  pl.BlockSpec((KV_WIDTH, d), lambda bi, i: (0, 0)),
                  pl.BlockSpec((1, HEAD_DIM), lambda bi, i: (0, 0)),
                  pl.BlockSpec((1, HEAD_DIM), lambda bi, i: (0, 0)),
                  pl.BlockSpec((tm, HEAD_DIM), lambda bi, i: (i, 0)),
                  pl.BlockSpec((tm, HEAD_DIM), lambda bi, i: (i, 0))],
        out_specs=[tok(ATTN_WIDTH), tok(KV_WIDTH),
                   pl.BlockSpec((1, KV_WIDTH, tm), lambda bi, i: (bi, 0, i)),
                   tok(CONV_WIDTH), tok(2 * d)],
        out_shape=[jax.ShapeDtypeStruct((b, s, ATTN_WIDTH), BF16),
                   jax.ShapeDtypeStruct((b, s, KV_WIDTH), BF16),
                   jax.ShapeDtypeStruct((b, KV_WIDTH, s), BF16),
                   jax.ShapeDtypeStruct((b, s, CONV_WIDTH), BF16),
                   jax.ShapeDtypeStruct((b, s, 2 * d), BF16)],
        compiler_params=_params(("arbitrary", "arbitrary")),
        name="in_proj",
    )(x, sh1, sc1, g1, w_in, w_vt, gq, gk, cos_t, sin_t)


def _sub_allreduce(x, op):
    for s in (4, 2, 1):
        x = op(x, pltpu.roll(x, s, 0))
    return x


def _attn_kernel(bounded_ref, q_ref, kc_ref, vct_ref, k_ref, vt_ref, o_ref, s0, s1, x0, x1, m_sc, l_sc, acc_sc,
                 *, tk):
    tq = q_ref.shape[1]
    m_cols = GROUP * tq
    nk = k_ref.shape[1] // tk
    qf = q_ref[0].astype(F32).T
    qt = jnp.concatenate([qf[:HEAD_DIM], qf[HEAD_DIM:]], axis=1).astype(BF16)
    slots = ((s0, x0), (s1, x1))

    def kchunk(j):
        return k_ref[0, pl.ds(pl.multiple_of(j * tk, tk), tk), :]

    def vchunk(j):
        return vt_ref[0, :, pl.ds(pl.multiple_of(j * tk, tk), tk)]

    def split(st):
        return st.reshape(st.shape[0] // SUBLANES, SUBLANES, m_cols)

    def scores(k, online):
        st = jnp.dot(k, qt, preferred_element_type=F32)
        return st, (jnp.max(split(st), axis=0) if online else None)

    def absorb(st, mx, vt, online):
        s3 = split(st)
        if online:
            m_prev = m_sc[...]
            m_new = jnp.maximum(m_prev, _sub_allreduce(mx, jnp.maximum))
            alpha = jnp.exp2(m_prev - m_new)
            p3 = jnp.exp2(s3 - m_new[None])
            l_sc[...] = alpha * l_sc[...] + _sub_allreduce(jnp.sum(p3, axis=0), jnp.add)
            pv = jnp.dot(vt, p3.reshape(st.shape).astype(BF16), preferred_element_type=F32)
            acc_sc[...] = alpha[0:1] * acc_sc[...] + pv
            m_sc[...] = m_new
        else:
            p3 = jnp.exp2(s3)
            l_sc[...] = l_sc[...] + jnp.sum(p3, axis=0)
            acc_sc[...] = acc_sc[...] + jnp.dot(vt, p3.reshape(st.shape).astype(BF16),
                                                preferred_element_type=F32)

    def stage(slot, k, online):
        st, mx = scores(k, online)
        slots[slot][0][...] = st
        if online:
            slots[slot][1][...] = mx

    def take(slot, vt, online):
        absorb(slots[slot][0][...], slots[slot][1][...] if online else None, vt, online)

    def sweep(online):
        if online:
            m_sc[...] = jnp.full(m_sc.shape, -jnp.inf, F32)
        l_sc[...] = jnp.zeros(l_sc.shape, F32)
        acc_sc[...] = jnp.zeros(acc_sc.shape, F32)
        stage(0, kchunk(0), online)

        def body(i, carry):
            j = 2 * i
            stage(1, kchunk(j + 1), online)
            take(0, vchunk(j), online)
            stage(0, kchunk(j + 2), online)
            take(1, vchunk(j + 1), online)
            return carry

        lax.fori_loop(0, nk // 2 - 1, body, 0)
        stage(1, kchunk(nk - 1), online)
        take(0, vchunk(nk - 2), online)
        sc, xc = scores(kc_ref[0], online)
        take(1, vchunk(nk - 1), online)
        absorb(sc, xc, vct_ref[0], online)
        denom = l_sc[...] if online else _sub_allreduce(l_sc[...], jnp.add)
        o = (acc_sc[...] / denom[0:1]).T
        o_ref[0, :, :HEAD_DIM] = o[:tq].astype(BF16)
        o_ref[0, :, HEAD_DIM:] = o[tq:].astype(BF16)

    @pl.when(bounded_ref[0] != 0)
    def _():
        sweep(online=False)

    @pl.when(bounded_ref[0] == 0)
    def _():
        sweep(online=True)


def _attention(bounded, q, kc, vct, kx, vxt, tq, tk):
    b, s, _ = q.shape
    lc = kc.shape[1]
    assert s % (2 * tk) == 0
    gw = GROUP * HEAD_DIM
    m_cols = GROUP * tq
    kv = lambda l: pl.BlockSpec((1, l, HEAD_DIM), lambda bi, h, i, bd: (bi, 0, h))
    kvt = lambda l: pl.BlockSpec((1, HEAD_DIM, l), lambda bi, h, i, bd: (bi, h, 0))
    qo = lambda: pl.BlockSpec((1, tq, gw), lambda bi, h, i, bd: (bi, i, h))
    return pl.pallas_call(
        functools.partial(_attn_kernel, tk=tk),
        grid_spec=pltpu.PrefetchScalarGridSpec(
            num_scalar_prefetch=1,
            grid=(b, N_KV_HEADS, s // tq),
            in_specs=[qo(), kv(lc), kvt(lc), kv(s), kvt(s)],
            out_specs=qo(),
            scratch_shapes=[pltpu.VMEM((tk, m_cols), F32), pltpu.VMEM((tk, m_cols), F32),
                            pltpu.VMEM((SUBLANES, m_cols), F32), pltpu.VMEM((SUBLANES, m_cols), F32),
                            pltpu.VMEM((SUBLANES, m_cols), F32), pltpu.VMEM((SUBLANES, m_cols), F32),
                            pltpu.VMEM((HEAD_DIM, m_cols), F32)]),
        out_shape=jax.ShapeDtypeStruct((b, s, ATTN_WIDTH), BF16),
        compiler_params=_params(("arbitrary", "arbitrary", "arbitrary")),
        name="attn",
    )(bounded, q, kc, vct, kx, vxt)


def _merge_kernel(o_ref, hp_ref, hc_ref, hn_ref, g_ref, x_ref, gt1_ref, sh2_ref, sc2_ref,
                  wdw_ref, bdw_ref, lng_ref, lnb_ref, wap_ref, wcp_ref, wout_ref, g2_ref, wrt_ref,
                  x1_ref, hx_ref, lg_ref, hcat, act_sc, shift_sc):
    i = pl.program_id(1)
    n_i = pl.num_programs(1)
    tm = hc_ref.shape[1]
    d = x_ref.shape[2]
    prev = hp_ref[0].astype(F32)
    nxt = hn_ref[0].astype(F32)
    hcat[0:HALO_ROWS, :] = jnp.where(i > 0, prev, jnp.zeros_like(prev))
    hcat[HALO_ROWS:HALO_ROWS + tm, :] = hc_ref[0].astype(F32)
    hcat[HALO_ROWS + tm:, :] = jnp.where(i < n_i - 1, nxt, jnp.zeros_like(nxt))
    base = HALO_ROWS - CONV_PAD
    reach = (base + CONV_KERNEL - 1) // SUBLANES * SUBLANES
    for r0 in range(0, tm, CONV_ROWS):
        acc = jnp.zeros((CONV_ROWS, CONV_WIDTH), F32) + bdw_ref[...]
        for res in range(SUBLANES):
            shift_sc[res] = hcat[r0 + res:r0 + res + CONV_ROWS + reach, :]
            for off in range(res, base + CONV_KERNEL, SUBLANES):
                j = off - base
                if 0 <= j < CONV_KERNEL:
                    a0 = off - res
                    acc = acc + shift_sc[res, a0:a0 + CONV_ROWS, :] * wdw_ref[j:j + 1, :]
        mu = jnp.mean(acc, axis=-1, keepdims=True)
        cen = acc - mu
        var = jnp.mean(cen * cen, axis=-1, keepdims=True)
        ln = cen * lax.rsqrt(var + EPS) * lng_ref[...] + lnb_ref[...]
        act_sc[r0:r0 + CONV_ROWS, :] = (ln * _sigmoid(ln)).astype(BF16)
    y_conv = jnp.dot(act_sc[...], wcp_ref[...], preferred_element_type=F32)
    y_attn = jnp.dot(o_ref[0], wap_ref[...], preferred_element_type=F32)
    z = g_ref[0, :, :d].astype(F32) * y_attn + g_ref[0, :, d:].astype(F32) * y_conv
    mix = jnp.dot(z.astype(BF16), wout_ref[...], preferred_element_type=F32)
    x1 = x_ref[0] + gt1_ref[0] * mix
    x1_ref[0] = x1
    hx = _norm_modulate(x1, g2_ref[...], sh2_ref[0], sc2_ref[0])
    half = d // 2
    hx_ref[...] = pltpu.bitcast(_pack_bf16_pair(hx[:, :half], hx[:, half:]), I32)
    lg_ref[...] = lax.dot_general(wrt_ref[...], hx, (((1,), (1,)), ((), ())),
                                  precision=HIGHEST, preferred_element_type=F32)


def _merge(o, hglu, g, x, gt1, sh2, sc2, w_dw, b_dw, ln_g, ln_b, w_ap, w_cp, w_out, g2, w_rt, tm):
    b, s, d = x.shape
    nt = s // tm
    hb = tm // HALO_ROWS
    n_halo = s // HALO_ROWS
    bvec = lambda: pl.BlockSpec((1, 1, d), lambda bi, i: (bi, 0, 0))
    full = lambda a: pl.BlockSpec(a.shape, lambda bi, i: (0,) * a.ndim)
    tok = lambda w: pl.BlockSpec((1, tm, w), lambda bi, i: (bi, i, 0))
    return pl.pallas_call(
        _merge_kernel,
        grid=(b, nt),
        in_specs=[tok(ATTN_WIDTH),
                  pl.BlockSpec((1, HALO_ROWS, CONV_WIDTH), lambda bi, i: (bi, jnp.maximum(i * hb - 1, 0), 0)),
                  tok(CONV_WIDTH),
                  pl.BlockSpec((1, HALO_ROWS, CONV_WIDTH),
                               lambda bi, i: (bi, jnp.minimum((i + 1) * hb, n_halo - 1), 0)),
                  tok(2 * d), tok(d), bvec(), bvec(), bvec(),
                  full(w_dw), full(b_dw), full(ln_g), full(ln_b), full(w_ap), full(w_cp), full(w_out),
                  full(g2), full(w_rt)],
        out_specs=[tok(d),
                   pl.BlockSpec((tm, d // 2), lambda bi, i: (bi * nt + i, 0)),
                   pl.BlockSpec((N_EXPERTS, tm), lambda bi, i: (0, bi * nt + i))],
        out_shape=[jax.ShapeDtypeStruct((b, s, d), F32),
                   jax.ShapeDtypeStruct((b * s, d // 2), I32),
                   jax.ShapeDtypeStruct((N_EXPERTS, b * s), F32)],
        scratch_shapes=[pltpu.VMEM((tm + 2 * HALO_ROWS, CONV_WIDTH), F32), pltpu.VMEM((tm, CONV_WIDTH), BF16),
                        pltpu.VMEM((SUBLANES,
                                    CONV_ROWS + (CONV_KERNEL + HALO_ROWS - CONV_PAD - 1) // SUBLANES * SUBLANES,
                                    CONV_WIDTH), F32)],
        compiler_params=_params(("arbitrary", "arbitrary")),
        name="merge",
    )(o, hglu, hglu, hglu, g, x, gt1, sh2, sc2, w_dw, b_dw, ln_g, ln_b, w_ap, w_cp, w_out, g2, w_rt)


def _route_kernel(lg_ref, bias_ref, tri_ref, e_ref, w_ref, r_ref, cnt_ref, run_sc):
    step = pl.program_id(0)
    n_strips = lg_ref.shape[1] // LANES
    nv = N_EXPERTS // SUBLANES
    gv = EXPERTS_PER_GROUP // SUBLANES

    @pl.when(step == 0)
    def _():
        run_sc[...] = jnp.zeros(run_sc.shape, F32)

    row = (lax.broadcasted_iota(I32, (nv, SUBLANES, LANES), 0) * SUBLANES
           + lax.broadcasted_iota(I32, (nv, SUBLANES, LANES), 1))
    sub = lax.broadcasted_iota(I32, (SUBLANES, LANES), 0)
    bias = bias_ref[...].reshape(nv, SUBLANES, LANES)
    neg_inf = jnp.float32(-jnp.inf)

    for st in range(n_strips):
        lanes = slice(st * LANES, (st + 1) * LANES)
        scores = _sigmoid(lg_ref[:, lanes]).reshape(nv, SUBLANES, LANES)
        biased = scores + bias
        gscore = []
        for g in range(N_GROUPS):
            m1 = biased[g * gv]
            m2 = jnp.full((SUBLANES, LANES), neg_inf, F32)
            for t in range(1, gv):
                v = biased[g * gv + t]
                m2 = jnp.maximum(m2, jnp.minimum(m1, v))
                m1 = jnp.maximum(m1, v)
            for s in (4, 2, 1):
                p1 = pltpu.roll(m1, s, 0)
                p2 = pltpu.roll(m2, s, 0)
                m2 = jnp.maximum(jnp.minimum(m1, p1), jnp.maximum(m2, p2))
                m1 = jnp.maximum(m1, p1)
            gscore.append(m1 + m2)
        masked = []
        for g in range(N_GROUPS):
            beaten = jnp.zeros((SUBLANES, LANES), I32)
            for o in range(N_GROUPS):
                if o == g:
                    continue
                wins = (gscore[o] > gscore[g]) | ((gscore[o] == gscore[g]) & (o < g))
                beaten = beaten + wins.astype(I32)
            keep = beaten < TOPK_GROUPS
            for t in range(gv):
                masked.append(jnp.where(keep, biased[g * gv + t], neg_inf))
        cand = jnp.stack(masked, axis=0)
        sel = jnp.zeros((nv, SUBLANES, LANES), jnp.bool_)
        picks, pick_scores = [], []
        for _ in range(TOP_K):
            mx = _sub_allreduce(jnp.max(cand, axis=0), jnp.maximum)
            idx = _sub_allreduce(jnp.min(jnp.where(cand == mx, row, N_EXPERTS), axis=0), jnp.minimum)
            hit = row == idx
            pick_scores.append(_sub_allreduce(jnp.sum(jnp.where(hit, scores, 0.0), axis=0), jnp.add))
            picks.append(idx)
            sel = sel | hit
            cand = jnp.where(hit, neg_inf, cand)
        sel_b = sel.astype(F32).astype(BF16).reshape(N_EXPERTS, LANES)
        before = jnp.dot(sel_b, tri_ref[0], preferred_element_type=F32)
        total = jnp.dot(sel_b, tri_ref[1], preferred_element_type=F32)
        rank_all = (before + run_sc[...]).reshape(nv, SUBLANES, LANES)
        run_sc[...] = run_sc[...] + total
        denom = pick_scores[0]
        for kk in range(1, TOP_K):
            denom = denom + pick_scores[kk]
        e_out = jnp.zeros((SUBLANES, LANES), I32)
        w_out = jnp.zeros((SUBLANES, LANES), F32)
        r_out = jnp.zeros((SUBLANES, LANES), I32)
        for kk in range(TOP_K):
            rk = _sub_allreduce(jnp.sum(jnp.where(row == picks[kk], rank_all, 0.0), axis=0), jnp.add)
            e_out = jnp.where(sub == kk, picks[kk], e_out)
            w_out = jnp.where(sub == kk, pick_scores[kk] / denom * ROUTED_SCALE, w_out)
            r_out = jnp.where(sub == kk, rk.astype(I32), r_out)
        e_ref[:, lanes] = e_out
        w_ref[:, lanes] = w_out
        r_ref[:, lanes] = r_out

    cnt_ref[...] = run_sc[...].astype(I32)


def _route(logits_t, bias, tb):
    n = logits_t.shape[1]
    iota_r = lax.broadcasted_iota(I32, (LANES, LANES), 0)
    iota_c = lax.broadcasted_iota(I32, (LANES, LANES), 1)
    tri = jnp.stack([(iota_r < iota_c), jnp.ones((LANES, LANES), jnp.bool_)]).astype(BF16)
    bias_b = jnp.broadcast_to(bias.reshape(N_EXPERTS, 1), (N_EXPERTS, LANES)).astype(F32)
    tokrow = lambda dt: jax.ShapeDtypeStruct((TOP_K, n), dt)
    return pl.pallas_call(
        _route_kernel,
        grid=(n // tb,),
        in_specs=[pl.BlockSpec((N_EXPERTS, tb), lambda i: (0, i)),
                  pl.BlockSpec((N_EXPERTS, LANES), lambda i: (0, 0)),
                  pl.BlockSpec((2, LANES, LANES), lambda i: (0, 0, 0))],
        out_specs=[pl.BlockSpec((TOP_K, tb), lambda i: (0, i)),
                   pl.BlockSpec((TOP_K, tb), lambda i: (0, i)),
                   pl.BlockSpec((TOP_K, tb), lambda i: (0, i)),
                   pl.BlockSpec((N_EXPERTS, LANES), lambda i: (0, 0))],
        out_shape=[tokrow(I32), tokrow(F32), tokrow(I32),
                   jax.ShapeDtypeStruct((N_EXPERTS, LANES), I32)],
        scratch_shapes=[pltpu.VMEM((N_EXPERTS, LANES), F32)],
        compiler_params=_params(("arbitrary",)),
        name="route",
    )(logits_t, bias_b, tri)


def _slots_kernel(pstart_ref, e_ref, r_ref, o_ref):
    e = e_ref[...]

    def body(x, acc):
        return acc + jnp.where(e == x, pstart_ref[x], 0)

    o_ref[...] = lax.fori_loop(0, N_EXPERTS, body, r_ref[...])


def _slots(pstart, e_idx, rank, tb):
    n = e_idx.shape[1]
    blk = lambda: pl.BlockSpec((TOP_K, tb), lambda i, ps: (0, i))
    return pl.pallas_call(
        _slots_kernel,
        grid_spec=pltpu.PrefetchScalarGridSpec(
            num_scalar_prefetch=1, grid=(n // tb,), in_specs=[blk(), blk()], out_specs=blk()),
        out_shape=jax.ShapeDtypeStruct((TOP_K, n), I32),
        compiler_params=_params(("arbitrary",)),
        name="slots",
    )(pstart, e_idx, rank)


def _sc_mesh():
    return plsc.VectorSubcoreMesh(core_axis_name="c", subcore_axis_name="s")


def _sc_worker():
    return lax.axis_index("c") * SC_SUBCORES + lax.axis_index("s")


def _sc_dispatch(hx, pos, cap):
    n, w = hx.shape
    per_worker = n // SC_WINDOW // SC_WORKERS
    assert per_worker * SC_WINDOW * SC_WORKERS == n

    @pl.kernel(out_type=jax.ShapeDtypeStruct((cap, w), hx.dtype), mesh=_sc_mesh(),
               scratch_types=[pltpu.VMEM((SC_WINDOW, w), hx.dtype), pltpu.VMEM((TOP_K, SC_WINDOW), I32),
                              pltpu.SemaphoreType.DMA])
    def scatter_rows(x_hbm, i_hbm, o_hbm, xbuf, ibuf, sem):
        wid = _sc_worker()

        @pl.loop(0, per_worker)
        def _(j):
            row0 = (wid * per_worker + j) * SC_WINDOW
            pltpu.sync_copy(x_hbm.at[pl.ds(row0, SC_WINDOW)], xbuf)
            pltpu.sync_copy(i_hbm.at[:, pl.ds(row0, SC_WINDOW)], ibuf)
            copies = [pltpu.async_copy(xbuf, o_hbm.at[ibuf.at[kk]], sem) for kk in range(TOP_K)]
            for cp in copies:
                cp.wait()

    return scatter_rows(hx, pos)


def _sc_gather(ys, pos):
    n = pos.shape[1]
    w = ys.shape[1]
    per_worker = n // SC_WINDOW // SC_WORKERS
    assert per_worker * SC_WINDOW * SC_WORKERS == n

    @pl.kernel(out_type=jax.ShapeDtypeStruct((TOP_K, n, w), ys.dtype), mesh=_sc_mesh(),
               scratch_types=[pltpu.VMEM((SC_WINDOW, w), ys.dtype), pltpu.VMEM((TOP_K, SC_WINDOW), I32)])
    def gather_rows(y_hbm, i_hbm, o_hbm, ybuf, ibuf):
        wid = _sc_worker()

        @pl.loop(0, per_worker)
        def _(j):
            row0 = (wid * per_worker + j) * SC_WINDOW
            pltpu.sync_copy(i_hbm.at[:, pl.ds(row0, SC_WINDOW)], ibuf)
            for kk in range(TOP_K):
                pltpu.sync_copy(y_hbm.at[ibuf.at[kk]], ybuf)
                pltpu.sync_copy(ybuf, o_hbm.at[kk, pl.ds(row0, SC_WINDOW)])

    return gather_rows(ys, pos)


def _experts_kernel(blk_e_ref, valid_ref, blk_in_ref, blk_out_ref, xs_ref, wgu_ref, wdn_ref, ys_ref,
                    wgu_sc, wdn_sc):
    del blk_in_ref, blk_out_ref
    i = pl.program_id(0)
    valid = valid_ref[i]

    @pl.when(valid > 0)
    def _():
        prev = blk_e_ref[jnp.maximum(i - 1, 0)]

        @pl.when((i == 0) | (blk_e_ref[i] != prev))
        def _():
            wgu_sc[...] = wgu_ref[0].astype(BF16)
            wdn_sc[...] = wdn_ref[0].astype(BF16)

        rows = lax.broadcasted_iota(I32, xs_ref.shape, 0)
        xu = jnp.where(rows < valid, pltpu.bitcast(xs_ref[...], U32), jnp.uint32(0))
        lo, hi = _unpack_bf16_pair(xu)
        half = lo.shape[1]
        gu = (jnp.dot(lo.astype(BF16), wgu_sc[:half, :], preferred_element_type=F32)
              + jnp.dot(hi.astype(BF16), wgu_sc[half:, :], preferred_element_type=F32))
        gt = gu[:, :EXPERT_HIDDEN]
        act = (gt * _sigmoid(gt) * gu[:, EXPERT_HIDDEN:]).astype(BF16)
        y = jnp.dot(act, wdn_sc[...], preferred_element_type=F32)
        ys_ref[...] = pltpu.bitcast(_pack_bf16_pair(y[:, :half], y[:, half:]), I32)

    @pl.when(valid <= 0)
    def _():
        ys_ref[...] = jnp.zeros(ys_ref.shape, I32)


def _experts(blk_e, blk_valid, xs, w_gu, w_dn):
    cap, w = xs.shape
    n_e, d, h2 = w_gu.shape
    n_blocks = cap // EXPERT_BLOCK
    step = jnp.arange(n_blocks, dtype=I32)
    n_used = jnp.sum((blk_valid > 0).astype(I32))
    blk_in = jnp.minimum(step, jnp.maximum(n_used - 1, 0)).astype(I32)
    blk_out = jnp.where(blk_valid > 0, step, n_blocks).astype(I32)
    return pl.pallas_call(
        _experts_kernel,
        grid_spec=pltpu.PrefetchScalarGridSpec(
            num_scalar_prefetch=4,
            grid=(n_blocks,),
            in_specs=[pl.BlockSpec((EXPERT_BLOCK, w), lambda i, be, bv, bi, bo: (bi[i], 0)),
                      pl.BlockSpec((1, d, h2), lambda i, be, bv, bi, bo: (be[i], 0, 0)),
                      pl.BlockSpec((1, h2 // 2, d), lambda i, be, bv, bi, bo: (be[i], 0, 0))],
            out_specs=pl.BlockSpec((EXPERT_BLOCK, w), lambda i, be, bv, bi, bo: (bo[i], 0)),
            scratch_shapes=[pltpu.VMEM((d, h2), BF16), pltpu.VMEM((h2 // 2, d), BF16)]),
        out_shape=jax.ShapeDtypeStruct((cap + EXPERT_BLOCK, w), I32),
        compiler_params=_params(("arbitrary",)),
        name="experts",
    )(blk_e, blk_valid, blk_in, blk_out, xs, w_gu, w_dn)


def _combine_kernel(yg_ref, wt_ref, hx_ref, x1_ref, gt2_ref, wsg_ref, wsd_ref, fg_ref, *rest):
    o_ref = rest[-1]
    lo, hi = _unpack_bf16_pair(pltpu.bitcast(hx_ref[...], U32))
    half = lo.shape[1]
    gu = (jnp.dot(lo.astype(BF16), wsg_ref[:half, :], preferred_element_type=F32)
          + jnp.dot(hi.astype(BF16), wsg_ref[half:, :], preferred_element_type=F32))
    gt = gu[:, :SHARED_HIDDEN]
    act = (gt * _sigmoid(gt) * gu[:, SHARED_HIDDEN:]).astype(BF16)
    y = jnp.dot(act, wsd_ref[...], preferred_element_type=F32)
    y_lo = y[:, :half]
    y_hi = y[:, half:]
    for kk in range(TOP_K):
        r_lo, r_hi = _unpack_bf16_pair(pltpu.bitcast(yg_ref[kk], U32))
        wk = wt_ref[:, kk:kk + 1]
        y_lo = y_lo + wk * r_lo
        y_hi = y_hi + wk * r_hi
    x2_lo = x1_ref[:, :half] + gt2_ref[0, :, :half] * y_lo
    x2_hi = x1_ref[:, half:] + gt2_ref[0, :, half:] * y_hi
    ms = (jnp.sum(x2_lo * x2_lo, axis=-1, keepdims=True)
          + jnp.sum(x2_hi * x2_hi, axis=-1, keepdims=True)) / (2 * half)
    inv = lax.rsqrt(ms + EPS)
    o_ref[:, :half] = x2_lo * inv * fg_ref[:, :half]
    o_ref[:, half:] = x2_hi * inv * fg_ref[:, half:]


def _combine(yg, wt, hx, x1, gt2, w_sg, w_sd, fg, tm, tiles_per_batch, tile0, out_prev):
    n, w = hx.shape
    d = 2 * w
    full = lambda a: pl.BlockSpec(a.shape, lambda i: (0,) * a.ndim)
    in_specs = [pl.BlockSpec((TOP_K, tm, w), lambda i: (0, i, 0)),
                pl.BlockSpec((tm, TOP_K), lambda i: (tile0 + i, 0)),
                pl.BlockSpec((tm, w), lambda i: (tile0 + i, 0)),
                pl.BlockSpec((tm, d), lambda i: (tile0 + i, 0)),
                pl.BlockSpec((1, 1, d), lambda i: ((tile0 + i) // tiles_per_batch, 0, 0)),
                full(w_sg), full(w_sd), full(fg)]
    args = [yg, wt, hx, x1, gt2, w_sg, w_sd, fg]
    aliases = {}
    if out_prev is not None:
        in_specs.append(pl.BlockSpec(memory_space=pl.ANY))
        args.append(out_prev)
        aliases = {len(args) - 1: 0}
    return pl.pallas_call(
        _combine_kernel,
        grid=(yg.shape[1] // tm,),
        in_specs=in_specs,
        out_specs=pl.BlockSpec((tm, d), lambda i: (tile0 + i, 0)),
        out_shape=jax.ShapeDtypeStruct((n, d), F32),
        input_output_aliases=aliases,
        compiler_params=_params(("arbitrary",)),
        name="combine",
    )(*args)


def _rope_tables(seq):
    rows = seq // GRID_W
    pos_row = np.repeat(np.arange(rows, dtype=np.float32), GRID_W)
    pos_col = np.tile(np.arange(GRID_W, dtype=np.float32), rows)
    inv_freq = (ROPE_THETA ** (-np.arange(0, AXIS_DIM, 2, dtype=np.float32) / AXIS_DIM)).astype(np.float32)
    ar = pos_row[:, None] * inv_freq
    ac = pos_col[:, None] * inv_freq
    cos_t = np.concatenate([np.cos(ar), np.cos(ar), np.cos(ac), np.cos(ac)], axis=1)
    sin_t = np.concatenate([-np.sin(ar), np.sin(ar), -np.sin(ac), np.sin(ac)], axis=1)
    return jnp.asarray(cos_t, F32), jnp.asarray(sin_t, F32)


def _tile(n, want):
    t = min(n, want)
    assert n % t == 0, (n, want)
    return t


def kernel(x, c, ctx, c_ctx, w_mod, b_mod, norm1_g, w_in, q_norm_g, k_norm_g, w_dw, b_dw, conv_ln_g, conv_ln_b, w_attn_proj, w_conv_proj, w_out, norm2_g, w_router, router_bias, w_exp_gu, w_exp_dn, w_sh_gu, w_sh_dn, final_g):
    b, s, d = x.shape
    depth = w_mod.shape[0]
    assert depth == 1, "single-layer block"
    n = b * s
    row = lambda v: v.reshape(1, -1)

    cc = jnp.zeros((SUBLANES, d), F32).at[:b].set(c).at[b].set(c_ctx)
    mod = _modulation(cc, w_mod[0], row(b_mod[0]))
    mod_x = mod[:b].reshape(b, 1, 6, d)
    sh1, sc1, gt1, sh2, sc2, gt2 = [mod_x[:, :, j, :] for j in range(6)]
    mod_c = mod[b].reshape(6, d)
    csh1, csc1 = row(mod_c[0]), row(mod_c[1])

    w_in_b = w_in[0].astype(BF16)
    q_end, kv_end = ATTN_WIDTH, ATTN_WIDTH + 2 * KV_WIDTH
    k_end = q_end + KV_WIDTH
    w_vt = w_in_b[:, k_end:kv_end].T
    kc, vct = _ctx_kv(ctx, csh1, csc1, row(norm1_g[0]), w_in_b[:, q_end:k_end], w_vt, row(k_norm_g[0]))

    cos_t, sin_t = _rope_tables(s)
    q, kx, vxt, hglu, gates = _in_proj(x, sh1, sc1, row(norm1_g[0]), w_in_b, w_vt, row(q_norm_g[0]),
                                       row(k_norm_g[0]), cos_t, sin_t, _tile(s, 512))
    score_bound = (HEAD_DIM * ATTN_SCALE * LOG2E) * jnp.max(jnp.abs(q_norm_g[0])) * jnp.max(jnp.abs(k_norm_g[0]))
    bounded = (score_bound <= SAFE_EXP2_ARG).astype(I32).reshape(1)
    o = _attention(bounded, q, kc, vct, kx, vxt, _tile(s, 1024), _tile(s // 2, 1024))

    x1, hx, logits_t = _merge(o, hglu, gates, x, gt1, sh2, sc2, w_dw[0], row(b_dw[0]), row(conv_ln_g[0]),
                              row(conv_ln_b[0]), w_attn_proj[0].astype(BF16), w_conv_proj[0].astype(BF16),
                              w_out[0].astype(BF16), row(norm2_g[0]), w_router[0].T, _tile(s, 256))

    e_idx, wts, rank, counts = _route(logits_t, router_bias[0], _tile(n, 512))

    cnt = counts[:, 0]
    padded = (cnt + EXPERT_BLOCK - 1) // EXPERT_BLOCK * EXPERT_BLOCK
    pends = jnp.cumsum(padded)
    pstart = (pends - padded).astype(I32)
    n_blocks = (n * TOP_K + N_EXPERTS * (EXPERT_BLOCK - 1)) // EXPERT_BLOCK
    cap = n_blocks * EXPERT_BLOCK
    blk_row0 = jnp.arange(n_blocks, dtype=I32) * EXPERT_BLOCK
    blk_e = jnp.minimum(jnp.sum(pends[None, :] <= blk_row0[:, None], axis=1), N_EXPERTS - 1).astype(I32)
    blk_valid = jnp.clip(pstart[blk_e] + cnt[blk_e] - blk_row0, 0, EXPERT_BLOCK).astype(I32)

    pos = _slots(pstart, e_idx, rank, _tile(n, 4096))
    xs = _sc_dispatch(hx, pos, cap)
    ys = _experts(blk_e, blk_valid, xs, w_exp_gu[0], w_exp_dn[0])
    tm = _tile(n, 256)
    chunk = n // COMBINE_CHUNKS if n % (COMBINE_CHUNKS * SC_WINDOW * SC_WORKERS) == 0 else n
    wt_t, x1_2d = wts.T, x1.reshape(n, d)
    w_sg, w_sd = w_sh_gu[0].astype(BF16), w_sh_dn[0].astype(BF16)
    out = None
    for c0 in range(0, n, chunk):
        yg = _sc_gather(ys, pos[:, c0:c0 + chunk])
        out = _combine(yg, wt_t, hx, x1_2d, gt2, w_sg, w_sd, row(final_g), tm, s // tm, c0 // tm, out)
    return out.reshape(b, s, d)
```

```python
import functools
import math

import jax
import jax.numpy as jnp
import numpy as np
from jax import lax
from jax.experimental import pallas as pl
from jax.experimental.pallas import tpu as pltpu
from jax.experimental.pallas import tpu_sc as plsc

F32 = jnp.float32
BF16 = jnp.bfloat16
U32 = jnp.uint32
I32 = jnp.int32

GRID_W = 64
N_HEADS = 8
N_KV_HEADS = 4
GROUP = N_HEADS // N_KV_HEADS
HEAD_DIM = 128
AXIS_DIM = HEAD_DIM // 2
ATTN_WIDTH = N_HEADS * HEAD_DIM
KV_WIDTH = N_KV_HEADS * HEAD_DIM
ROPE_THETA = 10000.0
ATTN_SCALE = HEAD_DIM ** -0.5
CONV_WIDTH = 512
CONV_KERNEL = 31
CONV_PAD = CONV_KERNEL // 2
N_EXPERTS = 256
TOP_K = 8
N_GROUPS = 8
TOPK_GROUPS = 4
EXPERTS_PER_GROUP = N_EXPERTS // N_GROUPS
EXPERT_HIDDEN = 256
SHARED_HIDDEN = 256
ROUTED_SCALE = 2.5
EPS = 1e-6
LOG2E = 1.4426950408889634
SAFE_EXP2_ARG = 64.0

LANES = 128
SUBLANES = 8
VMEM_LIMIT_BYTES = 56 * 1024 * 1024

HALO_ROWS = 16
CONV_ROWS = 64
EXPERT_BLOCK = 512
SC_SUBCORES = 16
SC_WORKERS = 2 * SC_SUBCORES
SC_WINDOW = 128
COMBINE_CHUNKS = 4
HIGHEST = lax.Precision.HIGHEST


def _params(sem):
    return pltpu.CompilerParams(dimension_semantics=sem, vmem_limit_bytes=VMEM_LIMIT_BYTES)


def _sigmoid(x):
    return 1.0 / (1.0 + jnp.exp(-x))


def _pack_bf16_pair(lo, hi):
    lo_b = pltpu.bitcast(lo.astype(BF16).astype(F32), U32)
    hi_b = pltpu.bitcast(hi.astype(BF16).astype(F32), U32)
    return (lo_b >> 16) | (hi_b & jnp.uint32(0xFFFF0000))


def _unpack_bf16_pair(u):
    lo = pltpu.bitcast(u << 16, F32)
    hi = pltpu.bitcast(u & jnp.uint32(0xFFFF0000), F32)
    return lo, hi


def _mod_kernel(cc_ref, w_ref, b_ref, o_ref):
    cc = cc_ref[...]
    s = cc * _sigmoid(cc)
    o_ref[...] = jnp.dot(s, w_ref[...], precision=HIGHEST, preferred_element_type=F32) + b_ref[...]


def _modulation(cc, w_mod, b_mod):
    d, n = w_mod.shape
    tn = n // 4
    return pl.pallas_call(
        _mod_kernel,
        grid=(n // tn,),
        in_specs=[pl.BlockSpec((SUBLANES, d), lambda j: (0, 0)),
                  pl.BlockSpec((d, tn), lambda j: (0, j)),
                  pl.BlockSpec((1, tn), lambda j: (0, j))],
        out_specs=pl.BlockSpec((SUBLANES, tn), lambda j: (0, j)),
        out_shape=jax.ShapeDtypeStruct((SUBLANES, n), F32),
        compiler_params=_params(("arbitrary",)),
        name="mod",
    )(cc, w_mod, b_mod)


def _norm_modulate(x, g, sh, sc):
    ms = jnp.mean(x * x, axis=-1, keepdims=True)
    return (x * lax.rsqrt(ms + EPS) * g) * (1.0 + sc) + sh


def _head_norm(p, gain):
    r = lax.rsqrt(jnp.mean(p * p, axis=-1, keepdims=True) + EPS)
    return p * r * gain


def _dot_nt(a, b):
    return lax.dot_general(a, b, (((1,), (1,)), ((), ())), preferred_element_type=F32)


def _ctx_kv_kernel(x_ref, sh_ref, sc_ref, g1_ref, wk_ref, wvt_ref, gk_ref, k_ref, vt_ref):
    h = _norm_modulate(x_ref[0], g1_ref[...], sh_ref[...], sc_ref[...]).astype(BF16)
    pk = jnp.dot(h, wk_ref[...], preferred_element_type=F32)
    for j in range(N_KV_HEADS):
        sl = slice(j * HEAD_DIM, (j + 1) * HEAD_DIM)
        k_ref[0, :, sl] = _head_norm(pk[:, sl], gk_ref[...]).astype(BF16)
    vt_ref[0] = _dot_nt(wvt_ref[...], h).astype(BF16)


def _ctx_kv(ctx, csh, csc, g1, w_k, w_vt, gk):
    b, lc, d = ctx.shape
    vec = lambda: pl.BlockSpec((1, d), lambda i: (0, 0))
    return pl.pallas_call(
        _ctx_kv_kernel,
        grid=(b,),
        in_specs=[pl.BlockSpec((1, lc, d), lambda i: (i, 0, 0)), vec(), vec(), vec(),
                  pl.BlockSpec((d, KV_WIDTH), lambda i: (0, 0)),
                  pl.BlockSpec((KV_WIDTH, d), lambda i: (0, 0)),
                  pl.BlockSpec((1, HEAD_DIM), lambda i: (0, 0))],
        out_specs=[pl.BlockSpec((1, lc, KV_WIDTH), lambda i: (i, 0, 0)),
                   pl.BlockSpec((1, KV_WIDTH, lc), lambda i: (i, 0, 0))],
        out_shape=[jax.ShapeDtypeStruct((b, lc, KV_WIDTH), BF16),
                   jax.ShapeDtypeStruct((b, KV_WIDTH, lc), BF16)],
        compiler_params=_params(("arbitrary",)),
        name="ctx_kv",
    )(ctx, csh, csc, g1, w_k, w_vt, gk)


def _in_proj_kernel(x_ref, sh_ref, sc_ref, g1_ref, w_ref, wvt_ref, gq_ref, gk_ref, cos_ref, sin_ref,
                    q_ref, k_ref, vt_ref, h_ref, g_ref):
    h = _norm_modulate(x_ref[0], g1_ref[...], sh_ref[0], sc_ref[0]).astype(BF16)
    cos = cos_ref[...]
    sin = sin_ref[...]
    lane = lax.broadcasted_iota(I32, cos.shape, 1)
    upper = (lane & (AXIS_DIM // 2)) != 0

    def rope(p):
        swapped = jnp.where(upper, pltpu.roll(p, AXIS_DIM // 2, 1),
                            pltpu.roll(p, HEAD_DIM - AXIS_DIM // 2, 1))
        return p * cos + swapped * sin

    q_end = ATTN_WIDTH
    k_end = q_end + KV_WIDTH
    v_end = k_end + KV_WIDTH
    u_end = v_end + 2 * CONV_WIDTH
    pq = jnp.dot(h, w_ref[:, :q_end], preferred_element_type=F32)
    for j in range(N_HEADS):
        sl = slice(j * HEAD_DIM, (j + 1) * HEAD_DIM)
        q_ref[0, :, sl] = (rope(_head_norm(pq[:, sl], gq_ref[...])) * (ATTN_SCALE * LOG2E)).astype(BF16)
    pk = jnp.dot(h, w_ref[:, q_end:k_end], preferred_element_type=F32)
    for j in range(N_KV_HEADS):
        sl = slice(j * HEAD_DIM, (j + 1) * HEAD_DIM)
        k_ref[0, :, sl] = rope(_head_norm(pk[:, sl], gk_ref[...])).astype(BF16)
    vt_ref[0] = _dot_nt(wvt_ref[...], h).astype(BF16)
    u = jnp.dot(h, w_ref[:, v_end:u_end], preferred_element_type=F32)
    h_ref[0] = (u[:, :CONV_WIDTH] * _sigmoid(u[:, CONV_WIDTH:])).astype(BF16)
    g_ref[0] = _sigmoid(jnp.dot(h, w_ref[:, u_end:], preferred_element_type=F32)).astype(BF16)


def _in_proj(x, sh1, sc1, g1, w_in, w_vt, gq, gk, cos_t, sin_t, tm):
    b, s, d = x.shape
    n_in = w_in.shape[1]
    bvec = lambda: pl.BlockSpec((1, 1, d), lambda bi, i: (bi, 0, 0))
    tok = lambda w: pl.BlockSpec((1, tm, w), lambda bi, i: (bi, i, 0))
    return pl.pallas_call(
        _in_proj_kernel,
        grid=(b, s // tm),
        in_specs=[tok(d), bvec(), bvec(),
                  pl.BlockSpec((1, d), lambda bi, i: (0, 0)),
                  pl.BlockSpec((d, n_in), lambda bi, i: (0, 0)),
                  pl.BlockSpec((KV_WIDTH, d), lambda bi, i: (0, 0)),
                  pl.BlockSpec((1, HEAD_DIM), lambda bi, i: (0, 0)),
                  pl.BlockSpec((1, HEAD_DIM), lambda bi, i: (0, 0)),
                  pl.BlockSpec((tm, HEAD_DIM), lambda bi, i: (i, 0)),
                  pl.BlockSpec((tm, HEAD_DIM), lambda bi, i: (i, 0))],
        out_specs=[tok(ATTN_WIDTH), tok(KV_WIDTH),
                   pl.BlockSpec((1, KV_WIDTH, tm), lambda bi, i: (bi, 0, i)),
                   tok(CONV_WIDTH), tok(2 * d)],
        out_shape=[jax.ShapeDtypeStruct((b, s, ATTN_WIDTH), BF16),
                   jax.ShapeDtypeStruct((b, s, KV_WIDTH), BF16),
                   jax.ShapeDtypeStruct((b, KV_WIDTH, s), BF16),
                   jax.ShapeDtypeStruct((b, s, CONV_WIDTH), BF16),
                   jax.ShapeDtypeStruct((b, s, 2 * d), BF16)],
        compiler_params=_params(("arbitrary", "arbitrary")),
        name="in_proj",
    )(x, sh1, sc1, g1, w_in, w_vt, gq, gk, cos_t, sin_t)


def _sub_allreduce(x, op):
    for s in (4, 2, 1):
        x = op(x, pltpu.roll(x, s, 0))
    return x


def _attn_kernel(bounded_ref, q_ref, kc_ref, vct_ref, k_ref, vt_ref, o_ref, s0, s1, x0, x1, m_sc, l_sc, acc_sc,
                 *, tk):
    tq = q_ref.shape[1]
    m_cols = GROUP * tq
    nk = k_ref.shape[1] // tk
    qf = q_ref[0].astype(F32).T
    qt = jnp.concatenate([qf[:HEAD_DIM], qf[HEAD_DIM:]], axis=1).astype(BF16)
    slots = ((s0, x0), (s1, x1))

    def kchunk(j):
        return k_ref[0, pl.ds(pl.multiple_of(j * tk, tk), tk), :]

    def vchunk(j):
        return vt_ref[0, :, pl.ds(pl.multiple_of(j * tk, tk), tk)]

    def split(st):
        return st.reshape(st.shape[0] // SUBLANES, SUBLANES, m_cols)

    def scores(k, online):
        st = jnp.dot(k, qt, preferred_element_type=F32)
        return st, (jnp.max(split(st), axis=0) if online else None)

    def absorb(st, mx, vt, online):
        s3 = split(st)
        if online:
            m_prev = m_sc[...]
            m_new = jnp.maximum(m_prev, _sub_allreduce(mx, jnp.maximum))
            alpha = jnp.exp2(m_prev - m_new)
            p3 = jnp.exp2(s3 - m_new[None])
            l_sc[...] = alpha * l_sc[...] + _sub_allreduce(jnp.sum(p3, axis=0), jnp.add)
            pv = jnp.dot(vt, p3.reshape(st.shape).astype(BF16), preferred_element_type=F32)
            acc_sc[...] = alpha[0:1] * acc_sc[...] + pv
            m_sc[...] = m_new
        else:
            p3 = jnp.exp2(s3)
            l_sc[...] = l_sc[...] + jnp.sum(p3, axis=0)
            acc_sc[...] = acc_sc[...] + jnp.dot(vt, p3.reshape(st.shape).astype(BF16),
                                                preferred_element_type=F32)

    def stage(slot, k, online):
        st, mx = scores(k, online)
        slots[slot][0][...] = st
        if online:
            slots[slot][1][...] = mx

    def take(slot, vt, online):
        absorb(slots[slot][0][...], slots[slot][1][...] if online else None, vt, online)

    def sweep(online):
        if online:
            m_sc[...] = jnp.full(m_sc.shape, -jnp.inf, F32)
        l_sc[...] = jnp.zeros(l_sc.shape, F32)
        acc_sc[...] = jnp.zeros(acc_sc.shape, F32)
        stage(0, kchunk(0), online)

        def body(i, carry):
            j = 2 * i
            stage(1, kchunk(j + 1), online)
            take(0, vchunk(j), online)
            stage(0, kchunk(j + 2), online)
            take(1, vchunk(j + 1), online)
            return carry

        lax.fori_loop(0, nk // 2 - 1, body, 0)
        stage(1, kchunk(nk - 1), online)
        take(0, vchunk(nk - 2), online)
        sc, xc = scores(kc_ref[0], online)
        take(1, vchunk(nk - 1), online)
        absorb(sc, xc, vct_ref[0], online)
        denom = l_sc[...] if online else _sub_allreduce(l_sc[...], jnp.add)
        o = (acc_sc[...] / denom[0:1]).T
        o_ref[0, :, :HEAD_DIM] = o[:tq].astype(BF16)
        o_ref[0, :, HEAD_DIM:] = o[tq:].astype(BF16)

    @pl.when(bounded_ref[0] != 0)
    def _():
        sweep(online=False)

    @pl.when(bounded_ref[0] == 0)
    def _():
        sweep(online=True)


def _attention(bounded, q, kc, vct, kx, vxt, tq, tk):
    b, s, _ = q.shape
    lc = kc.shape[1]
    assert s % (2 * tk) == 0
    gw = GROUP * HEAD_DIM
    m_cols = GROUP * tq
    kv = lambda l: pl.BlockSpec((1, l, HEAD_DIM), lambda bi, h, i, bd: (bi, 0, h))
    kvt = lambda l: pl.BlockSpec((1, HEAD_DIM, l), lambda bi, h, i, bd: (bi, h, 0))
    qo = lambda: pl.BlockSpec((1, tq, gw), lambda bi, h, i, bd: (bi, i, h))
    return pl.pallas_call(
        functools.partial(_attn_kernel, tk=tk),
        grid_spec=pltpu.PrefetchScalarGridSpec(
            num_scalar_prefetch=1,
            grid=(b, N_KV_HEADS, s // tq),
            in_specs=[qo(), kv(lc), kvt(lc), kv(s), kvt(s)],
            out_specs=qo(),
            scratch_shapes=[pltpu.VMEM((tk, m_cols), F32), pltpu.VMEM((tk, m_cols), F32),
                            pltpu.VMEM((SUBLANES, m_cols), F32), pltpu.VMEM((SUBLANES, m_cols), F32),
                            pltpu.VMEM((SUBLANES, m_cols), F32), pltpu.VMEM((SUBLANES, m_cols), F32),
                            pltpu.VMEM((HEAD_DIM, m_cols), F32)]),
        out_shape=jax.ShapeDtypeStruct((b, s, ATTN_WIDTH), BF16),
        compiler_params=_params(("arbitrary", "arbitrary", "arbitrary")),
        name="attn",
    )(bounded, q, kc, vct, kx, vxt)


def _merge_kernel(o_ref, hp_ref, hc_ref, hn_ref, g_ref, x_ref, gt1_ref, sh2_ref, sc2_ref,
                  wdw_ref, bdw_ref, lng_ref, lnb_ref, wap_ref, wcp_ref, wout_ref, g2_ref, wrt_ref,
                  x1_ref, hx_ref, lg_ref, hcat, act_sc, shift_sc):
    i = pl.program_id(1)
    n_i = pl.num_programs(1)
    tm = hc_ref.shape[1]
    d = x_ref.shape[2]
    prev = hp_ref[0].astype(F32)
    nxt = hn_ref[0].astype(F32)
    hcat[0:HALO_ROWS, :] = jnp.where(i > 0, prev, jnp.zeros_like(prev))
    hcat[HALO_ROWS:HALO_ROWS + tm, :] = hc_ref[0].astype(F32)
    hcat[HALO_ROWS + tm:, :] = jnp.where(i < n_i - 1, nxt, jnp.zeros_like(nxt))
    base = HALO_ROWS - CONV_PAD
    reach = (base + CONV_KERNEL - 1) // SUBLANES * SUBLANES
    for r0 in range(0, tm, CONV_ROWS):
        acc = jnp.zeros((CONV_ROWS, CONV_WIDTH), F32) + bdw_ref[...]
        for res in range(SUBLANES):
            shift_sc[res] = hcat[r0 + res:r0 + res + CONV_ROWS + reach, :]
            for off in range(res, base + CONV_KERNEL, SUBLANES):
                j = off - base
                if 0 <= j < CONV_KERNEL:
                    a0 = off - res
                    acc = acc + shift_sc[res, a0:a0 + CONV_ROWS, :] * wdw_ref[j:j + 1, :]
        mu = jnp.mean(acc, axis=-1, keepdims=True)
        cen = acc - mu
        var = jnp.mean(cen * cen, axis=-1, keepdims=True)
        ln = cen * lax.rsqrt(var + EPS) * lng_ref[...] + lnb_ref[...]
        act_sc[r0:r0 + CONV_ROWS, :] = (ln * _sigmoid(ln)).astype(BF16)
    y_conv = jnp.dot(act_sc[...], wcp_ref[...], preferred_element_type=F32)
    y_attn = jnp.dot(o_ref[0], wap_ref[...], preferred_element_type=F32)
    z = g_ref[0, :, :d].astype(F32) * y_attn + g_ref[0, :, d:].astype(F32) * y_conv
    mix = jnp.dot(z.astype(BF16), wout_ref[...], preferred_element_type=F32)
    x1 = x_ref[0] + gt1_ref[0] * mix
    x1_ref[0] = x1
    hx = _norm_modulate(x1, g2_ref[...], sh2_ref[0], sc2_ref[0])
    half = d // 2
    hx_ref[...] = pltpu.bitcast(_pack_bf16_pair(hx[:, :half], hx[:, half:]), I32)
    lg_ref[...] = lax.dot_general(wrt_ref[...], hx, (((1,), (1,)), ((), ())),
                                  precision=HIGHEST, preferred_element_type=F32)


def _merge(o, hglu, g, x, gt1, sh2, sc2, w_dw, b_dw, ln_g, ln_b, w_ap, w_cp, w_out, g2, w_rt, tm):
    b, s, d = x.shape
    nt = s // tm
    hb = tm // HALO_ROWS
    n_halo = s // HALO_ROWS
    bvec = lambda: pl.BlockSpec((1, 1, d), lambda bi, i: (bi, 0, 0))
    full = lambda a: pl.BlockSpec(a.shape, lambda bi, i: (0,) * a.ndim)
    tok = lambda w: pl.BlockSpec((1, tm, w), lambda bi, i: (bi, i, 0))
    return pl.pallas_call(
        _merge_kernel,
        grid=(b, nt),
        in_specs=[tok(ATTN_WIDTH),
                  pl.BlockSpec((1, HALO_ROWS, CONV_WIDTH), lambda bi, i: (bi, jnp.maximum(i * hb - 1, 0), 0)),
                  tok(CONV_WIDTH),
                  pl.BlockSpec((1, HALO_ROWS, CONV_WIDTH),
                               lambda bi, i: (bi, jnp.minimum((i + 1) * hb, n_halo - 1), 0)),
                  tok(2 * d), tok(d), bvec(), bvec(), bvec(),
                  full(w_dw), full(b_dw), full(ln_g), full(ln_b), full(w_ap), full(w_cp), full(w_out),
                  full(g2), full(w_rt)],
        out_specs=[tok(d),
                   pl.BlockSpec((tm, d // 2), lambda bi, i: (bi * nt + i, 0)),
                   pl.BlockSpec((N_EXPERTS, tm), lambda bi, i: (0, bi * nt + i))],
        out_shape=[jax.ShapeDtypeStruct((b, s, d), F32),
                   jax.ShapeDtypeStruct((b * s, d // 2), I32),
                   jax.ShapeDtypeStruct((N_EXPERTS, b * s), F32)],
        scratch_shapes=[pltpu.VMEM((tm + 2 * HALO_ROWS, CONV_WIDTH), F32), pltpu.VMEM((tm, CONV_WIDTH), BF16),
                        pltpu.VMEM((SUBLANES,
                                    CONV_ROWS + (CONV_KERNEL + HALO_ROWS - CONV_PAD - 1) // SUBLANES * SUBLANES,
                                    CONV_WIDTH), F32)],
        compiler_params=_params(("arbitrary", "arbitrary")),
        name="merge",
    )(o, hglu, hglu, hglu, g, x, gt1, sh2, sc2, w_dw, b_dw, ln_g, ln_b, w_ap, w_cp, w_out, g2, w_rt)


def _route_kernel(lg_ref, bias_ref, tri_ref, e_ref, w_ref, r_ref, cnt_ref, run_sc):
    step = pl.program_id(0)
    n_strips = lg_ref.shape[1] // LANES
    nv = N_EXPERTS // SUBLANES
    gv = EXPERTS_PER_GROUP // SUBLANES

    @pl.when(step == 0)
    def _():
        run_sc[...] = jnp.zeros(run_sc.shape, F32)

    row = (lax.broadcasted_iota(I32, (nv, SUBLANES, LANES), 0) * SUBLANES
           + lax.broadcasted_iota(I32, (nv, SUBLANES, LANES), 1))
    sub = lax.broadcasted_iota(I32, (SUBLANES, LANES), 0)
    bias = bias_ref[...].reshape(nv, SUBLANES, LANES)
    neg_inf = jnp.float32(-jnp.inf)

    for st in range(n_strips):
        lanes = slice(st * LANES, (st + 1) * LANES)
        scores = _sigmoid(lg_ref[:, lanes]).reshape(nv, SUBLANES, LANES)
        biased = scores + bias
        gscore = []
        for g in range(N_GROUPS):
            m1 = biased[g * gv]
            m2 = jnp.full((SUBLANES, LANES), neg_inf, F32)
            for t in range(1, gv):
                v = biased[g * gv + t]
                m2 = jnp.maximum(m2, jnp.minimum(m1, v))
                m1 = jnp.maximum(m1, v)
            for s in (4, 2, 1):
                p1 = pltpu.roll(m1, s, 0)
                p2 = pltpu.roll(m2, s, 0)
                m2 = jnp.maximum(jnp.minimum(m1, p1), jnp.maximum(m2, p2))
                m1 = jnp.maximum(m1, p1)
            gscore.append(m1 + m2)
        masked = []
        for g in range(N_GROUPS):
            beaten = jnp.zeros((SUBLANES, LANES), I32)
            for o in range(N_GROUPS):
                if o == g:
                    continue
                wins = (gscore[o] > gscore[g]) | ((gscore[o] == gscore[g]) & (o < g))
                beaten = beaten + wins.astype(I32)
            keep = beaten < TOPK_GROUPS
            for t in range(gv):
                masked.append(jnp.where(keep, biased[g * gv + t], neg_inf))
        cand = jnp.stack(masked, axis=0)
        sel = jnp.zeros((nv, SUBLANES, LANES), jnp.bool_)
        picks, pick_scores = [], []
        for _ in range(TOP_K):
            mx = _sub_allreduce(jnp.max(cand, axis=0), jnp.maximum)
            idx = _sub_allreduce(jnp.min(jnp.where(cand == mx, row, N_EXPERTS), axis=0), jnp.minimum)
            hit = row == idx
            pick_scores.append(_sub_allreduce(jnp.sum(jnp.where(hit, scores, 0.0), axis=0), jnp.add))
            picks.append(idx)
            sel = sel | hit
            cand = jnp.where(hit, neg_inf, cand)
        sel_b = sel.astype(F32).astype(BF16).reshape(N_EXPERTS, LANES)
        before = jnp.dot(sel_b, tri_ref[0], preferred_element_type=F32)
        total = jnp.dot(sel_b, tri_ref[1], preferred_element_type=F32)
        rank_all = (before + run_sc[...]).reshape(nv, SUBLANES, LANES)
        run_sc[...] = run_sc[...] + total
        denom = pick_scores[0]
        for kk in range(1, TOP_K):
            denom = denom + pick_scores[kk]
        e_out = jnp.zeros((SUBLANES, LANES), I32)
        w_out = jnp.zeros((SUBLANES, LANES), F32)
        r_out = jnp.zeros((SUBLANES, LANES), I32)
        for kk in range(TOP_K):
            rk = _sub_allreduce(jnp.sum(jnp.where(row == picks[kk], rank_all, 0.0), axis=0), jnp.add)
            e_out = jnp.where(sub == kk, picks[kk], e_out)
            w_out = jnp.where(sub == kk, pick_scores[kk] / denom * ROUTED_SCALE, w_out)
            r_out = jnp.where(sub == kk, rk.astype(I32), r_out)
        e_ref[:, lanes] = e_out
        w_ref[:, lanes] = w_out
        r_ref[:, lanes] = r_out

    cnt_ref[...] = run_sc[...].astype(I32)


def _route(logits_t, bias, tb):
    n = logits_t.shape[1]
    iota_r = lax.broadcasted_iota(I32, (LANES, LANES), 0)
    iota_c = lax.broadcasted_iota(I32, (LANES, LANES), 1)
    tri = jnp.stack([(iota_r < iota_c), jnp.ones((LANES, LANES), jnp.bool_)]).astype(BF16)
    bias_b = jnp.broadcast_to(bias.reshape(N_EXPERTS, 1), (N_EXPERTS, LANES)).astype(F32)
    tokrow = lambda dt: jax.ShapeDtypeStruct((TOP_K, n), dt)
    return pl.pallas_call(
        _route_kernel,
        grid=(n // tb,),
        in_specs=[pl.BlockSpec((N_EXPERTS, tb), lambda i: (0, i)),
                  pl.BlockSpec((N_EXPERTS, LANES), lambda i: (0, 0)),
                  pl.BlockSpec((2, LANES, LANES), lambda i: (0, 0, 0))],
        out_specs=[pl.BlockSpec((TOP_K, tb), lambda i: (0, i)),
                   pl.BlockSpec((TOP_K, tb), lambda i: (0, i)),
                   pl.BlockSpec((TOP_K, tb), lambda i: (0, i)),
                   pl.BlockSpec((N_EXPERTS, LANES), lambda i: (0, 0))],
        out_shape=[tokrow(I32), tokrow(F32), tokrow(I32),
                   jax.ShapeDtypeStruct((N_EXPERTS, LANES), I32)],
        scratch_shapes=[pltpu.VMEM((N_EXPERTS, LANES), F32)],
        compiler_params=_params(("arbitrary",)),
        name="route",
    )(logits_t, bias_b, tri)


def _slots_kernel(pstart_ref, e_ref, r_ref, o_ref):
    e = e_ref[...]

    def body(x, acc):
        return acc + jnp.where(e == x, pstart_ref[x], 0)

    o_ref[...] = lax.fori_loop(0, N_EXPERTS, body, r_ref[...])


def _slots(pstart, e_idx, rank, tb):
    n = e_idx.shape[1]
    blk = lambda: pl.BlockSpec((TOP_K, tb), lambda i, ps: (0, i))
    return pl.pallas_call(
        _slots_kernel,
        grid_spec=pltpu.PrefetchScalarGridSpec(
            num_scalar_prefetch=1, grid=(n // tb,), in_specs=[blk(), blk()], out_specs=blk()),
        out_shape=jax.ShapeDtypeStruct((TOP_K, n), I32),
        compiler_params=_params(("arbitrary",)),
        name="slots",
    )(pstart, e_idx, rank)


def _sc_mesh():
    return plsc.VectorSubcoreMesh(core_axis_name="c", subcore_axis_name="s")


def _sc_worker():
    return lax.axis_index("c") * SC_SUBCORES + lax.axis_index("s")


def _sc_dispatch(hx, pos, cap):
    n, w = hx.shape
    per_worker = n // SC_WINDOW // SC_WORKERS
    assert per_worker * SC_WINDOW * SC_WORKERS == n

    @pl.kernel(out_type=jax.ShapeDtypeStruct((cap, w), hx.dtype), mesh=_sc_mesh(),
               scratch_types=[pltpu.VMEM((SC_WINDOW, w), hx.dtype), pltpu.VMEM((TOP_K, SC_WINDOW), I32),
                              pltpu.SemaphoreType.DMA])
    def scatter_rows(x_hbm, i_hbm, o_hbm, xbuf, ibuf, sem):
        wid = _sc_worker()

        @pl.loop(0, per_worker)
        def _(j):
            row0 = (wid * per_worker + j) * SC_WINDOW
            pltpu.sync_copy(x_hbm.at[pl.ds(row0, SC_WINDOW)], xbuf)
            pltpu.sync_copy(i_hbm.at[:, pl.ds(row0, SC_WINDOW)], ibuf)
            copies = [pltpu.async_copy(xbuf, o_hbm.at[ibuf.at[kk]], sem) for kk in range(TOP_K)]
            for cp in copies:
                cp.wait()

    return scatter_rows(hx, pos)


def _sc_gather(ys, pos):
    n = pos.shape[1]
    w = ys.shape[1]
    per_worker = n // SC_WINDOW // SC_WORKERS
    assert per_worker * SC_WINDOW * SC_WORKERS == n

    @pl.kernel(out_type=jax.ShapeDtypeStruct((TOP_K, n, w), ys.dtype), mesh=_sc_mesh(),
               scratch_types=[pltpu.VMEM((SC_WINDOW, w), ys.dtype), pltpu.VMEM((TOP_K, SC_WINDOW), I32)])
    def gather_rows(y_hbm, i_hbm, o_hbm, ybuf, ibuf):
        wid = _sc_worker()

        @pl.loop(0, per_worker)
        def _(j):
            row0 = (wid * per_worker + j) * SC_WINDOW
            pltpu.sync_copy(i_hbm.at[:, pl.ds(row0, SC_WINDOW)], ibuf)
            for kk in range(TOP_K):
                pltpu.sync_copy(y_hbm.at[ibuf.at[kk]], ybuf)
                pltpu.sync_copy(ybuf, o_hbm.at[kk, pl.ds(row0, SC_WINDOW)])

    return gather_rows(ys, pos)


def _experts_kernel(blk_e_ref, valid_ref, blk_in_ref, blk_out_ref, next_e_ref, slot_ref,
                    xs_ref, wgu_hbm, wdn_hbm, ys_ref, gu_buf, dn_buf, sem, wgu_sc, wdn_sc):
    del blk_in_ref, blk_out_ref
    i = pl.program_id(0)
    valid = valid_ref[i]

    def weight_copies(e, slot):
        return (pltpu.make_async_copy(wgu_hbm.at[e], gu_buf.at[slot], sem.at[0, slot]),
                pltpu.make_async_copy(wdn_hbm.at[e], dn_buf.at[slot], sem.at[1, slot]))

    @pl.when(valid > 0)
    def _():
        e = blk_e_ref[i]
        slot = slot_ref[i]
        prev = blk_e_ref[jnp.maximum(i - 1, 0)]

        @pl.when(i == 0)
        def _():
            for cp in weight_copies(e, slot):
                cp.start()

        @pl.when((i == 0) | (e != prev))
        def _():
            for cp in weight_copies(e, slot):
                cp.wait()
            nxt = next_e_ref[i]

            @pl.when(nxt >= 0)
            def _():
                for cp in weight_copies(nxt, 1 - slot):
                    cp.start()

            wgu_sc[...] = gu_buf[slot].astype(BF16)
            wdn_sc[...] = dn_buf[slot].astype(BF16)

        rows = lax.broadcasted_iota(I32, xs_ref.shape, 0)
        xu = jnp.where(rows < valid, pltpu.bitcast(xs_ref[...], U32), jnp.uint32(0))
        lo, hi = _unpack_bf16_pair(xu)
        half = lo.shape[1]
        gu = (jnp.dot(lo.astype(BF16), wgu_sc[:half, :], preferred_element_type=F32)
              + jnp.dot(hi.astype(BF16), wgu_sc[half:, :], preferred_element_type=F32))
        gt = gu[:, :EXPERT_HIDDEN]
        act = (gt * _sigmoid(gt) * gu[:, EXPERT_HIDDEN:]).astype(BF16)
        y = jnp.dot(act, wdn_sc[...], preferred_element_type=F32)
        ys_ref[...] = pltpu.bitcast(_pack_bf16_pair(y[:, :half], y[:, half:]), I32)

    @pl.when(valid <= 0)
    def _():
        ys_ref[...] = jnp.zeros(ys_ref.shape, I32)


def _experts(blk_e, blk_valid, next_expert, expert_slot, xs, w_gu, w_dn):
    cap, w = xs.shape
    n_e, d, h2 = w_gu.shape
    n_blocks = cap // EXPERT_BLOCK
    blk_next = next_expert[blk_e].astype(I32)
    blk_slot = expert_slot[blk_e].astype(I32)
    step = jnp.arange(n_blocks, dtype=I32)
    n_used = jnp.sum((blk_valid > 0).astype(I32))
    blk_in = jnp.minimum(step, jnp.maximum(n_used - 1, 0)).astype(I32)
    blk_out = jnp.where(blk_valid > 0, step, n_blocks).astype(I32)
    return pl.pallas_call(
        _experts_kernel,
        grid_spec=pltpu.PrefetchScalarGridSpec(
            num_scalar_prefetch=6,
            grid=(n_blocks,),
            in_specs=[pl.BlockSpec((EXPERT_BLOCK, w), lambda i, be, bv, bi, bo, bn, bs: (bi[i], 0)),
                      pl.BlockSpec(memory_space=pl.ANY),
                      pl.BlockSpec(memory_space=pl.ANY)],
            out_specs=pl.BlockSpec((EXPERT_BLOCK, w), lambda i, be, bv, bi, bo, bn, bs: (bo[i], 0)),
            scratch_shapes=[pltpu.VMEM((2, d, h2), F32), pltpu.VMEM((2, h2 // 2, d), F32),
                            pltpu.SemaphoreType.DMA((2, 2)),
                            pltpu.VMEM((d, h2), BF16), pltpu.VMEM((h2 // 2, d), BF16)]),
        out_shape=jax.ShapeDtypeStruct((cap + EXPERT_BLOCK, w), I32),
        compiler_params=_params(("arbitrary",)),
        name="experts",
    )(blk_e, blk_valid, blk_in, blk_out, blk_next, blk_slot, xs, w_gu, w_dn)


def _combine_kernel(yg_ref, wt_ref, hx_ref, x1_ref, gt2_ref, wsg_ref, wsd_ref, fg_ref, *rest):
    o_ref = rest[-1]
    lo, hi = _unpack_bf16_pair(pltpu.bitcast(hx_ref[...], U32))
    half = lo.shape[1]
    gu = (jnp.dot(lo.astype(BF16), wsg_ref[:half, :], preferred_element_type=F32)
          + jnp.dot(hi.astype(BF16), wsg_ref[half:, :], preferred_element_type=F32))
    gt = gu[:, :SHARED_HIDDEN]
    act = (gt * _sigmoid(gt) * gu[:, SHARED_HIDDEN:]).astype(BF16)
    y = jnp.dot(act, wsd_ref[...], preferred_element_type=F32)
    y_lo = y[:, :half]
    y_hi = y[:, half:]
    for kk in range(TOP_K):
        r_lo, r_hi = _unpack_bf16_pair(pltpu.bitcast(yg_ref[kk], U32))
        wk = wt_ref[:, kk:kk + 1]
        y_lo = y_lo + wk * r_lo
        y_hi = y_hi + wk * r_hi
    x2_lo = x1_ref[:, :half] + gt2_ref[0, :, :half] * y_lo
    x2_hi = x1_ref[:, half:] + gt2_ref[0, :, half:] * y_hi
    ms = (jnp.sum(x2_lo * x2_lo, axis=-1, keepdims=True)
          + jnp.sum(x2_hi * x2_hi, axis=-1, keepdims=True)) / (2 * half)
    inv = lax.rsqrt(ms + EPS)
    o_ref[:, :half] = x2_lo * inv * fg_ref[:, :half]
    o_ref[:, half:] = x2_hi * inv * fg_ref[:, half:]


def _combine(yg, wt, hx, x1, gt2, w_sg, w_sd, fg, tm, tiles_per_batch, tile0, out_prev):
    n, w = hx.shape
    d = 2 * w
    full = lambda a: pl.BlockSpec(a.shape, lambda i: (0,) * a.ndim)
    in_specs = [pl.BlockSpec((TOP_K, tm, w), lambda i: (0, i, 0)),
                pl.BlockSpec((tm, TOP_K), lambda i: (tile0 + i, 0)),
                pl.BlockSpec((tm, w), lambda i: (tile0 + i, 0)),
                pl.BlockSpec((tm, d), lambda i: (tile0 + i, 0)),
                pl.BlockSpec((1, 1, d), lambda i: ((tile0 + i) // tiles_per_batch, 0, 0)),
                full(w_sg), full(w_sd), full(fg)]
    args = [yg, wt, hx, x1, gt2, w_sg, w_sd, fg]
    aliases = {}
    if out_prev is not None:
        in_specs.append(pl.BlockSpec(memory_space=pl.ANY))
        args.append(out_prev)
        aliases = {len(args) - 1: 0}
    return pl.pallas_call(
        _combine_kernel,
        grid=(yg.shape[1] // tm,),
        in_specs=in_specs,
        out_specs=pl.BlockSpec((tm, d), lambda i: (tile0 + i, 0)),
        out_shape=jax.ShapeDtypeStruct((n, d), F32),
        input_output_aliases=aliases,
        compiler_params=_params(("arbitrary",)),
        name="combine",
    )(*args)


def _rope_tables(seq):
    rows = seq // GRID_W
    pos_row = np.repeat(np.arange(rows, dtype=np.float32), GRID_W)
    pos_col = np.tile(np.arange(GRID_W, dtype=np.float32), rows)
    inv_freq = (ROPE_THETA ** (-np.arange(0, AXIS_DIM, 2, dtype=np.float32) / AXIS_DIM)).astype(np.float32)
    ar = pos_row[:, None] * inv_freq
    ac = pos_col[:, None] * inv_freq
    cos_t = np.concatenate([np.cos(ar), np.cos(ar), np.cos(ac), np.cos(ac)], axis=1)
    sin_t = np.concatenate([-np.sin(ar), np.sin(ar), -np.sin(ac), np.sin(ac)], axis=1)
    return jnp.asarray(cos_t, F32), jnp.asarray(sin_t, F32)


def _tile(n, want):
    t = min(n, want)
    assert n % t == 0, (n, want)
    return t


def kernel(x, c, ctx, c_ctx, w_mod, b_mod, norm1_g, w_in, q_norm_g, k_norm_g, w_dw, b_dw, conv_ln_g, conv_ln_b, w_attn_proj, w_conv_proj, w_out, norm2_g, w_router, router_bias, w_exp_gu, w_exp_dn, w_sh_gu, w_sh_dn, final_g):
    b, s, d = x.shape
    depth = w_mod.shape[0]
    assert depth == 1, "single-layer block"
    n = b * s
    row = lambda v: v.reshape(1, -1)

    cc = jnp.zeros((SUBLANES, d), F32).at[:b].set(c).at[b].set(c_ctx)
    mod = _modulation(cc, w_mod[0], row(b_mod[0]))
    mod_x = mod[:b].reshape(b, 1, 6, d)
    sh1, sc1, gt1, sh2, sc2, gt2 = [mod_x[:, :, j, :] for j in range(6)]
    mod_c = mod[b].reshape(6, d)
    csh1, csc1 = row(mod_c[0]), row(mod_c[1])

    w_in_b = w_in[0].astype(BF16)
    q_end, kv_end = ATTN_WIDTH, ATTN_WIDTH + 2 * KV_WIDTH
    k_end = q_end + KV_WIDTH
    w_vt = w_in_b[:, k_end:kv_end].T
    kc, vct = _ctx_kv(ctx, csh1, csc1, row(norm1_g[0]), w_in_b[:, q_end:k_end], w_vt, row(k_norm_g[0]))

    cos_t, sin_t = _rope_tables(s)
    q, kx, vxt, hglu, gates = _in_proj(x, sh1, sc1, row(norm1_g[0]), w_in_b, w_vt, row(q_norm_g[0]),
                                       row(k_norm_g[0]), cos_t, sin_t, _tile(s, 512))
    score_bound = (HEAD_DIM * ATTN_SCALE * LOG2E) * jnp.max(jnp.abs(q_norm_g[0])) * jnp.max(jnp.abs(k_norm_g[0]))
    bounded = (score_bound <= SAFE_EXP2_ARG).astype(I32).reshape(1)
    o = _attention(bounded, q, kc, vct, kx, vxt, _tile(s, 1024), _tile(s // 2, 1024))

    x1, hx, logits_t = _merge(o, hglu, gates, x, gt1, sh2, sc2, w_dw[0], row(b_dw[0]), row(conv_ln_g[0]),
                              row(conv_ln_b[0]), w_attn_proj[0].astype(BF16), w_conv_proj[0].astype(BF16),
                              w_out[0].astype(BF16), row(norm2_g[0]), w_router[0].T, _tile(s, 256))

    e_idx, wts, rank, counts = _route(logits_t, router_bias[0], _tile(n, 512))

    cnt = counts[:, 0]
    padded = (cnt + EXPERT_BLOCK - 1) // EXPERT_BLOCK * EXPERT_BLOCK
    pends = jnp.cumsum(padded)
    pstart = (pends - padded).astype(I32)
    n_blocks = (n * TOP_K + N_EXPERTS * (EXPERT_BLOCK - 1)) // EXPERT_BLOCK
    cap = n_blocks * EXPERT_BLOCK
    blk_row0 = jnp.arange(n_blocks, dtype=I32) * EXPERT_BLOCK
    blk_e = jnp.minimum(jnp.sum(pends[None, :] <= blk_row0[:, None], axis=1), N_EXPERTS - 1).astype(I32)
    blk_valid = jnp.clip(pstart[blk_e] + cnt[blk_e] - blk_row0, 0, EXPERT_BLOCK).astype(I32)

    pos = _slots(pstart, e_idx, rank, _tile(n, 4096))
    xs = _sc_dispatch(hx, pos, cap)
    has_rows = cnt > 0
    later = lax.cummin(jnp.where(has_rows, jnp.arange(N_EXPERTS, dtype=I32), N_EXPERTS), reverse=True)
    next_expert = jnp.concatenate([later[1:], jnp.full((1,), N_EXPERTS, I32)])
    next_expert = jnp.where(next_expert >= N_EXPERTS, -1, next_expert)
    expert_slot = (jnp.cumsum(has_rows.astype(I32)) - 1) % 2
    ys = _experts(blk_e, blk_valid, next_expert, expert_slot, xs, w_exp_gu[0], w_exp_dn[0])
    tm = _tile(n, 256)
    chunk = n // COMBINE_CHUNKS if n % (COMBINE_CHUNKS * SC_WINDOW * SC_WORKERS) == 0 else n
    wt_t, x1_2d = wts.T, x1.reshape(n, d)
    w_sg, w_sd = w_sh_gu[0].astype(BF16), w_sh_dn[0].astype(BF16)
    out = None
    for c0 in range(0, n, chunk):
        yg = _sc_gather(ys, pos[:, c0:c0 + chunk])
        out = _combine(yg, wt_t, hx, x1_2d, gt2, w_sg, w_sd, row(final_g), tm, s // tm, c0 // tm, out)
    return out.reshape(b, s, d)
```

```python
import functools
import math

import jax
import jax.numpy as jnp
import numpy as np
from jax import lax
from jax.experimental import pallas as pl
from jax.experimental.pallas import tpu as pltpu
from jax.experimental.pallas import tpu_sc as plsc

F32 = jnp.float32
BF16 = jnp.bfloat16
U32 = jnp.uint32
I32 = jnp.int32

GRID_W = 64
N_HEADS = 8
N_KV_HEADS = 4
GROUP = N_HEADS // N_KV_HEADS
HEAD_DIM = 128
AXIS_DIM = HEAD_DIM // 2
ATTN_WIDTH = N_HEADS * HEAD_DIM
KV_WIDTH = N_KV_HEADS * HEAD_DIM
ROPE_THETA = 10000.0
ATTN_SCALE = HEAD_DIM ** -0.5
CONV_WIDTH = 512
CONV_KERNEL = 31
CONV_PAD = CONV_KERNEL // 2
N_EXPERTS = 256
TOP_K = 8
N_GROUPS = 8
TOPK_GROUPS = 4
EXPERTS_PER_GROUP = N_EXPERTS // N_GROUPS
EXPERT_HIDDEN = 256
SHARED_HIDDEN = 256
ROUTED_SCALE = 2.5
EPS = 1e-6
LOG2E = 1.4426950408889634
SAFE_EXP2_ARG = 64.0

LANES = 128
SUBLANES = 8
VMEM_LIMIT_BYTES = 56 * 1024 * 1024

HALO_ROWS = 16
CONV_ROWS = 64
MERGE_PARTS = 2
EXPERT_BLOCK = 512
SC_SUBCORES = 16
SC_WORKERS = 2 * SC_SUBCORES
SC_WINDOW = 128
COMBINE_CHUNKS = 4
HIGHEST = lax.Precision.HIGHEST


def _params(sem):
    return pltpu.CompilerParams(dimension_semantics=sem, vmem_limit_bytes=VMEM_LIMIT_BYTES)


def _sigmoid(x):
    return 1.0 / (1.0 + jnp.exp(-x))


def _pack_bf16_pair(lo, hi):
    lo_b = pltpu.bitcast(lo.astype(BF16).astype(F32), U32)
    hi_b = pltpu.bitcast(hi.astype(BF16).astype(F32), U32)
    return (lo_b >> 16) | (hi_b & jnp.uint32(0xFFFF0000))


def _unpack_bf16_pair(u):
    lo = pltpu.bitcast(u << 16, F32)
    hi = pltpu.bitcast(u & jnp.uint32(0xFFFF0000), F32)
    return lo, hi


def _mod_kernel(cc_ref, w_ref, b_ref, o_ref):
    cc = cc_ref[...]
    s = cc * _sigmoid(cc)
    o_ref[...] = jnp.dot(s, w_ref[...], precision=HIGHEST, preferred_element_type=F32) + b_ref[...]


def _modulation(cc, w_mod, b_mod):
    d, n = w_mod.shape
    tn = n // 4
    return pl.pallas_call(
        _mod_kernel,
        grid=(n // tn,),
        in_specs=[pl.BlockSpec((SUBLANES, d), lambda j: (0, 0)),
                  pl.BlockSpec((d, tn), lambda j: (0, j)),
                  pl.BlockSpec((1, tn), lambda j: (0, j))],
        out_specs=pl.BlockSpec((SUBLANES, tn), lambda j: (0, j)),
        out_shape=jax.ShapeDtypeStruct((SUBLANES, n), F32),
        compiler_params=_params(("arbitrary",)),
        name="mod",
    )(cc, w_mod, b_mod)


def _norm_modulate(x, g, sh, sc):
    ms = jnp.mean(x * x, axis=-1, keepdims=True)
    return (x * lax.rsqrt(ms + EPS) * g) * (1.0 + sc) + sh


def _head_norm(p, gain):
    r = lax.rsqrt(jnp.mean(p * p, axis=-1, keepdims=True) + EPS)
    return p * r * gain


def _dot_nt(a, b):
    return lax.dot_general(a, b, (((1,), (1,)), ((), ())), preferred_element_type=F32)


def _ctx_kv_kernel(x_ref, sh_ref, sc_ref, g1_ref, wk_ref, wvt_ref, gk_ref, k_ref, vt_ref):
    h = _norm_modulate(x_ref[0], g1_ref[...], sh_ref[...], sc_ref[...]).astype(BF16)
    pk = jnp.dot(h, wk_ref[...], preferred_element_type=F32)
    for j in range(N_KV_HEADS):
        sl = slice(j * HEAD_DIM, (j + 1) * HEAD_DIM)
        k_ref[0, :, sl] = _head_norm(pk[:, sl], gk_ref[...]).astype(BF16)
    vt_ref[0] = _dot_nt(wvt_ref[...], h).astype(BF16)


def _ctx_kv(ctx, csh, csc, g1, w_k, w_vt, gk):
    b, lc, d = ctx.shape
    vec = lambda: pl.BlockSpec((1, d), lambda i: (0, 0))
    return pl.pallas_call(
        _ctx_kv_kernel,
        grid=(b,),
        in_specs=[pl.BlockSpec((1, lc, d), lambda i: (i, 0, 0)), vec(), vec(), vec(),
                  pl.BlockSpec((d, KV_WIDTH), lambda i: (0, 0)),
                  pl.BlockSpec((KV_WIDTH, d), lambda i: (0, 0)),
                  pl.BlockSpec((1, HEAD_DIM), lambda i: (0, 0))],
        out_specs=[pl.BlockSpec((1, lc, KV_WIDTH), lambda i: (i, 0, 0)),
                   pl.BlockSpec((1, KV_WIDTH, lc), lambda i: (i, 0, 0))],
        out_shape=[jax.ShapeDtypeStruct((b, lc, KV_WIDTH), BF16),
                   jax.ShapeDtypeStruct((b, KV_WIDTH, lc), BF16)],
        compiler_params=_params(("arbitrary",)),
        name="ctx_kv",
    )(ctx, csh, csc, g1, w_k, w_vt, gk)


def _in_proj_kernel(x_ref, sh_ref, sc_ref, g1_ref, w_ref, wvt_ref, gq_ref, gk_ref, cos_ref, sin_ref,
                    q_ref, k_ref, vt_ref, h_ref, g_ref):
    h = _norm_modulate(x_ref[0], g1_ref[...], sh_ref[0], sc_ref[0]).astype(BF16)
    cos = cos_ref[...]
    sin = sin_ref[...]
    lane = lax.broadcasted_iota(I32, cos.shape, 1)
    upper = (lane & (AXIS_DIM // 2)) != 0

    def rope(p):
        swapped = jnp.where(upper, pltpu.roll(p, AXIS_DIM // 2, 1),
                            pltpu.roll(p, HEAD_DIM - AXIS_DIM // 2, 1))
        return p * cos + swapped * sin

    q_end = ATTN_WIDTH
    k_end = q_end + KV_WIDTH
    v_end = k_end + KV_WIDTH
    u_end = v_end + 2 * CONV_WIDTH
    pq = jnp.dot(h, w_ref[:, :q_end], preferred_element_type=F32)
    for j in range(N_HEADS):
        sl = slice(j * HEAD_DIM, (j + 1) * HEAD_DIM)
        q_ref[0, :, sl] = (rope(_head_norm(pq[:, sl], gq_ref[...])) * (ATTN_SCALE * LOG2E)).astype(BF16)
    pk = jnp.dot(h, w_ref[:, q_end:k_end], preferred_element_type=F32)
    for j in range(N_KV_HEADS):
        sl = slice(j * HEAD_DIM, (j + 1) * HEAD_DIM)
        k_ref[0, :, sl] = rope(_head_norm(pk[:, sl], gk_ref[...])).astype(BF16)
    vt_ref[0] = _dot_nt(wvt_ref[...], h).astype(BF16)
    u = jnp.dot(h, w_ref[:, v_end:u_end], preferred_element_type=F32)
    h_ref[0] = (u[:, :CONV_WIDTH] * _sigmoid(u[:, CONV_WIDTH:])).astype(BF16)
    g_ref[0] = _sigmoid(jnp.dot(h, w_ref[:, u_end:], preferred_element_type=F32)).astype(BF16)


def _in_proj(x, sh1, sc1, g1, w_in, w_vt, gq, gk, cos_t, sin_t, tm):
    b, s, d = x.shape
    n_in = w_in.shape[1]
    bvec = lambda: pl.BlockSpec((1, 1, d), lambda bi, i: (bi, 0, 0))
    tok = lambda w: pl.BlockSpec((1, tm, w), lambda bi, i: (bi, i, 0))
    return pl.pallas_call(
        _in_proj_kernel,
        grid=(b, s // tm),
        in_specs=[tok(d), bvec(), bvec(),
                  pl.BlockSpec((1, d), lambda bi, i: (0, 0)),
                  pl.BlockSpec((d, n_in), lambda bi, i: (0, 0)),
                  pl.BlockSpec((KV_WIDTH, d), lambda bi, i: (0, 0)),
                  pl.BlockSpec((1, HEAD_DIM), lambda bi, i: (0, 0)),
                  pl.BlockSpec((1, HEAD_DIM), lambda bi, i: (0, 0)),
                  pl.BlockSpec((tm, HEAD_DIM), lambda bi, i: (i, 0)),
                  pl.BlockSpec((tm, HEAD_DIM), lambda bi, i: (i, 0))],
        out_specs=[tok(ATTN_WIDTH), tok(KV_WIDTH),
                   pl.BlockSpec((1, KV_WIDTH, tm), lambda bi, i: (bi, 0, i)),
                   tok(CONV_WIDTH), tok(2 * d)],
        out_shape=[jax.ShapeDtypeStruct((b, s, ATTN_WIDTH), BF16),
                   jax.ShapeDtypeStruct((b, s, KV_WIDTH), BF16),
                   jax.ShapeDtypeStruct((b, KV_WIDTH, s), BF16),
                   jax.ShapeDtypeStruct((b, s, CONV_WIDTH), BF16),
                   jax.ShapeDtypeStruct((b, s, 2 * d), BF16)],
        compiler_params=_params(("arbitrary", "arbitrary")),
        name="in_proj",
    )(x, sh1, sc1, g1, w_in, w_vt, gq, gk, cos_t, sin_t)


def _sub_allreduce(x, op):
    for s in (4, 2, 1):
        x = op(x, pltpu.roll(x, s, 0))
    return x


def _attn_kernel(bounded_ref, q_ref, kc_ref, vct_ref, k_ref, vt_ref, o_ref, s0, s1, x0, x1, m_sc, l_sc, acc_sc,
                 *, tk):
    tq = q_ref.shape[1]
    m_cols = GROUP * tq
    nk = k_ref.shape[1] // tk
    qf = q_ref[0].astype(F32).T
    qt = jnp.concatenate([qf[:HEAD_DIM], qf[HEAD_DIM:]], axis=1).astype(BF16)
    slots = ((s0, x0), (s1, x1))

    def kchunk(j):
        return k_ref[0, pl.ds(pl.multiple_of(j * tk, tk), tk), :]

    def vchunk(j):
        return vt_ref[0, :, pl.ds(pl.multiple_of(j * tk, tk), tk)]

    def split(st):
        return st.reshape(st.shape[0] // SUBLANES, SUBLANES, m_cols)

    def scores(k, online):
        st = jnp.dot(k, qt, preferred_element_type=F32)
        return st, (jnp.max(split(st), axis=0) if online else None)

    def absorb(st, mx, vt, online):
        s3 = split(st)
        if online:
            m_prev = m_sc[...]
            m_new = jnp.maximum(m_prev, _sub_allreduce(mx, jnp.maximum))
            alpha = jnp.exp2(m_prev - m_new)
            p3 = jnp.exp2(s3 - m_new[None])
            l_sc[...] = alpha * l_sc[...] + _sub_allreduce(jnp.sum(p3, axis=0), jnp.add)
            pv = jnp.dot(vt, p3.reshape(st.shape).astype(BF16), preferred_element_type=F32)
            acc_sc[...] = alpha[0:1] * acc_sc[...] + pv
            m_sc[...] = m_new
        else:
            p3 = jnp.exp2(s3)
            l_sc[...] = l_sc[...] + jnp.sum(p3, axis=0)
            acc_sc[...] = acc_sc[...] + jnp.dot(vt, p3.reshape(st.shape).astype(BF16),
                                                preferred_element_type=F32)

    def stage(slot, k, online):
        st, mx = scores(k, online)
        slots[slot][0][...] = st
        if online:
            slots[slot][1][...] = mx

    def take(slot, vt, online):
        absorb(slots[slot][0][...], slots[slot][1][...] if online else None, vt, online)

    def sweep(online):
        if online:
            m_sc[...] = jnp.full(m_sc.shape, -jnp.inf, F32)
        l_sc[...] = jnp.zeros(l_sc.shape, F32)
        acc_sc[...] = jnp.zeros(acc_sc.shape, F32)
        stage(0, kchunk(0), online)

        def body(i, carry):
            j = 2 * i
            stage(1, kchunk(j + 1), online)
            take(0, vchunk(j), online)
            stage(0, kchunk(j + 2), online)
            take(1, vchunk(j + 1), online)
            return carry

        lax.fori_loop(0, nk // 2 - 1, body, 0)
        stage(1, kchunk(nk - 1), online)
        take(0, vchunk(nk - 2), online)
        sc, xc = scores(kc_ref[0], online)
        take(1, vchunk(nk - 1), online)
        absorb(sc, xc, vct_ref[0], online)
        denom = l_sc[...] if online else _sub_allreduce(l_sc[...], jnp.add)
        o = (acc_sc[...] / denom[0:1]).T
        o_ref[0, :, :HEAD_DIM] = o[:tq].astype(BF16)
        o_ref[0, :, HEAD_DIM:] = o[tq:].astype(BF16)

    @pl.when(bounded_ref[0] != 0)
    def _():
        sweep(online=False)

    @pl.when(bounded_ref[0] == 0)
    def _():
        sweep(online=True)


def _attention(bounded, q, kc, vct, kx, vxt, tq, tk):
    b, s, _ = q.shape
    lc = kc.shape[1]
    assert s % (2 * tk) == 0
    gw = GROUP * HEAD_DIM
    m_cols = GROUP * tq
    kv = lambda l: pl.BlockSpec((1, l, HEAD_DIM), lambda bi, h, i, bd: (bi, 0, h))
    kvt = lambda l: pl.BlockSpec((1, HEAD_DIM, l), lambda bi, h, i, bd: (bi, h, 0))
    qo = lambda: pl.BlockSpec((1, tq, gw), lambda bi, h, i, bd: (bi, i, h))
    return pl.pallas_call(
        functools.partial(_attn_kernel, tk=tk),
        grid_spec=pltpu.PrefetchScalarGridSpec(
            num_scalar_prefetch=1,
            grid=(b, N_KV_HEADS, s // tq),
            in_specs=[qo(), kv(lc), kvt(lc), kv(s), kvt(s)],
            out_specs=qo(),
            scratch_shapes=[pltpu.VMEM((tk, m_cols), F32), pltpu.VMEM((tk, m_cols), F32),
                            pltpu.VMEM((SUBLANES, m_cols), F32), pltpu.VMEM((SUBLANES, m_cols), F32),
                            pltpu.VMEM((SUBLANES, m_cols), F32), pltpu.VMEM((SUBLANES, m_cols), F32),
                            pltpu.VMEM((HEAD_DIM, m_cols), F32)]),
        out_shape=jax.ShapeDtypeStruct((b, s, ATTN_WIDTH), BF16),
        compiler_params=_params(("arbitrary", "arbitrary", "arbitrary")),
        name="attn",
    )(bounded, q, kc, vct, kx, vxt)


def _split3(x):
    x1 = x.astype(BF16)
    r1 = x - x1.astype(F32)
    x2 = r1.astype(BF16)
    x3 = (r1 - x2.astype(F32)).astype(BF16)
    return x1, x2, x3


def _merge_kernel(o_ref, hp_ref, hc_ref, hn_ref, g_ref, x_ref, gt1_ref, sh2_ref, sc2_ref,
                  wdw_ref, bdw_ref, lng_ref, lnb_ref, wap_ref, wcp_ref, wout_ref, g2_ref, wrt_ref,
                  x1_ref, hx_ref, lg_ref, hcat, *part_scratch):
    i = pl.program_id(1)
    n_i = pl.num_programs(1)
    tm = hc_ref.shape[1]
    d = x_ref.shape[2]
    half = d // 2
    prev = hp_ref[0].astype(F32)
    nxt = hn_ref[0].astype(F32)
    hcat[0:HALO_ROWS, :] = jnp.where(i > 0, prev, jnp.zeros_like(prev))
    hcat[HALO_ROWS:HALO_ROWS + tm, :] = hc_ref[0].astype(F32)
    hcat[HALO_ROWS + tm:, :] = jnp.where(i < n_i - 1, nxt, jnp.zeros_like(nxt))
    base = HALO_ROWS - CONV_PAD
    reach = (base + CONV_KERNEL - 1) // SUBLANES * SUBLANES

    rows_per_part = tm // MERGE_PARTS
    for part in range(MERGE_PARTS):
        act_sc, shift_sc = part_scratch[2 * part], part_scratch[2 * part + 1]
        p0 = part * rows_per_part
        rows = slice(p0, p0 + rows_per_part)
        for r0 in range(0, rows_per_part, CONV_ROWS):
            acc = jnp.zeros((CONV_ROWS, CONV_WIDTH), F32) + bdw_ref[...]
            for res in range(SUBLANES):
                shift_sc[res] = hcat[p0 + r0 + res:p0 + r0 + res + CONV_ROWS + reach, :]
                for off in range(res, base + CONV_KERNEL, SUBLANES):
                    j = off - base
                    if 0 <= j < CONV_KERNEL:
                        a0 = off - res
                        acc = acc + shift_sc[res, a0:a0 + CONV_ROWS, :] * wdw_ref[j:j + 1, :]
            mu = jnp.mean(acc, axis=-1, keepdims=True)
            cen = acc - mu
            var = jnp.mean(cen * cen, axis=-1, keepdims=True)
            ln = cen * lax.rsqrt(var + EPS) * lng_ref[...] + lnb_ref[...]
            act_sc[r0:r0 + CONV_ROWS, :] = (ln * _sigmoid(ln)).astype(BF16)
        y_conv = jnp.dot(act_sc[...], wcp_ref[...], preferred_element_type=F32)
        y_attn = jnp.dot(o_ref[0, rows, :], wap_ref[...], preferred_element_type=F32)
        z = g_ref[0, rows, :d].astype(F32) * y_attn + g_ref[0, rows, d:].astype(F32) * y_conv
        mix = jnp.dot(z.astype(BF16), wout_ref[...], preferred_element_type=F32)
        x1 = x_ref[0, rows, :] + gt1_ref[0] * mix
        x1_ref[0, rows, :] = x1
        hx = _norm_modulate(x1, g2_ref[...], sh2_ref[0], sc2_ref[0])
        hx_ref[rows, :] = pltpu.bitcast(_pack_bf16_pair(hx[:, :half], hx[:, half:]), I32)
        h1, h2, h3 = _split3(hx)
        w1, w2, w3 = wrt_ref[0], wrt_ref[1], wrt_ref[2]
        lg_ref[:, rows] = (((_dot_nt(w3, h1) + _dot_nt(w1, h3)) + _dot_nt(w2, h2))
                           + (_dot_nt(w2, h1) + _dot_nt(w1, h2))) + _dot_nt(w1, h1)


def _merge(o, hglu, g, x, gt1, sh2, sc2, w_dw, b_dw, ln_g, ln_b, w_ap, w_cp, w_out, g2, w_rt, tm):
    b, s, d = x.shape
    nt = s // tm
    hb = tm // HALO_ROWS
    n_halo = s // HALO_ROWS
    bvec = lambda: pl.BlockSpec((1, 1, d), lambda bi, i: (bi, 0, 0))
    full = lambda a: pl.BlockSpec(a.shape, lambda bi, i: (0,) * a.ndim)
    tok = lambda w: pl.BlockSpec((1, tm, w), lambda bi, i: (bi, i, 0))
    return pl.pallas_call(
        _merge_kernel,
        grid=(b, nt),
        in_specs=[tok(ATTN_WIDTH),
                  pl.BlockSpec((1, HALO_ROWS, CONV_WIDTH), lambda bi, i: (bi, jnp.maximum(i * hb - 1, 0), 0)),
                  tok(CONV_WIDTH),
                  pl.BlockSpec((1, HALO_ROWS, CONV_WIDTH),
                               lambda bi, i: (bi, jnp.minimum((i + 1) * hb, n_halo - 1), 0)),
                  tok(2 * d), tok(d), bvec(), bvec(), bvec(),
                  full(w_dw), full(b_dw), full(ln_g), full(ln_b), full(w_ap), full(w_cp), full(w_out),
                  full(g2), full(w_rt)],
        out_specs=[tok(d),
                   pl.BlockSpec((tm, d // 2), lambda bi, i: (bi * nt + i, 0)),
                   pl.BlockSpec((N_EXPERTS, tm), lambda bi, i: (0, bi * nt + i))],
        out_shape=[jax.ShapeDtypeStruct((b, s, d), F32),
                   jax.ShapeDtypeStruct((b * s, d // 2), I32),
                   jax.ShapeDtypeStruct((N_EXPERTS, b * s), F32)],
        scratch_shapes=[pltpu.VMEM((tm + 2 * HALO_ROWS, CONV_WIDTH), F32)]
        + [pltpu.VMEM((tm // MERGE_PARTS, CONV_WIDTH), BF16),
           pltpu.VMEM((SUBLANES, CONV_ROWS + (CONV_KERNEL + HALO_ROWS - CONV_PAD - 1) // SUBLANES * SUBLANES,
                       CONV_WIDTH), F32)] * MERGE_PARTS,
        compiler_params=_params(("arbitrary", "arbitrary")),
        name="merge",
    )(o, hglu, hglu, hglu, g, x, gt1, sh2, sc2, w_dw, b_dw, ln_g, ln_b, w_ap, w_cp, w_out, g2, w_rt)


def _route_kernel(lg_ref, bias_ref, tri_ref, e_ref, w_ref, r_ref, cnt_ref, run_sc):
    step = pl.program_id(0)
    n_strips = lg_ref.shape[1] // LANES
    nv = N_EXPERTS // SUBLANES
    gv = EXPERTS_PER_GROUP // SUBLANES

    @pl.when(step == 0)
    def _():
        run_sc[...] = jnp.zeros(run_sc.shape, F32)

    row = (lax.broadcasted_iota(I32, (nv, SUBLANES, LANES), 0) * SUBLANES
           + lax.broadcasted_iota(I32, (nv, SUBLANES, LANES), 1))
    sub = lax.broadcasted_iota(I32, (SUBLANES, LANES), 0)
    bias = bias_ref[...].reshape(nv, SUBLANES, LANES)
    neg_inf = jnp.float32(-jnp.inf)

    for st in range(n_strips):
        lanes = slice(st * LANES, (st + 1) * LANES)
        scores = _sigmoid(lg_ref[:, lanes]).reshape(nv, SUBLANES, LANES)
        biased = scores + bias
        gscore = []
        for g in range(N_GROUPS):
            m1 = biased[g * gv]
            m2 = jnp.full((SUBLANES, LANES), neg_inf, F32)
            for t in range(1, gv):
                v = biased[g * gv + t]
                m2 = jnp.maximum(m2, jnp.minimum(m1, v))
                m1 = jnp.maximum(m1, v)
            for s in (4, 2, 1):
                p1 = pltpu.roll(m1, s, 0)
                p2 = pltpu.roll(m2, s, 0)
                m2 = jnp.maximum(jnp.minimum(m1, p1), jnp.maximum(m2, p2))
                m1 = jnp.maximum(m1, p1)
            gscore.append(m1 + m2)
        masked = []
        for g in range(N_GROUPS):
            beaten = jnp.zeros((SUBLANES, LANES), I32)
            for o in range(N_GROUPS):
                if o == g:
                    continue
                wins = (gscore[o] > gscore[g]) | ((gscore[o] == gscore[g]) & (o < g))
                beaten = beaten + wins.astype(I32)
            keep = beaten < TOPK_GROUPS
            for t in range(gv):
                masked.append(jnp.where(keep, biased[g * gv + t], neg_inf))
        cand = jnp.stack(masked, axis=0)
        sel = jnp.zeros((nv, SUBLANES, LANES), jnp.bool_)
        picks, pick_scores = [], []
        for _ in range(TOP_K):
            mx = _sub_allreduce(jnp.max(cand, axis=0), jnp.maximum)
            idx = _sub_allreduce(jnp.min(jnp.where(cand == mx, row, N_EXPERTS), axis=0), jnp.minimum)
            hit = row == idx
            pick_scores.append(_sub_allreduce(jnp.sum(jnp.where(hit, scores, 0.0), axis=0), jnp.add))
            picks.append(idx)
            sel = sel | hit
            cand = jnp.where(hit, neg_inf, cand)
        sel_b = sel.astype(F32).astype(BF16).reshape(N_EXPERTS, LANES)
        before = jnp.dot(sel_b, tri_ref[0], preferred_element_type=F32)
        total = jnp.dot(sel_b, tri_ref[1], preferred_element_type=F32)
        rank_all = (before + run_sc[...]).reshape(nv, SUBLANES, LANES)
        run_sc[...] = run_sc[...] + total
        denom = pick_scores[0]
        for kk in range(1, TOP_K):
            denom = denom + pick_scores[kk]
        e_out = jnp.zeros((SUBLANES, LANES), I32)
        w_out = jnp.zeros((SUBLANES, LANES), F32)
        r_out = jnp.zeros((SUBLANES, LANES), I32)
        for kk in range(TOP_K):
            rk = _sub_allreduce(jnp.sum(jnp.where(row == picks[kk], rank_all, 0.0), axis=0), jnp.add)
            e_out = jnp.where(sub == kk, picks[kk], e_out)
            w_out = jnp.where(sub == kk, pick_scores[kk] / denom * ROUTED_SCALE, w_out)
            r_out = jnp.where(sub == kk, rk.astype(I32), r_out)
        e_ref[:, lanes] = e_out
        w_ref[:, lanes] = w_out
        r_ref[:, lanes] = r_out

    cnt_ref[...] = run_sc[...].astype(I32)


def _route(logits_t, bias, tb):
    n = logits_t.shape[1]
    iota_r = lax.broadcasted_iota(I32, (LANES, LANES), 0)
    iota_c = lax.broadcasted_iota(I32, (LANES, LANES), 1)
    tri = jnp.stack([(iota_r < iota_c), jnp.ones((LANES, LANES), jnp.bool_)]).astype(BF16)
    bias_b = jnp.broadcast_to(bias.reshape(N_EXPERTS, 1), (N_EXPERTS, LANES)).astype(F32)
    tokrow = lambda dt: jax.ShapeDtypeStruct((TOP_K, n), dt)
    return pl.pallas_call(
        _route_kernel,
        grid=(n // tb,),
        in_specs=[pl.BlockSpec((N_EXPERTS, tb), lambda i: (0, i)),
                  pl.BlockSpec((N_EXPERTS, LANES), lambda i: (0, 0)),
                  pl.BlockSpec((2, LANES, LANES), lambda i: (0, 0, 0))],
        out_specs=[pl.BlockSpec((TOP_K, tb), lambda i: (0, i)),
                   pl.BlockSpec((TOP_K, tb), lambda i: (0, i)),
                   pl.BlockSpec((TOP_K, tb), lambda i: (0, i)),
                   pl.BlockSpec((N_EXPERTS, LANES), lambda i: (0, 0))],
        out_shape=[tokrow(I32), tokrow(F32), tokrow(I32),
                   jax.ShapeDtypeStruct((N_EXPERTS, LANES), I32)],
        scratch_shapes=[pltpu.VMEM((N_EXPERTS, LANES), F32)],
        compiler_params=_params(("arbitrary",)),
        name="route",
    )(logits_t, bias_b, tri)


def _slots_kernel(pstart_ref, e_ref, r_ref, o_ref):
    e = e_ref[...]

    def body(x, acc):
        return acc + jnp.where(e == x, pstart_ref[x], 0)

    o_ref[...] = lax.fori_loop(0, N_EXPERTS, body, r_ref[...])


def _slots(pstart, e_idx, rank, tb):
    n = e_idx.shape[1]
    blk = lambda: pl.BlockSpec((TOP_K, tb), lambda i, ps: (0, i))
    return pl.pallas_call(
        _slots_kernel,
        grid_spec=pltpu.PrefetchScalarGridSpec(
            num_scalar_prefetch=1, grid=(n // tb,), in_specs=[blk(), blk()], out_specs=blk()),
        out_shape=jax.ShapeDtypeStruct((TOP_K, n), I32),
        compiler_params=_params(("arbitrary",)),
        name="slots",
    )(pstart, e_idx, rank)


def _sc_mesh():
    return plsc.VectorSubcoreMesh(core_axis_name="c", subcore_axis_name="s")


def _sc_worker():
    return lax.axis_index("c") * SC_SUBCORES + lax.axis_index("s")


def _sc_dispatch(hx, pos, cap):
    n, w = hx.shape
    per_worker = n // SC_WINDOW // SC_WORKERS
    assert per_worker * SC_WINDOW * SC_WORKERS == n

    @pl.kernel(out_type=jax.ShapeDtypeStruct((cap, w), hx.dtype), mesh=_sc_mesh(),
               scratch_types=[pltpu.VMEM((SC_WINDOW, w), hx.dtype), pltpu.VMEM((TOP_K, SC_WINDOW), I32),
                              pltpu.SemaphoreType.DMA])
    def scatter_rows(x_hbm, i_hbm, o_hbm, xbuf, ibuf, sem):
        wid = _sc_worker()

        @pl.loop(0, per_worker)
        def _(j):
            row0 = (wid * per_worker + j) * SC_WINDOW
            pltpu.sync_copy(x_hbm.at[pl.ds(row0, SC_WINDOW)], xbuf)
            pltpu.sync_copy(i_hbm.at[:, pl.ds(row0, SC_WINDOW)], ibuf)
            copies = [pltpu.async_copy(xbuf, o_hbm.at[ibuf.at[kk]], sem) for kk in range(TOP_K)]
            for cp in copies:
                cp.wait()

    return scatter_rows(hx, pos)


def _sc_gather(ys, pos):
    n = pos.shape[1]
    w = ys.shape[1]
    per_worker = n // SC_WINDOW // SC_WORKERS
    assert per_worker * SC_WINDOW * SC_WORKERS == n

    @pl.kernel(out_type=jax.ShapeDtypeStruct((TOP_K, n, w), ys.dtype), mesh=_sc_mesh(),
               scratch_types=[pltpu.VMEM((SC_WINDOW, w), ys.dtype), pltpu.VMEM((TOP_K, SC_WINDOW), I32)])
    def gather_rows(y_hbm, i_hbm, o_hbm, ybuf, ibuf):
        wid = _sc_worker()

        @pl.loop(0, per_worker)
        def _(j):
            row0 = (wid * per_worker + j) * SC_WINDOW
            pltpu.sync_copy(i_hbm.at[:, pl.ds(row0, SC_WINDOW)], ibuf)
            for kk in range(TOP_K):
                pltpu.sync_copy(y_hbm.at[ibuf.at[kk]], ybuf)
                pltpu.sync_copy(ybuf, o_hbm.at[kk, pl.ds(row0, SC_WINDOW)])

    return gather_rows(ys, pos)


def _experts_kernel(blk_e_ref, valid_ref, blk_in_ref, blk_out_ref, next_e_ref, slot_ref,
                    xs_ref, wgu_hbm, wdn_hbm, ys_ref, gu_buf, dn_buf, sem, wgu_sc, wdn_sc):
    del blk_in_ref, blk_out_ref
    i = pl.program_id(0)
    valid = valid_ref[i]

    def weight_copies(e, slot):
        return (pltpu.make_async_copy(wgu_hbm.at[e], gu_buf.at[slot], sem.at[0, slot]),
                pltpu.make_async_copy(wdn_hbm.at[e], dn_buf.at[slot], sem.at[1, slot]))

    @pl.when(valid > 0)
    def _():
        e = blk_e_ref[i]
        slot = slot_ref[i]
        prev = blk_e_ref[jnp.maximum(i - 1, 0)]

        @pl.when(i == 0)
        def _():
            for cp in weight_copies(e, slot):
                cp.start()

        @pl.when((i == 0) | (e != prev))
        def _():
            for cp in weight_copies(e, slot):
                cp.wait()
            nxt = next_e_ref[i]

            @pl.when(nxt >= 0)
            def _():
                for cp in weight_copies(nxt, 1 - slot):
                    cp.start()

            wgu_sc[...] = gu_buf[slot].astype(BF16)
            wdn_sc[...] = dn_buf[slot].astype(BF16)

        rows = lax.broadcasted_iota(I32, xs_ref.shape, 0)
        xu = jnp.where(rows < valid, pltpu.bitcast(xs_ref[...], U32), jnp.uint32(0))
        lo, hi = _unpack_bf16_pair(xu)
        half = lo.shape[1]
        gu = (jnp.dot(lo.astype(BF16), wgu_sc[:half, :], preferred_element_type=F32)
              + jnp.dot(hi.astype(BF16), wgu_sc[half:, :], preferred_element_type=F32))
        gt = gu[:, :EXPERT_HIDDEN]
        act = (gt * _sigmoid(gt) * gu[:, EXPERT_HIDDEN:]).astype(BF16)
        y = jnp.dot(act, wdn_sc[...], preferred_element_type=F32)
        ys_ref[...] = pltpu.bitcast(_pack_bf16_pair(y[:, :half], y[:, half:]), I32)

    @pl.when(valid <= 0)
    def _():
        ys_ref[...] = jnp.zeros(ys_ref.shape, I32)


def _experts(blk_e, blk_valid, next_expert, expert_slot, xs, w_gu, w_dn):
    cap, w = xs.shape
    n_e, d, h2 = w_gu.shape
    n_blocks = cap // EXPERT_BLOCK
    blk_next = next_expert[blk_e].astype(I32)
    blk_slot = expert_slot[blk_e].astype(I32)
    step = jnp.arange(n_blocks, dtype=I32)
    n_used = jnp.sum((blk_valid > 0).astype(I32))
    blk_in = jnp.minimum(step, jnp.maximum(n_used - 1, 0)).astype(I32)
    blk_out = jnp.where(blk_valid > 0, step, n_blocks).astype(I32)
    return pl.pallas_call(
        _experts_kernel,
        grid_spec=pltpu.PrefetchScalarGridSpec(
            num_scalar_prefetch=6,
            grid=(n_blocks,),
            in_specs=[pl.BlockSpec((EXPERT_BLOCK, w), lambda i, be, bv, bi, bo, bn, bs: (bi[i], 0)),
                      pl.BlockSpec(memory_space=pl.ANY),
                      pl.BlockSpec(memory_space=pl.ANY)],
            out_specs=pl.BlockSpec((EXPERT_BLOCK, w), lambda i, be, bv, bi, bo, bn, bs: (bo[i], 0)),
            scratch_shapes=[pltpu.VMEM((2, d, h2), F32), pltpu.VMEM((2, h2 // 2, d), F32),
                            pltpu.SemaphoreType.DMA((2, 2)),
                            pltpu.VMEM((d, h2), BF16), pltpu.VMEM((h2 // 2, d), BF16)]),
        out_shape=jax.ShapeDtypeStruct((cap + EXPERT_BLOCK, w), I32),
        compiler_params=_params(("arbitrary",)),
        name="experts",
    )(blk_e, blk_valid, blk_in, blk_out, blk_next, blk_slot, xs, w_gu, w_dn)


def _combine_kernel(yg_ref, wt_ref, hx_ref, x1_ref, gt2_ref, wsg_ref, wsd_ref, fg_ref, *rest):
    o_ref = rest[-1]
    lo, hi = _unpack_bf16_pair(pltpu.bitcast(hx_ref[...], U32))
    half = lo.shape[1]
    gu = (jnp.dot(lo.astype(BF16), wsg_ref[:half, :], preferred_element_type=F32)
          + jnp.dot(hi.astype(BF16), wsg_ref[half:, :], preferred_element_type=F32))
    gt = gu[:, :SHARED_HIDDEN]
    act = (gt * _sigmoid(gt) * gu[:, SHARED_HIDDEN:]).astype(BF16)
    y = jnp.dot(act, wsd_ref[...], preferred_element_type=F32)
    y_lo = y[:, :half]
    y_hi = y[:, half:]
    for kk in range(TOP_K):
        r_lo, r_hi = _unpack_bf16_pair(pltpu.bitcast(yg_ref[kk], U32))
        wk = wt_ref[:, kk:kk + 1]
        y_lo = y_lo + wk * r_lo
        y_hi = y_hi + wk * r_hi
    x2_lo = x1_ref[:, :half] + gt2_ref[0, :, :half] * y_lo
    x2_hi = x1_ref[:, half:] + gt2_ref[0, :, half:] * y_hi
    ms = (jnp.sum(x2_lo * x2_lo, axis=-1, keepdims=True)
          + jnp.sum(x2_hi * x2_hi, axis=-1, keepdims=True)) / (2 * half)
    inv = lax.rsqrt(ms + EPS)
    o_ref[:, :half] = x2_lo * inv * fg_ref[:, :half]
    o_ref[:, half:] = x2_hi * inv * fg_ref[:, half:]


def _combine(yg, wt, hx, x1, gt2, w_sg, w_sd, fg, tm, tiles_per_batch, tile0, out_prev):
    n, w = hx.shape
    d = 2 * w
    full = lambda a: pl.BlockSpec(a.shape, lambda i: (0,) * a.ndim)
    in_specs = [pl.BlockSpec((TOP_K, tm, w), lambda i: (0, i, 0)),
                pl.BlockSpec((tm, TOP_K), lambda i: (tile0 + i, 0)),
                pl.BlockSpec((tm, w), lambda i: (tile0 + i, 0)),
                pl.BlockSpec((tm, d), lambda i: (tile0 + i, 0)),
                pl.BlockSpec((1, 1, d), lambda i: ((tile0 + i) // tiles_per_batch, 0, 0)),
                full(w_sg), full(w_sd), full(fg)]
    args = [yg, wt, hx, x1, gt2, w_sg, w_sd, fg]
    aliases = {}
    if out_prev is not None:
        in_specs.append(pl.BlockSpec(memory_space=pl.ANY))
        args.append(out_prev)
        aliases = {len(args) - 1: 0}
    return pl.pallas_call(
        _combine_kernel,
        grid=(yg.shape[1] // tm,),
        in_specs=in_specs,
        out_specs=pl.BlockSpec((tm, d), lambda i: (tile0 + i, 0)),
        out_shape=jax.ShapeDtypeStruct((n, d), F32),
        input_output_aliases=aliases,
        compiler_params=_params(("arbitrary",)),
        name="combine",
    )(*args)


def _rope_tables(seq):
    rows = seq // GRID_W
    pos_row = np.repeat(np.arange(rows, dtype=np.float32), GRID_W)
    pos_col = np.tile(np.arange(GRID_W, dtype=np.float32), rows)
    inv_freq = (ROPE_THETA ** (-np.arange(0, AXIS_DIM, 2, dtype=np.float32) / AXIS_DIM)).astype(np.float32)
    ar = pos_row[:, None] * inv_freq
    ac = pos_col[:, None] * inv_freq
    cos_t = np.concatenate([np.cos(ar), np.cos(ar), np.cos(ac), np.cos(ac)], axis=1)
    sin_t = np.concatenate([-np.sin(ar), np.sin(ar), -np.sin(ac), np.sin(ac)], axis=1)
    return jnp.asarray(cos_t, F32), jnp.asarray(sin_t, F32)


def _tile(n, want):
    t = min(n, want)
    assert n % t == 0, (n, want)
    return t


def kernel(x, c, ctx, c_ctx, w_mod, b_mod, norm1_g, w_in, q_norm_g, k_norm_g, w_dw, b_dw, conv_ln_g, conv_ln_b, w_attn_proj, w_conv_proj, w_out, norm2_g, w_router, router_bias, w_exp_gu, w_exp_dn, w_sh_gu, w_sh_dn, final_g):
    b, s, d = x.shape
    depth = w_mod.shape[0]
    assert depth == 1, "single-layer block"
    n = b * s
    row = lambda v: v.reshape(1, -1)

    cc = jnp.zeros((SUBLANES, d), F32).at[:b].set(c).at[b].set(c_ctx)
    mod = _modulation(cc, w_mod[0], row(b_mod[0]))
    mod_x = mod[:b].reshape(b, 1, 6, d)
    sh1, sc1, gt1, sh2, sc2, gt2 = [mod_x[:, :, j, :] for j in range(6)]
    mod_c = mod[b].reshape(6, d)
    csh1, csc1 = row(mod_c[0]), row(mod_c[1])

    w_in_b = w_in[0].astype(BF16)
    q_end, kv_end = ATTN_WIDTH, ATTN_WIDTH + 2 * KV_WIDTH
    k_end = q_end + KV_WIDTH
    w_vt = w_in_b[:, k_end:kv_end].T
    kc, vct = _ctx_kv(ctx, csh1, csc1, row(norm1_g[0]), w_in_b[:, q_end:k_end], w_vt, row(k_norm_g[0]))

    cos_t, sin_t = _rope_tables(s)
    q, kx, vxt, hglu, gates = _in_proj(x, sh1, sc1, row(norm1_g[0]), w_in_b, w_vt, row(q_norm_g[0]),
                                       row(k_norm_g[0]), cos_t, sin_t, _tile(s, 512))
    score_bound = (HEAD_DIM * ATTN_SCALE * LOG2E) * jnp.max(jnp.abs(q_norm_g[0])) * jnp.max(jnp.abs(k_norm_g[0]))
    bounded = (score_bound <= SAFE_EXP2_ARG).astype(I32).reshape(1)
    o = _attention(bounded, q, kc, vct, kx, vxt, _tile(s, 1024), _tile(s // 2, 1024))

    top16 = lambda a: lax.bitcast_convert_type(lax.bitcast_convert_type(a, U32) & jnp.uint32(0xFFFF0000), F32)
    w_rt = w_router[0].T
    w_rt1 = top16(w_rt)
    w_rt2 = top16(w_rt - w_rt1)
    w_rt3 = w_rt - w_rt1 - w_rt2
    w_rt1, w_rt2, w_rt3 = w_rt1.astype(BF16), w_rt2.astype(BF16), w_rt3.astype(BF16)
    x1, hx, logits_t = _merge(o, hglu, gates, x, gt1, sh2, sc2, w_dw[0], row(b_dw[0]), row(conv_ln_g[0]),
                              row(conv_ln_b[0]), w_attn_proj[0].astype(BF16), w_conv_proj[0].astype(BF16),
                              w_out[0].astype(BF16), row(norm2_g[0]), jnp.stack([w_rt1, w_rt2, w_rt3]),
                              _tile(s, 512))

    e_idx, wts, rank, counts = _route(logits_t, router_bias[0], _tile(n, 512))

    cnt = counts[:, 0]
    padded = (cnt + EXPERT_BLOCK - 1) // EXPERT_BLOCK * EXPERT_BLOCK
    pends = jnp.cumsum(padded)
    pstart = (pends - padded).astype(I32)
    n_blocks = (n * TOP_K + N_EXPERTS * (EXPERT_BLOCK - 1)) // EXPERT_BLOCK
    cap = n_blocks * EXPERT_BLOCK
    blk_row0 = jnp.arange(n_blocks, dtype=I32) * EXPERT_BLOCK
    blk_e = jnp.minimum(jnp.sum(pends[None, :] <= blk_row0[:, None], axis=1), N_EXPERTS - 1).astype(I32)
    blk_valid = jnp.clip(pstart[blk_e] + cnt[blk_e] - blk_row0, 0, EXPERT_BLOCK).astype(I32)

    pos = _slots(pstart, e_idx, rank, _tile(n, 4096))
    xs = _sc_dispatch(hx, pos, cap)
    has_rows = cnt > 0
    later = lax.cummin(jnp.where(has_rows, jnp.arange(N_EXPERTS, dtype=I32), N_EXPERTS), reverse=True)
    next_expert = jnp.concatenate([later[1:], jnp.full((1,), N_EXPERTS, I32)])
    next_expert = jnp.where(next_expert >= N_EXPERTS, -1, next_expert)
    expert_slot = (jnp.cumsum(has_rows.astype(I32)) - 1) % 2
    ys = _experts(blk_e, blk_valid, next_expert, expert_slot, xs, w_exp_gu[0], w_exp_dn[0])
    tm = _tile(n, 256)
    chunk = n // COMBINE_CHUNKS if n % (COMBINE_CHUNKS * SC_WINDOW * SC_WORKERS) == 0 else n
    wt_t, x1_2d = wts.T, x1.reshape(n, d)
    w_sg, w_sd = w_sh_gu[0].astype(BF16), w_sh_dn[0].astype(BF16)
    out = None
    for c0 in range(0, n, chunk):
        yg = _sc_gather(ys, pos[:, c0:c0 + chunk])
        out = _combine(yg, wt_t, hx, x1_2d, gt2, w_sg, w_sd, row(final_g), tm, s // tm, c0 // tm, out)
    return out.reshape(b, s, d)
```

```python
import functools
import math

import jax
import jax.numpy as jnp
import numpy as np
from jax import lax
from jax.experimental import pallas as pl
from jax.experimental.pallas import tpu as pltpu
from jax.experimental.pallas import tpu_sc as plsc

F32 = jnp.float32
BF16 = jnp.bfloat16
U32 = jnp.uint32
I32 = jnp.int32

GRID_W = 64
N_HEADS = 8
N_KV_HEADS = 4
GROUP = N_HEADS // N_KV_HEADS
HEAD_DIM = 128
AXIS_DIM = HEAD_DIM // 2
ATTN_WIDTH = N_HEADS * HEAD_DIM
KV_WIDTH = N_KV_HEADS * HEAD_DIM
ROPE_THETA = 10000.0
ATTN_SCALE = HEAD_DIM ** -0.5
CONV_WIDTH = 512
CONV_KERNEL = 31
CONV_PAD = CONV_KERNEL // 2
N_EXPERTS = 256
TOP_K = 8
N_GROUPS = 8
TOPK_GROUPS = 4
EXPERTS_PER_GROUP = N_EXPERTS // N_GROUPS
EXPERT_HIDDEN = 256
SHARED_HIDDEN = 256
ROUTED_SCALE = 2.5
EPS = 1e-6
LOG2E = 1.4426950408889634
SAFE_EXP2_ARG = 64.0

LANES = 128
SUBLANES = 8
VMEM_LIMIT_BYTES = 56 * 1024 * 1024

HALO_ROWS = 16
CONV_ROWS = 64
MERGE_PARTS = 2
EXPERT_BLOCK = 512
SC_SUBCORES = 16
SC_WORKERS = 2 * SC_SUBCORES
SC_WINDOW = 128
COMBINE_CHUNKS = 4
HIGHEST = lax.Precision.HIGHEST


def _params(sem):
    return pltpu.CompilerParams(dimension_semantics=sem, vmem_limit_bytes=VMEM_LIMIT_BYTES)


def _sigmoid(x):
    return 1.0 / (1.0 + jnp.exp(-x))


def _pack_bf16_pair(lo, hi):
    lo_b = pltpu.bitcast(lo.astype(BF16).astype(F32), U32)
    hi_b = pltpu.bitcast(hi.astype(BF16).astype(F32), U32)
    return (lo_b >> 16) | (hi_b & jnp.uint32(0xFFFF0000))


def _unpack_bf16_pair(u):
    lo = pltpu.bitcast(u << 16, F32)
    hi = pltpu.bitcast(u & jnp.uint32(0xFFFF0000), F32)
    return lo, hi


def _mod_kernel(cc_ref, w_ref, b_ref, o_ref):
    cc = cc_ref[...]
    s = cc * _sigmoid(cc)
    o_ref[...] = jnp.dot(s, w_ref[...], precision=HIGHEST, preferred_element_type=F32) + b_ref[...]


def _modulation(cc, w_mod, b_mod):
    d, n = w_mod.shape
    tn = n // 4
    return pl.pallas_call(
        _mod_kernel,
        grid=(n // tn,),
        in_specs=[pl.BlockSpec((SUBLANES, d), lambda j: (0, 0)),
                  pl.BlockSpec((d, tn), lambda j: (0, j)),
                  pl.BlockSpec((1, tn), lambda j: (0, j))],
        out_specs=pl.BlockSpec((SUBLANES, tn), lambda j: (0, j)),
        out_shape=jax.ShapeDtypeStruct((SUBLANES, n), F32),
        compiler_params=_params(("arbitrary",)),
        name="mod",
    )(cc, w_mod, b_mod)


def _norm_modulate(x, g, sh, sc):
    ms = jnp.mean(x * x, axis=-1, keepdims=True)
    return (x * lax.rsqrt(ms + EPS) * g) * (1.0 + sc) + sh


def _head_norm(p, gain):
    r = lax.rsqrt(jnp.mean(p * p, axis=-1, keepdims=True) + EPS)
    return p * r * gain


def _dot_nt(a, b):
    return lax.dot_general(a, b, (((1,), (1,)), ((), ())), preferred_element_type=F32)


def _ctx_kv_kernel(x_ref, sh_ref, sc_ref, g1_ref, wk_ref, wvt_ref, gk_ref, k_ref, vt_ref):
    h = _norm_modulate(x_ref[0], g1_ref[...], sh_ref[...], sc_ref[...]).astype(BF16)
    pk = jnp.dot(h, wk_ref[...], preferred_element_type=F32)
    for j in range(N_KV_HEADS):
        sl = slice(j * HEAD_DIM, (j + 1) * HEAD_DIM)
        k_ref[0, :, sl] = _head_norm(pk[:, sl], gk_ref[...]).astype(BF16)
    vt_ref[0] = _dot_nt(wvt_ref[...], h).astype(BF16)


def _ctx_kv(ctx, csh, csc, g1, w_k, w_vt, gk):
    b, lc, d = ctx.shape
    vec = lambda: pl.BlockSpec((1, d), lambda i: (0, 0))
    return pl.pallas_call(
        _ctx_kv_kernel,
        grid=(b,),
        in_specs=[pl.BlockSpec((1, lc, d), lambda i: (i, 0, 0)), vec(), vec(), vec(),
                  pl.BlockSpec((d, KV_WIDTH), lambda i: (0, 0)),
                  pl.BlockSpec((KV_WIDTH, d), lambda i: (0, 0)),
                  pl.BlockSpec((1, HEAD_DIM), lambda i: (0, 0))],
        out_specs=[pl.BlockSpec((1, lc, KV_WIDTH), lambda i: (i, 0, 0)),
                   pl.BlockSpec((1, KV_WIDTH, lc), lambda i: (i, 0, 0))],
        out_shape=[jax.ShapeDtypeStruct((b, lc, KV_WIDTH), BF16),
                   jax.ShapeDtypeStruct((b, KV_WIDTH, lc), BF16)],
        compiler_params=_params(("arbitrary",)),
        name="ctx_kv",
    )(ctx, csh, csc, g1, w_k, w_vt, gk)


def _in_proj_kernel(x_ref, sh_ref, sc_ref, g1_ref, w_ref, wvt_ref, gq_ref, gk_ref, cos_ref, sin_ref,
                    q_ref, k_ref, vt_ref, h_ref, g_ref):
    h = _norm_modulate(x_ref[0], g1_ref[...], sh_ref[0], sc_ref[0]).astype(BF16)
    cos = cos_ref[...]
    sin = sin_ref[...]
    lane = lax.broadcasted_iota(I32, cos.shape, 1)
    upper = (lane & (AXIS_DIM // 2)) != 0

    def rope(p):
        swapped = jnp.where(upper, pltpu.roll(p, AXIS_DIM // 2, 1),
                            pltpu.roll(p, HEAD_DIM - AXIS_DIM // 2, 1))
        return p * cos + swapped * sin

    q_end = ATTN_WIDTH
    k_end = q_end + KV_WIDTH
    v_end = k_end + KV_WIDTH
    u_end = v_end + 2 * CONV_WIDTH
    pq = jnp.dot(h, w_ref[:, :q_end], preferred_element_type=F32)
    for j in range(N_HEADS):
        sl = slice(j * HEAD_DIM, (j + 1) * HEAD_DIM)
        q_ref[0, :, sl] = (rope(_head_norm(pq[:, sl], gq_ref[...])) * (ATTN_SCALE * LOG2E)).astype(BF16)
    pk = jnp.dot(h, w_ref[:, q_end:k_end], preferred_element_type=F32)
    for j in range(N_KV_HEADS):
        sl = slice(j * HEAD_DIM, (j + 1) * HEAD_DIM)
        k_ref[0, :, sl] = rope(_head_norm(pk[:, sl], gk_ref[...])).astype(BF16)
    vt_ref[0] = _dot_nt(wvt_ref[...], h).astype(BF16)
    u = jnp.dot(h, w_ref[:, v_end:u_end], preferred_element_type=F32)
    h_ref[0] = (u[:, :CONV_WIDTH] * _sigmoid(u[:, CONV_WIDTH:])).astype(BF16)
    g_ref[0] = _sigmoid(jnp.dot(h, w_ref[:, u_end:], preferred_element_type=F32)).astype(BF16)


def _in_proj(x, sh1, sc1, g1, w_in, w_vt, gq, gk, cos_t, sin_t, tm):
    b, s, d = x.shape
    n_in = w_in.shape[1]
    bvec = lambda: pl.BlockSpec((1, 1, d), lambda bi, i: (bi, 0, 0))
    tok = lambda w: pl.BlockSpec((1, tm, w), lambda bi, i: (bi, i, 0))
    return pl.pallas_call(
        _in_proj_kernel,
        grid=(b, s // tm),
        in_specs=[tok(d), bvec(), bvec(),
                  pl.BlockSpec((1, d), lambda bi, i: (0, 0)),
                  pl.BlockSpec((d, n_in), lambda bi, i: (0, 0)),
                  pl.BlockSpec((KV_WIDTH, d), lambda bi, i: (0, 0)),
                  pl.BlockSpec((1, HEAD_DIM), lambda bi, i: (0, 0)),
                  pl.BlockSpec((1, HEAD_DIM), lambda bi, i: (0, 0)),
                  pl.BlockSpec((tm, HEAD_DIM), lambda bi, i: (i, 0)),
                  pl.BlockSpec((tm, HEAD_DIM), lambda bi, i: (i, 0))],
        out_specs=[tok(ATTN_WIDTH), tok(KV_WIDTH),
                   pl.BlockSpec((1, KV_WIDTH, tm), lambda bi, i: (bi, 0, i)),
                   tok(CONV_WIDTH), tok(2 * d)],
        out_shape=[jax.ShapeDtypeStruct((b, s, ATTN_WIDTH), BF16),
                   jax.ShapeDtypeStruct((b, s, KV_WIDTH), BF16),
                   jax.ShapeDtypeStruct((b, KV_WIDTH, s), BF16),
                   jax.ShapeDtypeStruct((b, s, CONV_WIDTH), BF16),
                   jax.ShapeDtypeStruct((b, s, 2 * d), BF16)],
        compiler_params=_params(("arbitrary", "arbitrary")),
        name="in_proj",
    )(x, sh1, sc1, g1, w_in, w_vt, gq, gk, cos_t, sin_t)


def _sub_allreduce(x, op):
    for s in (4, 2, 1):
        x = op(x, pltpu.roll(x, s, 0))
    return x


def _attn_kernel(bounded_ref, q_ref, kc_ref, vct_ref, k_ref, vt_ref, hp_ref, hc_ref, hn_ref, wdw_ref, bdw_ref,
                 o_ref, c_ref, s0, s1, x0, x1, m_sc, l_sc, acc_sc, hcat, shift_sc, *, tk):
    tq = q_ref.shape[1]
    m_cols = GROUP * tq
    nk = k_ref.shape[1] // tk
    qf = q_ref[0].astype(F32).T
    qt = jnp.concatenate([qf[:HEAD_DIM], qf[HEAD_DIM:]], axis=1).astype(BF16)
    slots = ((s0, x0), (s1, x1))
    n_conv = max(nk // 2, 1)
    conv_rows = tq // n_conv
    conv_base = HALO_ROWS - CONV_PAD
    conv_reach = (conv_base + CONV_KERNEL - 1) // SUBLANES * SUBLANES

    def conv_fill():
        ti = pl.program_id(2)
        prev = hp_ref[0].astype(F32)
        nxt = hn_ref[0].astype(F32)
        hcat[0:HALO_ROWS, :] = jnp.where(ti > 0, prev, jnp.zeros_like(prev))
        hcat[HALO_ROWS:HALO_ROWS + tq, :] = hc_ref[0].astype(F32)
        hcat[HALO_ROWS + tq:, :] = jnp.where(ti < pl.num_programs(2) - 1, nxt, jnp.zeros_like(nxt))

    def conv_block(blk):
        r0 = blk * conv_rows
        for sub in range(0, conv_rows, CONV_ROWS):
            start = r0 + sub if isinstance(r0, int) else pl.multiple_of(r0 + sub, CONV_ROWS)
            window = hcat[pl.ds(start, CONV_ROWS + 2 * HALO_ROWS), :]
            acc = jnp.zeros((CONV_ROWS, HEAD_DIM), F32) + bdw_ref[...]
            for res in range(SUBLANES):
                shift_sc[res] = window[res:res + CONV_ROWS + conv_reach, :]
                for off in range(res, conv_base + CONV_KERNEL, SUBLANES):
                    j = off - conv_base
                    if 0 <= j < CONV_KERNEL:
                        a0 = off - res
                        acc = acc + shift_sc[res, a0:a0 + CONV_ROWS, :] * wdw_ref[j:j + 1, :]
            c_ref[0, pl.ds(start, CONV_ROWS), :] = acc

    def kchunk(j):
        return k_ref[0, pl.ds(pl.multiple_of(j * tk, tk), tk), :]

    def vchunk(j):
        return vt_ref[0, :, pl.ds(pl.multiple_of(j * tk, tk), tk)]

    def split(st):
        return st.reshape(st.shape[0] // SUBLANES, SUBLANES, m_cols)

    def scores(k, online):
        st = jnp.dot(k, qt, preferred_element_type=F32)
        return st, (jnp.max(split(st), axis=0) if online else None)

    def absorb(st, mx, vt, online):
        s3 = split(st)
        if online:
            m_prev = m_sc[...]
            m_new = jnp.maximum(m_prev, _sub_allreduce(mx, jnp.maximum))
            alpha = jnp.exp2(m_prev - m_new)
            p3 = jnp.exp2(s3 - m_new[None])
            l_sc[...] = alpha * l_sc[...] + _sub_allreduce(jnp.sum(p3, axis=0), jnp.add)
            pv = jnp.dot(vt, p3.reshape(st.shape).astype(BF16), preferred_element_type=F32)
            acc_sc[...] = alpha[0:1] * acc_sc[...] + pv
            m_sc[...] = m_new
        else:
            p3 = jnp.exp2(s3)
            l_sc[...] = l_sc[...] + jnp.sum(p3, axis=0)
            acc_sc[...] = acc_sc[...] + jnp.dot(vt, p3.reshape(st.shape).astype(BF16),
                                                preferred_element_type=F32)

    def stage(slot, k, online):
        st, mx = scores(k, online)
        slots[slot][0][...] = st
        if online:
            slots[slot][1][...] = mx

    def take(slot, vt, online):
        absorb(slots[slot][0][...], slots[slot][1][...] if online else None, vt, online)

    def sweep(online):
        if online:
            m_sc[...] = jnp.full(m_sc.shape, -jnp.inf, F32)
        l_sc[...] = jnp.zeros(l_sc.shape, F32)
        acc_sc[...] = jnp.zeros(acc_sc.shape, F32)
        conv_fill()
        stage(0, kchunk(0), online)

        def body(i, carry):
            j = 2 * i
            stage(1, kchunk(j + 1), online)
            take(0, vchunk(j), online)
            conv_block(i)
            stage(0, kchunk(j + 2), online)
            take(1, vchunk(j + 1), online)
            return carry

        lax.fori_loop(0, nk // 2 - 1, body, 0)
        conv_block(n_conv - 1)
        stage(1, kchunk(nk - 1), online)
        take(0, vchunk(nk - 2), online)
        sc, xc = scores(kc_ref[0], online)
        take(1, vchunk(nk - 1), online)
        absorb(sc, xc, vct_ref[0], online)
        denom = l_sc[...] if online else _sub_allreduce(l_sc[...], jnp.add)
        o = (acc_sc[...] / denom[0:1]).T
        o_ref[0, :, :HEAD_DIM] = o[:tq].astype(BF16)
        o_ref[0, :, HEAD_DIM:] = o[tq:].astype(BF16)

    @pl.when(bounded_ref[0] != 0)
    def _():
        sweep(online=False)

    @pl.when(bounded_ref[0] == 0)
    def _():
        sweep(online=True)


def _attention(bounded, q, kc, vct, kx, vxt, hglu, w_dw, b_dw, tq, tk):
    b, s, _ = q.shape
    lc = kc.shape[1]
    assert s % (2 * tk) == 0 and CONV_WIDTH == N_KV_HEADS * HEAD_DIM
    gw = GROUP * HEAD_DIM
    m_cols = GROUP * tq
    hb = tq // HALO_ROWS
    n_halo = s // HALO_ROWS
    reach = (HALO_ROWS - CONV_PAD + CONV_KERNEL - 1) // SUBLANES * SUBLANES
    kv = lambda l: pl.BlockSpec((1, l, HEAD_DIM), lambda bi, h, i, bd: (bi, 0, h))
    kvt = lambda l: pl.BlockSpec((1, HEAD_DIM, l), lambda bi, h, i, bd: (bi, h, 0))
    qo = lambda: pl.BlockSpec((1, tq, gw), lambda bi, h, i, bd: (bi, i, h))
    chan = lambda rows: pl.BlockSpec((rows, HEAD_DIM), lambda bi, h, i, bd: (0, h))
    return pl.pallas_call(
        functools.partial(_attn_kernel, tk=tk),
        grid_spec=pltpu.PrefetchScalarGridSpec(
            num_scalar_prefetch=1,
            grid=(b, N_KV_HEADS, s // tq),
            in_specs=[qo(), kv(lc), kvt(lc), kv(s), kvt(s),
                      pl.BlockSpec((1, HALO_ROWS, HEAD_DIM),
                                   lambda bi, h, i, bd: (bi, jnp.maximum(i * hb - 1, 0), h)),
                      pl.BlockSpec((1, tq, HEAD_DIM), lambda bi, h, i, bd: (bi, i, h)),
                      pl.BlockSpec((1, HALO_ROWS, HEAD_DIM),
                                   lambda bi, h, i, bd: (bi, jnp.minimum((i + 1) * hb, n_halo - 1), h)),
                      chan(CONV_KERNEL), chan(1)],
            out_specs=[qo(), pl.BlockSpec((1, tq, HEAD_DIM), lambda bi, h, i, bd: (bi, i, h))],
            scratch_shapes=[pltpu.VMEM((tk, m_cols), F32), pltpu.VMEM((tk, m_cols), F32),
                            pltpu.VMEM((SUBLANES, m_cols), F32), pltpu.VMEM((SUBLANES, m_cols), F32),
                            pltpu.VMEM((SUBLANES, m_cols), F32), pltpu.VMEM((SUBLANES, m_cols), F32),
                            pltpu.VMEM((HEAD_DIM, m_cols), F32),
                            pltpu.VMEM((tq + 2 * HALO_ROWS, HEAD_DIM), F32),
                            pltpu.VMEM((SUBLANES, CONV_ROWS + reach, HEAD_DIM), F32)]),
        out_shape=[jax.ShapeDtypeStruct((b, s, ATTN_WIDTH), BF16),
                   jax.ShapeDtypeStruct((b, s, CONV_WIDTH), F32)],
        compiler_params=_params(("arbitrary", "arbitrary", "arbitrary")),
        name="attn",
    )(bounded, q, kc, vct, kx, vxt, hglu, hglu, hglu, w_dw, b_dw)


def _split3(x):
    x1 = x.astype(BF16)
    r1 = x - x1.astype(F32)
    x2 = r1.astype(BF16)
    x3 = (r1 - x2.astype(F32)).astype(BF16)
    return x1, x2, x3


def _merge_kernel(o_ref, c_ref, g_ref, x_ref, gt1_ref, sh2_ref, sc2_ref,
                  lng_ref, lnb_ref, wap_ref, wcp_ref, wout_ref, g2_ref, wrt_ref,
                  x1_ref, hx_ref, lg_ref):
    tm = x_ref.shape[1]
    d = x_ref.shape[2]
    half = d // 2
    rows_per_part = tm // MERGE_PARTS
    for part in range(MERGE_PARTS):
        p0 = part * rows_per_part
        rows = slice(p0, p0 + rows_per_part)
        conv = c_ref[0, rows, :]
        mu = jnp.mean(conv, axis=-1, keepdims=True)
        cen = conv - mu
        var = jnp.mean(cen * cen, axis=-1, keepdims=True)
        ln = cen * lax.rsqrt(var + EPS) * lng_ref[...] + lnb_ref[...]
        act = (ln * _sigmoid(ln)).astype(BF16)
        y_conv = jnp.dot(act, wcp_ref[...], preferred_element_type=F32)
        y_attn = jnp.dot(o_ref[0, rows, :], wap_ref[...], preferred_element_type=F32)
        z = g_ref[0, rows, :d].astype(F32) * y_attn + g_ref[0, rows, d:].astype(F32) * y_conv
        mix = jnp.dot(z.astype(BF16), wout_ref[...], preferred_element_type=F32)
        x1 = x_ref[0, rows, :] + gt1_ref[0] * mix
        x1_ref[0, rows, :] = x1
        hx = _norm_modulate(x1, g2_ref[...], sh2_ref[0], sc2_ref[0])
        hx_ref[rows, :] = pltpu.bitcast(_pack_bf16_pair(hx[:, :half], hx[:, half:]), I32)
        h1, h2, h3 = _split3(hx)
        w1, w2, w3 = wrt_ref[0], wrt_ref[1], wrt_ref[2]
        lg_ref[:, rows] = (((_dot_nt(w3, h1) + _dot_nt(w1, h3)) + _dot_nt(w2, h2))
                           + (_dot_nt(w2, h1) + _dot_nt(w1, h2))) + _dot_nt(w1, h1)


def _merge(o, conv, g, x, gt1, sh2, sc2, ln_g, ln_b, w_ap, w_cp, w_out, g2, w_rt, tm):
    b, s, d = x.shape
    nt = s // tm
    bvec = lambda: pl.BlockSpec((1, 1, d), lambda bi, i: (bi, 0, 0))
    full = lambda a: pl.BlockSpec(a.shape, lambda bi, i: (0,) * a.ndim)
    tok = lambda w: pl.BlockSpec((1, tm, w), lambda bi, i: (bi, i, 0))
    return pl.pallas_call(
        _merge_kernel,
        grid=(b, nt),
        in_specs=[tok(ATTN_WIDTH), tok(CONV_WIDTH), tok(2 * d), tok(d), bvec(), bvec(), bvec(),
                  full(ln_g), full(ln_b), full(w_ap), full(w_cp), full(w_out), full(g2), full(w_rt)],
        out_specs=[tok(d),
                   pl.BlockSpec((tm, d // 2), lambda bi, i: (bi * nt + i, 0)),
                   pl.BlockSpec((N_EXPERTS, tm), lambda bi, i: (0, bi * nt + i))],
        out_shape=[jax.ShapeDtypeStruct((b, s, d), F32),
                   jax.ShapeDtypeStruct((b * s, d // 2), I32),
                   jax.ShapeDtypeStruct((N_EXPERTS, b * s), F32)],
        compiler_params=_params(("arbitrary", "arbitrary")),
        name="merge",
    )(o, conv, g, x, gt1, sh2, sc2, ln_g, ln_b, w_ap, w_cp, w_out, g2, w_rt)


def _route_kernel(lg_ref, bias_ref, tri_ref, e_ref, w_ref, r_ref, cnt_ref, run_sc):
    step = pl.program_id(0)
    n_strips = lg_ref.shape[1] // LANES
    nv = N_EXPERTS // SUBLANES
    gv = EXPERTS_PER_GROUP // SUBLANES

    @pl.when(step == 0)
    def _():
        run_sc[...] = jnp.zeros(run_sc.shape, F32)

    row = (lax.broadcasted_iota(I32, (nv, SUBLANES, LANES), 0) * SUBLANES
           + lax.broadcasted_iota(I32, (nv, SUBLANES, LANES), 1))
    sub = lax.broadcasted_iota(I32, (SUBLANES, LANES), 0)
    bias = bias_ref[...].reshape(nv, SUBLANES, LANES)
    neg_inf = jnp.float32(-jnp.inf)

    for st in range(n_strips):
        lanes = slice(st * LANES, (st + 1) * LANES)
        scores = _sigmoid(lg_ref[:, lanes]).reshape(nv, SUBLANES, LANES)
        biased = scores + bias
        gscore = []
        for g in range(N_GROUPS):
            m1 = biased[g * gv]
            m2 = jnp.full((SUBLANES, LANES), neg_inf, F32)
            for t in range(1, gv):
                v = biased[g * gv + t]
                m2 = jnp.maximum(m2, jnp.minimum(m1, v))
                m1 = jnp.maximum(m1, v)
            for s in (4, 2, 1):
                p1 = pltpu.roll(m1, s, 0)
                p2 = pltpu.roll(m2, s, 0)
                m2 = jnp.maximum(jnp.minimum(m1, p1), jnp.maximum(m2, p2))
                m1 = jnp.maximum(m1, p1)
            gscore.append(m1 + m2)
        masked = []
        for g in range(N_GROUPS):
            beaten = jnp.zeros((SUBLANES, LANES), I32)
            for o in range(N_GROUPS):
                if o == g:
                    continue
                wins = (gscore[o] > gscore[g]) | ((gscore[o] == gscore[g]) & (o < g))
                beaten = beaten + wins.astype(I32)
            keep = beaten < TOPK_GROUPS
            for t in range(gv):
                masked.append(jnp.where(keep, biased[g * gv + t], neg_inf))
        cand = jnp.stack(masked, axis=0)
        sel = jnp.zeros((nv, SUBLANES, LANES), jnp.bool_)
        picks, pick_scores = [], []
        for _ in range(TOP_K):
            mx = _sub_allreduce(jnp.max(cand, axis=0), jnp.maximum)
            idx = _sub_allreduce(jnp.min(jnp.where(cand == mx, row, N_EXPERTS), axis=0), jnp.minimum)
            hit = row == idx
            pick_scores.append(_sub_allreduce(jnp.sum(jnp.where(hit, scores, 0.0), axis=0), jnp.add))
            picks.append(idx)
            sel = sel | hit
            cand = jnp.where(hit, neg_inf, cand)
        sel_b = sel.astype(F32).astype(BF16).reshape(N_EXPERTS, LANES)
        before = jnp.dot(sel_b, tri_ref[0], preferred_element_type=F32)
        total = jnp.dot(sel_b, tri_ref[1], preferred_element_type=F32)
        rank_all = (before + run_sc[...]).reshape(nv, SUBLANES, LANES)
        run_sc[...] = run_sc[...] + total
        denom = pick_scores[0]
        for kk in range(1, TOP_K):
            denom = denom + pick_scores[kk]
        e_out = jnp.zeros((SUBLANES, LANES), I32)
        w_out = jnp.zeros((SUBLANES, LANES), F32)
        r_out = jnp.zeros((SUBLANES, LANES), I32)
        for kk in range(TOP_K):
            rk = _sub_allreduce(jnp.sum(jnp.where(row == picks[kk], rank_all, 0.0), axis=0), jnp.add)
            e_out = jnp.where(sub == kk, picks[kk], e_out)
            w_out = jnp.where(sub == kk, pick_scores[kk] / denom * ROUTED_SCALE, w_out)
            r_out = jnp.where(sub == kk, rk.astype(I32), r_out)
        e_ref[:, lanes] = e_out
        w_ref[:, lanes] = w_out
        r_ref[:, lanes] = r_out

    cnt_ref[...] = run_sc[...].astype(I32)


def _route(logits_t, bias, tb):
    n = logits_t.shape[1]
    iota_r = lax.broadcasted_iota(I32, (LANES, LANES), 0)
    iota_c = lax.broadcasted_iota(I32, (LANES, LANES), 1)
    tri = jnp.stack([(iota_r < iota_c), jnp.ones((LANES, LANES), jnp.bool_)]).astype(BF16)
    bias_b = jnp.broadcast_to(bias.reshape(N_EXPERTS, 1), (N_EXPERTS, LANES)).astype(F32)
    tokrow = lambda dt: jax.ShapeDtypeStruct((TOP_K, n), dt)
    return pl.pallas_call(
        _route_kernel,
        grid=(n // tb,),
        in_specs=[pl.BlockSpec((N_EXPERTS, tb), lambda i: (0, i)),
                  pl.BlockSpec((N_EXPERTS, LANES), lambda i: (0, 0)),
                  pl.BlockSpec((2, LANES, LANES), lambda i: (0, 0, 0))],
        out_specs=[pl.BlockSpec((TOP_K, tb), lambda i: (0, i)),
                   pl.BlockSpec((TOP_K, tb), lambda i: (0, i)),
                   pl.BlockSpec((TOP_K, tb), lambda i: (0, i)),
                   pl.BlockSpec((N_EXPERTS, LANES), lambda i: (0, 0))],
        out_shape=[tokrow(I32), tokrow(F32), tokrow(I32),
                   jax.ShapeDtypeStruct((N_EXPERTS, LANES), I32)],
        scratch_shapes=[pltpu.VMEM((N_EXPERTS, LANES), F32)],
        compiler_params=_params(("arbitrary",)),
        name="route",
    )(logits_t, bias_b, tri)


def _slots_kernel(pstart_ref, e_ref, r_ref, o_ref):
    e = e_ref[...]

    def body(x, acc):
        return acc + jnp.where(e == x, pstart_ref[x], 0)

    o_ref[...] = lax.fori_loop(0, N_EXPERTS, body, r_ref[...])


def _slots(pstart, e_idx, rank, tb):
    n = e_idx.shape[1]
    blk = lambda: pl.BlockSpec((TOP_K, tb), lambda i, ps: (0, i))
    return pl.pallas_call(
        _slots_kernel,
        grid_spec=pltpu.PrefetchScalarGridSpec(
            num_scalar_prefetch=1, grid=(n // tb,), in_specs=[blk(), blk()], out_specs=blk()),
        out_shape=jax.ShapeDtypeStruct((TOP_K, n), I32),
        compiler_params=_params(("arbitrary",)),
        name="slots",
    )(pstart, e_idx, rank)


def _sc_mesh():
    return plsc.VectorSubcoreMesh(core_axis_name="c", subcore_axis_name="s")


def _sc_worker():
    return lax.axis_index("c") * SC_SUBCORES + lax.axis_index("s")


def _sc_dispatch(hx, pos, cap):
    n, w = hx.shape
    per_worker = n // SC_WINDOW // SC_WORKERS
    assert per_worker * SC_WINDOW * SC_WORKERS == n

    @pl.kernel(out_type=jax.ShapeDtypeStruct((cap, w), hx.dtype), mesh=_sc_mesh(),
               scratch_types=[pltpu.VMEM((SC_WINDOW, w), hx.dtype), pltpu.VMEM((TOP_K, SC_WINDOW), I32),
                              pltpu.SemaphoreType.DMA])
    def scatter_rows(x_hbm, i_hbm, o_hbm, xbuf, ibuf, sem):
        wid = _sc_worker()

        @pl.loop(0, per_worker)
        def _(j):
            row0 = (wid * per_worker + j) * SC_WINDOW
            pltpu.sync_copy(x_hbm.at[pl.ds(row0, SC_WINDOW)], xbuf)
            pltpu.sync_copy(i_hbm.at[:, pl.ds(row0, SC_WINDOW)], ibuf)
            copies = [pltpu.async_copy(xbuf, o_hbm.at[ibuf.at[kk]], sem) for kk in range(TOP_K)]
            for cp in copies:
                cp.wait()

    return scatter_rows(hx, pos)


def _sc_gather(ys, pos):
    n = pos.shape[1]
    w = ys.shape[1]
    per_worker = n // SC_WINDOW // SC_WORKERS
    assert per_worker * SC_WINDOW * SC_WORKERS == n

    @pl.kernel(out_type=jax.ShapeDtypeStruct((TOP_K, n, w), ys.dtype), mesh=_sc_mesh(),
               scratch_types=[pltpu.VMEM((SC_WINDOW, w), ys.dtype), pltpu.VMEM((TOP_K, SC_WINDOW), I32)])
    def gather_rows(y_hbm, i_hbm, o_hbm, ybuf, ibuf):
        wid = _sc_worker()

        @pl.loop(0, per_worker)
        def _(j):
            row0 = (wid * per_worker + j) * SC_WINDOW
            pltpu.sync_copy(i_hbm.at[:, pl.ds(row0, SC_WINDOW)], ibuf)
            for kk in range(TOP_K):
                pltpu.sync_copy(y_hbm.at[ibuf.at[kk]], ybuf)
                pltpu.sync_copy(ybuf, o_hbm.at[kk, pl.ds(row0, SC_WINDOW)])

    return gather_rows(ys, pos)


def _experts_kernel(blk_e_ref, valid_ref, blk_in_ref, blk_out_ref, next_e_ref, slot_ref,
                    xs_ref, wgu_hbm, wdn_hbm, ys_ref, gu_buf, dn_buf, sem, wgu_sc, wdn_sc):
    del blk_in_ref, blk_out_ref
    i = pl.program_id(0)
    valid = valid_ref[i]

    def weight_copies(e, slot):
        return (pltpu.make_async_copy(wgu_hbm.at[e], gu_buf.at[slot], sem.at[0, slot]),
                pltpu.make_async_copy(wdn_hbm.at[e], dn_buf.at[slot], sem.at[1, slot]))

    @pl.when(valid > 0)
    def _():
        e = blk_e_ref[i]
        slot = slot_ref[i]
        prev = blk_e_ref[jnp.maximum(i - 1, 0)]

        @pl.when(i == 0)
        def _():
            for cp in weight_copies(e, slot):
                cp.start()

        @pl.when((i == 0) | (e != prev))
        def _():
            for cp in weight_copies(e, slot):
                cp.wait()
            nxt = next_e_ref[i]

            @pl.when(nxt >= 0)
            def _():
                for cp in weight_copies(nxt, 1 - slot):
                    cp.start()

            wgu_sc[...] = gu_buf[slot].astype(BF16)
            wdn_sc[...] = dn_buf[slot].astype(BF16)

        rows = lax.broadcasted_iota(I32, xs_ref.shape, 0)
        xu = jnp.where(rows < valid, pltpu.bitcast(xs_ref[...], U32), jnp.uint32(0))
        lo, hi = _unpack_bf16_pair(xu)
        half = lo.shape[1]
        gu = (jnp.dot(lo.astype(BF16), wgu_sc[:half, :], preferred_element_type=F32)
              + jnp.dot(hi.astype(BF16), wgu_sc[half:, :], preferred_element_type=F32))
        gt = gu[:, :EXPERT_HIDDEN]
        act = (gt * _sigmoid(gt) * gu[:, EXPERT_HIDDEN:]).astype(BF16)
        y = jnp.dot(act, wdn_sc[...], preferred_element_type=F32)
        ys_ref[...] = pltpu.bitcast(_pack_bf16_pair(y[:, :half], y[:, half:]), I32)

    @pl.when(valid <= 0)
    def _():
        ys_ref[...] = jnp.zeros(ys_ref.shape, I32)


def _experts(blk_e, blk_valid, next_expert, expert_slot, xs, w_gu, w_dn):
    cap, w = xs.shape
    n_e, d, h2 = w_gu.shape
    n_blocks = cap // EXPERT_BLOCK
    blk_next = next_expert[blk_e].astype(I32)
    blk_slot = expert_slot[blk_e].astype(I32)
    step = jnp.arange(n_blocks, dtype=I32)
    n_used = jnp.sum((blk_valid > 0).astype(I32))
    blk_in = jnp.minimum(step, jnp.maximum(n_used - 1, 0)).astype(I32)
    blk_out = jnp.where(blk_valid > 0, step, n_blocks).astype(I32)
    return pl.pallas_call(
        _experts_kernel,
        grid_spec=pltpu.PrefetchScalarGridSpec(
            num_scalar_prefetch=6,
            grid=(n_blocks,),
            in_specs=[pl.BlockSpec((EXPERT_BLOCK, w), lambda i, be, bv, bi, bo, bn, bs: (bi[i], 0)),
                      pl.BlockSpec(memory_space=pl.ANY),
                      pl.BlockSpec(memory_space=pl.ANY)],
            out_specs=pl.BlockSpec((EXPERT_BLOCK, w), lambda i, be, bv, bi, bo, bn, bs: (bo[i], 0)),
            scratch_shapes=[pltpu.VMEM((2, d, h2), F32), pltpu.VMEM((2, h2 // 2, d), F32),
                            pltpu.SemaphoreType.DMA((2, 2)),
                            pltpu.VMEM((d, h2), BF16), pltpu.VMEM((h2 // 2, d), BF16)]),
        out_shape=jax.ShapeDtypeStruct((cap + EXPERT_BLOCK, w), I32),
        compiler_params=_params(("arbitrary",)),
        name="experts",
    )(blk_e, blk_valid, blk_in, blk_out, blk_next, blk_slot, xs, w_gu, w_dn)


def _combine_kernel(yg_ref, wt_ref, hx_ref, x1_ref, gt2_ref, wsg_ref, wsd_ref, fg_ref, *rest):
    o_ref = rest[-1]
    lo, hi = _unpack_bf16_pair(pltpu.bitcast(hx_ref[...], U32))
    half = lo.shape[1]
    gu = (jnp.dot(lo.astype(BF16), wsg_ref[:half, :], preferred_element_type=F32)
          + jnp.dot(hi.astype(BF16), wsg_ref[half:, :], preferred_element_type=F32))
    gt = gu[:, :SHARED_HIDDEN]
    act = (gt * _sigmoid(gt) * gu[:, SHARED_HIDDEN:]).astype(BF16)
    y = jnp.dot(act, wsd_ref[...], preferred_element_type=F32)
    y_lo = y[:, :half]
    y_hi = y[:, half:]
    for kk in range(TOP_K):
        r_lo, r_hi = _unpack_bf16_pair(pltpu.bitcast(yg_ref[kk], U32))
        wk = wt_ref[:, kk:kk + 1]
        y_lo = y_lo + wk * r_lo
        y_hi = y_hi + wk * r_hi
    x2_lo = x1_ref[:, :half] + gt2_ref[0, :, :half] * y_lo
    x2_hi = x1_ref[:, half:] + gt2_ref[0, :, half:] * y_hi
    ms = (jnp.sum(x2_lo * x2_lo, axis=-1, keepdims=True)
          + jnp.sum(x2_hi * x2_hi, axis=-1, keepdims=True)) / (2 * half)
    inv = lax.rsqrt(ms + EPS)
    o_ref[:, :half] = x2_lo * inv * fg_ref[:, :half]
    o_ref[:, half:] = x2_hi * inv * fg_ref[:, half:]


def _combine(yg, wt, hx, x1, gt2, w_sg, w_sd, fg, tm, tiles_per_batch, tile0, out_prev):
    n, w = hx.shape
    d = 2 * w
    full = lambda a: pl.BlockSpec(a.shape, lambda i: (0,) * a.ndim)
    in_specs = [pl.BlockSpec((TOP_K, tm, w), lambda i: (0, i, 0)),
                pl.BlockSpec((tm, TOP_K), lambda i: (tile0 + i, 0)),
                pl.BlockSpec((tm, w), lambda i: (tile0 + i, 0)),
                pl.BlockSpec((tm, d), lambda i: (tile0 + i, 0)),
                pl.BlockSpec((1, 1, d), lambda i: ((tile0 + i) // tiles_per_batch, 0, 0)),
                full(w_sg), full(w_sd), full(fg)]
    args = [yg, wt, hx, x1, gt2, w_sg, w_sd, fg]
    aliases = {}
    if out_prev is not None:
        in_specs.append(pl.BlockSpec(memory_space=pl.ANY))
        args.append(out_prev)
        aliases = {len(args) - 1: 0}
    return pl.pallas_call(
        _combine_kernel,
        grid=(yg.shape[1] // tm,),
        in_specs=in_specs,
        out_specs=pl.BlockSpec((tm, d), lambda i: (tile0 + i, 0)),
        out_shape=jax.ShapeDtypeStruct((n, d), F32),
        input_output_aliases=aliases,
        compiler_params=_params(("arbitrary",)),
        name="combine",
    )(*args)


def _rope_tables(seq):
    rows = seq // GRID_W
    pos_row = np.repeat(np.arange(rows, dtype=np.float32), GRID_W)
    pos_col = np.tile(np.arange(GRID_W, dtype=np.float32), rows)
    inv_freq = (ROPE_THETA ** (-np.arange(0, AXIS_DIM, 2, dtype=np.float32) / AXIS_DIM)).astype(np.float32)
    ar = pos_row[:, None] * inv_freq
    ac = pos_col[:, None] * inv_freq
    cos_t = np.concatenate([np.cos(ar), np.cos(ar), np.cos(ac), np.cos(ac)], axis=1)
    sin_t = np.concatenate([-np.sin(ar), np.sin(ar), -np.sin(ac), np.sin(ac)], axis=1)
    return jnp.asarray(cos_t, F32), jnp.asarray(sin_t, F32)


def _tile(n, want):
    t = min(n, want)
    assert n % t == 0, (n, want)
    return t


def kernel(x, c, ctx, c_ctx, w_mod, b_mod, norm1_g, w_in, q_norm_g, k_norm_g, w_dw, b_dw, conv_ln_g, conv_ln_b, w_attn_proj, w_conv_proj, w_out, norm2_g, w_router, router_bias, w_exp_gu, w_exp_dn, w_sh_gu, w_sh_dn, final_g):
    b, s, d = x.shape
    depth = w_mod.shape[0]
    assert depth == 1, "single-layer block"
    n = b * s
    row = lambda v: v.reshape(1, -1)

    cc = jnp.zeros((SUBLANES, d), F32).at[:b].set(c).at[b].set(c_ctx)
    mod = _modulation(cc, w_mod[0], row(b_mod[0]))
    mod_x = mod[:b].reshape(b, 1, 6, d)
    sh1, sc1, gt1, sh2, sc2, gt2 = [mod_x[:, :, j, :] for j in range(6)]
    mod_c = mod[b].reshape(6, d)
    csh1, csc1 = row(mod_c[0]), row(mod_c[1])

    w_in_b = w_in[0].astype(BF16)
    q_end, kv_end = ATTN_WIDTH, ATTN_WIDTH + 2 * KV_WIDTH
    k_end = q_end + KV_WIDTH
    w_vt = w_in_b[:, k_end:kv_end].T
    kc, vct = _ctx_kv(ctx, csh1, csc1, row(norm1_g[0]), w_in_b[:, q_end:k_end], w_vt, row(k_norm_g[0]))

    cos_t, sin_t = _rope_tables(s)
    q, kx, vxt, hglu, gates = _in_proj(x, sh1, sc1, row(norm1_g[0]), w_in_b, w_vt, row(q_norm_g[0]),
                                       row(k_norm_g[0]), cos_t, sin_t, _tile(s, 512))
    score_bound = (HEAD_DIM * ATTN_SCALE * LOG2E) * jnp.max(jnp.abs(q_norm_g[0])) * jnp.max(jnp.abs(k_norm_g[0]))
    bounded = (score_bound <= SAFE_EXP2_ARG).astype(I32).reshape(1)
    o, conv = _attention(bounded, q, kc, vct, kx, vxt, hglu, w_dw[0], row(b_dw[0]),
                         _tile(s, 1024), _tile(s // 2, 1024))

    top16 = lambda a: lax.bitcast_convert_type(lax.bitcast_convert_type(a, U32) & jnp.uint32(0xFFFF0000), F32)
    w_rt = w_router[0].T
    w_rt1 = top16(w_rt)
    w_rt2 = top16(w_rt - w_rt1)
    w_rt3 = w_rt - w_rt1 - w_rt2
    w_rt1, w_rt2, w_rt3 = w_rt1.astype(BF16), w_rt2.astype(BF16), w_rt3.astype(BF16)
    x1, hx, logits_t = _merge(o, conv, gates, x, gt1, sh2, sc2, row(conv_ln_g[0]),
                              row(conv_ln_b[0]), w_attn_proj[0].astype(BF16), w_conv_proj[0].astype(BF16),
                              w_out[0].astype(BF16), row(norm2_g[0]), jnp.stack([w_rt1, w_rt2, w_rt3]),
                              _tile(s, 512))

    e_idx, wts, rank, counts = _route(logits_t, router_bias[0], _tile(n, 512))

    cnt = counts[:, 0]
    padded = (cnt + EXPERT_BLOCK - 1) // EXPERT_BLOCK * EXPERT_BLOCK
    pends = jnp.cumsum(padded)
    pstart = (pends - padded).astype(I32)
    n_blocks = (n * TOP_K + N_EXPERTS * (EXPERT_BLOCK - 1)) // EXPERT_BLOCK
    cap = n_blocks * EXPERT_BLOCK
    blk_row0 = jnp.arange(n_blocks, dtype=I32) * EXPERT_BLOCK
    blk_e = jnp.minimum(jnp.sum(pends[None, :] <= blk_row0[:, None], axis=1), N_EXPERTS - 1).astype(I32)
    blk_valid = jnp.clip(pstart[blk_e] + cnt[blk_e] - blk_row0, 0, EXPERT_BLOCK).astype(I32)

    pos = _slots(pstart, e_idx, rank, _tile(n, 4096))
    xs = _sc_dispatch(hx, pos, cap)
    has_rows = cnt > 0
    later = lax.cummin(jnp.where(has_rows, jnp.arange(N_EXPERTS, dtype=I32), N_EXPERTS), reverse=True)
    next_expert = jnp.concatenate([later[1:], jnp.full((1,), N_EXPERTS, I32)])
    next_expert = jnp.where(next_expert >= N_EXPERTS, -1, next_expert)
    expert_slot = (jnp.cumsum(has_rows.astype(I32)) - 1) % 2
    ys = _experts(blk_e, blk_valid, next_expert, expert_slot, xs, w_exp_gu[0], w_exp_dn[0])
    tm = _tile(n, 256)
    chunk = n // COMBINE_CHUNKS if n % (COMBINE_CHUNKS * SC_WINDOW * SC_WORKERS) == 0 else n
    wt_t, x1_2d = wts.T, x1.reshape(n, d)
    w_sg, w_sd = w_sh_gu[0].astype(BF16), w_sh_dn[0].astype(BF16)
    out = None
    for c0 in range(0, n, chunk):
        yg = _sc_gather(ys, pos[:, c0:c0 + chunk])
        out = _combine(yg, wt_t, hx, x1_2d, gt2, w_sg, w_sd, row(final_g), tm, s // tm, c0 // tm, out)
    return out.reshape(b, s, d)
```

```python
import functools
import math

import jax
import jax.numpy as jnp
import numpy as np
from jax import lax
from jax.experimental import pallas as pl
from jax.experimental.pallas import tpu as pltpu
from jax.experimental.pallas import tpu_sc as plsc

F32 = jnp.float32
BF16 = jnp.bfloat16
U32 = jnp.uint32
I32 = jnp.int32

GRID_W = 64
N_HEADS = 8
N_KV_HEADS = 4
GROUP = N_HEADS // N_KV_HEADS
HEAD_DIM = 128
AXIS_DIM = HEAD_DIM // 2
ATTN_WIDTH = N_HEADS * HEAD_DIM
KV_WIDTH = N_KV_HEADS * HEAD_DIM
ROPE_THETA = 10000.0
ATTN_SCALE = HEAD_DIM ** -0.5
CONV_WIDTH = 512
CONV_KERNEL = 31
CONV_PAD = CONV_KERNEL // 2
N_EXPERTS = 256
TOP_K = 8
N_GROUPS = 8
TOPK_GROUPS = 4
EXPERTS_PER_GROUP = N_EXPERTS // N_GROUPS
EXPERT_HIDDEN = 256
SHARED_HIDDEN = 256
ROUTED_SCALE = 2.5
EPS = 1e-6
LOG2E = 1.4426950408889634
SAFE_EXP2_ARG = 64.0

LANES = 128
SUBLANES = 8
VMEM_LIMIT_BYTES = 56 * 1024 * 1024

HALO_ROWS = 16
CONV_ROWS = 64
MERGE_PARTS = 2
EXPERT_BLOCK = 512
SC_SUBCORES = 16
SC_WORKERS = 2 * SC_SUBCORES
SC_WINDOW = 128
COMBINE_CHUNKS = 4
HIGHEST = lax.Precision.HIGHEST


def _params(sem):
    return pltpu.CompilerParams(dimension_semantics=sem, vmem_limit_bytes=VMEM_LIMIT_BYTES)


def _sigmoid(x):
    return 1.0 / (1.0 + jnp.exp(-x))


def _pack_bf16_pair(lo, hi):
    lo_b = pltpu.bitcast(lo.astype(BF16).astype(F32), U32)
    hi_b = pltpu.bitcast(hi.astype(BF16).astype(F32), U32)
    return (lo_b >> 16) | (hi_b & jnp.uint32(0xFFFF0000))


def _unpack_bf16_pair(u):
    lo = pltpu.bitcast(u << 16, F32)
    hi = pltpu.bitcast(u & jnp.uint32(0xFFFF0000), F32)
    return lo, hi


def _mod_kernel(cc_ref, w_ref, b_ref, o_ref):
    cc = cc_ref[...]
    s = cc * _sigmoid(cc)
    o_ref[...] = jnp.dot(s, w_ref[...], precision=HIGHEST, preferred_element_type=F32) + b_ref[...]


def _modulation(cc, w_mod, b_mod):
    d, n = w_mod.shape
    tn = n // 4
    return pl.pallas_call(
        _mod_kernel,
        grid=(n // tn,),
        in_specs=[pl.BlockSpec((SUBLANES, d), lambda j: (0, 0)),
                  pl.BlockSpec((d, tn), lambda j: (0, j)),
                  pl.BlockSpec((1, tn), lambda j: (0, j))],
        out_specs=pl.BlockSpec((SUBLANES, tn), lambda j: (0, j)),
        out_shape=jax.ShapeDtypeStruct((SUBLANES, n), F32),
        compiler_params=_params(("arbitrary",)),
        name="mod",
    )(cc, w_mod, b_mod)


def _norm_modulate(x, g, sh, sc):
    ms = jnp.mean(x * x, axis=-1, keepdims=True)
    return (x * lax.rsqrt(ms + EPS) * g) * (1.0 + sc) + sh


def _head_norm(p, gain):
    r = lax.rsqrt(jnp.mean(p * p, axis=-1, keepdims=True) + EPS)
    return p * r * gain


def _dot_nt(a, b):
    return lax.dot_general(a, b, (((1,), (1,)), ((), ())), preferred_element_type=F32)


def _ctx_kv_kernel(x_ref, sh_ref, sc_ref, g1_ref, wk_ref, wvt_ref, gk_ref, k_ref, vt_ref):
    h = _norm_modulate(x_ref[0], g1_ref[...], sh_ref[...], sc_ref[...]).astype(BF16)
    pk = jnp.dot(h, wk_ref[...], preferred_element_type=F32)
    for j in range(N_KV_HEADS):
        sl = slice(j * HEAD_DIM, (j + 1) * HEAD_DIM)
        k_ref[0, :, sl] = _head_norm(pk[:, sl], gk_ref[...]).astype(BF16)
    vt_ref[0] = _dot_nt(wvt_ref[...], h).astype(BF16)


def _ctx_kv(ctx, csh, csc, g1, w_k, w_vt, gk):
    b, lc, d = ctx.shape
    vec = lambda: pl.BlockSpec((1, d), lambda i: (0, 0))
    return pl.pallas_call(
        _ctx_kv_kernel,
        grid=(b,),
        in_specs=[pl.BlockSpec((1, lc, d), lambda i: (i, 0, 0)), vec(), vec(), vec(),
                  pl.BlockSpec((d, KV_WIDTH), lambda i: (0, 0)),
                  pl.BlockSpec((KV_WIDTH, d), lambda i: (0, 0)),
                  pl.BlockSpec((1, HEAD_DIM), lambda i: (0, 0))],
        out_specs=[pl.BlockSpec((1, lc, KV_WIDTH), lambda i: (i, 0, 0)),
                   pl.BlockSpec((1, KV_WIDTH, lc), lambda i: (i, 0, 0))],
        out_shape=[jax.ShapeDtypeStruct((b, lc, KV_WIDTH), BF16),
                   jax.ShapeDtypeStruct((b, KV_WIDTH, lc), BF16)],
        compiler_params=_params(("arbitrary",)),
        name="ctx_kv",
    )(ctx, csh, csc, g1, w_k, w_vt, gk)


def _in_proj_kernel(x_ref, sh_ref, sc_ref, g1_ref, w_ref, wvt_ref, gq_ref, gk_ref, cos_ref, sin_ref,
                    q_ref, k_ref, vt_ref, h_ref, g_ref):
    h = _norm_modulate(x_ref[0], g1_ref[...], sh_ref[0], sc_ref[0]).astype(BF16)
    cos = cos_ref[...]
    sin = sin_ref[...]
    lane = lax.broadcasted_iota(I32, cos.shape, 1)
    upper = (lane & (AXIS_DIM // 2)) != 0

    def rope(p):
        swapped = jnp.where(upper, pltpu.roll(p, AXIS_DIM // 2, 1),
                            pltpu.roll(p, HEAD_DIM - AXIS_DIM // 2, 1))
        return p * cos + swapped * sin

    q_end = ATTN_WIDTH
    k_end = q_end + KV_WIDTH
    v_end = k_end + KV_WIDTH
    u_end = v_end + 2 * CONV_WIDTH
    pq = jnp.dot(h, w_ref[:, :q_end], preferred_element_type=F32)
    for j in range(N_HEADS):
        sl = slice(j * HEAD_DIM, (j + 1) * HEAD_DIM)
        q_ref[0, :, sl] = (rope(_head_norm(pq[:, sl], gq_ref[...])) * (ATTN_SCALE * LOG2E)).astype(BF16)
    pk = jnp.dot(h, w_ref[:, q_end:k_end], preferred_element_type=F32)
    for j in range(N_KV_HEADS):
        sl = slice(j * HEAD_DIM, (j + 1) * HEAD_DIM)
        k_ref[0, :, sl] = rope(_head_norm(pk[:, sl], gk_ref[...])).astype(BF16)
    vt_ref[0] = _dot_nt(wvt_ref[...], h).astype(BF16)
    u = jnp.dot(h, w_ref[:, v_end:u_end], preferred_element_type=F32)
    h_ref[0] = (u[:, :CONV_WIDTH] * _sigmoid(u[:, CONV_WIDTH:])).astype(BF16)
    g_ref[0] = _sigmoid(jnp.dot(h, w_ref[:, u_end:], preferred_element_type=F32)).astype(BF16)


def _in_proj(x, sh1, sc1, g1, w_in, w_vt, gq, gk, cos_t, sin_t, tm):
    b, s, d = x.shape
    n_in = w_in.shape[1]
    bvec = lambda: pl.BlockSpec((1, 1, d), lambda bi, i: (bi, 0, 0))
    tok = lambda w: pl.BlockSpec((1, tm, w), lambda bi, i: (bi, i, 0))
    return pl.pallas_call(
        _in_proj_kernel,
        grid=(b, s // tm),
        in_specs=[tok(d), bvec(), bvec(),
                  pl.BlockSpec((1, d), lambda bi, i: (0, 0)),
                  pl.BlockSpec((d, n_in), lambda bi, i: (0, 0)),
                  pl.BlockSpec((KV_WIDTH, d), lambda bi, i: (0, 0)),
                  pl.BlockSpec((1, HEAD_DIM), lambda bi, i: (0, 0)),
                  pl.BlockSpec((1, HEAD_DIM), lambda bi, i: (0, 0)),
                  pl.BlockSpec((tm, HEAD_DIM), lambda bi, i: (i, 0)),
                  pl.BlockSpec((tm, HEAD_DIM), lambda bi, i: (i, 0))],
        out_specs=[tok(ATTN_WIDTH), tok(KV_WIDTH),
                   pl.BlockSpec((1, KV_WIDTH, tm), lambda bi, i: (bi, 0, i)),
                   tok(CONV_WIDTH), tok(2 * d)],
        out_shape=[jax.ShapeDtypeStruct((b, s, ATTN_WIDTH), BF16),
                   jax.ShapeDtypeStruct((b, s, KV_WIDTH), BF16),
                   jax.ShapeDtypeStruct((b, KV_WIDTH, s), BF16),
                   jax.ShapeDtypeStruct((b, s, CONV_WIDTH), BF16),
                   jax.ShapeDtypeStruct((b, s, 2 * d), BF16)],
        compiler_params=_params(("arbitrary", "arbitrary")),
        name="in_proj",
    )(x, sh1, sc1, g1, w_in, w_vt, gq, gk, cos_t, sin_t)


def _sub_allreduce(x, op):
    for s in (4, 2, 1):
        x = op(x, pltpu.roll(x, s, 0))
    return x


def _attn_kernel(bounded_ref, q_ref, kc_ref, vct_ref, k_ref, vt_ref, hp_ref, hc_ref, hn_ref, wdw_ref, bdw_ref,
                 o_ref, c_ref, s0, s1, x0, x1, m_sc, l_sc, acc_sc, hcat, shift_sc, *, tk):
    tq = q_ref.shape[1]
    m_cols = GROUP * tq
    nk = k_ref.shape[1] // tk
    qf = q_ref[0].astype(F32).T
    qt = jnp.concatenate([qf[:HEAD_DIM], qf[HEAD_DIM:]], axis=1).astype(BF16)
    slots = ((s0, x0), (s1, x1))
    n_conv = max(nk // 2, 1)
    conv_rows = tq // n_conv
    conv_base = HALO_ROWS - CONV_PAD
    conv_reach = (conv_base + CONV_KERNEL - 1) // SUBLANES * SUBLANES

    def conv_fill():
        ti = pl.program_id(2)
        prev = hp_ref[0].astype(F32)
        nxt = hn_ref[0].astype(F32)
        hcat[0:HALO_ROWS, :] = jnp.where(ti > 0, prev, jnp.zeros_like(prev))
        hcat[HALO_ROWS:HALO_ROWS + tq, :] = hc_ref[0].astype(F32)
        hcat[HALO_ROWS + tq:, :] = jnp.where(ti < pl.num_programs(2) - 1, nxt, jnp.zeros_like(nxt))

    def conv_block(blk):
        r0 = blk * conv_rows
        for sub in range(0, conv_rows, CONV_ROWS):
            start = r0 + sub if isinstance(r0, int) else pl.multiple_of(r0 + sub, CONV_ROWS)
            window = hcat[pl.ds(start, CONV_ROWS + 2 * HALO_ROWS), :]
            acc = jnp.zeros((CONV_ROWS, HEAD_DIM), F32) + bdw_ref[...]
            for res in range(SUBLANES):
                shift_sc[res] = window[res:res + CONV_ROWS + conv_reach, :]
                for off in range(res, conv_base + CONV_KERNEL, SUBLANES):
                    j = off - conv_base
                    if 0 <= j < CONV_KERNEL:
                        a0 = off - res
                        acc = acc + shift_sc[res, a0:a0 + CONV_ROWS, :] * wdw_ref[j:j + 1, :]
            c_ref[0, pl.ds(start, CONV_ROWS), :] = acc

    def kchunk(j):
        return k_ref[0, pl.ds(pl.multiple_of(j * tk, tk), tk), :]

    def vchunk(j):
        return vt_ref[0, :, pl.ds(pl.multiple_of(j * tk, tk), tk)]

    def split(st):
        return st.reshape(st.shape[0] // SUBLANES, SUBLANES, m_cols)

    def scores(k, online):
        st = jnp.dot(k, qt, preferred_element_type=F32)
        return st, (jnp.max(split(st), axis=0) if online else None)

    def absorb(st, mx, vt, online):
        s3 = split(st)
        if online:
            m_prev = m_sc[...]
            m_new = jnp.maximum(m_prev, _sub_allreduce(mx, jnp.maximum))
            alpha = jnp.exp2(m_prev - m_new)
            p3 = jnp.exp2(s3 - m_new[None])
            l_sc[...] = alpha * l_sc[...] + _sub_allreduce(jnp.sum(p3, axis=0), jnp.add)
            pv = jnp.dot(vt, p3.reshape(st.shape).astype(BF16), preferred_element_type=F32)
            acc_sc[...] = alpha[0:1] * acc_sc[...] + pv
            m_sc[...] = m_new
        else:
            p3 = jnp.exp2(s3)
            l_sc[...] = l_sc[...] + jnp.sum(p3, axis=0)
            acc_sc[...] = acc_sc[...] + jnp.dot(vt, p3.reshape(st.shape).astype(BF16),
                                                preferred_element_type=F32)

    def stage(slot, k, online):
        st, mx = scores(k, online)
        slots[slot][0][...] = st
        if online:
            slots[slot][1][...] = mx

    def take(slot, vt, online):
        absorb(slots[slot][0][...], slots[slot][1][...] if online else None, vt, online)

    def sweep(online):
        if online:
            m_sc[...] = jnp.full(m_sc.shape, -jnp.inf, F32)
        l_sc[...] = jnp.zeros(l_sc.shape, F32)
        acc_sc[...] = jnp.zeros(acc_sc.shape, F32)
        conv_fill()
        stage(0, kchunk(0), online)

        def body(i, carry):
            j = 2 * i
            stage(1, kchunk(j + 1), online)
            take(0, vchunk(j), online)
            conv_block(i)
            stage(0, kchunk(j + 2), online)
            take(1, vchunk(j + 1), online)
            return carry

        lax.fori_loop(0, nk // 2 - 1, body, 0)
        conv_block(n_conv - 1)
        stage(1, kchunk(nk - 1), online)
        take(0, vchunk(nk - 2), online)
        sc, xc = scores(kc_ref[0], online)
        take(1, vchunk(nk - 1), online)
        absorb(sc, xc, vct_ref[0], online)
        denom = l_sc[...] if online else _sub_allreduce(l_sc[...], jnp.add)
        o = (acc_sc[...] / denom[0:1]).T
        o_ref[0, :, :HEAD_DIM] = o[:tq].astype(BF16)
        o_ref[0, :, HEAD_DIM:] = o[tq:].astype(BF16)

    @pl.when(bounded_ref[0] != 0)
    def _():
        sweep(online=False)

    @pl.when(bounded_ref[0] == 0)
    def _():
        sweep(online=True)


def _attention(bounded, q, kc, vct, kx, vxt, hglu, w_dw, b_dw, tq, tk):
    b, s, _ = q.shape
    lc = kc.shape[1]
    assert s % (2 * tk) == 0 and CONV_WIDTH == N_KV_HEADS * HEAD_DIM
    gw = GROUP * HEAD_DIM
    m_cols = GROUP * tq
    hb = tq // HALO_ROWS
    n_halo = s // HALO_ROWS
    reach = (HALO_ROWS - CONV_PAD + CONV_KERNEL - 1) // SUBLANES * SUBLANES
    kv = lambda l: pl.BlockSpec((1, l, HEAD_DIM), lambda bi, h, i, bd: (bi, 0, h))
    kvt = lambda l: pl.BlockSpec((1, HEAD_DIM, l), lambda bi, h, i, bd: (bi, h, 0))
    qo = lambda: pl.BlockSpec((1, tq, gw), lambda bi, h, i, bd: (bi, i, h))
    chan = lambda rows: pl.BlockSpec((rows, HEAD_DIM), lambda bi, h, i, bd: (0, h))
    return pl.pallas_call(
        functools.partial(_attn_kernel, tk=tk),
        grid_spec=pltpu.PrefetchScalarGridSpec(
            num_scalar_prefetch=1,
            grid=(b, N_KV_HEADS, s // tq),
            in_specs=[qo(), kv(lc), kvt(lc), kv(s), kvt(s),
                      pl.BlockSpec((1, HALO_ROWS, HEAD_DIM),
                                   lambda bi, h, i, bd: (bi, jnp.maximum(i * hb - 1, 0), h)),
                      pl.BlockSpec((1, tq, HEAD_DIM), lambda bi, h, i, bd: (bi, i, h)),
                      pl.BlockSpec((1, HALO_ROWS, HEAD_DIM),
                                   lambda bi, h, i, bd: (bi, jnp.minimum((i + 1) * hb, n_halo - 1), h)),
                      chan(CONV_KERNEL), chan(1)],
            out_specs=[qo(), pl.BlockSpec((1, tq, HEAD_DIM), lambda bi, h, i, bd: (bi, i, h))],
            scratch_shapes=[pltpu.VMEM((tk, m_cols), F32), pltpu.VMEM((tk, m_cols), F32),
                            pltpu.VMEM((SUBLANES, m_cols), F32), pltpu.VMEM((SUBLANES, m_cols), F32),
                            pltpu.VMEM((SUBLANES, m_cols), F32), pltpu.VMEM((SUBLANES, m_cols), F32),
                            pltpu.VMEM((HEAD_DIM, m_cols), F32),
                            pltpu.VMEM((tq + 2 * HALO_ROWS, HEAD_DIM), F32),
                            pltpu.VMEM((SUBLANES, CONV_ROWS + reach, HEAD_DIM), F32)]),
        out_shape=[jax.ShapeDtypeStruct((b, s, ATTN_WIDTH), BF16),
                   jax.ShapeDtypeStruct((b, s, CONV_WIDTH), F32)],
        compiler_params=_params(("arbitrary", "arbitrary", "arbitrary")),
        name="attn",
    )(bounded, q, kc, vct, kx, vxt, hglu, hglu, hglu, w_dw, b_dw)


def _split3(x):
    x1 = x.astype(BF16)
    r1 = x - x1.astype(F32)
    x2 = r1.astype(BF16)
    x3 = (r1 - x2.astype(F32)).astype(BF16)
    return x1, x2, x3


def _merge_kernel(o_ref, c_ref, g_ref, x_ref, gt1_ref, sh2_ref, sc2_ref,
                  lng_ref, lnb_ref, wap_ref, wcp_ref, wout_ref, g2_ref,
                  x1_ref, hx_ref, hxf_ref):
    tm = x_ref.shape[1]
    d = x_ref.shape[2]
    half = d // 2
    rows_per_part = tm // MERGE_PARTS
    for part in range(MERGE_PARTS):
        p0 = part * rows_per_part
        rows = slice(p0, p0 + rows_per_part)
        conv = c_ref[0, rows, :]
        mu = jnp.mean(conv, axis=-1, keepdims=True)
        cen = conv - mu
        var = jnp.mean(cen * cen, axis=-1, keepdims=True)
        ln = cen * lax.rsqrt(var + EPS) * lng_ref[...] + lnb_ref[...]
        act = (ln * _sigmoid(ln)).astype(BF16)
        y_conv = jnp.dot(act, wcp_ref[...], preferred_element_type=F32)
        y_attn = jnp.dot(o_ref[0, rows, :], wap_ref[...], preferred_element_type=F32)
        z = g_ref[0, rows, :d].astype(F32) * y_attn + g_ref[0, rows, d:].astype(F32) * y_conv
        mix = jnp.dot(z.astype(BF16), wout_ref[...], preferred_element_type=F32)
        x1 = x_ref[0, rows, :] + gt1_ref[0] * mix
        x1_ref[0, rows, :] = x1
        hx = _norm_modulate(x1, g2_ref[...], sh2_ref[0], sc2_ref[0])
        hx_ref[rows, :] = pltpu.bitcast(_pack_bf16_pair(hx[:, :half], hx[:, half:]), I32)
        h1, h2, h3 = _split3(hx)
        hxf_ref[0, rows, :] = h1
        hxf_ref[1, rows, :] = h2
        hxf_ref[2, rows, :] = h3


def _merge(o, conv, g, x, gt1, sh2, sc2, ln_g, ln_b, w_ap, w_cp, w_out, g2, tm):
    b, s, d = x.shape
    nt = s // tm
    bvec = lambda: pl.BlockSpec((1, 1, d), lambda bi, i: (bi, 0, 0))
    full = lambda a: pl.BlockSpec(a.shape, lambda bi, i: (0,) * a.ndim)
    tok = lambda w: pl.BlockSpec((1, tm, w), lambda bi, i: (bi, i, 0))
    return pl.pallas_call(
        _merge_kernel,
        grid=(b, nt),
        in_specs=[tok(ATTN_WIDTH), tok(CONV_WIDTH), tok(2 * d), tok(d), bvec(), bvec(), bvec(),
                  full(ln_g), full(ln_b), full(w_ap), full(w_cp), full(w_out), full(g2)],
        out_specs=[tok(d),
                   pl.BlockSpec((tm, d // 2), lambda bi, i: (bi * nt + i, 0)),
                   pl.BlockSpec((3, tm, d), lambda bi, i: (0, bi * nt + i, 0))],
        out_shape=[jax.ShapeDtypeStruct((b, s, d), F32),
                   jax.ShapeDtypeStruct((b * s, d // 2), I32),
                   jax.ShapeDtypeStruct((3, b * s, d), BF16)],
        compiler_params=_params(("arbitrary", "arbitrary")),
        name="merge",
    )(o, conv, g, x, gt1, sh2, sc2, ln_g, ln_b, w_ap, w_cp, w_out, g2)


def _router_logits(w_ref, h_ref):
    h1, h2, h3 = h_ref[0], h_ref[1], h_ref[2]
    w1, w2, w3 = w_ref[0], w_ref[1], w_ref[2]
    return ((((_dot_nt(w3, h1) + _dot_nt(w1, h3)) + _dot_nt(w2, h2))
             + (_dot_nt(w2, h1) + _dot_nt(w1, h2))) + _dot_nt(w1, h1))


def _route_kernel(hx0_ref, hxa_ref, hxb_ref, wrt_ref, bias_ref, tri_ref, e_ref, w_ref, r_ref, cnt_ref,
                  lg_a, lg_b, run_sc):
    step = pl.program_id(0)
    tb = hxa_ref.shape[1]
    n_strips = tb // LANES
    nv = N_EXPERTS // SUBLANES
    gv = EXPERTS_PER_GROUP // SUBLANES

    @pl.when(step == 0)
    def _():
        run_sc[...] = jnp.zeros(run_sc.shape, F32)
        lg_a[...] = _router_logits(wrt_ref, hx0_ref)

    row = (lax.broadcasted_iota(I32, (nv, SUBLANES, LANES), 0) * SUBLANES
           + lax.broadcasted_iota(I32, (nv, SUBLANES, LANES), 1))
    sub = lax.broadcasted_iota(I32, (SUBLANES, LANES), 0)
    bias = bias_ref[...].reshape(nv, SUBLANES, LANES)
    neg_inf = jnp.float32(-jnp.inf)

    def topk_strip(lg, st, col0):
        lanes = slice(col0 + st * LANES, col0 + (st + 1) * LANES)
        scores = _sigmoid(lg[:, st * LANES:(st + 1) * LANES]).reshape(nv, SUBLANES, LANES)
        biased = scores + bias
        gscore = []
        for g in range(N_GROUPS):
            m1 = biased[g * gv]
            m2 = jnp.full((SUBLANES, LANES), neg_inf, F32)
            for t in range(1, gv):
                v = biased[g * gv + t]
                m2 = jnp.maximum(m2, jnp.minimum(m1, v))
                m1 = jnp.maximum(m1, v)
            for s in (4, 2, 1):
                p1 = pltpu.roll(m1, s, 0)
                p2 = pltpu.roll(m2, s, 0)
                m2 = jnp.maximum(jnp.minimum(m1, p1), jnp.maximum(m2, p2))
                m1 = jnp.maximum(m1, p1)
            gscore.append(m1 + m2)
        masked = []
        for g in range(N_GROUPS):
            beaten = jnp.zeros((SUBLANES, LANES), I32)
            for o in range(N_GROUPS):
                if o == g:
                    continue
                wins = (gscore[o] > gscore[g]) | ((gscore[o] == gscore[g]) & (o < g))
                beaten = beaten + wins.astype(I32)
            keep = beaten < TOPK_GROUPS
            for t in range(gv):
                masked.append(jnp.where(keep, biased[g * gv + t], neg_inf))
        cand = jnp.stack(masked, axis=0)
        sel = jnp.zeros((nv, SUBLANES, LANES), jnp.bool_)
        picks, pick_scores = [], []
        for _ in range(TOP_K):
            mx = _sub_allreduce(jnp.max(cand, axis=0), jnp.maximum)
            idx = _sub_allreduce(jnp.min(jnp.where(cand == mx, row, N_EXPERTS), axis=0), jnp.minimum)
            hit = row == idx
            pick_scores.append(_sub_allreduce(jnp.sum(jnp.where(hit, scores, 0.0), axis=0), jnp.add))
            picks.append(idx)
            sel = sel | hit
            cand = jnp.where(hit, neg_inf, cand)
        sel_b = sel.astype(F32).astype(BF16).reshape(N_EXPERTS, LANES)
        before = jnp.dot(sel_b, tri_ref[0], preferred_element_type=F32)
        total = jnp.dot(sel_b, tri_ref[1], preferred_element_type=F32)
        rank_all = (before + run_sc[...]).reshape(nv, SUBLANES, LANES)
        run_sc[...] = run_sc[...] + total
        denom = pick_scores[0]
        for kk in range(1, TOP_K):
            denom = denom + pick_scores[kk]
        e_out = jnp.zeros((SUBLANES, LANES), I32)
        w_out = jnp.zeros((SUBLANES, LANES), F32)
        r_out = jnp.zeros((SUBLANES, LANES), I32)
        for kk in range(TOP_K):
            rk = _sub_allreduce(jnp.sum(jnp.where(row == picks[kk], rank_all, 0.0), axis=0), jnp.add)
            e_out = jnp.where(sub == kk, picks[kk], e_out)
            w_out = jnp.where(sub == kk, pick_scores[kk] / denom * ROUTED_SCALE, w_out)
            r_out = jnp.where(sub == kk, rk.astype(I32), r_out)
        e_ref[:, lanes] = e_out
        w_ref[:, lanes] = w_out
        r_ref[:, lanes] = r_out

    lg_b[...] = _router_logits(wrt_ref, hxa_ref)
    for st in range(n_strips):
        topk_strip(lg_a, st, 0)
    lg_a[...] = _router_logits(wrt_ref, hxb_ref)
    for st in range(n_strips):
        topk_strip(lg_b, st, tb)
    cnt_ref[...] = run_sc[...].astype(I32)


def _route(hxf, w_rt3, bias, tb):
    _, n, d = hxf.shape
    nt = n // tb
    assert nt % 2 == 0
    iota_r = lax.broadcasted_iota(I32, (LANES, LANES), 0)
    iota_c = lax.broadcasted_iota(I32, (LANES, LANES), 1)
    tri = jnp.stack([(iota_r < iota_c), jnp.ones((LANES, LANES), jnp.bool_)]).astype(BF16)
    bias_b = jnp.broadcast_to(bias.reshape(N_EXPERTS, 1), (N_EXPERTS, LANES)).astype(F32)
    tokrow = lambda dt: jax.ShapeDtypeStruct((TOP_K, n), dt)
    out_blk = lambda: pl.BlockSpec((TOP_K, 2 * tb), lambda i: (0, i))
    return pl.pallas_call(
        _route_kernel,
        grid=(nt // 2,),
        in_specs=[pl.BlockSpec((3, tb, d), lambda i: (0, 0, 0)),
                  pl.BlockSpec((3, tb, d), lambda i: (0, 2 * i + 1, 0)),
                  pl.BlockSpec((3, tb, d), lambda i: (0, jnp.minimum(2 * i + 2, nt - 1), 0)),
                  pl.BlockSpec(w_rt3.shape, lambda i: (0, 0, 0)),
                  pl.BlockSpec((N_EXPERTS, LANES), lambda i: (0, 0)),
                  pl.BlockSpec((2, LANES, LANES), lambda i: (0, 0, 0))],
        out_specs=[out_blk(), out_blk(), out_blk(),
                   pl.BlockSpec((N_EXPERTS, LANES), lambda i: (0, 0))],
        out_shape=[tokrow(I32), tokrow(F32), tokrow(I32),
                   jax.ShapeDtypeStruct((N_EXPERTS, LANES), I32)],
        scratch_shapes=[pltpu.VMEM((N_EXPERTS, tb), F32), pltpu.VMEM((N_EXPERTS, tb), F32),
                        pltpu.VMEM((N_EXPERTS, LANES), F32)],
        compiler_params=_params(("arbitrary",)),
        name="route",
    )(hxf, hxf, hxf, w_rt3, bias_b, tri)


def _slots_kernel(pstart_ref, e_ref, r_ref, o_ref):
    e = e_ref[...]

    def body(x, acc):
        return acc + jnp.where(e == x, pstart_ref[x], 0)

    o_ref[...] = lax.fori_loop(0, N_EXPERTS, body, r_ref[...])


def _slots(pstart, e_idx, rank, tb):
    n = e_idx.shape[1]
    blk = lambda: pl.BlockSpec((TOP_K, tb), lambda i, ps: (0, i))
    return pl.pallas_call(
        _slots_kernel,
        grid_spec=pltpu.PrefetchScalarGridSpec(
            num_scalar_prefetch=1, grid=(n // tb,), in_specs=[blk(), blk()], out_specs=blk()),
        out_shape=jax.ShapeDtypeStruct((TOP_K, n), I32),
        compiler_params=_params(("arbitrary",)),
        name="slots",
    )(pstart, e_idx, rank)


def _sc_mesh():
    return plsc.VectorSubcoreMesh(core_axis_name="c", subcore_axis_name="s")


def _sc_worker():
    return lax.axis_index("c") * SC_SUBCORES + lax.axis_index("s")


def _sc_dispatch(hx, pos, cap):
    n, w = hx.shape
    per_worker = n // SC_WINDOW // SC_WORKERS
    assert per_worker * SC_WINDOW * SC_WORKERS == n

    @pl.kernel(out_type=jax.ShapeDtypeStruct((cap, w), hx.dtype), mesh=_sc_mesh(),
               scratch_types=[pltpu.VMEM((SC_WINDOW, w), hx.dtype), pltpu.VMEM((TOP_K, SC_WINDOW), I32),
                              pltpu.SemaphoreType.DMA])
    def scatter_rows(x_hbm, i_hbm, o_hbm, xbuf, ibuf, sem):
        wid = _sc_worker()

        @pl.loop(0, per_worker)
        def _(j):
            row0 = (wid * per_worker + j) * SC_WINDOW
            pltpu.sync_copy(x_hbm.at[pl.ds(row0, SC_WINDOW)], xbuf)
            pltpu.sync_copy(i_hbm.at[:, pl.ds(row0, SC_WINDOW)], ibuf)
            copies = [pltpu.async_copy(xbuf, o_hbm.at[ibuf.at[kk]], sem) for kk in range(TOP_K)]
            for cp in copies:
                cp.wait()

    return scatter_rows(hx, pos)


def _sc_gather(ys, pos):
    n = pos.shape[1]
    w = ys.shape[1]
    per_worker = n // SC_WINDOW // SC_WORKERS
    assert per_worker * SC_WINDOW * SC_WORKERS == n

    @pl.kernel(out_type=jax.ShapeDtypeStruct((TOP_K, n, w), ys.dtype), mesh=_sc_mesh(),
               scratch_types=[pltpu.VMEM((SC_WINDOW, w), ys.dtype), pltpu.VMEM((TOP_K, SC_WINDOW), I32)])
    def gather_rows(y_hbm, i_hbm, o_hbm, ybuf, ibuf):
        wid = _sc_worker()

        @pl.loop(0, per_worker)
        def _(j):
            row0 = (wid * per_worker + j) * SC_WINDOW
            pltpu.sync_copy(i_hbm.at[:, pl.ds(row0, SC_WINDOW)], ibuf)
            for kk in range(TOP_K):
                pltpu.sync_copy(y_hbm.at[ibuf.at[kk]], ybuf)
                pltpu.sync_copy(ybuf, o_hbm.at[kk, pl.ds(row0, SC_WINDOW)])

    return gather_rows(ys, pos)


def _experts_kernel(blk_e_ref, valid_ref, blk_in_ref, blk_out_ref, next_e_ref, slot_ref,
                    xs_ref, wgu_hbm, wdn_hbm, ys_ref, gu_buf, dn_buf, sem, wgu_sc, wdn_sc):
    del blk_in_ref, blk_out_ref
    i = pl.program_id(0)
    valid = valid_ref[i]

    def weight_copies(e, slot):
        return (pltpu.make_async_copy(wgu_hbm.at[e], gu_buf.at[slot], sem.at[0, slot]),
                pltpu.make_async_copy(wdn_hbm.at[e], dn_buf.at[slot], sem.at[1, slot]))

    @pl.when(valid > 0)
    def _():
        e = blk_e_ref[i]
        slot = slot_ref[i]
        prev = blk_e_ref[jnp.maximum(i - 1, 0)]

        @pl.when(i == 0)
        def _():
            for cp in weight_copies(e, slot):
                cp.start()

        @pl.when((i == 0) | (e != prev))
        def _():
            for cp in weight_copies(e, slot):
                cp.wait()
            nxt = next_e_ref[i]

            @pl.when(nxt >= 0)
            def _():
                for cp in weight_copies(nxt, 1 - slot):
                    cp.start()

            wgu_sc[...] = gu_buf[slot].astype(BF16)
            wdn_sc[...] = dn_buf[slot].astype(BF16)

        rows = lax.broadcasted_iota(I32, xs_ref.shape, 0)
        xu = jnp.where(rows < valid, pltpu.bitcast(xs_ref[...], U32), jnp.uint32(0))
        lo, hi = _unpack_bf16_pair(xu)
        half = lo.shape[1]
        gu = (jnp.dot(lo.astype(BF16), wgu_sc[:half, :], preferred_element_type=F32)
              + jnp.dot(hi.astype(BF16), wgu_sc[half:, :], preferred_element_type=F32))
        gt = gu[:, :EXPERT_HIDDEN]
        act = (gt * _sigmoid(gt) * gu[:, EXPERT_HIDDEN:]).astype(BF16)
        y = jnp.dot(act, wdn_sc[...], preferred_element_type=F32)
        ys_ref[...] = pltpu.bitcast(_pack_bf16_pair(y[:, :half], y[:, half:]), I32)

    @pl.when(valid <= 0)
    def _():
        ys_ref[...] = jnp.zeros(ys_ref.shape, I32)


def _experts(blk_e, blk_valid, next_expert, expert_slot, xs, w_gu, w_dn):
    cap, w = xs.shape
    n_e, d, h2 = w_gu.shape
    n_blocks = cap // EXPERT_BLOCK
    blk_next = next_expert[blk_e].astype(I32)
    blk_slot = expert_slot[blk_e].astype(I32)
    step = jnp.arange(n_blocks, dtype=I32)
    n_used = jnp.sum((blk_valid > 0).astype(I32))
    blk_in = jnp.minimum(step, jnp.maximum(n_used - 1, 0)).astype(I32)
    blk_out = jnp.where(blk_valid > 0, step, n_blocks).astype(I32)
    return pl.pallas_call(
        _experts_kernel,
        grid_spec=pltpu.PrefetchScalarGridSpec(
            num_scalar_prefetch=6,
            grid=(n_blocks,),
            in_specs=[pl.BlockSpec((EXPERT_BLOCK, w), lambda i, be, bv, bi, bo, bn, bs: (bi[i], 0)),
                      pl.BlockSpec(memory_space=pl.ANY),
                      pl.BlockSpec(memory_space=pl.ANY)],
            out_specs=pl.BlockSpec((EXPERT_BLOCK, w), lambda i, be, bv, bi, bo, bn, bs: (bo[i], 0)),
            scratch_shapes=[pltpu.VMEM((2, d, h2), F32), pltpu.VMEM((2, h2 // 2, d), F32),
                            pltpu.SemaphoreType.DMA((2, 2)),
                            pltpu.VMEM((d, h2), BF16), pltpu.VMEM((h2 // 2, d), BF16)]),
        out_shape=jax.ShapeDtypeStruct((cap + EXPERT_BLOCK, w), I32),
        compiler_params=_params(("arbitrary",)),
        name="experts",
    )(blk_e, blk_valid, blk_in, blk_out, blk_next, blk_slot, xs, w_gu, w_dn)


def _combine_kernel(yg_ref, wt_ref, hx_ref, x1_ref, gt2_ref, wsg_ref, wsd_ref, fg_ref, *rest):
    o_ref = rest[-1]
    lo, hi = _unpack_bf16_pair(pltpu.bitcast(hx_ref[...], U32))
    half = lo.shape[1]
    gu = (jnp.dot(lo.astype(BF16), wsg_ref[:half, :], preferred_element_type=F32)
          + jnp.dot(hi.astype(BF16), wsg_ref[half:, :], preferred_element_type=F32))
    gt = gu[:, :SHARED_HIDDEN]
    act = (gt * _sigmoid(gt) * gu[:, SHARED_HIDDEN:]).astype(BF16)
    y = jnp.dot(act, wsd_ref[...], preferred_element_type=F32)
    y_lo = y[:, :half]
    y_hi = y[:, half:]
    for kk in range(TOP_K):
        r_lo, r_hi = _unpack_bf16_pair(pltpu.bitcast(yg_ref[kk], U32))
        wk = wt_ref[:, kk:kk + 1]
        y_lo = y_lo + wk * r_lo
        y_hi = y_hi + wk * r_hi
    x2_lo = x1_ref[:, :half] + gt2_ref[0, :, :half] * y_lo
    x2_hi = x1_ref[:, half:] + gt2_ref[0, :, half:] * y_hi
    ms = (jnp.sum(x2_lo * x2_lo, axis=-1, keepdims=True)
          + jnp.sum(x2_hi * x2_hi, axis=-1, keepdims=True)) / (2 * half)
    inv = lax.rsqrt(ms + EPS)
    o_ref[:, :half] = x2_lo * inv * fg_ref[:, :half]
    o_ref[:, half:] = x2_hi * inv * fg_ref[:, half:]


def _combine(yg, wt, hx, x1, gt2, w_sg, w_sd, fg, tm, tiles_per_batch, tile0, out_prev):
    n, w = hx.shape
    d = 2 * w
    full = lambda a: pl.BlockSpec(a.shape, lambda i: (0,) * a.ndim)
    in_specs = [pl.BlockSpec((TOP_K, tm, w), lambda i: (0, i, 0)),
                pl.BlockSpec((tm, TOP_K), lambda i: (tile0 + i, 0)),
                pl.BlockSpec((tm, w), lambda i: (tile0 + i, 0)),
                pl.BlockSpec((tm, d), lambda i: (tile0 + i, 0)),
                pl.BlockSpec((1, 1, d), lambda i: ((tile0 + i) // tiles_per_batch, 0, 0)),
                full(w_sg), full(w_sd), full(fg)]
    args = [yg, wt, hx, x1, gt2, w_sg, w_sd, fg]
    aliases = {}
    if out_prev is not None:
        in_specs.append(pl.BlockSpec(memory_space=pl.ANY))
        args.append(out_prev)
        aliases = {len(args) - 1: 0}
    return pl.pallas_call(
        _combine_kernel,
        grid=(yg.shape[1] // tm,),
        in_specs=in_specs,
        out_specs=pl.BlockSpec((tm, d), lambda i: (tile0 + i, 0)),
        out_shape=jax.ShapeDtypeStruct((n, d), F32),
        input_output_aliases=aliases,
        compiler_params=_params(("arbitrary",)),
        name="combine",
    )(*args)


def _rope_tables(seq):
    rows = seq // GRID_W
    pos_row = np.repeat(np.arange(rows, dtype=np.float32), GRID_W)
    pos_col = np.tile(np.arange(GRID_W, dtype=np.float32), rows)
    inv_freq = (ROPE_THETA ** (-np.arange(0, AXIS_DIM, 2, dtype=np.float32) / AXIS_DIM)).astype(np.float32)
    ar = pos_row[:, None] * inv_freq
    ac = pos_col[:, None] * inv_freq
    cos_t = np.concatenate([np.cos(ar), np.cos(ar), np.cos(ac), np.cos(ac)], axis=1)
    sin_t = np.concatenate([-np.sin(ar), np.sin(ar), -np.sin(ac), np.sin(ac)], axis=1)
    return jnp.asarray(cos_t, F32), jnp.asarray(sin_t, F32)


def _tile(n, want):
    t = min(n, want)
    assert n % t == 0, (n, want)
    return t


def kernel(x, c, ctx, c_ctx, w_mod, b_mod, norm1_g, w_in, q_norm_g, k_norm_g, w_dw, b_dw, conv_ln_g, conv_ln_b, w_attn_proj, w_conv_proj, w_out, norm2_g, w_router, router_bias, w_exp_gu, w_exp_dn, w_sh_gu, w_sh_dn, final_g):
    b, s, d = x.shape
    depth = w_mod.shape[0]
    assert depth == 1, "single-layer block"
    n = b * s
    row = lambda v: v.reshape(1, -1)

    cc = jnp.zeros((SUBLANES, d), F32).at[:b].set(c).at[b].set(c_ctx)
    mod = _modulation(cc, w_mod[0], row(b_mod[0]))
    mod_x = mod[:b].reshape(b, 1, 6, d)
    sh1, sc1, gt1, sh2, sc2, gt2 = [mod_x[:, :, j, :] for j in range(6)]
    mod_c = mod[b].reshape(6, d)
    csh1, csc1 = row(mod_c[0]), row(mod_c[1])

    w_in_b = w_in[0].astype(BF16)
    q_end, kv_end = ATTN_WIDTH, ATTN_WIDTH + 2 * KV_WIDTH
    k_end = q_end + KV_WIDTH
    w_vt = w_in_b[:, k_end:kv_end].T
    kc, vct = _ctx_kv(ctx, csh1, csc1, row(norm1_g[0]), w_in_b[:, q_end:k_end], w_vt, row(k_norm_g[0]))

    cos_t, sin_t = _rope_tables(s)
    q, kx, vxt, hglu, gates = _in_proj(x, sh1, sc1, row(norm1_g[0]), w_in_b, w_vt, row(q_norm_g[0]),
                                       row(k_norm_g[0]), cos_t, sin_t, _tile(s, 512))
    score_bound = (HEAD_DIM * ATTN_SCALE * LOG2E) * jnp.max(jnp.abs(q_norm_g[0])) * jnp.max(jnp.abs(k_norm_g[0]))
    bounded = (score_bound <= SAFE_EXP2_ARG).astype(I32).reshape(1)
    o, conv = _attention(bounded, q, kc, vct, kx, vxt, hglu, w_dw[0], row(b_dw[0]),
                         _tile(s, 1024), _tile(s // 2, 1024))

    top16 = lambda a: lax.bitcast_convert_type(lax.bitcast_convert_type(a, U32) & jnp.uint32(0xFFFF0000), F32)
    w_rt = w_router[0].T
    w_rt1 = top16(w_rt)
    w_rt2 = top16(w_rt - w_rt1)
    w_rt3 = w_rt - w_rt1 - w_rt2
    w_rt1, w_rt2, w_rt3 = w_rt1.astype(BF16), w_rt2.astype(BF16), w_rt3.astype(BF16)
    x1, hx, hxf = _merge(o, conv, gates, x, gt1, sh2, sc2, row(conv_ln_g[0]),
                         row(conv_ln_b[0]), w_attn_proj[0].astype(BF16), w_conv_proj[0].astype(BF16),
                         w_out[0].astype(BF16), row(norm2_g[0]), _tile(s, 512))

    e_idx, wts, rank, counts = _route(hxf, jnp.stack([w_rt1, w_rt2, w_rt3]), router_bias[0],
                                      _tile(n // 2, 512))

    cnt = counts[:, 0]
    padded = (cnt + EXPERT_BLOCK - 1) // EXPERT_BLOCK * EXPERT_BLOCK
    pends = jnp.cumsum(padded)
    pstart = (pends - padded).astype(I32)
    n_blocks = (n * TOP_K + N_EXPERTS * (EXPERT_BLOCK - 1)) // EXPERT_BLOCK
    cap = n_blocks * EXPERT_BLOCK
    blk_row0 = jnp.arange(n_blocks, dtype=I32) * EXPERT_BLOCK
    blk_e = jnp.minimum(jnp.sum(pends[None, :] <= blk_row0[:, None], axis=1), N_EXPERTS - 1).astype(I32)
    blk_valid = jnp.clip(pstart[blk_e] + cnt[blk_e] - blk_row0, 0, EXPERT_BLOCK).astype(I32)

    pos = _slots(pstart, e_idx, rank, _tile(n, 4096))
    xs = _sc_dispatch(hx, pos, cap)
    has_rows = cnt > 0
    later = lax.cummin(jnp.where(has_rows, jnp.arange(N_EXPERTS, dtype=I32), N_EXPERTS), reverse=True)
    next_expert = jnp.concatenate([later[1:], jnp.full((1,), N_EXPERTS, I32)])
    next_expert = jnp.where(next_expert >= N_EXPERTS, -1, next_expert)
    expert_slot = (jnp.cumsum(has_rows.astype(I32)) - 1) % 2
    ys = _experts(blk_e, blk_valid, next_expert, expert_slot, xs, w_exp_gu[0], w_exp_dn[0])
    tm = _tile(n, 256)
    chunk = n // COMBINE_CHUNKS if n % (COMBINE_CHUNKS * SC_WINDOW * SC_WORKERS) == 0 else n
    wt_t, x1_2d = wts.T, x1.reshape(n, d)
    w_sg, w_sd = w_sh_gu[0].astype(BF16), w_sh_dn[0].astype(BF16)
    out = None
    for c0 in range(0, n, chunk):
        yg = _sc_gather(ys, pos[:, c0:c0 + chunk])
        out = _combine(yg, wt_t, hx, x1_2d, gt2, w_sg, w_sd, row(final_g), tm, s // tm, c0 // tm, out)
    return out.reshape(b, s, d)
```

```python
import functools
import math

import jax
import jax.numpy as jnp
import numpy as np
from jax import lax
from jax.experimental import pallas as pl
from jax.experimental.pallas import tpu as pltpu
from jax.experimental.pallas import tpu_sc as plsc

F32 = jnp.float32
BF16 = jnp.bfloat16
U32 = jnp.uint32
I32 = jnp.int32

GRID_W = 64
N_HEADS = 8
N_KV_HEADS = 4
GROUP = N_HEADS // N_KV_HEADS
HEAD_DIM = 128
AXIS_DIM = HEAD_DIM // 2
ATTN_WIDTH = N_HEADS * HEAD_DIM
KV_WIDTH = N_KV_HEADS * HEAD_DIM
ROPE_THETA = 10000.0
ATTN_SCALE = HEAD_DIM ** -0.5
CONV_WIDTH = 512
CONV_KERNEL = 31
CONV_PAD = CONV_KERNEL // 2
N_EXPERTS = 256
TOP_K = 8
N_GROUPS = 8
TOPK_GROUPS = 4
EXPERTS_PER_GROUP = N_EXPERTS // N_GROUPS
EXPERT_HIDDEN = 256
SHARED_HIDDEN = 256
ROUTED_SCALE = 2.5
EPS = 1e-6
LOG2E = 1.4426950408889634
SAFE_EXP2_ARG = 64.0

LANES = 128
SUBLANES = 8
VMEM_LIMIT_BYTES = 56 * 1024 * 1024

HALO_ROWS = 16
CONV_ROWS = 64
MERGE_PARTS = 2
EXPERT_BLOCK = 512
EXPERT_BLOCKS_PER_STEP = 2
SC_SUBCORES = 16
SC_WORKERS = 2 * SC_SUBCORES
SC_WINDOW = 128
COMBINE_CHUNKS = 4
HIGHEST = lax.Precision.HIGHEST


def _params(sem):
    return pltpu.CompilerParams(dimension_semantics=sem, vmem_limit_bytes=VMEM_LIMIT_BYTES)


def _sigmoid(x):
    return 1.0 / (1.0 + jnp.exp(-x))


def _pack_bf16_pair(lo, hi):
    lo_b = pltpu.bitcast(lo.astype(BF16).astype(F32), U32)
    hi_b = pltpu.bitcast(hi.astype(BF16).astype(F32), U32)
    return (lo_b >> 16) | (hi_b & jnp.uint32(0xFFFF0000))


def _unpack_bf16_pair(u):
    lo = pltpu.bitcast(u << 16, F32)
    hi = pltpu.bitcast(u & jnp.uint32(0xFFFF0000), F32)
    return lo, hi


def _mod_kernel(cc_ref, w_ref, b_ref, o_ref):
    cc = cc_ref[...]
    s = cc * _sigmoid(cc)
    o_ref[...] = jnp.dot(s, w_ref[...], precision=HIGHEST, preferred_element_type=F32) + b_ref[...]


def _modulation(cc, w_mod, b_mod):
    d, n = w_mod.shape
    tn = n // 4
    return pl.pallas_call(
        _mod_kernel,
        grid=(n // tn,),
        in_specs=[pl.BlockSpec((SUBLANES, d), lambda j: (0, 0)),
                  pl.BlockSpec((d, tn), lambda j: (0, j)),
                  pl.BlockSpec((1, tn), lambda j: (0, j))],
        out_specs=pl.BlockSpec((SUBLANES, tn), lambda j: (0, j)),
        out_shape=jax.ShapeDtypeStruct((SUBLANES, n), F32),
        compiler_params=_params(("arbitrary",)),
        name="mod",
    )(cc, w_mod, b_mod)


def _norm_modulate(x, g, sh, sc):
    ms = jnp.mean(x * x, axis=-1, keepdims=True)
    return (x * lax.rsqrt(ms + EPS) * g) * (1.0 + sc) + sh


def _head_norm(p, gain):
    r = lax.rsqrt(jnp.mean(p * p, axis=-1, keepdims=True) + EPS)
    return p * r * gain


def _dot_nt(a, b):
    return lax.dot_general(a, b, (((1,), (1,)), ((), ())), preferred_element_type=F32)


def _ctx_kv_kernel(x_ref, sh_ref, sc_ref, g1_ref, wk_ref, wvt_ref, gk_ref, k_ref, vt_ref):
    h = _norm_modulate(x_ref[0], g1_ref[...], sh_ref[...], sc_ref[...]).astype(BF16)
    pk = jnp.dot(h, wk_ref[...], preferred_element_type=F32)
    for j in range(N_KV_HEADS):
        sl = slice(j * HEAD_DIM, (j + 1) * HEAD_DIM)
        k_ref[0, :, sl] = _head_norm(pk[:, sl], gk_ref[...]).astype(BF16)
    vt_ref[0] = _dot_nt(wvt_ref[...], h).astype(BF16)


def _ctx_kv(ctx, csh, csc, g1, w_k, w_vt, gk):
    b, lc, d = ctx.shape
    vec = lambda: pl.BlockSpec((1, d), lambda i: (0, 0))
    return pl.pallas_call(
        _ctx_kv_kernel,
        grid=(b,),
        in_specs=[pl.BlockSpec((1, lc, d), lambda i: (i, 0, 0)), vec(), vec(), vec(),
                  pl.BlockSpec((d, KV_WIDTH), lambda i: (0, 0)),
                  pl.BlockSpec((KV_WIDTH, d), lambda i: (0, 0)),
                  pl.BlockSpec((1, HEAD_DIM), lambda i: (0, 0))],
        out_specs=[pl.BlockSpec((1, lc, KV_WIDTH), lambda i: (i, 0, 0)),
                   pl.BlockSpec((1, KV_WIDTH, lc), lambda i: (i, 0, 0))],
        out_shape=[jax.ShapeDtypeStruct((b, lc, KV_WIDTH), BF16),
                   jax.ShapeDtypeStruct((b, KV_WIDTH, lc), BF16)],
        compiler_params=_params(("arbitrary",)),
        name="ctx_kv",
    )(ctx, csh, csc, g1, w_k, w_vt, gk)


def _in_proj_kernel(x_ref, sh_ref, sc_ref, g1_ref, w_ref, wvt_ref, gq_ref, gk_ref, cos_ref, sin_ref,
                    q_ref, k_ref, vt_ref, h_ref, g_ref):
    h = _norm_modulate(x_ref[0], g1_ref[...], sh_ref[0], sc_ref[0]).astype(BF16)
    cos = cos_ref[...]
    sin = sin_ref[...]
    lane = lax.broadcasted_iota(I32, cos.shape, 1)
    upper = (lane & (AXIS_DIM // 2)) != 0

    def rope(p):
        swapped = jnp.where(upper, pltpu.roll(p, AXIS_DIM // 2, 1),
                            pltpu.roll(p, HEAD_DIM - AXIS_DIM // 2, 1))
        return p * cos + swapped * sin

    q_end = ATTN_WIDTH
    k_end = q_end + KV_WIDTH
    v_end = k_end + KV_WIDTH
    u_end = v_end + 2 * CONV_WIDTH
    pq = jnp.dot(h, w_ref[:, :q_end], preferred_element_type=F32)
    for j in range(N_HEADS):
        sl = slice(j * HEAD_DIM, (j + 1) * HEAD_DIM)
        q_ref[0, :, sl] = (rope(_head_norm(pq[:, sl], gq_ref[...])) * (ATTN_SCALE * LOG2E)).astype(BF16)
    pk = jnp.dot(h, w_ref[:, q_end:k_end], preferred_element_type=F32)
    for j in range(N_KV_HEADS):
        sl = slice(j * HEAD_DIM, (j + 1) * HEAD_DIM)
        k_ref[0, :, sl] = rope(_head_norm(pk[:, sl], gk_ref[...])).astype(BF16)
    vt_ref[0] = _dot_nt(wvt_ref[...], h).astype(BF16)
    u = jnp.dot(h, w_ref[:, v_end:u_end], preferred_element_type=F32)
    h_ref[0] = (u[:, :CONV_WIDTH] * _sigmoid(u[:, CONV_WIDTH:])).astype(BF16)
    g_ref[0] = _sigmoid(jnp.dot(h, w_ref[:, u_end:], preferred_element_type=F32)).astype(BF16)


def _in_proj(x, sh1, sc1, g1, w_in, w_vt, gq, gk, cos_t, sin_t, tm):
    b, s, d = x.shape
    n_in = w_in.shape[1]
    bvec = lambda: pl.BlockSpec((1, 1, d), lambda bi, i: (bi, 0, 0))
    tok = lambda w: pl.BlockSpec((1, tm, w), lambda bi, i: (bi, i, 0))
    return pl.pallas_call(
        _in_proj_kernel,
        grid=(b, s // tm),
        in_specs=[tok(d), bvec(), bvec(),
                  pl.BlockSpec((1, d), lambda bi, i: (0, 0)),
                  pl.BlockSpec((d, n_in), lambda bi, i: (0, 0)),
                  pl.BlockSpec((KV_WIDTH, d), lambda bi, i: (0, 0)),
                  pl.BlockSpec((1, HEAD_DIM), lambda bi, i: (0, 0)),
                  pl.BlockSpec((1, HEAD_DIM), lambda bi, i: (0, 0)),
                  pl.BlockSpec((tm, HEAD_DIM), lambda bi, i: (i, 0)),
                  pl.BlockSpec((tm, HEAD_DIM), lambda bi, i: (i, 0))],
        out_specs=[tok(ATTN_WIDTH), tok(KV_WIDTH),
                   pl.BlockSpec((1, KV_WIDTH, tm), lambda bi, i: (bi, 0, i)),
                   tok(CONV_WIDTH), tok(2 * d)],
        out_shape=[jax.ShapeDtypeStruct((b, s, ATTN_WIDTH), BF16),
                   jax.ShapeDtypeStruct((b, s, KV_WIDTH), BF16),
                   jax.ShapeDtypeStruct((b, KV_WIDTH, s), BF16),
                   jax.ShapeDtypeStruct((b, s, CONV_WIDTH), BF16),
                   jax.ShapeDtypeStruct((b, s, 2 * d), BF16)],
        compiler_params=_params(("arbitrary", "arbitrary")),
        name="in_proj",
    )(x, sh1, sc1, g1, w_in, w_vt, gq, gk, cos_t, sin_t)


def _sub_allreduce(x, op):
    for s in (4, 2, 1):
        x = op(x, pltpu.roll(x, s, 0))
    return x


def _attn_kernel(bounded_ref, q_ref, kc_ref, vct_ref, k_ref, vt_ref, hp_ref, hc_ref, hn_ref, wdw_ref, bdw_ref,
                 o_ref, c_ref, s0, s1, x0, x1, m_sc, l_sc, acc_sc, hcat, shift_sc, *, tk):
    tq = q_ref.shape[1]
    m_cols = GROUP * tq
    nk = k_ref.shape[1] // tk
    qf = q_ref[0].astype(F32).T
    qt = jnp.concatenate([qf[:HEAD_DIM], qf[HEAD_DIM:]], axis=1).astype(BF16)
    slots = ((s0, x0), (s1, x1))
    n_conv = max(nk // 2, 1)
    conv_rows = tq // n_conv
    conv_base = HALO_ROWS - CONV_PAD
    conv_reach = (conv_base + CONV_KERNEL - 1) // SUBLANES * SUBLANES

    def conv_fill():
        ti = pl.program_id(2)
        prev = hp_ref[0].astype(F32)
        nxt = hn_ref[0].astype(F32)
        hcat[0:HALO_ROWS, :] = jnp.where(ti > 0, prev, jnp.zeros_like(prev))
        hcat[HALO_ROWS:HALO_ROWS + tq, :] = hc_ref[0].astype(F32)
        hcat[HALO_ROWS + tq:, :] = jnp.where(ti < pl.num_programs(2) - 1, nxt, jnp.zeros_like(nxt))

    def conv_block(blk):
        r0 = blk * conv_rows
        for sub in range(0, conv_rows, CONV_ROWS):
            start = r0 + sub if isinstance(r0, int) else pl.multiple_of(r0 + sub, CONV_ROWS)
            window = hcat[pl.ds(start, CONV_ROWS + 2 * HALO_ROWS), :]
            acc = jnp.zeros((CONV_ROWS, HEAD_DIM), F32) + bdw_ref[...]
            for res in range(SUBLANES):
                shift_sc[res] = window[res:res + CONV_ROWS + conv_reach, :]
                for off in range(res, conv_base + CONV_KERNEL, SUBLANES):
                    j = off - conv_base
                    if 0 <= j < CONV_KERNEL:
                        a0 = off - res
                        acc = acc + shift_sc[res, a0:a0 + CONV_ROWS, :] * wdw_ref[j:j + 1, :]
            c_ref[0, pl.ds(start, CONV_ROWS), :] = acc

    def kchunk(j):
        return k_ref[0, pl.ds(pl.multiple_of(j * tk, tk), tk), :]

    def vchunk(j):
        return vt_ref[0, :, pl.ds(pl.multiple_of(j * tk, tk), tk)]

    def split(st):
        return st.reshape(st.shape[0] // SUBLANES, SUBLANES, m_cols)

    def scores(k, online):
        st = jnp.dot(k, qt, preferred_element_type=F32)
        return st, (jnp.max(split(st), axis=0) if online else None)

    def absorb(st, mx, vt, online):
        s3 = split(st)
        if online:
            m_prev = m_sc[...]
            m_new = jnp.maximum(m_prev, _sub_allreduce(mx, jnp.maximum))
            alpha = jnp.exp2(m_prev - m_new)
            p3 = jnp.exp2(s3 - m_new[None])
            l_sc[...] = alpha * l_sc[...] + _sub_allreduce(jnp.sum(p3, axis=0), jnp.add)
            pv = jnp.dot(vt, p3.reshape(st.shape).astype(BF16), preferred_element_type=F32)
            acc_sc[...] = alpha[0:1] * acc_sc[...] + pv
            m_sc[...] = m_new
        else:
            p3 = jnp.exp2(s3)
            l_sc[...] = l_sc[...] + jnp.sum(p3, axis=0)
            acc_sc[...] = acc_sc[...] + jnp.dot(vt, p3.reshape(st.shape).astype(BF16),
                                                preferred_element_type=F32)

    def stage(slot, k, online):
        st, mx = scores(k, online)
        slots[slot][0][...] = st
        if online:
            slots[slot][1][...] = mx

    def take(slot, vt, online):
        absorb(slots[slot][0][...], slots[slot][1][...] if online else None, vt, online)

    def sweep(online):
        if online:
            m_sc[...] = jnp.full(m_sc.shape, -jnp.inf, F32)
        l_sc[...] = jnp.zeros(l_sc.shape, F32)
        acc_sc[...] = jnp.zeros(acc_sc.shape, F32)
        conv_fill()
        stage(0, kchunk(0), online)

        def body(i, carry):
            j = 2 * i
            stage(1, kchunk(j + 1), online)
            take(0, vchunk(j), online)
            conv_block(i)
            stage(0, kchunk(j + 2), online)
            take(1, vchunk(j + 1), online)
            return carry

        lax.fori_loop(0, nk // 2 - 1, body, 0)
        conv_block(n_conv - 1)
        stage(1, kchunk(nk - 1), online)
        take(0, vchunk(nk - 2), online)
        sc, xc = scores(kc_ref[0], online)
        take(1, vchunk(nk - 1), online)
        absorb(sc, xc, vct_ref[0], online)
        denom = l_sc[...] if online else _sub_allreduce(l_sc[...], jnp.add)
        o = (acc_sc[...] / denom[0:1]).T
        o_ref[0, :, :HEAD_DIM] = o[:tq].astype(BF16)
        o_ref[0, :, HEAD_DIM:] = o[tq:].astype(BF16)

    @pl.when(bounded_ref[0] != 0)
    def _():
        sweep(online=False)

    @pl.when(bounded_ref[0] == 0)
    def _():
        sweep(online=True)


def _attention(bounded, q, kc, vct, kx, vxt, hglu, w_dw, b_dw, tq, tk):
    b, s, _ = q.shape
    lc = kc.shape[1]
    assert s % (2 * tk) == 0 and CONV_WIDTH == N_KV_HEADS * HEAD_DIM
    gw = GROUP * HEAD_DIM
    m_cols = GROUP * tq
    hb = tq // HALO_ROWS
    n_halo = s // HALO_ROWS
    reach = (HALO_ROWS - CONV_PAD + CONV_KERNEL - 1) // SUBLANES * SUBLANES
    kv = lambda l: pl.BlockSpec((1, l, HEAD_DIM), lambda bi, h, i, bd: (bi, 0, h))
    kvt = lambda l: pl.BlockSpec((1, HEAD_DIM, l), lambda bi, h, i, bd: (bi, h, 0))
    qo = lambda: pl.BlockSpec((1, tq, gw), lambda bi, h, i, bd: (bi, i, h))
    chan = lambda rows: pl.BlockSpec((rows, HEAD_DIM), lambda bi, h, i, bd: (0, h))
    return pl.pallas_call(
        functools.partial(_attn_kernel, tk=tk),
        grid_spec=pltpu.PrefetchScalarGridSpec(
            num_scalar_prefetch=1,
            grid=(b, N_KV_HEADS, s // tq),
            in_specs=[qo(), kv(lc), kvt(lc), kv(s), kvt(s),
                      pl.BlockSpec((1, HALO_ROWS, HEAD_DIM),
                                   lambda bi, h, i, bd: (bi, jnp.maximum(i * hb - 1, 0), h)),
                      pl.BlockSpec((1, tq, HEAD_DIM), lambda bi, h, i, bd: (bi, i, h)),
                      pl.BlockSpec((1, HALO_ROWS, HEAD_DIM),
                                   lambda bi, h, i, bd: (bi, jnp.minimum((i + 1) * hb, n_halo - 1), h)),
                      chan(CONV_KERNEL), chan(1)],
            out_specs=[qo(), pl.BlockSpec((1, tq, HEAD_DIM), lambda bi, h, i, bd: (bi, i, h))],
            scratch_shapes=[pltpu.VMEM((tk, m_cols), F32), pltpu.VMEM((tk, m_cols), F32),
                            pltpu.VMEM((SUBLANES, m_cols), F32), pltpu.VMEM((SUBLANES, m_cols), F32),
                            pltpu.VMEM((SUBLANES, m_cols), F32), pltpu.VMEM((SUBLANES, m_cols), F32),
                            pltpu.VMEM((HEAD_DIM, m_cols), F32),
                            pltpu.VMEM((tq + 2 * HALO_ROWS, HEAD_DIM), F32),
                            pltpu.VMEM((SUBLANES, CONV_ROWS + reach, HEAD_DIM), F32)]),
        out_shape=[jax.ShapeDtypeStruct((b, s, ATTN_WIDTH), BF16),
                   jax.ShapeDtypeStruct((b, s, CONV_WIDTH), F32)],
        compiler_params=_params(("arbitrary", "arbitrary", "arbitrary")),
        name="attn",
    )(bounded, q, kc, vct, kx, vxt, hglu, hglu, hglu, w_dw, b_dw)


def _split3(x):
    x1 = x.astype(BF16)
    r1 = x - x1.astype(F32)
    x2 = r1.astype(BF16)
    x3 = (r1 - x2.astype(F32)).astype(BF16)
    return x1, x2, x3


def _merge_kernel(o_ref, c_ref, g_ref, x_ref, gt1_ref, sh2_ref, sc2_ref,
                  lng_ref, lnb_ref, wap_ref, wcp_ref, wout_ref, g2_ref,
                  x1_ref, hx_ref, hxf_ref):
    tm = x_ref.shape[1]
    d = x_ref.shape[2]
    half = d // 2
    rows_per_part = tm // MERGE_PARTS
    for part in range(MERGE_PARTS):
        p0 = part * rows_per_part
        rows = slice(p0, p0 + rows_per_part)
        conv = c_ref[0, rows, :]
        mu = jnp.mean(conv, axis=-1, keepdims=True)
        cen = conv - mu
        var = jnp.mean(cen * cen, axis=-1, keepdims=True)
        ln = cen * lax.rsqrt(var + EPS) * lng_ref[...] + lnb_ref[...]
        act = (ln * _sigmoid(ln)).astype(BF16)
        y_conv = jnp.dot(act, wcp_ref[...], preferred_element_type=F32)
        y_attn = jnp.dot(o_ref[0, rows, :], wap_ref[...], preferred_element_type=F32)
        z = g_ref[0, rows, :d].astype(F32) * y_attn + g_ref[0, rows, d:].astype(F32) * y_conv
        mix = jnp.dot(z.astype(BF16), wout_ref[...], preferred_element_type=F32)
        x1 = x_ref[0, rows, :] + gt1_ref[0] * mix
        x1_ref[0, rows, :] = x1
        hx = _norm_modulate(x1, g2_ref[...], sh2_ref[0], sc2_ref[0])
        hx_ref[rows, :] = pltpu.bitcast(_pack_bf16_pair(hx[:, :half], hx[:, half:]), I32)
        h1, h2, h3 = _split3(hx)
        hxf_ref[0, rows, :] = h1
        hxf_ref[1, rows, :] = h2
        hxf_ref[2, rows, :] = h3


def _merge(o, conv, g, x, gt1, sh2, sc2, ln_g, ln_b, w_ap, w_cp, w_out, g2, tm):
    b, s, d = x.shape
    nt = s // tm
    bvec = lambda: pl.BlockSpec((1, 1, d), lambda bi, i: (bi, 0, 0))
    full = lambda a: pl.BlockSpec(a.shape, lambda bi, i: (0,) * a.ndim)
    tok = lambda w: pl.BlockSpec((1, tm, w), lambda bi, i: (bi, i, 0))
    return pl.pallas_call(
        _merge_kernel,
        grid=(b, nt),
        in_specs=[tok(ATTN_WIDTH), tok(CONV_WIDTH), tok(2 * d), tok(d), bvec(), bvec(), bvec(),
                  full(ln_g), full(ln_b), full(w_ap), full(w_cp), full(w_out), full(g2)],
        out_specs=[tok(d),
                   pl.BlockSpec((tm, d // 2), lambda bi, i: (bi * nt + i, 0)),
                   pl.BlockSpec((3, tm, d), lambda bi, i: (0, bi * nt + i, 0))],
        out_shape=[jax.ShapeDtypeStruct((b, s, d), F32),
                   jax.ShapeDtypeStruct((b * s, d // 2), I32),
                   jax.ShapeDtypeStruct((3, b * s, d), BF16)],
        compiler_params=_params(("arbitrary", "arbitrary")),
        name="merge",
    )(o, conv, g, x, gt1, sh2, sc2, ln_g, ln_b, w_ap, w_cp, w_out, g2)


def _router_logits(w_ref, h_ref):
    h1, h2, h3 = h_ref[0], h_ref[1], h_ref[2]
    w1, w2, w3 = w_ref[0], w_ref[1], w_ref[2]
    return ((((_dot_nt(w3, h1) + _dot_nt(w1, h3)) + _dot_nt(w2, h2))
             + (_dot_nt(w2, h1) + _dot_nt(w1, h2))) + _dot_nt(w1, h1))


def _route_kernel(hx0_ref, hxa_ref, hxb_ref, wrt_ref, bias_ref, tri_ref, e_ref, w_ref, r_ref, cnt_ref,
                  lg_a, lg_b, run_sc):
    step = pl.program_id(0)
    tb = hxa_ref.shape[1]
    n_strips = tb // LANES
    nv = N_EXPERTS // SUBLANES
    gv = EXPERTS_PER_GROUP // SUBLANES

    @pl.when(step == 0)
    def _():
        run_sc[...] = jnp.zeros(run_sc.shape, F32)
        lg_a[...] = _router_logits(wrt_ref, hx0_ref)

    row = (lax.broadcasted_iota(I32, (nv, SUBLANES, LANES), 0) * SUBLANES
           + lax.broadcasted_iota(I32, (nv, SUBLANES, LANES), 1))
    sub = lax.broadcasted_iota(I32, (SUBLANES, LANES), 0)
    bias = bias_ref[...].reshape(nv, SUBLANES, LANES)
    neg_inf = jnp.float32(-jnp.inf)

    def topk_strip(lg, st, col0):
        lanes = slice(col0 + st * LANES, col0 + (st + 1) * LANES)
        scores = _sigmoid(lg[:, st * LANES:(st + 1) * LANES]).reshape(nv, SUBLANES, LANES)
        biased = scores + bias
        gscore = []
        for g in range(N_GROUPS):
            m1 = biased[g * gv]
            m2 = jnp.full((SUBLANES, LANES), neg_inf, F32)
            for t in range(1, gv):
                v = biased[g * gv + t]
                m2 = jnp.maximum(m2, jnp.minimum(m1, v))
                m1 = jnp.maximum(m1, v)
            for s in (4, 2, 1):
                p1 = pltpu.roll(m1, s, 0)
                p2 = pltpu.roll(m2, s, 0)
                m2 = jnp.maximum(jnp.minimum(m1, p1), jnp.maximum(m2, p2))
                m1 = jnp.maximum(m1, p1)
            gscore.append(m1 + m2)
        masked = []
        for g in range(N_GROUPS):
            beaten = jnp.zeros((SUBLANES, LANES), I32)
            for o in range(N_GROUPS):
                if o == g:
                    continue
                wins = (gscore[o] > gscore[g]) | ((gscore[o] == gscore[g]) & (o < g))
                beaten = beaten + wins.astype(I32)
            keep = beaten < TOPK_GROUPS
            for t in range(gv):
                masked.append(jnp.where(keep, biased[g * gv + t], neg_inf))
        cand = jnp.stack(masked, axis=0)
        sel = jnp.zeros((nv, SUBLANES, LANES), jnp.bool_)
        picks, pick_scores = [], []
        for _ in range(TOP_K):
            mx = _sub_allreduce(jnp.max(cand, axis=0), jnp.maximum)
            idx = _sub_allreduce(jnp.min(jnp.where(cand == mx, row, N_EXPERTS), axis=0), jnp.minimum)
            hit = row == idx
            pick_scores.append(_sub_allreduce(jnp.sum(jnp.where(hit, scores, 0.0), axis=0), jnp.add))
            picks.append(idx)
            sel = sel | hit
            cand = jnp.where(hit, neg_inf, cand)
        sel_b = sel.astype(F32).astype(BF16).reshape(N_EXPERTS, LANES)
        before = jnp.dot(sel_b, tri_ref[0], preferred_element_type=F32)
        total = jnp.dot(sel_b, tri_ref[1], preferred_element_type=F32)
        rank_all = (before + run_sc[...]).reshape(nv, SUBLANES, LANES)
        run_sc[...] = run_sc[...] + total
        denom = pick_scores[0]
        for kk in range(1, TOP_K):
            denom = denom + pick_scores[kk]
        e_out = jnp.zeros((SUBLANES, LANES), I32)
        w_out = jnp.zeros((SUBLANES, LANES), F32)
        r_out = jnp.zeros((SUBLANES, LANES), I32)
        for kk in range(TOP_K):
            rk = _sub_allreduce(jnp.sum(jnp.where(row == picks[kk], rank_all, 0.0), axis=0), jnp.add)
            e_out = jnp.where(sub == kk, picks[kk], e_out)
            w_out = jnp.where(sub == kk, pick_scores[kk] / denom * ROUTED_SCALE, w_out)
            r_out = jnp.where(sub == kk, rk.astype(I32), r_out)
        e_ref[:, lanes] = e_out
        w_ref[:, lanes] = w_out
        r_ref[:, lanes] = r_out

    lg_b[...] = _router_logits(wrt_ref, hxa_ref)
    for st in range(n_strips):
        topk_strip(lg_a, st, 0)
    lg_a[...] = _router_logits(wrt_ref, hxb_ref)
    for st in range(n_strips):
        topk_strip(lg_b, st, tb)
    cnt_ref[...] = run_sc[...].astype(I32)


def _route(hxf, w_rt3, bias, tb):
    _, n, d = hxf.shape
    nt = n // tb
    assert nt % 2 == 0
    iota_r = lax.broadcasted_iota(I32, (LANES, LANES), 0)
    iota_c = lax.broadcasted_iota(I32, (LANES, LANES), 1)
    tri = jnp.stack([(iota_r < iota_c), jnp.ones((LANES, LANES), jnp.bool_)]).astype(BF16)
    bias_b = jnp.broadcast_to(bias.reshape(N_EXPERTS, 1), (N_EXPERTS, LANES)).astype(F32)
    tokrow = lambda dt: jax.ShapeDtypeStruct((TOP_K, n), dt)
    out_blk = lambda: pl.BlockSpec((TOP_K, 2 * tb), lambda i: (0, i))
    return pl.pallas_call(
        _route_kernel,
        grid=(nt // 2,),
        in_specs=[pl.BlockSpec((3, tb, d), lambda i: (0, 0, 0)),
                  pl.BlockSpec((3, tb, d), lambda i: (0, 2 * i + 1, 0)),
                  pl.BlockSpec((3, tb, d), lambda i: (0, jnp.minimum(2 * i + 2, nt - 1), 0)),
                  pl.BlockSpec(w_rt3.shape, lambda i: (0, 0, 0)),
                  pl.BlockSpec((N_EXPERTS, LANES), lambda i: (0, 0)),
                  pl.BlockSpec((2, LANES, LANES), lambda i: (0, 0, 0))],
        out_specs=[out_blk(), out_blk(), out_blk(),
                   pl.BlockSpec((N_EXPERTS, LANES), lambda i: (0, 0))],
        out_shape=[tokrow(I32), tokrow(F32), tokrow(I32),
                   jax.ShapeDtypeStruct((N_EXPERTS, LANES), I32)],
        scratch_shapes=[pltpu.VMEM((N_EXPERTS, tb), F32), pltpu.VMEM((N_EXPERTS, tb), F32),
                        pltpu.VMEM((N_EXPERTS, LANES), F32)],
        compiler_params=_params(("arbitrary",)),
        name="route",
    )(hxf, hxf, hxf, w_rt3, bias_b, tri)


def _slots_kernel(pstart_ref, e_ref, r_ref, o_ref):
    e = e_ref[...]

    def body(x, acc):
        return acc + jnp.where(e == x, pstart_ref[x], 0)

    o_ref[...] = lax.fori_loop(0, N_EXPERTS, body, r_ref[...])


def _slots(pstart, e_idx, rank, tb):
    n = e_idx.shape[1]
    blk = lambda: pl.BlockSpec((TOP_K, tb), lambda i, ps: (0, i))
    return pl.pallas_call(
        _slots_kernel,
        grid_spec=pltpu.PrefetchScalarGridSpec(
            num_scalar_prefetch=1, grid=(n // tb,), in_specs=[blk(), blk()], out_specs=blk()),
        out_shape=jax.ShapeDtypeStruct((TOP_K, n), I32),
        compiler_params=_params(("arbitrary",)),
        name="slots",
    )(pstart, e_idx, rank)


def _sc_mesh():
    return plsc.VectorSubcoreMesh(core_axis_name="c", subcore_axis_name="s")


def _sc_worker():
    return lax.axis_index("c") * SC_SUBCORES + lax.axis_index("s")


def _sc_dispatch(hx, pos, cap):
    n, w = hx.shape
    per_worker = n // SC_WINDOW // SC_WORKERS
    assert per_worker * SC_WINDOW * SC_WORKERS == n

    @pl.kernel(out_type=jax.ShapeDtypeStruct((cap, w), hx.dtype), mesh=_sc_mesh(),
               scratch_types=[pltpu.VMEM((SC_WINDOW, w), hx.dtype), pltpu.VMEM((TOP_K, SC_WINDOW), I32),
                              pltpu.SemaphoreType.DMA])
    def scatter_rows(x_hbm, i_hbm, o_hbm, xbuf, ibuf, sem):
        wid = _sc_worker()

        @pl.loop(0, per_worker)
        def _(j):
            row0 = (wid * per_worker + j) * SC_WINDOW
            pltpu.sync_copy(x_hbm.at[pl.ds(row0, SC_WINDOW)], xbuf)
            pltpu.sync_copy(i_hbm.at[:, pl.ds(row0, SC_WINDOW)], ibuf)
            copies = [pltpu.async_copy(xbuf, o_hbm.at[ibuf.at[kk]], sem) for kk in range(TOP_K)]
            for cp in copies:
                cp.wait()

    return scatter_rows(hx, pos)


def _sc_gather(ys, pos):
    n = pos.shape[1]
    w = ys.shape[1]
    per_worker = n // SC_WINDOW // SC_WORKERS
    assert per_worker * SC_WINDOW * SC_WORKERS == n

    @pl.kernel(out_type=jax.ShapeDtypeStruct((TOP_K, n, w), ys.dtype), mesh=_sc_mesh(),
               scratch_types=[pltpu.VMEM((SC_WINDOW, w), ys.dtype), pltpu.VMEM((TOP_K, SC_WINDOW), I32)])
    def gather_rows(y_hbm, i_hbm, o_hbm, ybuf, ibuf):
        wid = _sc_worker()

        @pl.loop(0, per_worker)
        def _(j):
            row0 = (wid * per_worker + j) * SC_WINDOW
            pltpu.sync_copy(i_hbm.at[:, pl.ds(row0, SC_WINDOW)], ibuf)
            for kk in range(TOP_K):
                pltpu.sync_copy(y_hbm.at[ibuf.at[kk]], ybuf)
                pltpu.sync_copy(ybuf, o_hbm.at[kk, pl.ds(row0, SC_WINDOW)])

    return gather_rows(ys, pos)


def _experts_kernel(blk_e_ref, valid_ref, blk_in_ref, blk_out_ref, next_e_ref, slot_ref,
                    xs_ref, wgu_hbm, wdn_hbm, ys_ref, gu_buf, dn_buf, sem, wgu_sc, wdn_sc):
    del blk_in_ref, blk_out_ref

    def weight_copies(e, slot):
        return (pltpu.make_async_copy(wgu_hbm.at[e], gu_buf.at[slot], sem.at[0, slot]),
                pltpu.make_async_copy(wdn_hbm.at[e], dn_buf.at[slot], sem.at[1, slot]))

    def one_block(i, rsl):
        valid = valid_ref[i]

        @pl.when(valid > 0)
        def _():
            e = blk_e_ref[i]
            slot = slot_ref[i]
            prev = blk_e_ref[jnp.maximum(i - 1, 0)]

            @pl.when(i == 0)
            def _():
                for cp in weight_copies(e, slot):
                    cp.start()

            @pl.when((i == 0) | (e != prev))
            def _():
                for cp in weight_copies(e, slot):
                    cp.wait()
                nxt = next_e_ref[i]

                @pl.when(nxt >= 0)
                def _():
                    for cp in weight_copies(nxt, 1 - slot):
                        cp.start()

                wgu_sc[...] = gu_buf[slot].astype(BF16)
                wdn_sc[...] = dn_buf[slot].astype(BF16)

            rows = lax.broadcasted_iota(I32, (EXPERT_BLOCK, xs_ref.shape[1]), 0)
            xu = jnp.where(rows < valid, pltpu.bitcast(xs_ref[rsl, :], U32), jnp.uint32(0))
            lo, hi = _unpack_bf16_pair(xu)
            half = lo.shape[1]
            gu = (jnp.dot(lo.astype(BF16), wgu_sc[:half, :], preferred_element_type=F32)
                  + jnp.dot(hi.astype(BF16), wgu_sc[half:, :], preferred_element_type=F32))
            gt = gu[:, :EXPERT_HIDDEN]
            act = (gt * _sigmoid(gt) * gu[:, EXPERT_HIDDEN:]).astype(BF16)
            y = jnp.dot(act, wdn_sc[...], preferred_element_type=F32)
            ys_ref[rsl, :] = pltpu.bitcast(_pack_bf16_pair(y[:, :half], y[:, half:]), I32)

        @pl.when(valid <= 0)
        def _():
            ys_ref[rsl, :] = jnp.zeros((EXPERT_BLOCK, ys_ref.shape[1]), I32)

    step = pl.program_id(0)
    for sub in range(EXPERT_BLOCKS_PER_STEP):
        one_block(step * EXPERT_BLOCKS_PER_STEP + sub, slice(sub * EXPERT_BLOCK, (sub + 1) * EXPERT_BLOCK))


def _experts(blk_e, blk_valid, next_expert, expert_slot, xs, w_gu, w_dn):
    cap, w = xs.shape
    n_e, d, h2 = w_gu.shape
    n_blocks = cap // EXPERT_BLOCK
    assert n_blocks % EXPERT_BLOCKS_PER_STEP == 0
    n_steps = n_blocks // EXPERT_BLOCKS_PER_STEP
    step_rows = EXPERT_BLOCKS_PER_STEP * EXPERT_BLOCK
    blk_next = next_expert[blk_e].astype(I32)
    blk_slot = expert_slot[blk_e].astype(I32)
    step = jnp.arange(n_steps, dtype=I32)
    step_used = blk_valid[::EXPERT_BLOCKS_PER_STEP] > 0
    n_used = jnp.sum(step_used.astype(I32))
    blk_in = jnp.minimum(step, jnp.maximum(n_used - 1, 0)).astype(I32)
    blk_out = jnp.where(step_used, step, n_steps).astype(I32)
    return pl.pallas_call(
        _experts_kernel,
        grid_spec=pltpu.PrefetchScalarGridSpec(
            num_scalar_prefetch=6,
            grid=(n_steps,),
            in_specs=[pl.BlockSpec((step_rows, w), lambda i, be, bv, bi, bo, bn, bs: (bi[i], 0)),
                      pl.BlockSpec(memory_space=pl.ANY),
                      pl.BlockSpec(memory_space=pl.ANY)],
            out_specs=pl.BlockSpec((step_rows, w), lambda i, be, bv, bi, bo, bn, bs: (bo[i], 0)),
            scratch_shapes=[pltpu.VMEM((2, d, h2), F32), pltpu.VMEM((2, h2 // 2, d), F32),
                            pltpu.SemaphoreType.DMA((2, 2)),
                            pltpu.VMEM((d, h2), BF16), pltpu.VMEM((h2 // 2, d), BF16)]),
        out_shape=jax.ShapeDtypeStruct((cap + step_rows, w), I32),
        compiler_params=_params(("arbitrary",)),
        name="experts",
    )(blk_e, blk_valid, blk_in, blk_out, blk_next, blk_slot, xs, w_gu, w_dn)


def _combine_kernel(yg_ref, wt_ref, hx_ref, x1_ref, gt2_ref, wsg_ref, wsd_ref, fg_ref, *rest):
    o_ref = rest[-1]
    lo, hi = _unpack_bf16_pair(pltpu.bitcast(hx_ref[...], U32))
    half = lo.shape[1]
    gu = (jnp.dot(lo.astype(BF16), wsg_ref[:half, :], preferred_element_type=F32)
          + jnp.dot(hi.astype(BF16), wsg_ref[half:, :], preferred_element_type=F32))
    gt = gu[:, :SHARED_HIDDEN]
    act = (gt * _sigmoid(gt) * gu[:, SHARED_HIDDEN:]).astype(BF16)
    y = jnp.dot(act, wsd_ref[...], preferred_element_type=F32)
    y_lo = y[:, :half]
    y_hi = y[:, half:]
    for kk in range(TOP_K):
        r_lo, r_hi = _unpack_bf16_pair(pltpu.bitcast(yg_ref[kk], U32))
        wk = wt_ref[:, kk:kk + 1]
        y_lo = y_lo + wk * r_lo
        y_hi = y_hi + wk * r_hi
    x2_lo = x1_ref[:, :half] + gt2_ref[0, :, :half] * y_lo
    x2_hi = x1_ref[:, half:] + gt2_ref[0, :, half:] * y_hi
    ms = (jnp.sum(x2_lo * x2_lo, axis=-1, keepdims=True)
          + jnp.sum(x2_hi * x2_hi, axis=-1, keepdims=True)) / (2 * half)
    inv = lax.rsqrt(ms + EPS)
    o_ref[:, :half] = x2_lo * inv * fg_ref[:, :half]
    o_ref[:, half:] = x2_hi * inv * fg_ref[:, half:]


def _combine(yg, wt, hx, x1, gt2, w_sg, w_sd, fg, tm, tiles_per_batch, tile0, out_prev):
    n, w = hx.shape
    d = 2 * w
    full = lambda a: pl.BlockSpec(a.shape, lambda i: (0,) * a.ndim)
    in_specs = [pl.BlockSpec((TOP_K, tm, w), lambda i: (0, i, 0)),
                pl.BlockSpec((tm, TOP_K), lambda i: (tile0 + i, 0)),
                pl.BlockSpec((tm, w), lambda i: (tile0 + i, 0)),
                pl.BlockSpec((tm, d), lambda i: (tile0 + i, 0)),
                pl.BlockSpec((1, 1, d), lambda i: ((tile0 + i) // tiles_per_batch, 0, 0)),
                full(w_sg), full(w_sd), full(fg)]
    args = [yg, wt, hx, x1, gt2, w_sg, w_sd, fg]
    aliases = {}
    if out_prev is not None:
        in_specs.append(pl.BlockSpec(memory_space=pl.ANY))
        args.append(out_prev)
        aliases = {len(args) - 1: 0}
    return pl.pallas_call(
        _combine_kernel,
        grid=(yg.shape[1] // tm,),
        in_specs=in_specs,
        out_specs=pl.BlockSpec((tm, d), lambda i: (tile0 + i, 0)),
        out_shape=jax.ShapeDtypeStruct((n, d), F32),
        input_output_aliases=aliases,
        compiler_params=_params(("arbitrary",)),
        name="combine",
    )(*args)


def _rope_tables(seq):
    rows = seq // GRID_W
    pos_row = np.repeat(np.arange(rows, dtype=np.float32), GRID_W)
    pos_col = np.tile(np.arange(GRID_W, dtype=np.float32), rows)
    inv_freq = (ROPE_THETA ** (-np.arange(0, AXIS_DIM, 2, dtype=np.float32) / AXIS_DIM)).astype(np.float32)
    ar = pos_row[:, None] * inv_freq
    ac = pos_col[:, None] * inv_freq
    cos_t = np.concatenate([np.cos(ar), np.cos(ar), np.cos(ac), np.cos(ac)], axis=1)
    sin_t = np.concatenate([-np.sin(ar), np.sin(ar), -np.sin(ac), np.sin(ac)], axis=1)
    return jnp.asarray(cos_t, F32), jnp.asarray(sin_t, F32)


def _tile(n, want):
    t = min(n, want)
    assert n % t == 0, (n, want)
    return t


def kernel(x, c, ctx, c_ctx, w_mod, b_mod, norm1_g, w_in, q_norm_g, k_norm_g, w_dw, b_dw, conv_ln_g, conv_ln_b, w_attn_proj, w_conv_proj, w_out, norm2_g, w_router, router_bias, w_exp_gu, w_exp_dn, w_sh_gu, w_sh_dn, final_g):
    b, s, d = x.shape
    depth = w_mod.shape[0]
    assert depth == 1, "single-layer block"
    n = b * s
    row = lambda v: v.reshape(1, -1)

    cc = jnp.zeros((SUBLANES, d), F32).at[:b].set(c).at[b].set(c_ctx)
    mod = _modulation(cc, w_mod[0], row(b_mod[0]))
    mod_x = mod[:b].reshape(b, 1, 6, d)
    sh1, sc1, gt1, sh2, sc2, gt2 = [mod_x[:, :, j, :] for j in range(6)]
    mod_c = mod[b].reshape(6, d)
    csh1, csc1 = row(mod_c[0]), row(mod_c[1])

    w_in_b = w_in[0].astype(BF16)
    q_end, kv_end = ATTN_WIDTH, ATTN_WIDTH + 2 * KV_WIDTH
    k_end = q_end + KV_WIDTH
    w_vt = w_in_b[:, k_end:kv_end].T
    kc, vct = _ctx_kv(ctx, csh1, csc1, row(norm1_g[0]), w_in_b[:, q_end:k_end], w_vt, row(k_norm_g[0]))

    cos_t, sin_t = _rope_tables(s)
    q, kx, vxt, hglu, gates = _in_proj(x, sh1, sc1, row(norm1_g[0]), w_in_b, w_vt, row(q_norm_g[0]),
                                       row(k_norm_g[0]), cos_t, sin_t, _tile(s, 512))
    score_bound = (HEAD_DIM * ATTN_SCALE * LOG2E) * jnp.max(jnp.abs(q_norm_g[0])) * jnp.max(jnp.abs(k_norm_g[0]))
    bounded = (score_bound <= SAFE_EXP2_ARG).astype(I32).reshape(1)
    o, conv = _attention(bounded, q, kc, vct, kx, vxt, hglu, w_dw[0], row(b_dw[0]),
                         _tile(s, 1024), _tile(s // 2, 1024))

    top16 = lambda a: lax.bitcast_convert_type(lax.bitcast_convert_type(a, U32) & jnp.uint32(0xFFFF0000), F32)
    w_rt = w_router[0].T
    w_rt1 = top16(w_rt)
    w_rt2 = top16(w_rt - w_rt1)
    w_rt3 = w_rt - w_rt1 - w_rt2
    w_rt1, w_rt2, w_rt3 = w_rt1.astype(BF16), w_rt2.astype(BF16), w_rt3.astype(BF16)
    x1, hx, hxf = _merge(o, conv, gates, x, gt1, sh2, sc2, row(conv_ln_g[0]),
                         row(conv_ln_b[0]), w_attn_proj[0].astype(BF16), w_conv_proj[0].astype(BF16),
                         w_out[0].astype(BF16), row(norm2_g[0]), _tile(s, 512))

    e_idx, wts, rank, counts = _route(hxf, jnp.stack([w_rt1, w_rt2, w_rt3]), router_bias[0],
                                      _tile(n // 2, 512))

    cnt = counts[:, 0]
    padded = (cnt + EXPERT_BLOCK - 1) // EXPERT_BLOCK * EXPERT_BLOCK
    pends = jnp.cumsum(padded)
    pstart = (pends - padded).astype(I32)
    n_blocks = (n * TOP_K + N_EXPERTS * (EXPERT_BLOCK - 1)) // EXPERT_BLOCK
    n_blocks = -(-n_blocks // EXPERT_BLOCKS_PER_STEP) * EXPERT_BLOCKS_PER_STEP
    cap = n_blocks * EXPERT_BLOCK
    blk_row0 = jnp.arange(n_blocks, dtype=I32) * EXPERT_BLOCK
    blk_e = jnp.minimum(jnp.sum(pends[None, :] <= blk_row0[:, None], axis=1), N_EXPERTS - 1).astype(I32)
    blk_valid = jnp.clip(pstart[blk_e] + cnt[blk_e] - blk_row0, 0, EXPERT_BLOCK).astype(I32)

    pos = _slots(pstart, e_idx, rank, _tile(n, 4096))
    xs = _sc_dispatch(hx, pos, cap)
    has_rows = cnt > 0
    later = lax.cummin(jnp.where(has_rows, jnp.arange(N_EXPERTS, dtype=I32), N_EXPERTS), reverse=True)
    next_expert = jnp.concatenate([later[1:], jnp.full((1,), N_EXPERTS, I32)])
    next_expert = jnp.where(next_expert >= N_EXPERTS, -1, next_expert)
    expert_slot = (jnp.cumsum(has_rows.astype(I32)) - 1) % 2
    ys = _experts(blk_e, blk_valid, next_expert, expert_slot, xs, w_exp_gu[0], w_exp_dn[0])
    tm = _tile(n, 256)
    chunk = n // COMBINE_CHUNKS if n % (COMBINE_CHUNKS * SC_WINDOW * SC_WORKERS) == 0 else n
    wt_t, x1_2d = wts.T, x1.reshape(n, d)
    w_sg, w_sd = w_sh_gu[0].astype(BF16), w_sh_dn[0].astype(BF16)
    out = None
    for c0 in range(0, n, chunk):
        yg = _sc_gather(ys, pos[:, c0:c0 + chunk])
        out = _combine(yg, wt_t, hx, x1_2d, gt2, w_sg, w_sd, row(final_g), tm, s // tm, c0 // tm, out)
    return out.reshape(b, s, d)
```

```python
import functools
import math

import jax
import jax.numpy as jnp
import numpy as np
from jax import lax
from jax.experimental import pallas as pl
from jax.experimental.pallas import tpu as pltpu
from jax.experimental.pallas import tpu_sc as plsc

F32 = jnp.float32
BF16 = jnp.bfloat16
U32 = jnp.uint32
I32 = jnp.int32

GRID_W = 64
N_HEADS = 8
N_KV_HEADS = 4
GROUP = N_HEADS // N_KV_HEADS
HEAD_DIM = 128
AXIS_DIM = HEAD_DIM // 2
ATTN_WIDTH = N_HEADS * HEAD_DIM
KV_WIDTH = N_KV_HEADS * HEAD_DIM
ROPE_THETA = 10000.0
ATTN_SCALE = HEAD_DIM ** -0.5
CONV_WIDTH = 512
CONV_KERNEL = 31
CONV_PAD = CONV_KERNEL // 2
N_EXPERTS = 256
TOP_K = 8
N_GROUPS = 8
TOPK_GROUPS = 4
EXPERTS_PER_GROUP = N_EXPERTS // N_GROUPS
EXPERT_HIDDEN = 256
SHARED_HIDDEN = 256
ROUTED_SCALE = 2.5
EPS = 1e-6
LOG2E = 1.4426950408889634
SAFE_EXP2_ARG = 64.0

LANES = 128
SUBLANES = 8
VMEM_LIMIT_BYTES = 56 * 1024 * 1024

HALO_ROWS = 16
CONV_ROWS = 64
MERGE_PARTS = 2
EXPERT_BLOCK = 512
EXPERT_BLOCKS_PER_STEP = 4
SC_SUBCORES = 16
SC_WORKERS = 2 * SC_SUBCORES
SC_WINDOW = 128
COMBINE_CHUNKS = 4
HIGHEST = lax.Precision.HIGHEST


def _params(sem):
    return pltpu.CompilerParams(dimension_semantics=sem, vmem_limit_bytes=VMEM_LIMIT_BYTES)


def _sigmoid(x):
    return 1.0 / (1.0 + jnp.exp(-x))


def _pack_bf16_pair(lo, hi):
    lo_b = pltpu.bitcast(lo.astype(BF16).astype(F32), U32)
    hi_b = pltpu.bitcast(hi.astype(BF16).astype(F32), U32)
    return (lo_b >> 16) | (hi_b & jnp.uint32(0xFFFF0000))


def _unpack_bf16_pair(u):
    lo = pltpu.bitcast(u << 16, F32)
    hi = pltpu.bitcast(u & jnp.uint32(0xFFFF0000), F32)
    return lo, hi


def _mod_kernel(cc_ref, w_ref, b_ref, o_ref):
    cc = cc_ref[...]
    s = cc * _sigmoid(cc)
    o_ref[...] = jnp.dot(s, w_ref[...], precision=HIGHEST, preferred_element_type=F32) + b_ref[...]


def _modulation(cc, w_mod, b_mod):
    d, n = w_mod.shape
    tn = n // 4
    return pl.pallas_call(
        _mod_kernel,
        grid=(n // tn,),
        in_specs=[pl.BlockSpec((SUBLANES, d), lambda j: (0, 0)),
                  pl.BlockSpec((d, tn), lambda j: (0, j)),
                  pl.BlockSpec((1, tn), lambda j: (0, j))],
        out_specs=pl.BlockSpec((SUBLANES, tn), lambda j: (0, j)),
        out_shape=jax.ShapeDtypeStruct((SUBLANES, n), F32),
        compiler_params=_params(("arbitrary",)),
        name="mod",
    )(cc, w_mod, b_mod)


def _norm_modulate(x, g, sh, sc):
    ms = jnp.mean(x * x, axis=-1, keepdims=True)
    return (x * lax.rsqrt(ms + EPS) * g) * (1.0 + sc) + sh


def _head_norm(p, gain):
    r = lax.rsqrt(jnp.mean(p * p, axis=-1, keepdims=True) + EPS)
    return p * r * gain


def _dot_nt(a, b):
    return lax.dot_general(a, b, (((1,), (1,)), ((), ())), preferred_element_type=F32)


def _ctx_kv_kernel(x_ref, sh_ref, sc_ref, g1_ref, wk_ref, wvt_ref, gk_ref, k_ref, vt_ref):
    h = _norm_modulate(x_ref[0], g1_ref[...], sh_ref[...], sc_ref[...]).astype(BF16)
    pk = jnp.dot(h, wk_ref[...], preferred_element_type=F32)
    for j in range(N_KV_HEADS):
        sl = slice(j * HEAD_DIM, (j + 1) * HEAD_DIM)
        k_ref[0, :, sl] = _head_norm(pk[:, sl], gk_ref[...]).astype(BF16)
    vt_ref[0] = _dot_nt(wvt_ref[...], h).astype(BF16)


def _ctx_kv(ctx, csh, csc, g1, w_k, w_vt, gk):
    b, lc, d = ctx.shape
    vec = lambda: pl.BlockSpec((1, d), lambda i: (0, 0))
    return pl.pallas_call(
        _ctx_kv_kernel,
        grid=(b,),
        in_specs=[pl.BlockSpec((1, lc, d), lambda i: (i, 0, 0)), vec(), vec(), vec(),
                  pl.BlockSpec((d, KV_WIDTH), lambda i: (0, 0)),
                  pl.BlockSpec((KV_WIDTH, d), lambda i: (0, 0)),
                  pl.BlockSpec((1, HEAD_DIM), lambda i: (0, 0))],
        out_specs=[pl.BlockSpec((1, lc, KV_WIDTH), lambda i: (i, 0, 0)),
                   pl.BlockSpec((1, KV_WIDTH, lc), lambda i: (i, 0, 0))],
        out_shape=[jax.ShapeDtypeStruct((b, lc, KV_WIDTH), BF16),
                   jax.ShapeDtypeStruct((b, KV_WIDTH, lc), BF16)],
        compiler_params=_params(("arbitrary",)),
        name="ctx_kv",
    )(ctx, csh, csc, g1, w_k, w_vt, gk)


def _in_proj_kernel(x_ref, sh_ref, sc_ref, g1_ref, w_ref, wvt_ref, gq_ref, gk_ref, cos_ref, sin_ref,
                    q_ref, k_ref, vt_ref, h_ref, g_ref):
    h = _norm_modulate(x_ref[0], g1_ref[...], sh_ref[0], sc_ref[0]).astype(BF16)
    cos = cos_ref[...]
    sin = sin_ref[...]
    lane = lax.broadcasted_iota(I32, cos.shape, 1)
    upper = (lane & (AXIS_DIM // 2)) != 0

    def rope(p):
        swapped = jnp.where(upper, pltpu.roll(p, AXIS_DIM // 2, 1),
                            pltpu.roll(p, HEAD_DIM - AXIS_DIM // 2, 1))
        return p * cos + swapped * sin

    q_end = ATTN_WIDTH
    k_end = q_end + KV_WIDTH
    v_end = k_end + KV_WIDTH
    u_end = v_end + 2 * CONV_WIDTH
    pq = jnp.dot(h, w_ref[:, :q_end], preferred_element_type=F32)
    for j in range(N_HEADS):
        sl = slice(j * HEAD_DIM, (j + 1) * HEAD_DIM)
        q_ref[0, :, sl] = (rope(_head_norm(pq[:, sl], gq_ref[...])) * (ATTN_SCALE * LOG2E)).astype(BF16)
    pk = jnp.dot(h, w_ref[:, q_end:k_end], preferred_element_type=F32)
    for j in range(N_KV_HEADS):
        sl = slice(j * HEAD_DIM, (j + 1) * HEAD_DIM)
        k_ref[0, :, sl] = rope(_head_norm(pk[:, sl], gk_ref[...])).astype(BF16)
    vt_ref[0] = _dot_nt(wvt_ref[...], h).astype(BF16)
    u = jnp.dot(h, w_ref[:, v_end:u_end], preferred_element_type=F32)
    h_ref[0] = (u[:, :CONV_WIDTH] * _sigmoid(u[:, CONV_WIDTH:])).astype(BF16)
    g_ref[0] = _sigmoid(jnp.dot(h, w_ref[:, u_end:], preferred_element_type=F32)).astype(BF16)


def _in_proj(x, sh1, sc1, g1, w_in, w_vt, gq, gk, cos_t, sin_t, tm):
    b, s, d = x.shape
    n_in = w_in.shape[1]
    bvec = lambda: pl.BlockSpec((1, 1, d), lambda bi, i: (bi, 0, 0))
    tok = lambda w: pl.BlockSpec((1, tm, w), lambda bi, i: (bi, i, 0))
    return pl.pallas_call(
        _in_proj_kernel,
        grid=(b, s // tm),
        in_specs=[tok(d), bvec(), bvec(),
                  pl.BlockSpec((1, d), lambda bi, i: (0, 0)),
                  pl.BlockSpec((d, n_in), lambda bi, i: (0, 0)),
                  pl.BlockSpec((KV_WIDTH, d), lambda bi, i: (0, 0)),
                  pl.BlockSpec((1, HEAD_DIM), lambda bi, i: (0, 0)),
                  pl.BlockSpec((1, HEAD_DIM), lambda bi, i: (0, 0)),
                  pl.BlockSpec((tm, HEAD_DIM), lambda bi, i: (i, 0)),
                  pl.BlockSpec((tm, HEAD_DIM), lambda bi, i: (i, 0))],
        out_specs=[tok(ATTN_WIDTH), tok(KV_WIDTH),
                   pl.BlockSpec((1, KV_WIDTH, tm), lambda bi, i: (bi, 0, i)),
                   tok(CONV_WIDTH), tok(2 * d)],
        out_shape=[jax.ShapeDtypeStruct((b, s, ATTN_WIDTH), BF16),
                   jax.ShapeDtypeStruct((b, s, KV_WIDTH), BF16),
                   jax.ShapeDtypeStruct((b, KV_WIDTH, s), BF16),
                   jax.ShapeDtypeStruct((b, s, CONV_WIDTH), BF16),
                   jax.ShapeDtypeStruct((b, s, 2 * d), BF16)],
        compiler_params=_params(("arbitrary", "arbitrary")),
        name="in_proj",
    )(x, sh1, sc1, g1, w_in, w_vt, gq, gk, cos_t, sin_t)


def _sub_allreduce(x, op):
    for s in (4, 2, 1):
        x = op(x, pltpu.roll(x, s, 0))
    return x


def _attn_kernel(bounded_ref, q_ref, kc_ref, vct_ref, k_ref, vt_ref, hp_ref, hc_ref, hn_ref, wdw_ref, bdw_ref,
                 o_ref, c_ref, s0, s1, x0, x1, m_sc, l_sc, acc_sc, hcat, shift_sc, *, tk):
    tq = q_ref.shape[1]
    m_cols = GROUP * tq
    nk = k_ref.shape[1] // tk
    qf = q_ref[0].astype(F32).T
    qt = jnp.concatenate([qf[:HEAD_DIM], qf[HEAD_DIM:]], axis=1).astype(BF16)
    slots = ((s0, x0), (s1, x1))
    n_conv = max(nk // 2, 1)
    conv_rows = tq // n_conv
    conv_base = HALO_ROWS - CONV_PAD
    conv_reach = (conv_base + CONV_KERNEL - 1) // SUBLANES * SUBLANES

    def conv_fill():
        ti = pl.program_id(2)
        prev = hp_ref[0].astype(F32)
        nxt = hn_ref[0].astype(F32)
        hcat[0:HALO_ROWS, :] = jnp.where(ti > 0, prev, jnp.zeros_like(prev))
        hcat[HALO_ROWS:HALO_ROWS + tq, :] = hc_ref[0].astype(F32)
        hcat[HALO_ROWS + tq:, :] = jnp.where(ti < pl.num_programs(2) - 1, nxt, jnp.zeros_like(nxt))

    def conv_block(blk):
        r0 = blk * conv_rows
        for sub in range(0, conv_rows, CONV_ROWS):
            start = r0 + sub if isinstance(r0, int) else pl.multiple_of(r0 + sub, CONV_ROWS)
            window = hcat[pl.ds(start, CONV_ROWS + 2 * HALO_ROWS), :]
            acc = jnp.zeros((CONV_ROWS, HEAD_DIM), F32) + bdw_ref[...]
            for res in range(SUBLANES):
                shift_sc[res] = window[res:res + CONV_ROWS + conv_reach, :]
                for off in range(res, conv_base + CONV_KERNEL, SUBLANES):
                    j = off - conv_base
                    if 0 <= j < CONV_KERNEL:
                        a0 = off - res
                        acc = acc + shift_sc[res, a0:a0 + CONV_ROWS, :] * wdw_ref[j:j + 1, :]
            c_ref[0, pl.ds(start, CONV_ROWS), :] = acc

    def kchunk(j):
        return k_ref[0, pl.ds(pl.multiple_of(j * tk, tk), tk), :]

    def vchunk(j):
        return vt_ref[0, :, pl.ds(pl.multiple_of(j * tk, tk), tk)]

    def split(st):
        return st.reshape(st.shape[0] // SUBLANES, SUBLANES, m_cols)

    def scores(k, online):
        st = jnp.dot(k, qt, preferred_element_type=F32)
        return st, (jnp.max(split(st), axis=0) if online else None)

    def absorb(st, mx, vt, online):
        s3 = split(st)
        if online:
            m_prev = m_sc[...]
            m_new = jnp.maximum(m_prev, _sub_allreduce(mx, jnp.maximum))
            alpha = jnp.exp2(m_prev - m_new)
            p3 = jnp.exp2(s3 - m_new[None])
            l_sc[...] = alpha * l_sc[...] + _sub_allreduce(jnp.sum(p3, axis=0), jnp.add)
            pv = jnp.dot(vt, p3.reshape(st.shape).astype(BF16), preferred_element_type=F32)
            acc_sc[...] = alpha[0:1] * acc_sc[...] + pv
            m_sc[...] = m_new
        else:
            p3 = jnp.exp2(s3)
            l_sc[...] = l_sc[...] + jnp.sum(p3, axis=0)
            acc_sc[...] = acc_sc[...] + jnp.dot(vt, p3.reshape(st.shape).astype(BF16),
                                                preferred_element_type=F32)

    def stage(slot, k, online):
        st, mx = scores(k, online)
        slots[slot][0][...] = st
        if online:
            slots[slot][1][...] = mx

    def take(slot, vt, online):
        absorb(slots[slot][0][...], slots[slot][1][...] if online else None, vt, online)

    def sweep(online):
        if online:
            m_sc[...] = jnp.full(m_sc.shape, -jnp.inf, F32)
        l_sc[...] = jnp.zeros(l_sc.shape, F32)
        acc_sc[...] = jnp.zeros(acc_sc.shape, F32)
        conv_fill()
        stage(0, kchunk(0), online)

        def body(i, carry):
            j = 2 * i
            stage(1, kchunk(j + 1), online)
            take(0, vchunk(j), online)
            conv_block(i)
            stage(0, kchunk(j + 2), online)
            take(1, vchunk(j + 1), online)
            return carry

        lax.fori_loop(0, nk // 2 - 1, body, 0)
        conv_block(n_conv - 1)
        stage(1, kchunk(nk - 1), online)
        take(0, vchunk(nk - 2), online)
        sc, xc = scores(kc_ref[0], online)
        take(1, vchunk(nk - 1), online)
        absorb(sc, xc, vct_ref[0], online)
        denom = l_sc[...] if online else _sub_allreduce(l_sc[...], jnp.add)
        o = (acc_sc[...] / denom[0:1]).T
        o_ref[0, :, :HEAD_DIM] = o[:tq].astype(BF16)
        o_ref[0, :, HEAD_DIM:] = o[tq:].astype(BF16)

    @pl.when(bounded_ref[0] != 0)
    def _():
        sweep(online=False)

    @pl.when(bounded_ref[0] == 0)
    def _():
        sweep(online=True)


def _attention(bounded, q, kc, vct, kx, vxt, hglu, w_dw, b_dw, tq, tk):
    b, s, _ = q.shape
    lc = kc.shape[1]
    assert s % (2 * tk) == 0 and CONV_WIDTH == N_KV_HEADS * HEAD_DIM
    gw = GROUP * HEAD_DIM
    m_cols = GROUP * tq
    hb = tq // HALO_ROWS
    n_halo = s // HALO_ROWS
    reach = (HALO_ROWS - CONV_PAD + CONV_KERNEL - 1) // SUBLANES * SUBLANES
    kv = lambda l: pl.BlockSpec((1, l, HEAD_DIM), lambda bi, h, i, bd: (bi, 0, h))
    kvt = lambda l: pl.BlockSpec((1, HEAD_DIM, l), lambda bi, h, i, bd: (bi, h, 0))
    qo = lambda: pl.BlockSpec((1, tq, gw), lambda bi, h, i, bd: (bi, i, h))
    chan = lambda rows: pl.BlockSpec((rows, HEAD_DIM), lambda bi, h, i, bd: (0, h))
    return pl.pallas_call(
        functools.partial(_attn_kernel, tk=tk),
        grid_spec=pltpu.PrefetchScalarGridSpec(
            num_scalar_prefetch=1,
            grid=(b, N_KV_HEADS, s // tq),
            in_specs=[qo(), kv(lc), kvt(lc), kv(s), kvt(s),
                      pl.BlockSpec((1, HALO_ROWS, HEAD_DIM),
                                   lambda bi, h, i, bd: (bi, jnp.maximum(i * hb - 1, 0), h)),
                      pl.BlockSpec((1, tq, HEAD_DIM), lambda bi, h, i, bd: (bi, i, h)),
                      pl.BlockSpec((1, HALO_ROWS, HEAD_DIM),
                                   lambda bi, h, i, bd: (bi, jnp.minimum((i + 1) * hb, n_halo - 1), h)),
                      chan(CONV_KERNEL), chan(1)],
            out_specs=[qo(), pl.BlockSpec((1, tq, HEAD_DIM), lambda bi, h, i, bd: (bi, i, h))],
            scratch_shapes=[pltpu.VMEM((tk, m_cols), F32), pltpu.VMEM((tk, m_cols), F32),
                            pltpu.VMEM((SUBLANES, m_cols), F32), pltpu.VMEM((SUBLANES, m_cols), F32),
                            pltpu.VMEM((SUBLANES, m_cols), F32), pltpu.VMEM((SUBLANES, m_cols), F32),
                            pltpu.VMEM((HEAD_DIM, m_cols), F32),
                            pltpu.VMEM((tq + 2 * HALO_ROWS, HEAD_DIM), F32),
                            pltpu.VMEM((SUBLANES, CONV_ROWS + reach, HEAD_DIM), F32)]),
        out_shape=[jax.ShapeDtypeStruct((b, s, ATTN_WIDTH), BF16),
                   jax.ShapeDtypeStruct((b, s, CONV_WIDTH), F32)],
        compiler_params=_params(("arbitrary", "arbitrary", "arbitrary")),
        name="attn",
    )(bounded, q, kc, vct, kx, vxt, hglu, hglu, hglu, w_dw, b_dw)


def _split3(x):
    x1 = x.astype(BF16)
    r1 = x - x1.astype(F32)
    x2 = r1.astype(BF16)
    x3 = (r1 - x2.astype(F32)).astype(BF16)
    return x1, x2, x3


def _merge_kernel(o_ref, c_ref, g_ref, x_ref, gt1_ref, sh2_ref, sc2_ref,
                  lng_ref, lnb_ref, wap_ref, wcp_ref, wout_ref, g2_ref,
                  x1_ref, hx_ref, hxf_ref):
    tm = x_ref.shape[1]
    d = x_ref.shape[2]
    half = d // 2
    rows_per_part = tm // MERGE_PARTS
    for part in range(MERGE_PARTS):
        p0 = part * rows_per_part
        rows = slice(p0, p0 + rows_per_part)
        conv = c_ref[0, rows, :]
        mu = jnp.mean(conv, axis=-1, keepdims=True)
        cen = conv - mu
        var = jnp.mean(cen * cen, axis=-1, keepdims=True)
        ln = cen * lax.rsqrt(var + EPS) * lng_ref[...] + lnb_ref[...]
        act = (ln * _sigmoid(ln)).astype(BF16)
        y_conv = jnp.dot(act, wcp_ref[...], preferred_element_type=F32)
        y_attn = jnp.dot(o_ref[0, rows, :], wap_ref[...], preferred_element_type=F32)
        z = g_ref[0, rows, :d].astype(F32) * y_attn + g_ref[0, rows, d:].astype(F32) * y_conv
        mix = jnp.dot(z.astype(BF16), wout_ref[...], preferred_element_type=F32)
        x1 = x_ref[0, rows, :] + gt1_ref[0] * mix
        x1_ref[0, rows, :] = x1
        hx = _norm_modulate(x1, g2_ref[...], sh2_ref[0], sc2_ref[0])
        hx_ref[rows, :] = pltpu.bitcast(_pack_bf16_pair(hx[:, :half], hx[:, half:]), I32)
        h1, h2, h3 = _split3(hx)
        hxf_ref[0, rows, :] = h1
        hxf_ref[1, rows, :] = h2
        hxf_ref[2, rows, :] = h3


def _merge(o, conv, g, x, gt1, sh2, sc2, ln_g, ln_b, w_ap, w_cp, w_out, g2, tm):
    b, s, d = x.shape
    nt = s // tm
    bvec = lambda: pl.BlockSpec((1, 1, d), lambda bi, i: (bi, 0, 0))
    full = lambda a: pl.BlockSpec(a.shape, lambda bi, i: (0,) * a.ndim)
    tok = lambda w: pl.BlockSpec((1, tm, w), lambda bi, i: (bi, i, 0))
    return pl.pallas_call(
        _merge_kernel,
        grid=(b, nt),
        in_specs=[tok(ATTN_WIDTH), tok(CONV_WIDTH), tok(2 * d), tok(d), bvec(), bvec(), bvec(),
                  full(ln_g), full(ln_b), full(w_ap), full(w_cp), full(w_out), full(g2)],
        out_specs=[tok(d),
                   pl.BlockSpec((tm, d // 2), lambda bi, i: (bi * nt + i, 0)),
                   pl.BlockSpec((3, tm, d), lambda bi, i: (0, bi * nt + i, 0))],
        out_shape=[jax.ShapeDtypeStruct((b, s, d), F32),
                   jax.ShapeDtypeStruct((b * s, d // 2), I32),
                   jax.ShapeDtypeStruct((3, b * s, d), BF16)],
        compiler_params=_params(("arbitrary", "arbitrary")),
        name="merge",
    )(o, conv, g, x, gt1, sh2, sc2, ln_g, ln_b, w_ap, w_cp, w_out, g2)


def _router_logits(w_ref, h_ref):
    h1, h2, h3 = h_ref[0], h_ref[1], h_ref[2]
    w1, w2, w3 = w_ref[0], w_ref[1], w_ref[2]
    return ((((_dot_nt(w3, h1) + _dot_nt(w1, h3)) + _dot_nt(w2, h2))
             + (_dot_nt(w2, h1) + _dot_nt(w1, h2))) + _dot_nt(w1, h1))


def _route_kernel(hx0_ref, hxa_ref, hxb_ref, wrt_ref, bias_ref, tri_ref, e_ref, w_ref, r_ref, cnt_ref,
                  lg_a, lg_b, run_sc):
    step = pl.program_id(0)
    tb = hxa_ref.shape[1]
    n_strips = tb // LANES
    nv = N_EXPERTS // SUBLANES
    gv = EXPERTS_PER_GROUP // SUBLANES

    @pl.when(step == 0)
    def _():
        run_sc[...] = jnp.zeros(run_sc.shape, F32)
        lg_a[...] = _router_logits(wrt_ref, hx0_ref)

    row = (lax.broadcasted_iota(I32, (nv, SUBLANES, LANES), 0) * SUBLANES
           + lax.broadcasted_iota(I32, (nv, SUBLANES, LANES), 1))
    sub = lax.broadcasted_iota(I32, (SUBLANES, LANES), 0)
    bias = bias_ref[...].reshape(nv, SUBLANES, LANES)
    neg_inf = jnp.float32(-jnp.inf)

    def topk_strip(lg, st, col0):
        lanes = slice(col0 + st * LANES, col0 + (st + 1) * LANES)
        scores = _sigmoid(lg[:, st * LANES:(st + 1) * LANES]).reshape(nv, SUBLANES, LANES)
        biased = scores + bias
        gscore = []
        for g in range(N_GROUPS):
            m1 = biased[g * gv]
            m2 = jnp.full((SUBLANES, LANES), neg_inf, F32)
            for t in range(1, gv):
                v = biased[g * gv + t]
                m2 = jnp.maximum(m2, jnp.minimum(m1, v))
                m1 = jnp.maximum(m1, v)
            for s in (4, 2, 1):
                p1 = pltpu.roll(m1, s, 0)
                p2 = pltpu.roll(m2, s, 0)
                m2 = jnp.maximum(jnp.minimum(m1, p1), jnp.maximum(m2, p2))
                m1 = jnp.maximum(m1, p1)
            gscore.append(m1 + m2)
        masked = []
        for g in range(N_GROUPS):
            beaten = jnp.zeros((SUBLANES, LANES), I32)
            for o in range(N_GROUPS):
                if o == g:
                    continue
                wins = (gscore[o] > gscore[g]) | ((gscore[o] == gscore[g]) & (o < g))
                beaten = beaten + wins.astype(I32)
            keep = beaten < TOPK_GROUPS
            for t in range(gv):
                masked.append(jnp.where(keep, biased[g * gv + t], neg_inf))
        cand = jnp.stack(masked, axis=0)
        sel = jnp.zeros((nv, SUBLANES, LANES), jnp.bool_)
        picks, pick_scores = [], []
        for _ in range(TOP_K):
            mx = _sub_allreduce(jnp.max(cand, axis=0), jnp.maximum)
            idx = _sub_allreduce(jnp.min(jnp.where(cand == mx, row, N_EXPERTS), axis=0), jnp.minimum)
            hit = row == idx
            pick_scores.append(_sub_allreduce(jnp.sum(jnp.where(hit, scores, 0.0), axis=0), jnp.add))
            picks.append(idx)
            sel = sel | hit
            cand = jnp.where(hit, neg_inf, cand)
        sel_b = sel.astype(F32).astype(BF16).reshape(N_EXPERTS, LANES)
        before = jnp.dot(sel_b, tri_ref[0], preferred_element_type=F32)
        total = jnp.dot(sel_b, tri_ref[1], preferred_element_type=F32)
        rank_all = (before + run_sc[...]).reshape(nv, SUBLANES, LANES)
        run_sc[...] = run_sc[...] + total
        denom = pick_scores[0]
        for kk in range(1, TOP_K):
            denom = denom + pick_scores[kk]
        e_out = jnp.zeros((SUBLANES, LANES), I32)
        w_out = jnp.zeros((SUBLANES, LANES), F32)
        r_out = jnp.zeros((SUBLANES, LANES), I32)
        for kk in range(TOP_K):
            rk = _sub_allreduce(jnp.sum(jnp.where(row == picks[kk], rank_all, 0.0), axis=0), jnp.add)
            e_out = jnp.where(sub == kk, picks[kk], e_out)
            w_out = jnp.where(sub == kk, pick_scores[kk] / denom * ROUTED_SCALE, w_out)
            r_out = jnp.where(sub == kk, rk.astype(I32), r_out)
        e_ref[:, lanes] = e_out
        w_ref[:, lanes] = w_out
        r_ref[:, lanes] = r_out

    lg_b[...] = _router_logits(wrt_ref, hxa_ref)
    for st in range(n_strips):
        topk_strip(lg_a, st, 0)
    lg_a[...] = _router_logits(wrt_ref, hxb_ref)
    for st in range(n_strips):
        topk_strip(lg_b, st, tb)
    cnt_ref[...] = run_sc[...].astype(I32)


def _route(hxf, w_rt3, bias, tb):
    _, n, d = hxf.shape
    nt = n // tb
    assert nt % 2 == 0
    iota_r = lax.broadcasted_iota(I32, (LANES, LANES), 0)
    iota_c = lax.broadcasted_iota(I32, (LANES, LANES), 1)
    tri = jnp.stack([(iota_r < iota_c), jnp.ones((LANES, LANES), jnp.bool_)]).astype(BF16)
    bias_b = jnp.broadcast_to(bias.reshape(N_EXPERTS, 1), (N_EXPERTS, LANES)).astype(F32)
    tokrow = lambda dt: jax.ShapeDtypeStruct((TOP_K, n), dt)
    out_blk = lambda: pl.BlockSpec((TOP_K, 2 * tb), lambda i: (0, i))
    return pl.pallas_call(
        _route_kernel,
        grid=(nt // 2,),
        in_specs=[pl.BlockSpec((3, tb, d), lambda i: (0, 0, 0)),
                  pl.BlockSpec((3, tb, d), lambda i: (0, 2 * i + 1, 0)),
                  pl.BlockSpec((3, tb, d), lambda i: (0, jnp.minimum(2 * i + 2, nt - 1), 0)),
                  pl.BlockSpec(w_rt3.shape, lambda i: (0, 0, 0)),
                  pl.BlockSpec((N_EXPERTS, LANES), lambda i: (0, 0)),
                  pl.BlockSpec((2, LANES, LANES), lambda i: (0, 0, 0))],
        out_specs=[out_blk(), out_blk(), out_blk(),
                   pl.BlockSpec((N_EXPERTS, LANES), lambda i: (0, 0))],
        out_shape=[tokrow(I32), tokrow(F32), tokrow(I32),
                   jax.ShapeDtypeStruct((N_EXPERTS, LANES), I32)],
        scratch_shapes=[pltpu.VMEM((N_EXPERTS, tb), F32), pltpu.VMEM((N_EXPERTS, tb), F32),
                        pltpu.VMEM((N_EXPERTS, LANES), F32)],
        compiler_params=_params(("arbitrary",)),
        name="route",
    )(hxf, hxf, hxf, w_rt3, bias_b, tri)


def _slots_kernel(pstart_ref, e_ref, r_ref, o_ref):
    e = e_ref[...]

    def body(x, acc):
        return acc + jnp.where(e == x, pstart_ref[x], 0)

    o_ref[...] = lax.fori_loop(0, N_EXPERTS, body, r_ref[...])


def _slots(pstart, e_idx, rank, tb):
    n = e_idx.shape[1]
    blk = lambda: pl.BlockSpec((TOP_K, tb), lambda i, ps: (0, i))
    return pl.pallas_call(
        _slots_kernel,
        grid_spec=pltpu.PrefetchScalarGridSpec(
            num_scalar_prefetch=1, grid=(n // tb,), in_specs=[blk(), blk()], out_specs=blk()),
        out_shape=jax.ShapeDtypeStruct((TOP_K, n), I32),
        compiler_params=_params(("arbitrary",)),
        name="slots",
    )(pstart, e_idx, rank)


def _sc_mesh():
    return plsc.VectorSubcoreMesh(core_axis_name="c", subcore_axis_name="s")


def _sc_worker():
    return lax.axis_index("c") * SC_SUBCORES + lax.axis_index("s")


def _sc_dispatch(hx, pos, cap):
    n, w = hx.shape
    per_worker = n // SC_WINDOW // SC_WORKERS
    assert per_worker * SC_WINDOW * SC_WORKERS == n

    @pl.kernel(out_type=jax.ShapeDtypeStruct((cap, w), hx.dtype), mesh=_sc_mesh(),
               scratch_types=[pltpu.VMEM((SC_WINDOW, w), hx.dtype), pltpu.VMEM((TOP_K, SC_WINDOW), I32),
                              pltpu.SemaphoreType.DMA])
    def scatter_rows(x_hbm, i_hbm, o_hbm, xbuf, ibuf, sem):
        wid = _sc_worker()

        @pl.loop(0, per_worker)
        def _(j):
            row0 = (wid * per_worker + j) * SC_WINDOW
            pltpu.sync_copy(x_hbm.at[pl.ds(row0, SC_WINDOW)], xbuf)
            pltpu.sync_copy(i_hbm.at[:, pl.ds(row0, SC_WINDOW)], ibuf)
            copies = [pltpu.async_copy(xbuf, o_hbm.at[ibuf.at[kk]], sem) for kk in range(TOP_K)]
            for cp in copies:
                cp.wait()

    return scatter_rows(hx, pos)


def _sc_gather(ys, pos):
    n = pos.shape[1]
    w = ys.shape[1]
    per_worker = n // SC_WINDOW // SC_WORKERS
    assert per_worker * SC_WINDOW * SC_WORKERS == n

    @pl.kernel(out_type=jax.ShapeDtypeStruct((TOP_K, n, w), ys.dtype), mesh=_sc_mesh(),
               scratch_types=[pltpu.VMEM((SC_WINDOW, w), ys.dtype), pltpu.VMEM((TOP_K, SC_WINDOW), I32)])
    def gather_rows(y_hbm, i_hbm, o_hbm, ybuf, ibuf):
        wid = _sc_worker()

        @pl.loop(0, per_worker)
        def _(j):
            row0 = (wid * per_worker + j) * SC_WINDOW
            pltpu.sync_copy(i_hbm.at[:, pl.ds(row0, SC_WINDOW)], ibuf)
            for kk in range(TOP_K):
                pltpu.sync_copy(y_hbm.at[ibuf.at[kk]], ybuf)
                pltpu.sync_copy(ybuf, o_hbm.at[kk, pl.ds(row0, SC_WINDOW)])

    return gather_rows(ys, pos)


def _experts_kernel(blk_e_ref, valid_ref, blk_in_ref, blk_out_ref, next_e_ref, slot_ref,
                    xs_ref, wgu_hbm, wdn_hbm, ys_ref, gu_buf, dn_buf, sem, wgu_sc, wdn_sc):
    del blk_in_ref, blk_out_ref

    def weight_copies(e, slot):
        return (pltpu.make_async_copy(wgu_hbm.at[e], gu_buf.at[slot], sem.at[0, slot]),
                pltpu.make_async_copy(wdn_hbm.at[e], dn_buf.at[slot], sem.at[1, slot]))

    def one_block(i, rsl):
        valid = valid_ref[i]

        @pl.when(valid > 0)
        def _():
            e = blk_e_ref[i]
            slot = slot_ref[i]
            prev = blk_e_ref[jnp.maximum(i - 1, 0)]

            @pl.when(i == 0)
            def _():
                for cp in weight_copies(e, slot):
                    cp.start()

            @pl.when((i == 0) | (e != prev))
            def _():
                for cp in weight_copies(e, slot):
                    cp.wait()
                nxt = next_e_ref[i]

                @pl.when(nxt >= 0)
                def _():
                    for cp in weight_copies(nxt, 1 - slot):
                        cp.start()

                wgu_sc[...] = gu_buf[slot].astype(BF16)
                wdn_sc[...] = dn_buf[slot].astype(BF16)

            rows = lax.broadcasted_iota(I32, (EXPERT_BLOCK, xs_ref.shape[1]), 0)
            xu = jnp.where(rows < valid, pltpu.bitcast(xs_ref[rsl, :], U32), jnp.uint32(0))
            lo, hi = _unpack_bf16_pair(xu)
            half = lo.shape[1]
            gu = (jnp.dot(lo.astype(BF16), wgu_sc[:half, :], preferred_element_type=F32)
                  + jnp.dot(hi.astype(BF16), wgu_sc[half:, :], preferred_element_type=F32))
            gt = gu[:, :EXPERT_HIDDEN]
            act = (gt * _sigmoid(gt) * gu[:, EXPERT_HIDDEN:]).astype(BF16)
            y = jnp.dot(act, wdn_sc[...], preferred_element_type=F32)
            ys_ref[rsl, :] = pltpu.bitcast(_pack_bf16_pair(y[:, :half], y[:, half:]), I32)

        @pl.when(valid <= 0)
        def _():
            ys_ref[rsl, :] = jnp.zeros((EXPERT_BLOCK, ys_ref.shape[1]), I32)

    step = pl.program_id(0)
    for sub in range(EXPERT_BLOCKS_PER_STEP):
        one_block(step * EXPERT_BLOCKS_PER_STEP + sub, slice(sub * EXPERT_BLOCK, (sub + 1) * EXPERT_BLOCK))


def _experts(blk_e, blk_valid, next_expert, expert_slot, xs, w_gu, w_dn):
    cap, w = xs.shape
    n_e, d, h2 = w_gu.shape
    n_blocks = cap // EXPERT_BLOCK
    assert n_blocks % EXPERT_BLOCKS_PER_STEP == 0
    n_steps = n_blocks // EXPERT_BLOCKS_PER_STEP
    step_rows = EXPERT_BLOCKS_PER_STEP * EXPERT_BLOCK
    blk_next = next_expert[blk_e].astype(I32)
    blk_slot = expert_slot[blk_e].astype(I32)
    step = jnp.arange(n_steps, dtype=I32)
    step_used = blk_valid[::EXPERT_BLOCKS_PER_STEP] > 0
    n_used = jnp.sum(step_used.astype(I32))
    blk_in = jnp.minimum(step, jnp.maximum(n_used - 1, 0)).astype(I32)
    blk_out = jnp.where(step_used, step, n_steps).astype(I32)
    return pl.pallas_call(
        _experts_kernel,
        grid_spec=pltpu.PrefetchScalarGridSpec(
            num_scalar_prefetch=6,
            grid=(n_steps,),
            in_specs=[pl.BlockSpec((step_rows, w), lambda i, be, bv, bi, bo, bn, bs: (bi[i], 0)),
                      pl.BlockSpec(memory_space=pl.ANY),
                      pl.BlockSpec(memory_space=pl.ANY)],
            out_specs=pl.BlockSpec((step_rows, w), lambda i, be, bv, bi, bo, bn, bs: (bo[i], 0)),
            scratch_shapes=[pltpu.VMEM((2, d, h2), F32), pltpu.VMEM((2, h2 // 2, d), F32),
                            pltpu.SemaphoreType.DMA((2, 2)),
                            pltpu.VMEM((d, h2), BF16), pltpu.VMEM((h2 // 2, d), BF16)]),
        out_shape=jax.ShapeDtypeStruct((cap + step_rows, w), I32),
        compiler_params=_params(("arbitrary",)),
        name="experts",
    )(blk_e, blk_valid, blk_in, blk_out, blk_next, blk_slot, xs, w_gu, w_dn)


def _combine_kernel(yg_ref, wt_ref, hx_ref, x1_ref, gt2_ref, wsg_ref, wsd_ref, fg_ref, *rest):
    o_ref = rest[-1]
    lo, hi = _unpack_bf16_pair(pltpu.bitcast(hx_ref[...], U32))
    half = lo.shape[1]
    gu = (jnp.dot(lo.astype(BF16), wsg_ref[:half, :], preferred_element_type=F32)
          + jnp.dot(hi.astype(BF16), wsg_ref[half:, :], preferred_element_type=F32))
    gt = gu[:, :SHARED_HIDDEN]
    act = (gt * _sigmoid(gt) * gu[:, SHARED_HIDDEN:]).astype(BF16)
    y = jnp.dot(act, wsd_ref[...], preferred_element_type=F32)
    y_lo = y[:, :half]
    y_hi = y[:, half:]
    for kk in range(TOP_K):
        r_lo, r_hi = _unpack_bf16_pair(pltpu.bitcast(yg_ref[kk], U32))
        wk = wt_ref[:, kk:kk + 1]
        y_lo = y_lo + wk * r_lo
        y_hi = y_hi + wk * r_hi
    x2_lo = x1_ref[:, :half] + gt2_ref[0, :, :half] * y_lo
    x2_hi = x1_ref[:, half:] + gt2_ref[0, :, half:] * y_hi
    ms = (jnp.sum(x2_lo * x2_lo, axis=-1, keepdims=True)
          + jnp.sum(x2_hi * x2_hi, axis=-1, keepdims=True)) / (2 * half)
    inv = lax.rsqrt(ms + EPS)
    o_ref[:, :half] = x2_lo * inv * fg_ref[:, :half]
    o_ref[:, half:] = x2_hi * inv * fg_ref[:, half:]


def _combine(yg, wt, hx, x1, gt2, w_sg, w_sd, fg, tm, tiles_per_batch, tile0, out_prev):
    n, w = hx.shape
    d = 2 * w
    full = lambda a: pl.BlockSpec(a.shape, lambda i: (0,) * a.ndim)
    in_specs = [pl.BlockSpec((TOP_K, tm, w), lambda i: (0, i, 0)),
                pl.BlockSpec((tm, TOP_K), lambda i: (tile0 + i, 0)),
                pl.BlockSpec((tm, w), lambda i: (tile0 + i, 0)),
                pl.BlockSpec((tm, d), lambda i: (tile0 + i, 0)),
                pl.BlockSpec((1, 1, d), lambda i: ((tile0 + i) // tiles_per_batch, 0, 0)),
                full(w_sg), full(w_sd), full(fg)]
    args = [yg, wt, hx, x1, gt2, w_sg, w_sd, fg]
    aliases = {}
    if out_prev is not None:
        in_specs.append(pl.BlockSpec(memory_space=pl.ANY))
        args.append(out_prev)
        aliases = {len(args) - 1: 0}
    return pl.pallas_call(
        _combine_kernel,
        grid=(yg.shape[1] // tm,),
        in_specs=in_specs,
        out_specs=pl.BlockSpec((tm, d), lambda i: (tile0 + i, 0)),
        out_shape=jax.ShapeDtypeStruct((n, d), F32),
        input_output_aliases=aliases,
        compiler_params=_params(("arbitrary",)),
        name="combine",
    )(*args)


def _rope_tables(seq):
    rows = seq // GRID_W
    pos_row = np.repeat(np.arange(rows, dtype=np.float32), GRID_W)
    pos_col = np.tile(np.arange(GRID_W, dtype=np.float32), rows)
    inv_freq = (ROPE_THETA ** (-np.arange(0, AXIS_DIM, 2, dtype=np.float32) / AXIS_DIM)).astype(np.float32)
    ar = pos_row[:, None] * inv_freq
    ac = pos_col[:, None] * inv_freq
    cos_t = np.concatenate([np.cos(ar), np.cos(ar), np.cos(ac), np.cos(ac)], axis=1)
    sin_t = np.concatenate([-np.sin(ar), np.sin(ar), -np.sin(ac), np.sin(ac)], axis=1)
    return jnp.asarray(cos_t, F32), jnp.asarray(sin_t, F32)


def _tile(n, want):
    t = min(n, want)
    assert n % t == 0, (n, want)
    return t


def kernel(x, c, ctx, c_ctx, w_mod, b_mod, norm1_g, w_in, q_norm_g, k_norm_g, w_dw, b_dw, conv_ln_g, conv_ln_b, w_attn_proj, w_conv_proj, w_out, norm2_g, w_router, router_bias, w_exp_gu, w_exp_dn, w_sh_gu, w_sh_dn, final_g):
    b, s, d = x.shape
    depth = w_mod.shape[0]
    assert depth == 1, "single-layer block"
    n = b * s
    row = lambda v: v.reshape(1, -1)

    cc = jnp.zeros((SUBLANES, d), F32).at[:b].set(c).at[b].set(c_ctx)
    mod = _modulation(cc, w_mod[0], row(b_mod[0]))
    mod_x = mod[:b].reshape(b, 1, 6, d)
    sh1, sc1, gt1, sh2, sc2, gt2 = [mod_x[:, :, j, :] for j in range(6)]
    mod_c = mod[b].reshape(6, d)
    csh1, csc1 = row(mod_c[0]), row(mod_c[1])

    w_in_b = w_in[0].astype(BF16)
    q_end, kv_end = ATTN_WIDTH, ATTN_WIDTH + 2 * KV_WIDTH
    k_end = q_end + KV_WIDTH
    w_vt = w_in_b[:, k_end:kv_end].T
    kc, vct = _ctx_kv(ctx, csh1, csc1, row(norm1_g[0]), w_in_b[:, q_end:k_end], w_vt, row(k_norm_g[0]))

    cos_t, sin_t = _rope_tables(s)
    q, kx, vxt, hglu, gates = _in_proj(x, sh1, sc1, row(norm1_g[0]), w_in_b, w_vt, row(q_norm_g[0]),
                                       row(k_norm_g[0]), cos_t, sin_t, _tile(s, 512))
    score_bound = (HEAD_DIM * ATTN_SCALE * LOG2E) * jnp.max(jnp.abs(q_norm_g[0])) * jnp.max(jnp.abs(k_norm_g[0]))
    bounded = (score_bound <= SAFE_EXP2_ARG).astype(I32).reshape(1)
    o, conv = _attention(bounded, q, kc, vct, kx, vxt, hglu, w_dw[0], row(b_dw[0]),
                         _tile(s, 1024), _tile(s // 2, 1024))

    top16 = lambda a: lax.bitcast_convert_type(lax.bitcast_convert_type(a, U32) & jnp.uint32(0xFFFF0000), F32)
    w_rt = w_router[0].T
    w_rt1 = top16(w_rt)
    w_rt2 = top16(w_rt - w_rt1)
    w_rt3 = w_rt - w_rt1 - w_rt2
    w_rt1, w_rt2, w_rt3 = w_rt1.astype(BF16), w_rt2.astype(BF16), w_rt3.astype(BF16)
    x1, hx, hxf = _merge(o, conv, gates, x, gt1, sh2, sc2, row(conv_ln_g[0]),
                         row(conv_ln_b[0]), w_attn_proj[0].astype(BF16), w_conv_proj[0].astype(BF16),
                         w_out[0].astype(BF16), row(norm2_g[0]), _tile(s, 512))

    e_idx, wts, rank, counts = _route(hxf, jnp.stack([w_rt1, w_rt2, w_rt3]), router_bias[0],
                                      _tile(n // 2, 512))

    cnt = counts[:, 0]
    padded = (cnt + EXPERT_BLOCK - 1) // EXPERT_BLOCK * EXPERT_BLOCK
    pends = jnp.cumsum(padded)
    pstart = (pends - padded).astype(I32)
    n_blocks = (n * TOP_K + N_EXPERTS * (EXPERT_BLOCK - 1)) // EXPERT_BLOCK
    n_blocks = -(-n_blocks // EXPERT_BLOCKS_PER_STEP) * EXPERT_BLOCKS_PER_STEP
    cap = n_blocks * EXPERT_BLOCK
    blk_row0 = jnp.arange(n_blocks, dtype=I32) * EXPERT_BLOCK
    blk_e = jnp.minimum(jnp.sum(pends[None, :] <= blk_row0[:, None], axis=1), N_EXPERTS - 1).astype(I32)
    blk_valid = jnp.clip(pstart[blk_e] + cnt[blk_e] - blk_row0, 0, EXPERT_BLOCK).astype(I32)

    pos = _slots(pstart, e_idx, rank, _tile(n, 4096))
    xs = _sc_dispatch(hx, pos, cap)
    has_rows = cnt > 0
    later = lax.cummin(jnp.where(has_rows, jnp.arange(N_EXPERTS, dtype=I32), N_EXPERTS), reverse=True)
    next_expert = jnp.concatenate([later[1:], jnp.full((1,), N_EXPERTS, I32)])
    next_expert = jnp.where(next_expert >= N_EXPERTS, -1, next_expert)
    expert_slot = (jnp.cumsum(has_rows.astype(I32)) - 1) % 2
    ys = _experts(blk_e, blk_valid, next_expert, expert_slot, xs, w_exp_gu[0], w_exp_dn[0])
    tm = _tile(n, 256)
    chunk = n // COMBINE_CHUNKS if n % (COMBINE_CHUNKS * SC_WINDOW * SC_WORKERS) == 0 else n
    wt_t, x1_2d = wts.T, x1.reshape(n, d)
    w_sg, w_sd = w_sh_gu[0].astype(BF16), w_sh_dn[0].astype(BF16)
    out = None
    for c0 in range(0, n, chunk):
        yg = _sc_gather(ys, pos[:, c0:c0 + chunk])
        out = _combine(yg, wt_t, hx, x1_2d, gt2, w_sg, w_sd, row(final_g), tm, s // tm, c0 // tm, out)
    return out.reshape(b, s, d)
```

```python
import functools

import jax
import jax.numpy as jnp
import numpy as np
from jax import lax
from jax.experimental import pallas as pl
from jax.experimental.pallas import tpu as pltpu
from jax.experimental.pallas import tpu_sc as plsc

F32 = jnp.float32
BF16 = jnp.bfloat16
U32 = jnp.uint32
I32 = jnp.int32

GRID_W = 64
N_HEADS = 8
N_KV_HEADS = 4
GROUP = N_HEADS // N_KV_HEADS
HEAD_DIM = 128
AXIS_DIM = HEAD_DIM // 2
ATTN_WIDTH = N_HEADS * HEAD_DIM
KV_WIDTH = N_KV_HEADS * HEAD_DIM
ROPE_THETA = 10000.0
ATTN_SCALE = HEAD_DIM ** -0.5
CONV_WIDTH = 512
CONV_KERNEL = 31
CONV_PAD = CONV_KERNEL // 2
N_EXPERTS = 256
TOP_K = 8
N_GROUPS = 8
TOPK_GROUPS = 4
EXPERTS_PER_GROUP = N_EXPERTS // N_GROUPS
EXPERT_HIDDEN = 256
SHARED_HIDDEN = 256
ROUTED_SCALE = 2.5
EPS = 1e-6
LOG2E = 1.4426950408889634
SAFE_EXP2_ARG = 64.0

LANES = 128
SUBLANES = 8
VMEM_LIMIT_BYTES = 56 * 1024 * 1024

HALO_ROWS = 16
CONV_ROWS = 64
MERGE_PARTS = 2
EXPERT_BLOCK = 512
EXPERT_BLOCKS_PER_STEP = 2
SC_SUBCORES = 16
SC_WORKERS = 2 * SC_SUBCORES
SC_WINDOW = 128
COMBINE_CHUNKS = 4
HIGHEST = lax.Precision.HIGHEST


def _params(sem):
    return pltpu.CompilerParams(dimension_semantics=sem, vmem_limit_bytes=VMEM_LIMIT_BYTES)


def _sigmoid(x):
    return 1.0 / (1.0 + jnp.exp(-x))


def _pack_bf16_pair(lo, hi):
    lo_b = pltpu.bitcast(lo.astype(BF16).astype(F32), U32)
    hi_b = pltpu.bitcast(hi.astype(BF16).astype(F32), U32)
    return (lo_b >> 16) | (hi_b & jnp.uint32(0xFFFF0000))


def _unpack_bf16_pair(u):
    lo = pltpu.bitcast(u << 16, F32)
    hi = pltpu.bitcast(u & jnp.uint32(0xFFFF0000), F32)
    return lo, hi


def _mod_kernel(cc_ref, w_ref, b_ref, o_ref):
    cc = cc_ref[...]
    s = cc * _sigmoid(cc)
    o_ref[...] = jnp.dot(s, w_ref[...], precision=HIGHEST, preferred_element_type=F32) + b_ref[...]


def _modulation(cc, w_mod, b_mod):
    d, n = w_mod.shape
    tn = n // 4
    return pl.pallas_call(
        _mod_kernel,
        grid=(n // tn,),
        in_specs=[pl.BlockSpec((SUBLANES, d), lambda j: (0, 0)),
                  pl.BlockSpec((d, tn), lambda j: (0, j)),
                  pl.BlockSpec((1, tn), lambda j: (0, j))],
        out_specs=pl.BlockSpec((SUBLANES, tn), lambda j: (0, j)),
        out_shape=jax.ShapeDtypeStruct((SUBLANES, n), F32),
        compiler_params=_params(("arbitrary",)),
        name="mod",
    )(cc, w_mod, b_mod)


def _norm_modulate(x, g, sh, sc):
    ms = jnp.mean(x * x, axis=-1, keepdims=True)
    return (x * lax.rsqrt(ms + EPS) * g) * (1.0 + sc) + sh


def _head_norm(p, gain):
    r = lax.rsqrt(jnp.mean(p * p, axis=-1, keepdims=True) + EPS)
    return p * r * gain


def _dot_nt(a, b):
    return lax.dot_general(a, b, (((1,), (1,)), ((), ())), preferred_element_type=F32)


def _ctx_kv_kernel(x_ref, sh_ref, sc_ref, g1_ref, wk_ref, wvt_ref, gk_ref, k_ref, vt_ref):
    h = _norm_modulate(x_ref[0], g1_ref[...], sh_ref[...], sc_ref[...]).astype(BF16)
    pk = jnp.dot(h, wk_ref[...], preferred_element_type=F32)
    for j in range(N_KV_HEADS):
        sl = slice(j * HEAD_DIM, (j + 1) * HEAD_DIM)
        k_ref[0, :, sl] = _head_norm(pk[:, sl], gk_ref[...]).astype(BF16)
    vt_ref[0] = _dot_nt(wvt_ref[...], h).astype(BF16)


def _ctx_kv(ctx, csh, csc, g1, w_k, w_vt, gk):
    b, lc, d = ctx.shape
    vec = lambda: pl.BlockSpec((1, d), lambda i: (0, 0))
    return pl.pallas_call(
        _ctx_kv_kernel,
        grid=(b,),
        in_specs=[pl.BlockSpec((1, lc, d), lambda i: (i, 0, 0)), vec(), vec(), vec(),
                  pl.BlockSpec((d, KV_WIDTH), lambda i: (0, 0)),
                  pl.BlockSpec((KV_WIDTH, d), lambda i: (0, 0)),
                  pl.BlockSpec((1, HEAD_DIM), lambda i: (0, 0))],
        out_specs=[pl.BlockSpec((1, lc, KV_WIDTH), lambda i: (i, 0, 0)),
                   pl.BlockSpec((1, KV_WIDTH, lc), lambda i: (i, 0, 0))],
        out_shape=[jax.ShapeDtypeStruct((b, lc, KV_WIDTH), BF16),
                   jax.ShapeDtypeStruct((b, KV_WIDTH, lc), BF16)],
        compiler_params=_params(("arbitrary",)),
        name="ctx_kv",
    )(ctx, csh, csc, g1, w_k, w_vt, gk)


def _in_proj_kernel(x_ref, sh_ref, sc_ref, g1_ref, w_ref, wvt_ref, gq_ref, gk_ref, cos_ref, sin_ref,
                    q_ref, k_ref, vt_ref, h_ref, g_ref):
    h = _norm_modulate(x_ref[0], g1_ref[...], sh_ref[0], sc_ref[0]).astype(BF16)
    cos = cos_ref[...]
    sin = sin_ref[...]
    lane = lax.broadcasted_iota(I32, cos.shape, 1)
    upper = (lane & (AXIS_DIM // 2)) != 0

    def rope(p):
        swapped = jnp.where(upper, pltpu.roll(p, AXIS_DIM // 2, 1),
                            pltpu.roll(p, HEAD_DIM - AXIS_DIM // 2, 1))
        return p * cos + swapped * sin

    q_end = ATTN_WIDTH
    k_end = q_end + KV_WIDTH
    v_end = k_end + KV_WIDTH
    u_end = v_end + 2 * CONV_WIDTH
    pq = jnp.dot(h, w_ref[:, :q_end], preferred_element_type=F32)
    for j in range(N_HEADS):
        sl = slice(j * HEAD_DIM, (j + 1) * HEAD_DIM)
        q_ref[0, :, sl] = (rope(_head_norm(pq[:, sl], gq_ref[...])) * (ATTN_SCALE * LOG2E)).astype(BF16)
    pk = jnp.dot(h, w_ref[:, q_end:k_end], preferred_element_type=F32)
    for j in range(N_KV_HEADS):
        sl = slice(j * HEAD_DIM, (j + 1) * HEAD_DIM)
        k_ref[0, :, sl] = rope(_head_norm(pk[:, sl], gk_ref[...])).astype(BF16)
    vt_ref[0] = _dot_nt(wvt_ref[...], h).astype(BF16)
    u = jnp.dot(h, w_ref[:, v_end:u_end], preferred_element_type=F32)
    h_ref[0] = (u[:, :CONV_WIDTH] * _sigmoid(u[:, CONV_WIDTH:])).astype(BF16)
    g_ref[0] = _sigmoid(jnp.dot(h, w_ref[:, u_end:], preferred_element_type=F32)).astype(BF16)


def _in_proj(x, sh1, sc1, g1, w_in, w_vt, gq, gk, cos_t, sin_t, tm):
    b, s, d = x.shape
    n_in = w_in.shape[1]
    bvec = lambda: pl.BlockSpec((1, 1, d), lambda bi, i: (bi, 0, 0))
    tok = lambda w: pl.BlockSpec((1, tm, w), lambda bi, i: (bi, i, 0))
    return pl.pallas_call(
        _in_proj_kernel,
        grid=(b, s // tm),
        in_specs=[tok(d), bvec(), bvec(),
                  pl.BlockSpec((1, d), lambda bi, i: (0, 0)),
                  pl.BlockSpec((d, n_in), lambda bi, i: (0, 0)),
                  pl.BlockSpec((KV_WIDTH, d), lambda bi, i: (0, 0)),
                  pl.BlockSpec((1, HEAD_DIM), lambda bi, i: (0, 0)),
                  pl.BlockSpec((1, HEAD_DIM), lambda bi, i: (0, 0)),
                  pl.BlockSpec((tm, HEAD_DIM), lambda bi, i: (i, 0)),
                  pl.BlockSpec((tm, HEAD_DIM), lambda bi, i: (i, 0))],
        out_specs=[tok(ATTN_WIDTH), tok(KV_WIDTH),
                   pl.BlockSpec((1, KV_WIDTH, tm), lambda bi, i: (bi, 0, i)),
                   tok(CONV_WIDTH), tok(2 * d)],
        out_shape=[jax.ShapeDtypeStruct((b, s, ATTN_WIDTH), BF16),
                   jax.ShapeDtypeStruct((b, s, KV_WIDTH), BF16),
                   jax.ShapeDtypeStruct((b, KV_WIDTH, s), BF16),
                   jax.ShapeDtypeStruct((b, s, CONV_WIDTH), BF16),
                   jax.ShapeDtypeStruct((b, s, 2 * d), BF16)],
        compiler_params=_params(("arbitrary", "arbitrary")),
        name="in_proj",
    )(x, sh1, sc1, g1, w_in, w_vt, gq, gk, cos_t, sin_t)


def _sub_allreduce(x, op):
    for s in (4, 2, 1):
        x = op(x, pltpu.roll(x, s, 0))
    return x


def _attn_kernel(bounded_ref, q_ref, kc_ref, vct_ref, k_ref, vt_ref, hp_ref, hc_ref, hn_ref, wdw_ref, bdw_ref,
                 o_ref, c_ref, s0, s1, x0, x1, m_sc, l_sc, acc_sc, hcat, shift_sc, *, tk):
    tq = q_ref.shape[1]
    m_cols = GROUP * tq
    nk = k_ref.shape[1] // tk
    qf = q_ref[0].astype(F32).T
    qt = jnp.concatenate([qf[:HEAD_DIM], qf[HEAD_DIM:]], axis=1).astype(BF16)
    slots = ((s0, x0), (s1, x1))
    n_conv = max(nk // 2, 1)
    conv_rows = tq // n_conv
    conv_base = HALO_ROWS - CONV_PAD
    conv_reach = (conv_base + CONV_KERNEL - 1) // SUBLANES * SUBLANES

    def conv_fill():
        ti = pl.program_id(2)
        prev = hp_ref[0].astype(F32)
        nxt = hn_ref[0].astype(F32)
        hcat[0:HALO_ROWS, :] = jnp.where(ti > 0, prev, jnp.zeros_like(prev))
        hcat[HALO_ROWS:HALO_ROWS + tq, :] = hc_ref[0].astype(F32)
        hcat[HALO_ROWS + tq:, :] = jnp.where(ti < pl.num_programs(2) - 1, nxt, jnp.zeros_like(nxt))

    def conv_block(blk):
        r0 = blk * conv_rows
        for sub in range(0, conv_rows, CONV_ROWS):
            start = r0 + sub if isinstance(r0, int) else pl.multiple_of(r0 + sub, CONV_ROWS)
            window = hcat[pl.ds(start, CONV_ROWS + 2 * HALO_ROWS), :]
            acc = jnp.zeros((CONV_ROWS, HEAD_DIM), F32) + bdw_ref[...]
            for res in range(SUBLANES):
                shift_sc[res] = window[res:res + CONV_ROWS + conv_reach, :]
                for off in range(res, conv_base + CONV_KERNEL, SUBLANES):
                    j = off - conv_base
                    if 0 <= j < CONV_KERNEL:
                        a0 = off - res
                        acc = acc + shift_sc[res, a0:a0 + CONV_ROWS, :] * wdw_ref[j:j + 1, :]
            c_ref[0, pl.ds(start, CONV_ROWS), :] = acc

    def kchunk(j):
        return k_ref[0, pl.ds(pl.multiple_of(j * tk, tk), tk), :]

    def vchunk(j):
        return vt_ref[0, :, pl.ds(pl.multiple_of(j * tk, tk), tk)]

    def split(st):
        return st.reshape(st.shape[0] // SUBLANES, SUBLANES, m_cols)

    def scores(k, online):
        st = jnp.dot(k, qt, preferred_element_type=F32)
        return st, (jnp.max(split(st), axis=0) if online else None)

    def absorb(st, mx, vt, online):
        s3 = split(st)
        if online:
            m_prev = m_sc[...]
            m_new = jnp.maximum(m_prev, _sub_allreduce(mx, jnp.maximum))
            alpha = jnp.exp2(m_prev - m_new)
            p3 = jnp.exp2(s3 - m_new[None])
            l_sc[...] = alpha * l_sc[...] + _sub_allreduce(jnp.sum(p3, axis=0), jnp.add)
            pv = jnp.dot(vt, p3.reshape(st.shape).astype(BF16), preferred_element_type=F32)
            acc_sc[...] = alpha[0:1] * acc_sc[...] + pv
            m_sc[...] = m_new
        else:
            p3 = jnp.exp2(s3)
            l_sc[...] = l_sc[...] + jnp.sum(p3, axis=0)
            acc_sc[...] = acc_sc[...] + jnp.dot(vt, p3.reshape(st.shape).astype(BF16),
                                                preferred_element_type=F32)

    def stage(slot, k, online):
        st, mx = scores(k, online)
        slots[slot][0][...] = st
        if online:
            slots[slot][1][...] = mx

    def take(slot, vt, online):
        absorb(slots[slot][0][...], slots[slot][1][...] if online else None, vt, online)

    def sweep(online):
        if online:
            m_sc[...] = jnp.full(m_sc.shape, -jnp.inf, F32)
        l_sc[...] = jnp.zeros(l_sc.shape, F32)
        acc_sc[...] = jnp.zeros(acc_sc.shape, F32)
        conv_fill()
        stage(0, kchunk(0), online)

        def body(i, carry):
            j = 2 * i
            stage(1, kchunk(j + 1), online)
            take(0, vchunk(j), online)
            conv_block(i)
            stage(0, kchunk(j + 2), online)
            take(1, vchunk(j + 1), online)
            return carry

        lax.fori_loop(0, nk // 2 - 1, body, 0)
        conv_block(n_conv - 1)
        stage(1, kchunk(nk - 1), online)
        take(0, vchunk(nk - 2), online)
        sc, xc = scores(kc_ref[0], online)
        take(1, vchunk(nk - 1), online)
        absorb(sc, xc, vct_ref[0], online)
        denom = l_sc[...] if online else _sub_allreduce(l_sc[...], jnp.add)
        o = (acc_sc[...] / denom[0:1]).T
        o_ref[0, :, :HEAD_DIM] = o[:tq].astype(BF16)
        o_ref[0, :, HEAD_DIM:] = o[tq:].astype(BF16)

    @pl.when(bounded_ref[0] != 0)
    def _():
        sweep(online=False)

    @pl.when(bounded_ref[0] == 0)
    def _():
        sweep(online=True)


def _attention(bounded, q, kc, vct, kx, vxt, hglu, w_dw, b_dw, tq, tk):
    b, s, _ = q.shape
    lc = kc.shape[1]
    assert s % (2 * tk) == 0 and CONV_WIDTH == N_KV_HEADS * HEAD_DIM
    gw = GROUP * HEAD_DIM
    m_cols = GROUP * tq
    hb = tq // HALO_ROWS
    n_halo = s // HALO_ROWS
    reach = (HALO_ROWS - CONV_PAD + CONV_KERNEL - 1) // SUBLANES * SUBLANES
    kv = lambda l: pl.BlockSpec((1, l, HEAD_DIM), lambda bi, h, i, bd: (bi, 0, h))
    kvt = lambda l: pl.BlockSpec((1, HEAD_DIM, l), lambda bi, h, i, bd: (bi, h, 0))
    qo = lambda: pl.BlockSpec((1, tq, gw), lambda bi, h, i, bd: (bi, i, h))
    chan = lambda rows: pl.BlockSpec((rows, HEAD_DIM), lambda bi, h, i, bd: (0, h))
    return pl.pallas_call(
        functools.partial(_attn_kernel, tk=tk),
        grid_spec=pltpu.PrefetchScalarGridSpec(
            num_scalar_prefetch=1,
            grid=(b, N_KV_HEADS, s // tq),
            in_specs=[qo(), kv(lc), kvt(lc), kv(s), kvt(s),
                      pl.BlockSpec((1, HALO_ROWS, HEAD_DIM),
                                   lambda bi, h, i, bd: (bi, jnp.maximum(i * hb - 1, 0), h)),
                      pl.BlockSpec((1, tq, HEAD_DIM), lambda bi, h, i, bd: (bi, i, h)),
                      pl.BlockSpec((1, HALO_ROWS, HEAD_DIM),
                                   lambda bi, h, i, bd: (bi, jnp.minimum((i + 1) * hb, n_halo - 1), h)),
                      chan(CONV_KERNEL), chan(1)],
            out_specs=[qo(), pl.BlockSpec((1, tq, HEAD_DIM), lambda bi, h, i, bd: (bi, i, h))],
            scratch_shapes=[pltpu.VMEM((tk, m_cols), F32), pltpu.VMEM((tk, m_cols), F32),
                            pltpu.VMEM((SUBLANES, m_cols), F32), pltpu.VMEM((SUBLANES, m_cols), F32),
                            pltpu.VMEM((SUBLANES, m_cols), F32), pltpu.VMEM((SUBLANES, m_cols), F32),
                            pltpu.VMEM((HEAD_DIM, m_cols), F32),
                            pltpu.VMEM((tq + 2 * HALO_ROWS, HEAD_DIM), F32),
                            pltpu.VMEM((SUBLANES, CONV_ROWS + reach, HEAD_DIM), F32)]),
        out_shape=[jax.ShapeDtypeStruct((b, s, ATTN_WIDTH), BF16),
                   jax.ShapeDtypeStruct((b, s, CONV_WIDTH), F32)],
        compiler_params=_params(("arbitrary", "arbitrary", "arbitrary")),
        name="attn",
    )(bounded, q, kc, vct, kx, vxt, hglu, hglu, hglu, w_dw, b_dw)


def _split3(x):
    x1 = x.astype(BF16)
    r1 = x - x1.astype(F32)
    x2 = r1.astype(BF16)
    x3 = (r1 - x2.astype(F32)).astype(BF16)
    return x1, x2, x3


def _merge_kernel(o_ref, c_ref, g_ref, x_ref, gt1_ref, sh2_ref, sc2_ref,
                  lng_ref, lnb_ref, wap_ref, wcp_ref, wout_ref, g2_ref,
                  x1_ref, hx_ref, hxf_ref):
    tm = x_ref.shape[1]
    d = x_ref.shape[2]
    half = d // 2
    rows_per_part = tm // MERGE_PARTS
    for part in range(MERGE_PARTS):
        p0 = part * rows_per_part
        rows = slice(p0, p0 + rows_per_part)
        conv = c_ref[0, rows, :]
        mu = jnp.mean(conv, axis=-1, keepdims=True)
        cen = conv - mu
        var = jnp.mean(cen * cen, axis=-1, keepdims=True)
        ln = cen * lax.rsqrt(var + EPS) * lng_ref[...] + lnb_ref[...]
        act = (ln * _sigmoid(ln)).astype(BF16)
        y_conv = jnp.dot(act, wcp_ref[...], preferred_element_type=F32)
        y_attn = jnp.dot(o_ref[0, rows, :], wap_ref[...], preferred_element_type=F32)
        z = g_ref[0, rows, :d].astype(F32) * y_attn + g_ref[0, rows, d:].astype(F32) * y_conv
        mix = jnp.dot(z.astype(BF16), wout_ref[...], preferred_element_type=F32)
        x1 = x_ref[0, rows, :] + gt1_ref[0] * mix
        x1_ref[0, rows, :] = x1
        hx = _norm_modulate(x1, g2_ref[...], sh2_ref[0], sc2_ref[0])
        hx_ref[rows, :] = pltpu.bitcast(_pack_bf16_pair(hx[:, :half], hx[:, half:]), I32)
        h1, h2, h3 = _split3(hx)
        hxf_ref[0, rows, :] = h1
        hxf_ref[1, rows, :] = h2
        hxf_ref[2, rows, :] = h3


def _merge(o, conv, g, x, gt1, sh2, sc2, ln_g, ln_b, w_ap, w_cp, w_out, g2, tm):
    b, s, d = x.shape
    nt = s // tm
    bvec = lambda: pl.BlockSpec((1, 1, d), lambda bi, i: (bi, 0, 0))
    full = lambda a: pl.BlockSpec(a.shape, lambda bi, i: (0,) * a.ndim)
    tok = lambda w: pl.BlockSpec((1, tm, w), lambda bi, i: (bi, i, 0))
    return pl.pallas_call(
        _merge_kernel,
        grid=(b, nt),
        in_specs=[tok(ATTN_WIDTH), tok(CONV_WIDTH), tok(2 * d), tok(d), bvec(), bvec(), bvec(),
                  full(ln_g), full(ln_b), full(w_ap), full(w_cp), full(w_out), full(g2)],
        out_specs=[tok(d),
                   pl.BlockSpec((tm, d // 2), lambda bi, i: (bi * nt + i, 0)),
                   pl.BlockSpec((3, tm, d), lambda bi, i: (0, bi * nt + i, 0))],
        out_shape=[jax.ShapeDtypeStruct((b, s, d), F32),
                   jax.ShapeDtypeStruct((b * s, d // 2), I32),
                   jax.ShapeDtypeStruct((3, b * s, d), BF16)],
        compiler_params=_params(("arbitrary", "arbitrary")),
        name="merge",
    )(o, conv, g, x, gt1, sh2, sc2, ln_g, ln_b, w_ap, w_cp, w_out, g2)


def _router_logits(w_ref, h_ref):
    h1, h2, h3 = h_ref[0], h_ref[1], h_ref[2]
    w1, w2, w3 = w_ref[0], w_ref[1], w_ref[2]
    return ((((_dot_nt(w3, h1) + _dot_nt(w1, h3)) + _dot_nt(w2, h2))
             + (_dot_nt(w2, h1) + _dot_nt(w1, h2))) + _dot_nt(w1, h1))


def _route_kernel(hx0_ref, hxa_ref, hxb_ref, wrt_ref, bias_ref, tri_ref, e_ref, w_ref, r_ref, cnt_ref,
                  lg_a, lg_b, run_sc):
    step = pl.program_id(0)
    tb = hxa_ref.shape[1]
    n_strips = tb // LANES
    nv = N_EXPERTS // SUBLANES
    gv = EXPERTS_PER_GROUP // SUBLANES

    @pl.when(step == 0)
    def _():
        run_sc[...] = jnp.zeros(run_sc.shape, F32)
        lg_a[...] = _router_logits(wrt_ref, hx0_ref)

    row = (lax.broadcasted_iota(I32, (nv, SUBLANES, LANES), 0) * SUBLANES
           + lax.broadcasted_iota(I32, (nv, SUBLANES, LANES), 1))
    sub = lax.broadcasted_iota(I32, (SUBLANES, LANES), 0)
    bias = bias_ref[...].reshape(nv, SUBLANES, LANES)
    neg_inf = jnp.float32(-jnp.inf)

    def topk_strip(lg, st, col0):
        lanes = slice(col0 + st * LANES, col0 + (st + 1) * LANES)
        scores = _sigmoid(lg[:, st * LANES:(st + 1) * LANES]).reshape(nv, SUBLANES, LANES)
        biased = scores + bias
        gscore = []
        for g in range(N_GROUPS):
            m1 = biased[g * gv]
            m2 = jnp.full((SUBLANES, LANES), neg_inf, F32)
            for t in range(1, gv):
                v = biased[g * gv + t]
                m2 = jnp.maximum(m2, jnp.minimum(m1, v))
                m1 = jnp.maximum(m1, v)
            for s in (4, 2, 1):
                p1 = pltpu.roll(m1, s, 0)
                p2 = pltpu.roll(m2, s, 0)
                m2 = jnp.maximum(jnp.minimum(m1, p1), jnp.maximum(m2, p2))
                m1 = jnp.maximum(m1, p1)
            gscore.append(m1 + m2)
        masked = []
        for g in range(N_GROUPS):
            beaten = jnp.zeros((SUBLANES, LANES), I32)
            for o in range(N_GROUPS):
                if o == g:
                    continue
                wins = (gscore[o] > gscore[g]) | ((gscore[o] == gscore[g]) & (o < g))
                beaten = beaten + wins.astype(I32)
            keep = beaten < TOPK_GROUPS
            for t in range(gv):
                masked.append(jnp.where(keep, biased[g * gv + t], neg_inf))
        cand = jnp.stack(masked, axis=0)
        sel = jnp.zeros((nv, SUBLANES, LANES), jnp.bool_)
        picks, pick_scores = [], []
        for _ in range(TOP_K):
            mx = _sub_allreduce(jnp.max(cand, axis=0), jnp.maximum)
            idx = _sub_allreduce(jnp.min(jnp.where(cand == mx, row, N_EXPERTS), axis=0), jnp.minimum)
            hit = row == idx
            pick_scores.append(_sub_allreduce(jnp.sum(jnp.where(hit, scores, 0.0), axis=0), jnp.add))
            picks.append(idx)
            sel = sel | hit
            cand = jnp.where(hit, neg_inf, cand)
        sel_b = sel.astype(F32).astype(BF16).reshape(N_EXPERTS, LANES)
        before = jnp.dot(sel_b, tri_ref[0], preferred_element_type=F32)
        total = jnp.dot(sel_b, tri_ref[1], preferred_element_type=F32)
        rank_all = (before + run_sc[...]).reshape(nv, SUBLANES, LANES)
        run_sc[...] = run_sc[...] + total
        denom = pick_scores[0]
        for kk in range(1, TOP_K):
            denom = denom + pick_scores[kk]
        e_out = jnp.zeros((SUBLANES, LANES), I32)
        w_out = jnp.zeros((SUBLANES, LANES), F32)
        r_out = jnp.zeros((SUBLANES, LANES), I32)
        for kk in range(TOP_K):
            rk = _sub_allreduce(jnp.sum(jnp.where(row == picks[kk], rank_all, 0.0), axis=0), jnp.add)
            e_out = jnp.where(sub == kk, picks[kk], e_out)
            w_out = jnp.where(sub == kk, pick_scores[kk] / denom * ROUTED_SCALE, w_out)
            r_out = jnp.where(sub == kk, rk.astype(I32), r_out)
        e_ref[:, lanes] = e_out
        w_ref[:, lanes] = w_out
        r_ref[:, lanes] = r_out

    lg_b[...] = _router_logits(wrt_ref, hxa_ref)
    for st in range(n_strips):
        topk_strip(lg_a, st, 0)
    lg_a[...] = _router_logits(wrt_ref, hxb_ref)
    for st in range(n_strips):
        topk_strip(lg_b, st, tb)
    cnt_ref[...] = run_sc[...].astype(I32)


def _route(hxf, w_rt3, bias, tb):
    _, n, d = hxf.shape
    nt = n // tb
    assert nt % 2 == 0
    iota_r = lax.broadcasted_iota(I32, (LANES, LANES), 0)
    iota_c = lax.broadcasted_iota(I32, (LANES, LANES), 1)
    tri = jnp.stack([(iota_r < iota_c), jnp.ones((LANES, LANES), jnp.bool_)]).astype(BF16)
    bias_b = jnp.broadcast_to(bias.reshape(N_EXPERTS, 1), (N_EXPERTS, LANES)).astype(F32)
    tokrow = lambda dt: jax.ShapeDtypeStruct((TOP_K, n), dt)
    out_blk = lambda: pl.BlockSpec((TOP_K, 2 * tb), lambda i: (0, i))
    return pl.pallas_call(
        _route_kernel,
        grid=(nt // 2,),
        in_specs=[pl.BlockSpec((3, tb, d), lambda i: (0, 0, 0)),
                  pl.BlockSpec((3, tb, d), lambda i: (0, 2 * i + 1, 0)),
                  pl.BlockSpec((3, tb, d), lambda i: (0, jnp.minimum(2 * i + 2, nt - 1), 0)),
                  pl.BlockSpec(w_rt3.shape, lambda i: (0, 0, 0)),
                  pl.BlockSpec((N_EXPERTS, LANES), lambda i: (0, 0)),
                  pl.BlockSpec((2, LANES, LANES), lambda i: (0, 0, 0))],
        out_specs=[out_blk(), out_blk(), out_blk(),
                   pl.BlockSpec((N_EXPERTS, LANES), lambda i: (0, 0))],
        out_shape=[tokrow(I32), tokrow(F32), tokrow(I32),
                   jax.ShapeDtypeStruct((N_EXPERTS, LANES), I32)],
        scratch_shapes=[pltpu.VMEM((N_EXPERTS, tb), F32), pltpu.VMEM((N_EXPERTS, tb), F32),
                        pltpu.VMEM((N_EXPERTS, LANES), F32)],
        compiler_params=_params(("arbitrary",)),
        name="route",
    )(hxf, hxf, hxf, w_rt3, bias_b, tri)


def _slots_kernel(pstart_ref, e_ref, r_ref, o_ref):
    e = e_ref[...]

    def body(x, acc):
        return acc + jnp.where(e == x, pstart_ref[x], 0)

    o_ref[...] = lax.fori_loop(0, N_EXPERTS, body, r_ref[...])


def _slots(pstart, e_idx, rank, tb):
    n = e_idx.shape[1]
    blk = lambda: pl.BlockSpec((TOP_K, tb), lambda i, ps: (0, i))
    return pl.pallas_call(
        _slots_kernel,
        grid_spec=pltpu.PrefetchScalarGridSpec(
            num_scalar_prefetch=1, grid=(n // tb,), in_specs=[blk(), blk()], out_specs=blk()),
        out_shape=jax.ShapeDtypeStruct((TOP_K, n), I32),
        compiler_params=_params(("arbitrary",)),
        name="slots",
    )(pstart, e_idx, rank)


def _sc_mesh():
    return plsc.VectorSubcoreMesh(core_axis_name="c", subcore_axis_name="s")


def _sc_worker():
    return lax.axis_index("c") * SC_SUBCORES + lax.axis_index("s")


def _sc_dispatch(hx, pos, cap):
    n, w = hx.shape
    per_worker = n // SC_WINDOW // SC_WORKERS
    assert per_worker * SC_WINDOW * SC_WORKERS == n

    @pl.kernel(out_type=jax.ShapeDtypeStruct((cap, w), hx.dtype), mesh=_sc_mesh(),
               scratch_types=[pltpu.VMEM((SC_WINDOW, w), hx.dtype), pltpu.VMEM((TOP_K, SC_WINDOW), I32),
                              pltpu.SemaphoreType.DMA])
    def scatter_rows(x_hbm, i_hbm, o_hbm, xbuf, ibuf, sem):
        wid = _sc_worker()

        @pl.loop(0, per_worker)
        def _(j):
            row0 = (wid * per_worker + j) * SC_WINDOW
            pltpu.sync_copy(x_hbm.at[pl.ds(row0, SC_WINDOW)], xbuf)
            pltpu.sync_copy(i_hbm.at[:, pl.ds(row0, SC_WINDOW)], ibuf)
            copies = [pltpu.async_copy(xbuf, o_hbm.at[ibuf.at[kk]], sem) for kk in range(TOP_K)]
            for cp in copies:
                cp.wait()

    return scatter_rows(hx, pos)


def _sc_gather(ys, pos):
    n = pos.shape[1]
    w = ys.shape[1]
    per_worker = n // SC_WINDOW // SC_WORKERS
    assert per_worker * SC_WINDOW * SC_WORKERS == n

    @pl.kernel(out_type=jax.ShapeDtypeStruct((TOP_K, n, w), ys.dtype), mesh=_sc_mesh(),
               scratch_types=[pltpu.VMEM((SC_WINDOW, w), ys.dtype), pltpu.VMEM((TOP_K, SC_WINDOW), I32)])
    def gather_rows(y_hbm, i_hbm, o_hbm, ybuf, ibuf):
        wid = _sc_worker()

        @pl.loop(0, per_worker)
        def _(j):
            row0 = (wid * per_worker + j) * SC_WINDOW
            pltpu.sync_copy(i_hbm.at[:, pl.ds(row0, SC_WINDOW)], ibuf)
            for kk in range(TOP_K):
                pltpu.sync_copy(y_hbm.at[ibuf.at[kk]], ybuf)
                pltpu.sync_copy(ybuf, o_hbm.at[kk, pl.ds(row0, SC_WINDOW)])

    return gather_rows(ys, pos)


def _experts_kernel(blk_e_ref, valid_ref, blk_in_ref, blk_out_ref, next_e_ref, slot_ref,
                    xs_ref, wgu_hbm, wdn_hbm, ys_ref, gu_buf, dn_buf, sem, wgu_sc, wdn_sc):
    del blk_in_ref, blk_out_ref

    def weight_copies(e, slot):
        return (pltpu.make_async_copy(wgu_hbm.at[e], gu_buf.at[slot], sem.at[0, slot]),
                pltpu.make_async_copy(wdn_hbm.at[e], dn_buf.at[slot], sem.at[1, slot]))

    def one_block(i, rsl):
        valid = valid_ref[i]

        @pl.when(valid > 0)
        def _():
            e = blk_e_ref[i]
            slot = slot_ref[i]
            prev = blk_e_ref[jnp.maximum(i - 1, 0)]

            @pl.when(i == 0)
            def _():
                for cp in weight_copies(e, slot):
                    cp.start()

            @pl.when((i == 0) | (e != prev))
            def _():
                for cp in weight_copies(e, slot):
                    cp.wait()
                nxt = next_e_ref[i]

                @pl.when(nxt >= 0)
                def _():
                    for cp in weight_copies(nxt, 1 - slot):
                        cp.start()

                wgu_sc[...] = gu_buf[slot].astype(BF16)
                wdn_sc[...] = dn_buf[slot].astype(BF16)

            rows = lax.broadcasted_iota(I32, (EXPERT_BLOCK, xs_ref.shape[1]), 0)
            xu = jnp.where(rows < valid, pltpu.bitcast(xs_ref[rsl, :], U32), jnp.uint32(0))
            lo, hi = _unpack_bf16_pair(xu)
            half = lo.shape[1]
            gu = (jnp.dot(lo.astype(BF16), wgu_sc[:half, :], preferred_element_type=F32)
                  + jnp.dot(hi.astype(BF16), wgu_sc[half:, :], preferred_element_type=F32))
            gt = gu[:, :EXPERT_HIDDEN]
            act = (gt * _sigmoid(gt) * gu[:, EXPERT_HIDDEN:]).astype(BF16)
            y = jnp.dot(act, wdn_sc[...], preferred_element_type=F32)
            ys_ref[rsl, :] = pltpu.bitcast(_pack_bf16_pair(y[:, :half], y[:, half:]), I32)

        @pl.when(valid <= 0)
        def _():
            ys_ref[rsl, :] = jnp.zeros((EXPERT_BLOCK, ys_ref.shape[1]), I32)

    step = pl.program_id(0)
    for sub in range(EXPERT_BLOCKS_PER_STEP):
        one_block(step * EXPERT_BLOCKS_PER_STEP + sub, slice(sub * EXPERT_BLOCK, (sub + 1) * EXPERT_BLOCK))


def _experts(blk_e, blk_valid, next_expert, expert_slot, xs, w_gu, w_dn):
    cap, w = xs.shape
    n_e, d, h2 = w_gu.shape
    n_blocks = cap // EXPERT_BLOCK
    assert n_blocks % EXPERT_BLOCKS_PER_STEP == 0
    n_steps = n_blocks // EXPERT_BLOCKS_PER_STEP
    step_rows = EXPERT_BLOCKS_PER_STEP * EXPERT_BLOCK
    blk_next = next_expert[blk_e].astype(I32)
    blk_slot = expert_slot[blk_e].astype(I32)
    step = jnp.arange(n_steps, dtype=I32)
    step_used = blk_valid[::EXPERT_BLOCKS_PER_STEP] > 0
    n_used = jnp.sum(step_used.astype(I32))
    blk_in = jnp.minimum(step, jnp.maximum(n_used - 1, 0)).astype(I32)
    blk_out = jnp.where(step_used, step, n_steps).astype(I32)
    return pl.pallas_call(
        _experts_kernel,
        grid_spec=pltpu.PrefetchScalarGridSpec(
            num_scalar_prefetch=6,
            grid=(n_steps,),
            in_specs=[pl.BlockSpec((step_rows, w), lambda i, be, bv, bi, bo, bn, bs: (bi[i], 0)),
                      pl.BlockSpec(memory_space=pl.ANY),
                      pl.BlockSpec(memory_space=pl.ANY)],
            out_specs=pl.BlockSpec((step_rows, w), lambda i, be, bv, bi, bo, bn, bs: (bo[i], 0)),
            scratch_shapes=[pltpu.VMEM((2, d, h2), F32), pltpu.VMEM((2, h2 // 2, d), F32),
                            pltpu.SemaphoreType.DMA((2, 2)),
                            pltpu.VMEM((d, h2), BF16), pltpu.VMEM((h2 // 2, d), BF16)]),
        out_shape=jax.ShapeDtypeStruct((cap + step_rows, w), I32),
        compiler_params=_params(("arbitrary",)),
        name="experts",
    )(blk_e, blk_valid, blk_in, blk_out, blk_next, blk_slot, xs, w_gu, w_dn)


def _combine_kernel(yg_ref, wt_ref, hx_ref, x1_ref, gt2_ref, wsg_ref, wsd_ref, fg_ref, *rest):
    o_ref = rest[-1]
    lo, hi = _unpack_bf16_pair(pltpu.bitcast(hx_ref[...], U32))
    half = lo.shape[1]
    gu = (jnp.dot(lo.astype(BF16), wsg_ref[:half, :], preferred_element_type=F32)
          + jnp.dot(hi.astype(BF16), wsg_ref[half:, :], preferred_element_type=F32))
    gt = gu[:, :SHARED_HIDDEN]
    act = (gt * _sigmoid(gt) * gu[:, SHARED_HIDDEN:]).astype(BF16)
    y = jnp.dot(act, wsd_ref[...], preferred_element_type=F32)
    y_lo = y[:, :half]
    y_hi = y[:, half:]
    for kk in range(TOP_K):
        r_lo, r_hi = _unpack_bf16_pair(pltpu.bitcast(yg_ref[kk], U32))
        wk = wt_ref[:, kk:kk + 1]
        y_lo = y_lo + wk * r_lo
        y_hi = y_hi + wk * r_hi
    x2_lo = x1_ref[:, :half] + gt2_ref[0, :, :half] * y_lo
    x2_hi = x1_ref[:, half:] + gt2_ref[0, :, half:] * y_hi
    ms = (jnp.sum(x2_lo * x2_lo, axis=-1, keepdims=True)
          + jnp.sum(x2_hi * x2_hi, axis=-1, keepdims=True)) / (2 * half)
    inv = lax.rsqrt(ms + EPS)
    o_ref[:, :half] = x2_lo * inv * fg_ref[:, :half]
    o_ref[:, half:] = x2_hi * inv * fg_ref[:, half:]


def _combine(yg, wt, hx, x1, gt2, w_sg, w_sd, fg, tm, tiles_per_batch, tile0, out_prev):
    n, w = hx.shape
    d = 2 * w
    full = lambda a: pl.BlockSpec(a.shape, lambda i: (0,) * a.ndim)
    in_specs = [pl.BlockSpec((TOP_K, tm, w), lambda i: (0, i, 0)),
                pl.BlockSpec((tm, TOP_K), lambda i: (tile0 + i, 0)),
                pl.BlockSpec((tm, w), lambda i: (tile0 + i, 0)),
                pl.BlockSpec((tm, d), lambda i: (tile0 + i, 0)),
                pl.BlockSpec((1, 1, d), lambda i: ((tile0 + i) // tiles_per_batch, 0, 0)),
                full(w_sg), full(w_sd), full(fg)]
    args = [yg, wt, hx, x1, gt2, w_sg, w_sd, fg]
    aliases = {}
    if out_prev is not None:
        in_specs.append(pl.BlockSpec(memory_space=pl.ANY))
        args.append(out_prev)
        aliases = {len(args) - 1: 0}
    return pl.pallas_call(
        _combine_kernel,
        grid=(yg.shape[1] // tm,),
        in_specs=in_specs,
        out_specs=pl.BlockSpec((tm, d), lambda i: (tile0 + i, 0)),
        out_shape=jax.ShapeDtypeStruct((n, d), F32),
        input_output_aliases=aliases,
        compiler_params=_params(("arbitrary",)),
        name="combine",
    )(*args)


def _rope_tables(seq):
    rows = seq // GRID_W
    pos_row = np.repeat(np.arange(rows, dtype=np.float32), GRID_W)
    pos_col = np.tile(np.arange(GRID_W, dtype=np.float32), rows)
    inv_freq = (ROPE_THETA ** (-np.arange(0, AXIS_DIM, 2, dtype=np.float32) / AXIS_DIM)).astype(np.float32)
    ar = pos_row[:, None] * inv_freq
    ac = pos_col[:, None] * inv_freq
    cos_t = np.concatenate([np.cos(ar), np.cos(ar), np.cos(ac), np.cos(ac)], axis=1)
    sin_t = np.concatenate([-np.sin(ar), np.sin(ar), -np.sin(ac), np.sin(ac)], axis=1)
    return jnp.asarray(cos_t, F32), jnp.asarray(sin_t, F32)


def _tile(n, want):
    t = min(n, want)
    assert n % t == 0, (n, want)
    return t


def kernel(x, c, ctx, c_ctx, w_mod, b_mod, norm1_g, w_in, q_norm_g, k_norm_g, w_dw, b_dw, conv_ln_g, conv_ln_b, w_attn_proj, w_conv_proj, w_out, norm2_g, w_router, router_bias, w_exp_gu, w_exp_dn, w_sh_gu, w_sh_dn, final_g):
    b, s, d = x.shape
    depth = w_mod.shape[0]
    assert depth == 1, "single-layer block"
    n = b * s
    row = lambda v: v.reshape(1, -1)

    cc = jnp.zeros((SUBLANES, d), F32).at[:b].set(c).at[b].set(c_ctx)
    mod = _modulation(cc, w_mod[0], row(b_mod[0]))
    mod_x = mod[:b].reshape(b, 1, 6, d)
    sh1, sc1, gt1, sh2, sc2, gt2 = [mod_x[:, :, j, :] for j in range(6)]
    mod_c = mod[b].reshape(6, d)
    csh1, csc1 = row(mod_c[0]), row(mod_c[1])

    w_in_b = w_in[0].astype(BF16)
    q_end, kv_end = ATTN_WIDTH, ATTN_WIDTH + 2 * KV_WIDTH
    k_end = q_end + KV_WIDTH
    w_vt = w_in_b[:, k_end:kv_end].T
    kc, vct = _ctx_kv(ctx, csh1, csc1, row(norm1_g[0]), w_in_b[:, q_end:k_end], w_vt, row(k_norm_g[0]))

    cos_t, sin_t = _rope_tables(s)
    q, kx, vxt, hglu, gates = _in_proj(x, sh1, sc1, row(norm1_g[0]), w_in_b, w_vt, row(q_norm_g[0]),
                                       row(k_norm_g[0]), cos_t, sin_t, _tile(s, 512))
    score_bound = (HEAD_DIM * ATTN_SCALE * LOG2E) * jnp.max(jnp.abs(q_norm_g[0])) * jnp.max(jnp.abs(k_norm_g[0]))
    bounded = (score_bound <= SAFE_EXP2_ARG).astype(I32).reshape(1)
    o, conv = _attention(bounded, q, kc, vct, kx, vxt, hglu, w_dw[0], row(b_dw[0]),
                         _tile(s, 1024), _tile(s // 2, 1024))

    top16 = lambda a: lax.bitcast_convert_type(lax.bitcast_convert_type(a, U32) & jnp.uint32(0xFFFF0000), F32)
    w_rt = w_router[0].T
    w_rt1 = top16(w_rt)
    w_rt2 = top16(w_rt - w_rt1)
    w_rt3 = w_rt - w_rt1 - w_rt2
    w_rt1, w_rt2, w_rt3 = w_rt1.astype(BF16), w_rt2.astype(BF16), w_rt3.astype(BF16)
    x1, hx, hxf = _merge(o, conv, gates, x, gt1, sh2, sc2, row(conv_ln_g[0]),
                         row(conv_ln_b[0]), w_attn_proj[0].astype(BF16), w_conv_proj[0].astype(BF16),
                         w_out[0].astype(BF16), row(norm2_g[0]), _tile(s, 512))

    e_idx, wts, rank, counts = _route(hxf, jnp.stack([w_rt1, w_rt2, w_rt3]), router_bias[0],
                                      _tile(n // 2, 512))

    cnt = counts[:, 0]
    padded = (cnt + EXPERT_BLOCK - 1) // EXPERT_BLOCK * EXPERT_BLOCK
    pends = jnp.cumsum(padded)
    pstart = (pends - padded).astype(I32)
    n_blocks = (n * TOP_K + N_EXPERTS * (EXPERT_BLOCK - 1)) // EXPERT_BLOCK
    n_blocks = -(-n_blocks // EXPERT_BLOCKS_PER_STEP) * EXPERT_BLOCKS_PER_STEP
    cap = n_blocks * EXPERT_BLOCK
    blk_row0 = jnp.arange(n_blocks, dtype=I32) * EXPERT_BLOCK
    blk_e = jnp.minimum(jnp.sum(pends[None, :] <= blk_row0[:, None], axis=1), N_EXPERTS - 1).astype(I32)
    blk_valid = jnp.clip(pstart[blk_e] + cnt[blk_e] - blk_row0, 0, EXPERT_BLOCK).astype(I32)

    pos = _slots(pstart, e_idx, rank, _tile(n, 4096))
    xs = _sc_dispatch(hx, pos, cap)
    has_rows = cnt > 0
    later = lax.cummin(jnp.where(has_rows, jnp.arange(N_EXPERTS, dtype=I32), N_EXPERTS), reverse=True)
    next_expert = jnp.concatenate([later[1:], jnp.full((1,), N_EXPERTS, I32)])
    next_expert = jnp.where(next_expert >= N_EXPERTS, -1, next_expert)
    expert_slot = (jnp.cumsum(has_rows.astype(I32)) - 1) % 2
    ys = _experts(blk_e, blk_valid, next_expert, expert_slot, xs, w_exp_gu[0], w_exp_dn[0])
    tm = _tile(n, 256)
    chunk = n // COMBINE_CHUNKS if n % (COMBINE_CHUNKS * SC_WINDOW * SC_WORKERS) == 0 else n
    wt_t, x1_2d = wts.T, x1.reshape(n, d)
    w_sg, w_sd = w_sh_gu[0].astype(BF16), w_sh_dn[0].astype(BF16)
    out = None
    for c0 in range(0, n, chunk):
        yg = _sc_gather(ys, pos[:, c0:c0 + chunk])
        out = _combine(yg, wt_t, hx, x1_2d, gt2, w_sg, w_sd, row(final_g), tm, s // tm, c0 // tm, out)
    return out.reshape(b, s, d)
```

```python
import functools

import jax
import jax.numpy as jnp
import numpy as np
from jax import lax
from jax.experimental import pallas as pl
from jax.experimental.pallas import tpu as pltpu
from jax.experimental.pallas import tpu_sc as plsc

F32 = jnp.float32
BF16 = jnp.bfloat16
U32 = jnp.uint32
I32 = jnp.int32

GRID_W = 64
N_HEADS = 8
N_KV_HEADS = 4
GROUP = N_HEADS // N_KV_HEADS
HEAD_DIM = 128
AXIS_DIM = HEAD_DIM // 2
ATTN_WIDTH = N_HEADS * HEAD_DIM
KV_WIDTH = N_KV_HEADS * HEAD_DIM
ROPE_THETA = 10000.0
ATTN_SCALE = HEAD_DIM ** -0.5
CONV_WIDTH = 512
CONV_KERNEL = 31
CONV_PAD = CONV_KERNEL // 2
N_EXPERTS = 256
TOP_K = 8
N_GROUPS = 8
TOPK_GROUPS = 4
EXPERTS_PER_GROUP = N_EXPERTS // N_GROUPS
EXPERT_HIDDEN = 256
SHARED_HIDDEN = 256
ROUTED_SCALE = 2.5
EPS = 1e-6
LOG2E = 1.4426950408889634
SAFE_EXP2_ARG = 64.0

LANES = 128
SUBLANES = 8
VMEM_LIMIT_BYTES = 56 * 1024 * 1024

HALO_ROWS = 16
CONV_ROWS = 64
MERGE_PARTS = 2
EXPERT_BLOCK = 512
EXPERT_BLOCKS_PER_STEP = 2
SC_SUBCORES = 16
SC_WORKERS = 2 * SC_SUBCORES
SC_WINDOW = 128
COMBINE_CHUNKS = 8
HIGHEST = lax.Precision.HIGHEST


def _params(sem):
    return pltpu.CompilerParams(dimension_semantics=sem, vmem_limit_bytes=VMEM_LIMIT_BYTES)


def _sigmoid(x):
    return 1.0 / (1.0 + jnp.exp(-x))


def _pack_bf16_pair(lo, hi):
    lo_b = pltpu.bitcast(lo.astype(BF16).astype(F32), U32)
    hi_b = pltpu.bitcast(hi.astype(BF16).astype(F32), U32)
    return (lo_b >> 16) | (hi_b & jnp.uint32(0xFFFF0000))


def _unpack_bf16_pair(u):
    lo = pltpu.bitcast(u << 16, F32)
    hi = pltpu.bitcast(u & jnp.uint32(0xFFFF0000), F32)
    return lo, hi


def _mod_kernel(cc_ref, w_ref, b_ref, o_ref):
    cc = cc_ref[...]
    s = cc * _sigmoid(cc)
    o_ref[...] = jnp.dot(s, w_ref[...], precision=HIGHEST, preferred_element_type=F32) + b_ref[...]


def _modulation(cc, w_mod, b_mod):
    d, n = w_mod.shape
    tn = n // 4
    return pl.pallas_call(
        _mod_kernel,
        grid=(n // tn,),
        in_specs=[pl.BlockSpec((SUBLANES, d), lambda j: (0, 0)),
                  pl.BlockSpec((d, tn), lambda j: (0, j)),
                  pl.BlockSpec((1, tn), lambda j: (0, j))],
        out_specs=pl.BlockSpec((SUBLANES, tn), lambda j: (0, j)),
        out_shape=jax.ShapeDtypeStruct((SUBLANES, n), F32),
        compiler_params=_params(("arbitrary",)),
        name="mod",
    )(cc, w_mod, b_mod)


def _norm_modulate(x, g, sh, sc):
    ms = jnp.mean(x * x, axis=-1, keepdims=True)
    return (x * lax.rsqrt(ms + EPS) * g) * (1.0 + sc) + sh


def _head_norm(p, gain):
    r = lax.rsqrt(jnp.mean(p * p, axis=-1, keepdims=True) + EPS)
    return p * r * gain


def _dot_nt(a, b):
    return lax.dot_general(a, b, (((1,), (1,)), ((), ())), preferred_element_type=F32)


def _ctx_kv_kernel(x_ref, sh_ref, sc_ref, g1_ref, wk_ref, wvt_ref, gk_ref, k_ref, vt_ref):
    h = _norm_modulate(x_ref[0], g1_ref[...], sh_ref[...], sc_ref[...]).astype(BF16)
    pk = jnp.dot(h, wk_ref[...], preferred_element_type=F32)
    for j in range(N_KV_HEADS):
        sl = slice(j * HEAD_DIM, (j + 1) * HEAD_DIM)
        k_ref[0, :, sl] = _head_norm(pk[:, sl], gk_ref[...]).astype(BF16)
    vt_ref[0] = _dot_nt(wvt_ref[...], h).astype(BF16)


def _ctx_kv(ctx, csh, csc, g1, w_k, w_vt, gk):
    b, lc, d = ctx.shape
    vec = lambda: pl.BlockSpec((1, d), lambda i: (0, 0))
    return pl.pallas_call(
        _ctx_kv_kernel,
        grid=(b,),
        in_specs=[pl.BlockSpec((1, lc, d), lambda i: (i, 0, 0)), vec(), vec(), vec(),
                  pl.BlockSpec((d, KV_WIDTH), lambda i: (0, 0)),
                  pl.BlockSpec((KV_WIDTH, d), lambda i: (0, 0)),
                  pl.BlockSpec((1, HEAD_DIM), lambda i: (0, 0))],
        out_specs=[pl.BlockSpec((1, lc, KV_WIDTH), lambda i: (i, 0, 0)),
                   pl.BlockSpec((1, KV_WIDTH, lc), lambda i: (i, 0, 0))],
        out_shape=[jax.ShapeDtypeStruct((b, lc, KV_WIDTH), BF16),
                   jax.ShapeDtypeStruct((b, KV_WIDTH, lc), BF16)],
        compiler_params=_params(("arbitrary",)),
        name="ctx_kv",
    )(ctx, csh, csc, g1, w_k, w_vt, gk)


def _in_proj_kernel(x_ref, sh_ref, sc_ref, g1_ref, w_ref, wvt_ref, gq_ref, gk_ref, cos_ref, sin_ref,
                    q_ref, k_ref, vt_ref, h_ref, g_ref):
    h = _norm_modulate(x_ref[0], g1_ref[...], sh_ref[0], sc_ref[0]).astype(BF16)
    cos = cos_ref[...]
    sin = sin_ref[...]
    lane = lax.broadcasted_iota(I32, cos.shape, 1)
    upper = (lane & (AXIS_DIM // 2)) != 0

    def rope(p):
        swapped = jnp.where(upper, pltpu.roll(p, AXIS_DIM // 2, 1),
                            pltpu.roll(p, HEAD_DIM - AXIS_DIM // 2, 1))
        return p * cos + swapped * sin

    q_end = ATTN_WIDTH
    k_end = q_end + KV_WIDTH
    v_end = k_end + KV_WIDTH
    u_end = v_end + 2 * CONV_WIDTH
    pq = jnp.dot(h, w_ref[:, :q_end], preferred_element_type=F32)
    for j in range(N_HEADS):
        sl = slice(j * HEAD_DIM, (j + 1) * HEAD_DIM)
        q_ref[0, :, sl] = (rope(_head_norm(pq[:, sl], gq_ref[...])) * (ATTN_SCALE * LOG2E)).astype(BF16)
    pk = jnp.dot(h, w_ref[:, q_end:k_end], preferred_element_type=F32)
    for j in range(N_KV_HEADS):
        sl = slice(j * HEAD_DIM, (j + 1) * HEAD_DIM)
        k_ref[0, :, sl] = rope(_head_norm(pk[:, sl], gk_ref[...])).astype(BF16)
    vt_ref[0] = _dot_nt(wvt_ref[...], h).astype(BF16)
    u = jnp.dot(h, w_ref[:, v_end:u_end], preferred_element_type=F32)
    h_ref[0] = (u[:, :CONV_WIDTH] * _sigmoid(u[:, CONV_WIDTH:])).astype(BF16)
    g_ref[0] = _sigmoid(jnp.dot(h, w_ref[:, u_end:], preferred_element_type=F32)).astype(BF16)


def _in_proj(x, sh1, sc1, g1, w_in, w_vt, gq, gk, cos_t, sin_t, tm):
    b, s, d = x.shape
    n_in = w_in.shape[1]
    bvec = lambda: pl.BlockSpec((1, 1, d), lambda bi, i: (bi, 0, 0))
    tok = lambda w: pl.BlockSpec((1, tm, w), lambda bi, i: (bi, i, 0))
    return pl.pallas_call(
        _in_proj_kernel,
        grid=(b, s // tm),
        in_specs=[tok(d), bvec(), bvec(),
                  pl.BlockSpec((1, d), lambda bi, i: (0, 0)),
                  pl.BlockSpec((d, n_in), lambda bi, i: (0, 0)),
                  pl.BlockSpec((KV_WIDTH, d), lambda bi, i: (0, 0)),
                  pl.BlockSpec((1, HEAD_DIM), lambda bi, i: (0, 0)),
                  pl.BlockSpec((1, HEAD_DIM), lambda bi, i: (0, 0)),
                  pl.BlockSpec((tm, HEAD_DIM), lambda bi, i: (i, 0)),
                  pl.BlockSpec((tm, HEAD_DIM), lambda bi, i: (i, 0))],
        out_specs=[tok(ATTN_WIDTH), tok(KV_WIDTH),
                   pl.BlockSpec((1, KV_WIDTH, tm), lambda bi, i: (bi, 0, i)),
                   tok(CONV_WIDTH), tok(2 * d)],
        out_shape=[jax.ShapeDtypeStruct((b, s, ATTN_WIDTH), BF16),
                   jax.ShapeDtypeStruct((b, s, KV_WIDTH), BF16),
                   jax.ShapeDtypeStruct((b, KV_WIDTH, s), BF16),
                   jax.ShapeDtypeStruct((b, s, CONV_WIDTH), BF16),
                   jax.ShapeDtypeStruct((b, s, 2 * d), BF16)],
        compiler_params=_params(("arbitrary", "arbitrary")),
        name="in_proj",
    )(x, sh1, sc1, g1, w_in, w_vt, gq, gk, cos_t, sin_t)


def _sub_allreduce(x, op):
    for s in (4, 2, 1):
        x = op(x, pltpu.roll(x, s, 0))
    return x


def _attn_kernel(bounded_ref, q_ref, kc_ref, vct_ref, k_ref, vt_ref, hp_ref, hc_ref, hn_ref, wdw_ref, bdw_ref,
                 o_ref, c_ref, s0, s1, x0, x1, m_sc, l_sc, acc_sc, hcat, shift_sc, *, tk):
    tq = q_ref.shape[1]
    m_cols = GROUP * tq
    nk = k_ref.shape[1] // tk
    qf = q_ref[0].astype(F32).T
    qt = jnp.concatenate([qf[:HEAD_DIM], qf[HEAD_DIM:]], axis=1).astype(BF16)
    slots = ((s0, x0), (s1, x1))
    n_conv = max(nk // 2, 1)
    conv_rows = tq // n_conv
    conv_base = HALO_ROWS - CONV_PAD
    conv_reach = (conv_base + CONV_KERNEL - 1) // SUBLANES * SUBLANES

    def conv_fill():
        ti = pl.program_id(2)
        prev = hp_ref[0].astype(F32)
        nxt = hn_ref[0].astype(F32)
        hcat[0:HALO_ROWS, :] = jnp.where(ti > 0, prev, jnp.zeros_like(prev))
        hcat[HALO_ROWS:HALO_ROWS + tq, :] = hc_ref[0].astype(F32)
        hcat[HALO_ROWS + tq:, :] = jnp.where(ti < pl.num_programs(2) - 1, nxt, jnp.zeros_like(nxt))

    def conv_block(blk):
        r0 = blk * conv_rows
        for sub in range(0, conv_rows, CONV_ROWS):
            start = r0 + sub if isinstance(r0, int) else pl.multiple_of(r0 + sub, CONV_ROWS)
            window = hcat[pl.ds(start, CONV_ROWS + 2 * HALO_ROWS), :]
            acc = jnp.zeros((CONV_ROWS, HEAD_DIM), F32) + bdw_ref[...]
            for res in range(SUBLANES):
                shift_sc[res] = window[res:res + CONV_ROWS + conv_reach, :]
                for off in range(res, conv_base + CONV_KERNEL, SUBLANES):
                    j = off - conv_base
                    if 0 <= j < CONV_KERNEL:
                        a0 = off - res
                        acc = acc + shift_sc[res, a0:a0 + CONV_ROWS, :] * wdw_ref[j:j + 1, :]
            c_ref[0, pl.ds(start, CONV_ROWS), :] = acc

    def kchunk(j):
        return k_ref[0, pl.ds(pl.multiple_of(j * tk, tk), tk), :]

    def vchunk(j):
        return vt_ref[0, :, pl.ds(pl.multiple_of(j * tk, tk), tk)]

    def split(st):
        return st.reshape(st.shape[0] // SUBLANES, SUBLANES, m_cols)

    def scores(k, online):
        st = jnp.dot(k, qt, preferred_element_type=F32)
        return st, (jnp.max(split(st), axis=0) if online else None)

    def absorb(st, mx, vt, online):
        s3 = split(st)
        if online:
            m_prev = m_sc[...]
            m_new = jnp.maximum(m_prev, _sub_allreduce(mx, jnp.maximum))
            alpha = jnp.exp2(m_prev - m_new)
            p3 = jnp.exp2(s3 - m_new[None])
            l_sc[...] = alpha * l_sc[...] + _sub_allreduce(jnp.sum(p3, axis=0), jnp.add)
            pv = jnp.dot(vt, p3.reshape(st.shape).astype(BF16), preferred_element_type=F32)
            acc_sc[...] = alpha[0:1] * acc_sc[...] + pv
            m_sc[...] = m_new
        else:
            p3 = jnp.exp2(s3)
            l_sc[...] = l_sc[...] + jnp.sum(p3, axis=0)
            acc_sc[...] = acc_sc[...] + jnp.dot(vt, p3.reshape(st.shape).astype(BF16),
                                                preferred_element_type=F32)

    def stage(slot, k, online):
        st, mx = scores(k, online)
        slots[slot][0][...] = st
        if online:
            slots[slot][1][...] = mx

    def take(slot, vt, online):
        absorb(slots[slot][0][...], slots[slot][1][...] if online else None, vt, online)

    def sweep(online):
        if online:
            m_sc[...] = jnp.full(m_sc.shape, -jnp.inf, F32)
        l_sc[...] = jnp.zeros(l_sc.shape, F32)
        acc_sc[...] = jnp.zeros(acc_sc.shape, F32)
        conv_fill()
        stage(0, kchunk(0), online)

        def body(i, carry):
            j = 2 * i
            stage(1, kchunk(j + 1), online)
            take(0, vchunk(j), online)
            conv_block(i)
            stage(0, kchunk(j + 2), online)
            take(1, vchunk(j + 1), online)
            return carry

        lax.fori_loop(0, nk // 2 - 1, body, 0)
        conv_block(n_conv - 1)
        stage(1, kchunk(nk - 1), online)
        take(0, vchunk(nk - 2), online)
        sc, xc = scores(kc_ref[0], online)
        take(1, vchunk(nk - 1), online)
        absorb(sc, xc, vct_ref[0], online)
        denom = l_sc[...] if online else _sub_allreduce(l_sc[...], jnp.add)
        o = (acc_sc[...] / denom[0:1]).T
        o_ref[0, :, :HEAD_DIM] = o[:tq].astype(BF16)
        o_ref[0, :, HEAD_DIM:] = o[tq:].astype(BF16)

    @pl.when(bounded_ref[0] != 0)
    def _():
        sweep(online=False)

    @pl.when(bounded_ref[0] == 0)
    def _():
        sweep(online=True)


def _attention(bounded, q, kc, vct, kx, vxt, hglu, w_dw, b_dw, tq, tk):
    b, s, _ = q.shape
    lc = kc.shape[1]
    assert s % (2 * tk) == 0 and CONV_WIDTH == N_KV_HEADS * HEAD_DIM
    gw = GROUP * HEAD_DIM
    m_cols = GROUP * tq
    hb = tq // HALO_ROWS
    n_halo = s // HALO_ROWS
    reach = (HALO_ROWS - CONV_PAD + CONV_KERNEL - 1) // SUBLANES * SUBLANES
    kv = lambda l: pl.BlockSpec((1, l, HEAD_DIM), lambda bi, h, i, bd: (bi, 0, h))
    kvt = lambda l: pl.BlockSpec((1, HEAD_DIM, l), lambda bi, h, i, bd: (bi, h, 0))
    qo = lambda: pl.BlockSpec((1, tq, gw), lambda bi, h, i, bd: (bi, i, h))
    chan = lambda rows: pl.BlockSpec((rows, HEAD_DIM), lambda bi, h, i, bd: (0, h))
    return pl.pallas_call(
        functools.partial(_attn_kernel, tk=tk),
        grid_spec=pltpu.PrefetchScalarGridSpec(
            num_scalar_prefetch=1,
            grid=(b, N_KV_HEADS, s // tq),
            in_specs=[qo(), kv(lc), kvt(lc), kv(s), kvt(s),
                      pl.BlockSpec((1, HALO_ROWS, HEAD_DIM),
                                   lambda bi, h, i, bd: (bi, jnp.maximum(i * hb - 1, 0), h)),
                      pl.BlockSpec((1, tq, HEAD_DIM), lambda bi, h, i, bd: (bi, i, h)),
                      pl.BlockSpec((1, HALO_ROWS, HEAD_DIM),
                                   lambda bi, h, i, bd: (bi, jnp.minimum((i + 1) * hb, n_halo - 1), h)),
                      chan(CONV_KERNEL), chan(1)],
            out_specs=[qo(), pl.BlockSpec((1, tq, HEAD_DIM), lambda bi, h, i, bd: (bi, i, h))],
            scratch_shapes=[pltpu.VMEM((tk, m_cols), F32), pltpu.VMEM((tk, m_cols), F32),
                            pltpu.VMEM((SUBLANES, m_cols), F32), pltpu.VMEM((SUBLANES, m_cols), F32),
                            pltpu.VMEM((SUBLANES, m_cols), F32), pltpu.VMEM((SUBLANES, m_cols), F32),
                            pltpu.VMEM((HEAD_DIM, m_cols), F32),
                            pltpu.VMEM((tq + 2 * HALO_ROWS, HEAD_DIM), F32),
                            pltpu.VMEM((SUBLANES, CONV_ROWS + reach, HEAD_DIM), F32)]),
        out_shape=[jax.ShapeDtypeStruct((b, s, ATTN_WIDTH), BF16),
                   jax.ShapeDtypeStruct((b, s, CONV_WIDTH), F32)],
        compiler_params=_params(("arbitrary", "arbitrary", "arbitrary")),
        name="attn",
    )(bounded, q, kc, vct, kx, vxt, hglu, hglu, hglu, w_dw, b_dw)


def _split3(x):
    x1 = x.astype(BF16)
    r1 = x - x1.astype(F32)
    x2 = r1.astype(BF16)
    x3 = (r1 - x2.astype(F32)).astype(BF16)
    return x1, x2, x3


def _merge_kernel(o_ref, c_ref, g_ref, x_ref, gt1_ref, sh2_ref, sc2_ref,
                  lng_ref, lnb_ref, wap_ref, wcp_ref, wout_ref, g2_ref,
                  x1_ref, hx_ref, hxf_ref):
    tm = x_ref.shape[1]
    d = x_ref.shape[2]
    half = d // 2
    rows_per_part = tm // MERGE_PARTS
    for part in range(MERGE_PARTS):
        p0 = part * rows_per_part
        rows = slice(p0, p0 + rows_per_part)
        conv = c_ref[0, rows, :]
        mu = jnp.mean(conv, axis=-1, keepdims=True)
        cen = conv - mu
        var = jnp.mean(cen * cen, axis=-1, keepdims=True)
        ln = cen * lax.rsqrt(var + EPS) * lng_ref[...] + lnb_ref[...]
        act = (ln * _sigmoid(ln)).astype(BF16)
        y_conv = jnp.dot(act, wcp_ref[...], preferred_element_type=F32)
        y_attn = jnp.dot(o_ref[0, rows, :], wap_ref[...], preferred_element_type=F32)
        z = g_ref[0, rows, :d].astype(F32) * y_attn + g_ref[0, rows, d:].astype(F32) * y_conv
        mix = jnp.dot(z.astype(BF16), wout_ref[...], preferred_element_type=F32)
        x1 = x_ref[0, rows, :] + gt1_ref[0] * mix
        x1_ref[0, rows, :] = x1
        hx = _norm_modulate(x1, g2_ref[...], sh2_ref[0], sc2_ref[0])
        hx_ref[rows, :] = pltpu.bitcast(_pack_bf16_pair(hx[:, :half], hx[:, half:]), I32)
        h1, h2, h3 = _split3(hx)
        hxf_ref[0, rows, :] = h1
        hxf_ref[1, rows, :] = h2
        hxf_ref[2, rows, :] = h3


def _merge(o, conv, g, x, gt1, sh2, sc2, ln_g, ln_b, w_ap, w_cp, w_out, g2, tm):
    b, s, d = x.shape
    nt = s // tm
    bvec = lambda: pl.BlockSpec((1, 1, d), lambda bi, i: (bi, 0, 0))
    full = lambda a: pl.BlockSpec(a.shape, lambda bi, i: (0,) * a.ndim)
    tok = lambda w: pl.BlockSpec((1, tm, w), lambda bi, i: (bi, i, 0))
    return pl.pallas_call(
        _merge_kernel,
        grid=(b, nt),
        in_specs=[tok(ATTN_WIDTH), tok(CONV_WIDTH), tok(2 * d), tok(d), bvec(), bvec(), bvec(),
                  full(ln_g), full(ln_b), full(w_ap), full(w_cp), full(w_out), full(g2)],
        out_specs=[tok(d),
                   pl.BlockSpec((tm, d // 2), lambda bi, i: (bi * nt + i, 0)),
                   pl.BlockSpec((3, tm, d), lambda bi, i: (0, bi * nt + i, 0))],
        out_shape=[jax.ShapeDtypeStruct((b, s, d), F32),
                   jax.ShapeDtypeStruct((b * s, d // 2), I32),
                   jax.ShapeDtypeStruct((3, b * s, d), BF16)],
        compiler_params=_params(("arbitrary", "arbitrary")),
        name="merge",
    )(o, conv, g, x, gt1, sh2, sc2, ln_g, ln_b, w_ap, w_cp, w_out, g2)


def _router_logits(w_ref, h_ref):
    h1, h2, h3 = h_ref[0], h_ref[1], h_ref[2]
    w1, w2, w3 = w_ref[0], w_ref[1], w_ref[2]
    return ((((_dot_nt(w3, h1) + _dot_nt(w1, h3)) + _dot_nt(w2, h2))
             + (_dot_nt(w2, h1) + _dot_nt(w1, h2))) + _dot_nt(w1, h1))


def _route_kernel(hx0_ref, hxa_ref, hxb_ref, wrt_ref, bias_ref, tri_ref, e_ref, w_ref, r_ref, cnt_ref,
                  lg_a, lg_b, run_sc):
    step = pl.program_id(0)
    tb = hxa_ref.shape[1]
    n_strips = tb // LANES
    nv = N_EXPERTS // SUBLANES
    gv = EXPERTS_PER_GROUP // SUBLANES

    @pl.when(step == 0)
    def _():
        run_sc[...] = jnp.zeros(run_sc.shape, F32)
        lg_a[...] = _router_logits(wrt_ref, hx0_ref)

    row = (lax.broadcasted_iota(I32, (nv, SUBLANES, LANES), 0) * SUBLANES
           + lax.broadcasted_iota(I32, (nv, SUBLANES, LANES), 1))
    sub = lax.broadcasted_iota(I32, (SUBLANES, LANES), 0)
    bias = bias_ref[...].reshape(nv, SUBLANES, LANES)
    neg_inf = jnp.float32(-jnp.inf)

    def topk_strip(lg, st, col0):
        lanes = slice(col0 + st * LANES, col0 + (st + 1) * LANES)
        scores = _sigmoid(lg[:, st * LANES:(st + 1) * LANES]).reshape(nv, SUBLANES, LANES)
        biased = scores + bias
        gscore = []
        for g in range(N_GROUPS):
            m1 = biased[g * gv]
            m2 = jnp.full((SUBLANES, LANES), neg_inf, F32)
            for t in range(1, gv):
                v = biased[g * gv + t]
                m2 = jnp.maximum(m2, jnp.minimum(m1, v))
                m1 = jnp.maximum(m1, v)
            for s in (4, 2, 1):
                p1 = pltpu.roll(m1, s, 0)
                p2 = pltpu.roll(m2, s, 0)
                m2 = jnp.maximum(jnp.minimum(m1, p1), jnp.maximum(m2, p2))
                m1 = jnp.maximum(m1, p1)
            gscore.append(m1 + m2)
        masked = []
        for g in range(N_GROUPS):
            beaten = jnp.zeros((SUBLANES, LANES), I32)
            for o in range(N_GROUPS):
                if o == g:
                    continue
                wins = (gscore[o] > gscore[g]) | ((gscore[o] == gscore[g]) & (o < g))
                beaten = beaten + wins.astype(I32)
            keep = beaten < TOPK_GROUPS
            for t in range(gv):
                masked.append(jnp.where(keep, biased[g * gv + t], neg_inf))
        cand = jnp.stack(masked, axis=0)
        sel = jnp.zeros((nv, SUBLANES, LANES), jnp.bool_)
        picks, pick_scores = [], []
        for _ in range(TOP_K):
            mx = _sub_allreduce(jnp.max(cand, axis=0), jnp.maximum)
            idx = _sub_allreduce(jnp.min(jnp.where(cand == mx, row, N_EXPERTS), axis=0), jnp.minimum)
            hit = row == idx
            pick_scores.append(_sub_allreduce(jnp.sum(jnp.where(hit, scores, 0.0), axis=0), jnp.add))
            picks.append(idx)
            sel = sel | hit
            cand = jnp.where(hit, neg_inf, cand)
        sel_b = sel.astype(F32).astype(BF16).reshape(N_EXPERTS, LANES)
        before = jnp.dot(sel_b, tri_ref[0], preferred_element_type=F32)
        total = jnp.dot(sel_b, tri_ref[1], preferred_element_type=F32)
        rank_all = (before + run_sc[...]).reshape(nv, SUBLANES, LANES)
        run_sc[...] = run_sc[...] + total
        denom = pick_scores[0]
        for kk in range(1, TOP_K):
            denom = denom + pick_scores[kk]
        e_out = jnp.zeros((SUBLANES, LANES), I32)
        w_out = jnp.zeros((SUBLANES, LANES), F32)
        r_out = jnp.zeros((SUBLANES, LANES), I32)
        for kk in range(TOP_K):
            rk = _sub_allreduce(jnp.sum(jnp.where(row == picks[kk], rank_all, 0.0), axis=0), jnp.add)
            e_out = jnp.where(sub == kk, picks[kk], e_out)
            w_out = jnp.where(sub == kk, pick_scores[kk] / denom * ROUTED_SCALE, w_out)
            r_out = jnp.where(sub == kk, rk.astype(I32), r_out)
        e_ref[:, lanes] = e_out
        w_ref[:, lanes] = w_out
        r_ref[:, lanes] = r_out

    lg_b[...] = _router_logits(wrt_ref, hxa_ref)
    for st in range(n_strips):
        topk_strip(lg_a, st, 0)
    lg_a[...] = _router_logits(wrt_ref, hxb_ref)
    for st in range(n_strips):
        topk_strip(lg_b, st, tb)
    cnt_ref[...] = run_sc[...].astype(I32)


def _route(hxf, w_rt3, bias, tb):
    _, n, d = hxf.shape
    nt = n // tb
    assert nt % 2 == 0
    iota_r = lax.broadcasted_iota(I32, (LANES, LANES), 0)
    iota_c = lax.broadcasted_iota(I32, (LANES, LANES), 1)
    tri = jnp.stack([(iota_r < iota_c), jnp.ones((LANES, LANES), jnp.bool_)]).astype(BF16)
    bias_b = jnp.broadcast_to(bias.reshape(N_EXPERTS, 1), (N_EXPERTS, LANES)).astype(F32)
    tokrow = lambda dt: jax.ShapeDtypeStruct((TOP_K, n), dt)
    out_blk = lambda: pl.BlockSpec((TOP_K, 2 * tb), lambda i: (0, i))
    return pl.pallas_call(
        _route_kernel,
        grid=(nt // 2,),
        in_specs=[pl.BlockSpec((3, tb, d), lambda i: (0, 0, 0)),
                  pl.BlockSpec((3, tb, d), lambda i: (0, 2 * i + 1, 0)),
                  pl.BlockSpec((3, tb, d), lambda i: (0, jnp.minimum(2 * i + 2, nt - 1), 0)),
                  pl.BlockSpec(w_rt3.shape, lambda i: (0, 0, 0)),
                  pl.BlockSpec((N_EXPERTS, LANES), lambda i: (0, 0)),
                  pl.BlockSpec((2, LANES, LANES), lambda i: (0, 0, 0))],
        out_specs=[out_blk(), out_blk(), out_blk(),
                   pl.BlockSpec((N_EXPERTS, LANES), lambda i: (0, 0))],
        out_shape=[tokrow(I32), tokrow(F32), tokrow(I32),
                   jax.ShapeDtypeStruct((N_EXPERTS, LANES), I32)],
        scratch_shapes=[pltpu.VMEM((N_EXPERTS, tb), F32), pltpu.VMEM((N_EXPERTS, tb), F32),
                        pltpu.VMEM((N_EXPERTS, LANES), F32)],
        compiler_params=_params(("arbitrary",)),
        name="route",
    )(hxf, hxf, hxf, w_rt3, bias_b, tri)


def _slots_kernel(pstart_ref, e_ref, r_ref, o_ref):
    e = e_ref[...]

    def body(x, acc):
        return acc + jnp.where(e == x, pstart_ref[x], 0)

    o_ref[...] = lax.fori_loop(0, N_EXPERTS, body, r_ref[...])


def _slots(pstart, e_idx, rank, tb):
    n = e_idx.shape[1]
    blk = lambda: pl.BlockSpec((TOP_K, tb), lambda i, ps: (0, i))
    return pl.pallas_call(
        _slots_kernel,
        grid_spec=pltpu.PrefetchScalarGridSpec(
            num_scalar_prefetch=1, grid=(n // tb,), in_specs=[blk(), blk()], out_specs=blk()),
        out_shape=jax.ShapeDtypeStruct((TOP_K, n), I32),
        compiler_params=_params(("arbitrary",)),
        name="slots",
    )(pstart, e_idx, rank)


def _sc_mesh():
    return plsc.VectorSubcoreMesh(core_axis_name="c", subcore_axis_name="s")


def _sc_worker():
    return lax.axis_index("c") * SC_SUBCORES + lax.axis_index("s")


def _sc_dispatch(hx, pos, cap):
    n, w = hx.shape
    per_worker = n // SC_WINDOW // SC_WORKERS
    assert per_worker * SC_WINDOW * SC_WORKERS == n

    @pl.kernel(out_type=jax.ShapeDtypeStruct((cap, w), hx.dtype), mesh=_sc_mesh(),
               scratch_types=[pltpu.VMEM((SC_WINDOW, w), hx.dtype), pltpu.VMEM((TOP_K, SC_WINDOW), I32),
                              pltpu.SemaphoreType.DMA])
    def scatter_rows(x_hbm, i_hbm, o_hbm, xbuf, ibuf, sem):
        wid = _sc_worker()

        @pl.loop(0, per_worker)
        def _(j):
            row0 = (wid * per_worker + j) * SC_WINDOW
            pltpu.sync_copy(x_hbm.at[pl.ds(row0, SC_WINDOW)], xbuf)
            pltpu.sync_copy(i_hbm.at[:, pl.ds(row0, SC_WINDOW)], ibuf)
            copies = [pltpu.async_copy(xbuf, o_hbm.at[ibuf.at[kk]], sem) for kk in range(TOP_K)]
            for cp in copies:
                cp.wait()

    return scatter_rows(hx, pos)


def _sc_gather(ys, pos):
    n = pos.shape[1]
    w = ys.shape[1]
    per_worker = n // SC_WINDOW // SC_WORKERS
    assert per_worker * SC_WINDOW * SC_WORKERS == n

    @pl.kernel(out_type=jax.ShapeDtypeStruct((TOP_K, n, w), ys.dtype), mesh=_sc_mesh(),
               scratch_types=[pltpu.VMEM((SC_WINDOW, w), ys.dtype), pltpu.VMEM((TOP_K, SC_WINDOW), I32)])
    def gather_rows(y_hbm, i_hbm, o_hbm, ybuf, ibuf):
        wid = _sc_worker()

        @pl.loop(0, per_worker)
        def _(j):
            row0 = (wid * per_worker + j) * SC_WINDOW
            pltpu.sync_copy(i_hbm.at[:, pl.ds(row0, SC_WINDOW)], ibuf)
            for kk in range(TOP_K):
                pltpu.sync_copy(y_hbm.at[ibuf.at[kk]], ybuf)
                pltpu.sync_copy(ybuf, o_hbm.at[kk, pl.ds(row0, SC_WINDOW)])

    return gather_rows(ys, pos)


def _experts_kernel(blk_e_ref, valid_ref, blk_in_ref, blk_out_ref, next_e_ref, slot_ref,
                    xs_ref, wgu_hbm, wdn_hbm, ys_ref, gu_buf, dn_buf, sem, wgu_sc, wdn_sc):
    del blk_in_ref, blk_out_ref

    def weight_copies(e, slot):
        return (pltpu.make_async_copy(wgu_hbm.at[e], gu_buf.at[slot], sem.at[0, slot]),
                pltpu.make_async_copy(wdn_hbm.at[e], dn_buf.at[slot], sem.at[1, slot]))

    def one_block(i, rsl):
        valid = valid_ref[i]

        @pl.when(valid > 0)
        def _():
            e = blk_e_ref[i]
            slot = slot_ref[i]
            prev = blk_e_ref[jnp.maximum(i - 1, 0)]

            @pl.when(i == 0)
            def _():
                for cp in weight_copies(e, slot):
                    cp.start()

            @pl.when((i == 0) | (e != prev))
            def _():
                for cp in weight_copies(e, slot):
                    cp.wait()
                nxt = next_e_ref[i]

                @pl.when(nxt >= 0)
                def _():
                    for cp in weight_copies(nxt, 1 - slot):
                        cp.start()

                wgu_sc[...] = gu_buf[slot].astype(BF16)
                wdn_sc[...] = dn_buf[slot].astype(BF16)

            rows = lax.broadcasted_iota(I32, (EXPERT_BLOCK, xs_ref.shape[1]), 0)
            xu = jnp.where(rows < valid, pltpu.bitcast(xs_ref[rsl, :], U32), jnp.uint32(0))
            lo, hi = _unpack_bf16_pair(xu)
            half = lo.shape[1]
            gu = (jnp.dot(lo.astype(BF16), wgu_sc[:half, :], preferred_element_type=F32)
                  + jnp.dot(hi.astype(BF16), wgu_sc[half:, :], preferred_element_type=F32))
            gt = gu[:, :EXPERT_HIDDEN]
            act = (gt * _sigmoid(gt) * gu[:, EXPERT_HIDDEN:]).astype(BF16)
            y = jnp.dot(act, wdn_sc[...], preferred_element_type=F32)
            ys_ref[rsl, :] = pltpu.bitcast(_pack_bf16_pair(y[:, :half], y[:, half:]), I32)

        @pl.when(valid <= 0)
        def _():
            ys_ref[rsl, :] = jnp.zeros((EXPERT_BLOCK, ys_ref.shape[1]), I32)

    step = pl.program_id(0)
    for sub in range(EXPERT_BLOCKS_PER_STEP):
        one_block(step * EXPERT_BLOCKS_PER_STEP + sub, slice(sub * EXPERT_BLOCK, (sub + 1) * EXPERT_BLOCK))


def _experts(blk_e, blk_valid, next_expert, expert_slot, xs, w_gu, w_dn):
    cap, w = xs.shape
    n_e, d, h2 = w_gu.shape
    n_blocks = cap // EXPERT_BLOCK
    assert n_blocks % EXPERT_BLOCKS_PER_STEP == 0
    n_steps = n_blocks // EXPERT_BLOCKS_PER_STEP
    step_rows = EXPERT_BLOCKS_PER_STEP * EXPERT_BLOCK
    blk_next = next_expert[blk_e].astype(I32)
    blk_slot = expert_slot[blk_e].astype(I32)
    step = jnp.arange(n_steps, dtype=I32)
    step_used = blk_valid[::EXPERT_BLOCKS_PER_STEP] > 0
    n_used = jnp.sum(step_used.astype(I32))
    blk_in = jnp.minimum(step, jnp.maximum(n_used - 1, 0)).astype(I32)
    blk_out = jnp.where(step_used, step, n_steps).astype(I32)
    return pl.pallas_call(
        _experts_kernel,
        grid_spec=pltpu.PrefetchScalarGridSpec(
            num_scalar_prefetch=6,
            grid=(n_steps,),
            in_specs=[pl.BlockSpec((step_rows, w), lambda i, be, bv, bi, bo, bn, bs: (bi[i], 0)),
                      pl.BlockSpec(memory_space=pl.ANY),
                      pl.BlockSpec(memory_space=pl.ANY)],
            out_specs=pl.BlockSpec((step_rows, w), lambda i, be, bv, bi, bo, bn, bs: (bo[i], 0)),
            scratch_shapes=[pltpu.VMEM((2, d, h2), F32), pltpu.VMEM((2, h2 // 2, d), F32),
                            pltpu.SemaphoreType.DMA((2, 2)),
                            pltpu.VMEM((d, h2), BF16), pltpu.VMEM((h2 // 2, d), BF16)]),
        out_shape=jax.ShapeDtypeStruct((cap + step_rows, w), I32),
        compiler_params=_params(("arbitrary",)),
        name="experts",
    )(blk_e, blk_valid, blk_in, blk_out, blk_next, blk_slot, xs, w_gu, w_dn)


def _combine_kernel(yg_ref, wt_ref, hx_ref, x1_ref, gt2_ref, wsg_ref, wsd_ref, fg_ref, *rest):
    o_ref = rest[-1]
    lo, hi = _unpack_bf16_pair(pltpu.bitcast(hx_ref[...], U32))
    half = lo.shape[1]
    gu = (jnp.dot(lo.astype(BF16), wsg_ref[:half, :], preferred_element_type=F32)
          + jnp.dot(hi.astype(BF16), wsg_ref[half:, :], preferred_element_type=F32))
    gt = gu[:, :SHARED_HIDDEN]
    act = (gt * _sigmoid(gt) * gu[:, SHARED_HIDDEN:]).astype(BF16)
    y = jnp.dot(act, wsd_ref[...], preferred_element_type=F32)
    y_lo = y[:, :half]
    y_hi = y[:, half:]
    for kk in range(TOP_K):
        r_lo, r_hi = _unpack_bf16_pair(pltpu.bitcast(yg_ref[kk], U32))
        wk = wt_ref[:, kk:kk + 1]
        y_lo = y_lo + wk * r_lo
        y_hi = y_hi + wk * r_hi
    x2_lo = x1_ref[:, :half] + gt2_ref[0, :, :half] * y_lo
    x2_hi = x1_ref[:, half:] + gt2_ref[0, :, half:] * y_hi
    ms = (jnp.sum(x2_lo * x2_lo, axis=-1, keepdims=True)
          + jnp.sum(x2_hi * x2_hi, axis=-1, keepdims=True)) / (2 * half)
    inv = lax.rsqrt(ms + EPS)
    o_ref[:, :half] = x2_lo * inv * fg_ref[:, :half]
    o_ref[:, half:] = x2_hi * inv * fg_ref[:, half:]


def _combine(yg, wt, hx, x1, gt2, w_sg, w_sd, fg, tm, tiles_per_batch, tile0, out_prev):
    n, w = hx.shape
    d = 2 * w
    full = lambda a: pl.BlockSpec(a.shape, lambda i: (0,) * a.ndim)
    in_specs = [pl.BlockSpec((TOP_K, tm, w), lambda i: (0, i, 0)),
                pl.BlockSpec((tm, TOP_K), lambda i: (tile0 + i, 0)),
                pl.BlockSpec((tm, w), lambda i: (tile0 + i, 0)),
                pl.BlockSpec((tm, d), lambda i: (tile0 + i, 0)),
                pl.BlockSpec((1, 1, d), lambda i: ((tile0 + i) // tiles_per_batch, 0, 0)),
                full(w_sg), full(w_sd), full(fg)]
    args = [yg, wt, hx, x1, gt2, w_sg, w_sd, fg]
    aliases = {}
    if out_prev is not None:
        in_specs.append(pl.BlockSpec(memory_space=pl.ANY))
        args.append(out_prev)
        aliases = {len(args) - 1: 0}
    return pl.pallas_call(
        _combine_kernel,
        grid=(yg.shape[1] // tm,),
        in_specs=in_specs,
        out_specs=pl.BlockSpec((tm, d), lambda i: (tile0 + i, 0)),
        out_shape=jax.ShapeDtypeStruct((n, d), F32),
        input_output_aliases=aliases,
        compiler_params=_params(("arbitrary",)),
        name="combine",
    )(*args)


def _rope_tables(seq):
    rows = seq // GRID_W
    pos_row = np.repeat(np.arange(rows, dtype=np.float32), GRID_W)
    pos_col = np.tile(np.arange(GRID_W, dtype=np.float32), rows)
    inv_freq = (ROPE_THETA ** (-np.arange(0, AXIS_DIM, 2, dtype=np.float32) / AXIS_DIM)).astype(np.float32)
    ar = pos_row[:, None] * inv_freq
    ac = pos_col[:, None] * inv_freq
    cos_t = np.concatenate([np.cos(ar), np.cos(ar), np.cos(ac), np.cos(ac)], axis=1)
    sin_t = np.concatenate([-np.sin(ar), np.sin(ar), -np.sin(ac), np.sin(ac)], axis=1)
    return jnp.asarray(cos_t, F32), jnp.asarray(sin_t, F32)


def _tile(n, want):
    t = min(n, want)
    assert n % t == 0, (n, want)
    return t


def kernel(x, c, ctx, c_ctx, w_mod, b_mod, norm1_g, w_in, q_norm_g, k_norm_g, w_dw, b_dw, conv_ln_g, conv_ln_b, w_attn_proj, w_conv_proj, w_out, norm2_g, w_router, router_bias, w_exp_gu, w_exp_dn, w_sh_gu, w_sh_dn, final_g):
    b, s, d = x.shape
    depth = w_mod.shape[0]
    assert depth == 1, "single-layer block"
    n = b * s
    row = lambda v: v.reshape(1, -1)

    cc = jnp.zeros((SUBLANES, d), F32).at[:b].set(c).at[b].set(c_ctx)
    mod = _modulation(cc, w_mod[0], row(b_mod[0]))
    mod_x = mod[:b].reshape(b, 1, 6, d)
    sh1, sc1, gt1, sh2, sc2, gt2 = [mod_x[:, :, j, :] for j in range(6)]
    mod_c = mod[b].reshape(6, d)
    csh1, csc1 = row(mod_c[0]), row(mod_c[1])

    w_in_b = w_in[0].astype(BF16)
    q_end, kv_end = ATTN_WIDTH, ATTN_WIDTH + 2 * KV_WIDTH
    k_end = q_end + KV_WIDTH
    w_vt = w_in_b[:, k_end:kv_end].T
    kc, vct = _ctx_kv(ctx, csh1, csc1, row(norm1_g[0]), w_in_b[:, q_end:k_end], w_vt, row(k_norm_g[0]))

    cos_t, sin_t = _rope_tables(s)
    q, kx, vxt, hglu, gates = _in_proj(x, sh1, sc1, row(norm1_g[0]), w_in_b, w_vt, row(q_norm_g[0]),
                                       row(k_norm_g[0]), cos_t, sin_t, _tile(s, 512))
    score_bound = (HEAD_DIM * ATTN_SCALE * LOG2E) * jnp.max(jnp.abs(q_norm_g[0])) * jnp.max(jnp.abs(k_norm_g[0]))
    bounded = (score_bound <= SAFE_EXP2_ARG).astype(I32).reshape(1)
    o, conv = _attention(bounded, q, kc, vct, kx, vxt, hglu, w_dw[0], row(b_dw[0]),
                         _tile(s, 1024), _tile(s // 2, 1024))

    top16 = lambda a: lax.bitcast_convert_type(lax.bitcast_convert_type(a, U32) & jnp.uint32(0xFFFF0000), F32)
    w_rt = w_router[0].T
    w_rt1 = top16(w_rt)
    w_rt2 = top16(w_rt - w_rt1)
    w_rt3 = w_rt - w_rt1 - w_rt2
    w_rt1, w_rt2, w_rt3 = w_rt1.astype(BF16), w_rt2.astype(BF16), w_rt3.astype(BF16)
    x1, hx, hxf = _merge(o, conv, gates, x, gt1, sh2, sc2, row(conv_ln_g[0]),
                         row(conv_ln_b[0]), w_attn_proj[0].astype(BF16), w_conv_proj[0].astype(BF16),
                         w_out[0].astype(BF16), row(norm2_g[0]), _tile(s, 512))

    e_idx, wts, rank, counts = _route(hxf, jnp.stack([w_rt1, w_rt2, w_rt3]), router_bias[0],
                                      _tile(n // 2, 512))

    cnt = counts[:, 0]
    padded = (cnt + EXPERT_BLOCK - 1) // EXPERT_BLOCK * EXPERT_BLOCK
    pends = jnp.cumsum(padded)
    pstart = (pends - padded).astype(I32)
    n_blocks = (n * TOP_K + N_EXPERTS * (EXPERT_BLOCK - 1)) // EXPERT_BLOCK
    n_blocks = -(-n_blocks // EXPERT_BLOCKS_PER_STEP) * EXPERT_BLOCKS_PER_STEP
    cap = n_blocks * EXPERT_BLOCK
    blk_row0 = jnp.arange(n_blocks, dtype=I32) * EXPERT_BLOCK
    blk_e = jnp.minimum(jnp.sum(pends[None, :] <= blk_row0[:, None], axis=1), N_EXPERTS - 1).astype(I32)
    blk_valid = jnp.clip(pstart[blk_e] + cnt[blk_e] - blk_row0, 0, EXPERT_BLOCK).astype(I32)

    pos = _slots(pstart, e_idx, rank, _tile(n, 4096))
    xs = _sc_dispatch(hx, pos, cap)
    has_rows = cnt > 0
    later = lax.cummin(jnp.where(has_rows, jnp.arange(N_EXPERTS, dtype=I32), N_EXPERTS), reverse=True)
    next_expert = jnp.concatenate([later[1:], jnp.full((1,), N_EXPERTS, I32)])
    next_expert = jnp.where(next_expert >= N_EXPERTS, -1, next_expert)
    expert_slot = (jnp.cumsum(has_rows.astype(I32)) - 1) % 2
    ys = _experts(blk_e, blk_valid, next_expert, expert_slot, xs, w_exp_gu[0], w_exp_dn[0])
    tm = _tile(n, 256)
    chunk = n // COMBINE_CHUNKS if n % (COMBINE_CHUNKS * SC_WINDOW * SC_WORKERS) == 0 else n
    wt_t, x1_2d = wts.T, x1.reshape(n, d)
    w_sg, w_sd = w_sh_gu[0].astype(BF16), w_sh_dn[0].astype(BF16)
    out = None
    for c0 in range(0, n, chunk):
        yg = _sc_gather(ys, pos[:, c0:c0 + chunk])
        out = _combine(yg, wt_t, hx, x1_2d, gt2, w_sg, w_sd, row(final_g), tm, s // tm, c0 // tm, out)
    return out.reshape(b, s, d)
```

```python
import functools

import jax
import jax.numpy as jnp
import numpy as np
from jax import lax
from jax.experimental import pallas as pl
from jax.experimental.pallas import tpu as pltpu
from jax.experimental.pallas import tpu_sc as plsc

F32 = jnp.float32
BF16 = jnp.bfloat16
U32 = jnp.uint32
I32 = jnp.int32

GRID_W = 64
N_HEADS = 8
N_KV_HEADS = 4
GROUP = N_HEADS // N_KV_HEADS
HEAD_DIM = 128
AXIS_DIM = HEAD_DIM // 2
ATTN_WIDTH = N_HEADS * HEAD_DIM
KV_WIDTH = N_KV_HEADS * HEAD_DIM
ROPE_THETA = 10000.0
ATTN_SCALE = HEAD_DIM ** -0.5
CONV_WIDTH = 512
CONV_KERNEL = 31
CONV_PAD = CONV_KERNEL // 2
N_EXPERTS = 256
TOP_K = 8
N_GROUPS = 8
TOPK_GROUPS = 4
EXPERTS_PER_GROUP = N_EXPERTS // N_GROUPS
EXPERT_HIDDEN = 256
SHARED_HIDDEN = 256
ROUTED_SCALE = 2.5
EPS = 1e-6
LOG2E = 1.4426950408889634
SAFE_EXP2_ARG = 64.0

LANES = 128
SUBLANES = 8
VMEM_LIMIT_BYTES = 56 * 1024 * 1024

HALO_ROWS = 16
CONV_ROWS = 64
MERGE_PARTS = 2
EXPERT_BLOCK = 512
EXPERT_BLOCKS_PER_STEP = 2
SC_SUBCORES = 16
SC_WORKERS = 2 * SC_SUBCORES
SC_WINDOW = 128
COMBINE_CHUNKS = 4
HIGHEST = lax.Precision.HIGHEST


def _params(sem):
    return pltpu.CompilerParams(dimension_semantics=sem, vmem_limit_bytes=VMEM_LIMIT_BYTES)


def _sigmoid(x):
    return 1.0 / (1.0 + jnp.exp(-x))


def _pack_bf16_pair(lo, hi):
    lo_b = pltpu.bitcast(lo.astype(BF16).astype(F32), U32)
    hi_b = pltpu.bitcast(hi.astype(BF16).astype(F32), U32)
    return (lo_b >> 16) | (hi_b & jnp.uint32(0xFFFF0000))


def _unpack_bf16_pair(u):
    lo = pltpu.bitcast(u << 16, F32)
    hi = pltpu.bitcast(u & jnp.uint32(0xFFFF0000), F32)
    return lo, hi


def _mod_kernel(cc_ref, w_ref, b_ref, o_ref):
    cc = cc_ref[...]
    s = cc * _sigmoid(cc)
    o_ref[...] = jnp.dot(s, w_ref[...], precision=HIGHEST, preferred_element_type=F32) + b_ref[...]


def _modulation(cc, w_mod, b_mod):
    d, n = w_mod.shape
    tn = n // 4
    return pl.pallas_call(
        _mod_kernel,
        grid=(n // tn,),
        in_specs=[pl.BlockSpec((SUBLANES, d), lambda j: (0, 0)),
                  pl.BlockSpec((d, tn), lambda j: (0, j)),
                  pl.BlockSpec((1, tn), lambda j: (0, j))],
        out_specs=pl.BlockSpec((SUBLANES, tn), lambda j: (0, j)),
        out_shape=jax.ShapeDtypeStruct((SUBLANES, n), F32),
        compiler_params=_params(("arbitrary",)),
        name="mod",
    )(cc, w_mod, b_mod)


def _norm_modulate(x, g, sh, sc):
    ms = jnp.mean(x * x, axis=-1, keepdims=True)
    return (x * lax.rsqrt(ms + EPS) * g) * (1.0 + sc) + sh


def _head_norm(p, gain):
    r = lax.rsqrt(jnp.mean(p * p, axis=-1, keepdims=True) + EPS)
    return p * r * gain


def _dot_nt(a, b):
    return lax.dot_general(a, b, (((1,), (1,)), ((), ())), preferred_element_type=F32)


def _ctx_kv_kernel(x_ref, sh_ref, sc_ref, g1_ref, wk_ref, wvt_ref, gk_ref, k_ref, vt_ref):
    h = _norm_modulate(x_ref[0], g1_ref[...], sh_ref[...], sc_ref[...]).astype(BF16)
    pk = jnp.dot(h, wk_ref[...], preferred_element_type=F32)
    for j in range(N_KV_HEADS):
        sl = slice(j * HEAD_DIM, (j + 1) * HEAD_DIM)
        k_ref[0, :, sl] = _head_norm(pk[:, sl], gk_ref[...]).astype(BF16)
    vt_ref[0] = _dot_nt(wvt_ref[...], h).astype(BF16)


def _ctx_kv(ctx, csh, csc, g1, w_k, w_vt, gk):
    b, lc, d = ctx.shape
    vec = lambda: pl.BlockSpec((1, d), lambda i: (0, 0))
    return pl.pallas_call(
        _ctx_kv_kernel,
        grid=(b,),
        in_specs=[pl.BlockSpec((1, lc, d), lambda i: (i, 0, 0)), vec(), vec(), vec(),
                  pl.BlockSpec((d, KV_WIDTH), lambda i: (0, 0)),
                  pl.BlockSpec((KV_WIDTH, d), lambda i: (0, 0)),
                  pl.BlockSpec((1, HEAD_DIM), lambda i: (0, 0))],
        out_specs=[pl.BlockSpec((1, lc, KV_WIDTH), lambda i: (i, 0, 0)),
                   pl.BlockSpec((1, KV_WIDTH, lc), lambda i: (i, 0, 0))],
        out_shape=[jax.ShapeDtypeStruct((b, lc, KV_WIDTH), BF16),
                   jax.ShapeDtypeStruct((b, KV_WIDTH, lc), BF16)],
        compiler_params=_params(("arbitrary",)),
        name="ctx_kv",
    )(ctx, csh, csc, g1, w_k, w_vt, gk)


def _in_proj_kernel(x_ref, sh_ref, sc_ref, g1_ref, w_ref, wvt_ref, gq_ref, gk_ref, cos_ref, sin_ref,
                    q_ref, k_ref, vt_ref, h_ref, g_ref):
    h = _norm_modulate(x_ref[0], g1_ref[...], sh_ref[0], sc_ref[0]).astype(BF16)
    cos = cos_ref[...]
    sin = sin_ref[...]
    lane = lax.broadcasted_iota(I32, cos.shape, 1)
    upper = (lane & (AXIS_DIM // 2)) != 0

    def rope(p):
        swapped = jnp.where(upper, pltpu.roll(p, AXIS_DIM // 2, 1),
                            pltpu.roll(p, HEAD_DIM - AXIS_DIM // 2, 1))
        return p * cos + swapped * sin

    q_end = ATTN_WIDTH
    k_end = q_end + KV_WIDTH
    v_end = k_end + KV_WIDTH
    u_end = v_end + 2 * CONV_WIDTH
    pq = jnp.dot(h, w_ref[:, :q_end], preferred_element_type=F32)
    for j in range(N_HEADS):
        sl = slice(j * HEAD_DIM, (j + 1) * HEAD_DIM)
        q_ref[0, :, sl] = (rope(_head_norm(pq[:, sl], gq_ref[...])) * (ATTN_SCALE * LOG2E)).astype(BF16)
    pk = jnp.dot(h, w_ref[:, q_end:k_end], preferred_element_type=F32)
    for j in range(N_KV_HEADS):
        sl = slice(j * HEAD_DIM, (j + 1) * HEAD_DIM)
        k_ref[0, :, sl] = rope(_head_norm(pk[:, sl], gk_ref[...])).astype(BF16)
    vt_ref[0] = _dot_nt(wvt_ref[...], h).astype(BF16)
    u = jnp.dot(h, w_ref[:, v_end:u_end], preferred_element_type=F32)
    h_ref[0] = (u[:, :CONV_WIDTH] * _sigmoid(u[:, CONV_WIDTH:])).astype(BF16)
    g_ref[0] = _sigmoid(jnp.dot(h, w_ref[:, u_end:], preferred_element_type=F32)).astype(BF16)


def _in_proj(x, sh1, sc1, g1, w_in, w_vt, gq, gk, cos_t, sin_t, tm):
    b, s, d = x.shape
    n_in = w_in.shape[1]
    bvec = lambda: pl.BlockSpec((1, 1, d), lambda bi, i: (bi, 0, 0))
    tok = lambda w: pl.BlockSpec((1, tm, w), lambda bi, i: (bi, i, 0))
    return pl.pallas_call(
        _in_proj_kernel,
        grid=(b, s // tm),
        in_specs=[tok(d), bvec(), bvec(),
                  pl.BlockSpec((1, d), lambda bi, i: (0, 0)),
                  pl.BlockSpec((d, n_in), lambda bi, i: (0, 0)),
                  pl.BlockSpec((KV_WIDTH, d), lambda bi, i: (0, 0)),
                  pl.BlockSpec((1, HEAD_DIM), lambda bi, i: (0, 0)),
                  pl.BlockSpec((1, HEAD_DIM), lambda bi, i: (0, 0)),
                  pl.BlockSpec((tm, HEAD_DIM), lambda bi, i: (i, 0)),
                  pl.BlockSpec((tm, HEAD_DIM), lambda bi, i: (i, 0))],
        out_specs=[tok(ATTN_WIDTH), tok(KV_WIDTH),
                   pl.BlockSpec((1, KV_WIDTH, tm), lambda bi, i: (bi, 0, i)),
                   tok(CONV_WIDTH), tok(2 * d)],
        out_shape=[jax.ShapeDtypeStruct((b, s, ATTN_WIDTH), BF16),
                   jax.ShapeDtypeStruct((b, s, KV_WIDTH), BF16),
                   jax.ShapeDtypeStruct((b, KV_WIDTH, s), BF16),
                   jax.ShapeDtypeStruct((b, s, CONV_WIDTH), BF16),
                   jax.ShapeDtypeStruct((b, s, 2 * d), BF16)],
        compiler_params=_params(("arbitrary", "arbitrary")),
        name="in_proj",
    )(x, sh1, sc1, g1, w_in, w_vt, gq, gk, cos_t, sin_t)


def _sub_allreduce(x, op):
    for s in (4, 2, 1):
        x = op(x, pltpu.roll(x, s, 0))
    return x


def _attn_kernel(bounded_ref, q_ref, kc_ref, vct_ref, k_ref, vt_ref, hp_ref, hc_ref, hn_ref, wdw_ref, bdw_ref,
                 o_ref, c_ref, s0, s1, x0, x1, m_sc, l_sc, acc_sc, hcat, shift_sc, *, tk):
    tq = q_ref.shape[1]
    m_cols = GROUP * tq
    nk = k_ref.shape[1] // tk
    qf = q_ref[0].astype(F32).T
    qt = jnp.concatenate([qf[:HEAD_DIM], qf[HEAD_DIM:]], axis=1).astype(BF16)
    slots = ((s0, x0), (s1, x1))
    n_conv = max(nk // 2, 1)
    conv_rows = tq // n_conv
    conv_base = HALO_ROWS - CONV_PAD
    conv_reach = (conv_base + CONV_KERNEL - 1) // SUBLANES * SUBLANES

    def conv_fill():
        ti = pl.program_id(2)
        prev = hp_ref[0].astype(F32)
        nxt = hn_ref[0].astype(F32)
        hcat[0:HALO_ROWS, :] = jnp.where(ti > 0, prev, jnp.zeros_like(prev))
        hcat[HALO_ROWS:HALO_ROWS + tq, :] = hc_ref[0].astype(F32)
        hcat[HALO_ROWS + tq:, :] = jnp.where(ti < pl.num_programs(2) - 1, nxt, jnp.zeros_like(nxt))

    def conv_block(blk):
        r0 = blk * conv_rows
        for sub in range(0, conv_rows, CONV_ROWS):
            start = r0 + sub if isinstance(r0, int) else pl.multiple_of(r0 + sub, CONV_ROWS)
            window = hcat[pl.ds(start, CONV_ROWS + 2 * HALO_ROWS), :]
            acc = jnp.zeros((CONV_ROWS, HEAD_DIM), F32) + bdw_ref[...]
            for res in range(SUBLANES):
                shift_sc[res] = window[res:res + CONV_ROWS + conv_reach, :]
                for off in range(res, conv_base + CONV_KERNEL, SUBLANES):
                    j = off - conv_base
                    if 0 <= j < CONV_KERNEL:
                        a0 = off - res
                        acc = acc + shift_sc[res, a0:a0 + CONV_ROWS, :] * wdw_ref[j:j + 1, :]
            c_ref[0, pl.ds(start, CONV_ROWS), :] = acc

    def kchunk(j):
        return k_ref[0, pl.ds(pl.multiple_of(j * tk, tk), tk), :]

    def vchunk(j):
        return vt_ref[0, :, pl.ds(pl.multiple_of(j * tk, tk), tk)]

    def split(st):
        return st.reshape(st.shape[0] // SUBLANES, SUBLANES, m_cols)

    def scores(k, online):
        st = jnp.dot(k, qt, preferred_element_type=F32)
        return st, (jnp.max(split(st), axis=0) if online else None)

    def absorb(st, mx, vt, online):
        s3 = split(st)
        if online:
            m_prev = m_sc[...]
            m_new = jnp.maximum(m_prev, _sub_allreduce(mx, jnp.maximum))
            alpha = jnp.exp2(m_prev - m_new)
            p3 = jnp.exp2(s3 - m_new[None])
            l_sc[...] = alpha * l_sc[...] + _sub_allreduce(jnp.sum(p3, axis=0), jnp.add)
            pv = jnp.dot(vt, p3.reshape(st.shape).astype(BF16), preferred_element_type=F32)
            acc_sc[...] = alpha[0:1] * acc_sc[...] + pv
            m_sc[...] = m_new
        else:
            p3 = jnp.exp2(s3)
            l_sc[...] = l_sc[...] + jnp.sum(p3, axis=0)
            acc_sc[...] = acc_sc[...] + jnp.dot(vt, p3.reshape(st.shape).astype(BF16),
                                                preferred_element_type=F32)

    def stage(slot, k, online):
        st, mx = scores(k, online)
        slots[slot][0][...] = st
        if online:
            slots[slot][1][...] = mx

    def take(slot, vt, online):
        absorb(slots[slot][0][...], slots[slot][1][...] if online else None, vt, online)

    def sweep(online):
        if online:
            m_sc[...] = jnp.full(m_sc.shape, -jnp.inf, F32)
        l_sc[...] = jnp.zeros(l_sc.shape, F32)
        acc_sc[...] = jnp.zeros(acc_sc.shape, F32)
        conv_fill()
        stage(0, kchunk(0), online)

        def body(i, carry):
            j = 2 * i
            stage(1, kchunk(j + 1), online)
            take(0, vchunk(j), online)
            conv_block(i)
            stage(0, kchunk(j + 2), online)
            take(1, vchunk(j + 1), online)
            return carry

        lax.fori_loop(0, nk // 2 - 1, body, 0)
        conv_block(n_conv - 1)
        stage(1, kchunk(nk - 1), online)
        take(0, vchunk(nk - 2), online)
        sc, xc = scores(kc_ref[0], online)
        take(1, vchunk(nk - 1), online)
        absorb(sc, xc, vct_ref[0], online)
        denom = l_sc[...] if online else _sub_allreduce(l_sc[...], jnp.add)
        o = (acc_sc[...] / denom[0:1]).T
        o_ref[0, :, :HEAD_DIM] = o[:tq].astype(BF16)
        o_ref[0, :, HEAD_DIM:] = o[tq:].astype(BF16)

    @pl.when(bounded_ref[0] != 0)
    def _():
        sweep(online=False)

    @pl.when(bounded_ref[0] == 0)
    def _():
        sweep(online=True)


def _attention(bounded, q, kc, vct, kx, vxt, hglu, w_dw, b_dw, tq, tk):
    b, s, _ = q.shape
    lc = kc.shape[1]
    assert s % (2 * tk) == 0 and CONV_WIDTH == N_KV_HEADS * HEAD_DIM
    gw = GROUP * HEAD_DIM
    m_cols = GROUP * tq
    hb = tq // HALO_ROWS
    n_halo = s // HALO_ROWS
    reach = (HALO_ROWS - CONV_PAD + CONV_KERNEL - 1) // SUBLANES * SUBLANES
    kv = lambda l: pl.BlockSpec((1, l, HEAD_DIM), lambda bi, h, i, bd: (bi, 0, h))
    kvt = lambda l: pl.BlockSpec((1, HEAD_DIM, l), lambda bi, h, i, bd: (bi, h, 0))
    qo = lambda: pl.BlockSpec((1, tq, gw), lambda bi, h, i, bd: (bi, i, h))
    chan = lambda rows: pl.BlockSpec((rows, HEAD_DIM), lambda bi, h, i, bd: (0, h))
    return pl.pallas_call(
        functools.partial(_attn_kernel, tk=tk),
        grid_spec=pltpu.PrefetchScalarGridSpec(
            num_scalar_prefetch=1,
            grid=(b, N_KV_HEADS, s // tq),
            in_specs=[qo(), kv(lc), kvt(lc), kv(s), kvt(s),
                      pl.BlockSpec((1, HALO_ROWS, HEAD_DIM),
                                   lambda bi, h, i, bd: (bi, jnp.maximum(i * hb - 1, 0), h)),
                      pl.BlockSpec((1, tq, HEAD_DIM), lambda bi, h, i, bd: (bi, i, h)),
                      pl.BlockSpec((1, HALO_ROWS, HEAD_DIM),
                                   lambda bi, h, i, bd: (bi, jnp.minimum((i + 1) * hb, n_halo - 1), h)),
                      chan(CONV_KERNEL), chan(1)],
            out_specs=[qo(), pl.BlockSpec((1, tq, HEAD_DIM), lambda bi, h, i, bd: (bi, i, h))],
            scratch_shapes=[pltpu.VMEM((tk, m_cols), F32), pltpu.VMEM((tk, m_cols), F32),
                            pltpu.VMEM((SUBLANES, m_cols), F32), pltpu.VMEM((SUBLANES, m_cols), F32),
                            pltpu.VMEM((SUBLANES, m_cols), F32), pltpu.VMEM((SUBLANES, m_cols), F32),
                            pltpu.VMEM((HEAD_DIM, m_cols), F32),
                            pltpu.VMEM((tq + 2 * HALO_ROWS, HEAD_DIM), F32),
                            pltpu.VMEM((SUBLANES, CONV_ROWS + reach, HEAD_DIM), F32)]),
        out_shape=[jax.ShapeDtypeStruct((b, s, ATTN_WIDTH), BF16),
                   jax.ShapeDtypeStruct((b, s, CONV_WIDTH), F32)],
        compiler_params=_params(("arbitrary", "arbitrary", "arbitrary")),
        name="attn",
    )(bounded, q, kc, vct, kx, vxt, hglu, hglu, hglu, w_dw, b_dw)


def _split3(x):
    x1 = x.astype(BF16)
    r1 = x - x1.astype(F32)
    x2 = r1.astype(BF16)
    x3 = (r1 - x2.astype(F32)).astype(BF16)
    return x1, x2, x3


def _merge_kernel(o_ref, c_ref, g_ref, x_ref, gt1_ref, sh2_ref, sc2_ref,
                  lng_ref, lnb_ref, wap_ref, wcp_ref, wout_ref, g2_ref,
                  x1_ref, hx_ref, hxf_ref):
    tm = x_ref.shape[1]
    d = x_ref.shape[2]
    half = d // 2
    rows_per_part = tm // MERGE_PARTS
    for part in range(MERGE_PARTS):
        p0 = part * rows_per_part
        rows = slice(p0, p0 + rows_per_part)
        conv = c_ref[0, rows, :]
        mu = jnp.mean(conv, axis=-1, keepdims=True)
        cen = conv - mu
        var = jnp.mean(cen * cen, axis=-1, keepdims=True)
        ln = cen * lax.rsqrt(var + EPS) * lng_ref[...] + lnb_ref[...]
        act = (ln * _sigmoid(ln)).astype(BF16)
        y_conv = jnp.dot(act, wcp_ref[...], preferred_element_type=F32)
        y_attn = jnp.dot(o_ref[0, rows, :], wap_ref[...], preferred_element_type=F32)
        z = g_ref[0, rows, :d].astype(F32) * y_attn + g_ref[0, rows, d:].astype(F32) * y_conv
        mix = jnp.dot(z.astype(BF16), wout_ref[...], preferred_element_type=F32)
        x1 = x_ref[0, rows, :] + gt1_ref[0] * mix
        x1_ref[0, rows, :] = x1
        hx = _norm_modulate(x1, g2_ref[...], sh2_ref[0], sc2_ref[0])
        hx_ref[rows, :] = pltpu.bitcast(_pack_bf16_pair(hx[:, :half], hx[:, half:]), I32)
        h1, h2, h3 = _split3(hx)
        hxf_ref[0, rows, :] = h1
        hxf_ref[1, rows, :] = h2
        hxf_ref[2, rows, :] = h3


def _merge(o, conv, g, x, gt1, sh2, sc2, ln_g, ln_b, w_ap, w_cp, w_out, g2, tm):
    b, s, d = x.shape
    nt = s // tm
    bvec = lambda: pl.BlockSpec((1, 1, d), lambda bi, i: (bi, 0, 0))
    full = lambda a: pl.BlockSpec(a.shape, lambda bi, i: (0,) * a.ndim)
    tok = lambda w: pl.BlockSpec((1, tm, w), lambda bi, i: (bi, i, 0))
    return pl.pallas_call(
        _merge_kernel,
        grid=(b, nt),
        in_specs=[tok(ATTN_WIDTH), tok(CONV_WIDTH), tok(2 * d), tok(d), bvec(), bvec(), bvec(),
                  full(ln_g), full(ln_b), full(w_ap), full(w_cp), full(w_out), full(g2)],
        out_specs=[tok(d),
                   pl.BlockSpec((tm, d // 2), lambda bi, i: (bi * nt + i, 0)),
                   pl.BlockSpec((3, tm, d), lambda bi, i: (0, bi * nt + i, 0))],
        out_shape=[jax.ShapeDtypeStruct((b, s, d), F32),
                   jax.ShapeDtypeStruct((b * s, d // 2), I32),
                   jax.ShapeDtypeStruct((3, b * s, d), BF16)],
        compiler_params=_params(("arbitrary", "arbitrary")),
        name="merge",
    )(o, conv, g, x, gt1, sh2, sc2, ln_g, ln_b, w_ap, w_cp, w_out, g2)


def _router_logits(w_ref, h_ref):
    h1, h2, h3 = h_ref[0], h_ref[1], h_ref[2]
    w1, w2, w3 = w_ref[0], w_ref[1], w_ref[2]
    return ((((_dot_nt(w3, h1) + _dot_nt(w1, h3)) + _dot_nt(w2, h2))
             + (_dot_nt(w2, h1) + _dot_nt(w1, h2))) + _dot_nt(w1, h1))


def _route_kernel(hx0_ref, hxa_ref, hxb_ref, wrt_ref, bias_ref, tri_ref, e_ref, w_ref, r_ref, cnt_ref,
                  lg_a, lg_b, run_sc):
    step = pl.program_id(0)
    tb = hxa_ref.shape[1]
    n_strips = tb // LANES
    nv = N_EXPERTS // SUBLANES
    gv = EXPERTS_PER_GROUP // SUBLANES

    @pl.when(step == 0)
    def _():
        run_sc[...] = jnp.zeros(run_sc.shape, F32)
        lg_a[...] = _router_logits(wrt_ref, hx0_ref)

    row = (lax.broadcasted_iota(I32, (nv, SUBLANES, LANES), 0) * SUBLANES
           + lax.broadcasted_iota(I32, (nv, SUBLANES, LANES), 1))
    sub = lax.broadcasted_iota(I32, (SUBLANES, LANES), 0)
    bias = bias_ref[...].reshape(nv, SUBLANES, LANES)
    neg_inf = jnp.float32(-jnp.inf)

    def topk_strip(lg, st, col0):
        lanes = slice(col0 + st * LANES, col0 + (st + 1) * LANES)
        scores = _sigmoid(lg[:, st * LANES:(st + 1) * LANES]).reshape(nv, SUBLANES, LANES)
        biased = scores + bias
        gscore = []
        for g in range(N_GROUPS):
            m1 = biased[g * gv]
            m2 = jnp.full((SUBLANES, LANES), neg_inf, F32)
            for t in range(1, gv):
                v = biased[g * gv + t]
                m2 = jnp.maximum(m2, jnp.minimum(m1, v))
                m1 = jnp.maximum(m1, v)
            for s in (4, 2, 1):
                p1 = pltpu.roll(m1, s, 0)
                p2 = pltpu.roll(m2, s, 0)
                m2 = jnp.maximum(jnp.minimum(m1, p1), jnp.maximum(m2, p2))
                m1 = jnp.maximum(m1, p1)
            gscore.append(m1 + m2)
        masked = []
        for g in range(N_GROUPS):
            beaten = jnp.zeros((SUBLANES, LANES), I32)
            for o in range(N_GROUPS):
                if o == g:
                    continue
                wins = (gscore[o] > gscore[g]) | ((gscore[o] == gscore[g]) & (o < g))
                beaten = beaten + wins.astype(I32)
            keep = beaten < TOPK_GROUPS
            for t in range(gv):
                masked.append(jnp.where(keep, biased[g * gv + t], neg_inf))
        cand = jnp.stack(masked, axis=0)
        sel = jnp.zeros((nv, SUBLANES, LANES), jnp.bool_)
        picks, pick_scores = [], []
        for _ in range(TOP_K):
            mx = _sub_allreduce(jnp.max(cand, axis=0), jnp.maximum)
            idx = _sub_allreduce(jnp.min(jnp.where(cand == mx, row, N_EXPERTS), axis=0), jnp.minimum)
            hit = row == idx
            pick_scores.append(_sub_allreduce(jnp.sum(jnp.where(hit, scores, 0.0), axis=0), jnp.add))
            picks.append(idx)
            sel = sel | hit
            cand = jnp.where(hit, neg_inf, cand)
        sel_b = sel.astype(F32).astype(BF16).reshape(N_EXPERTS, LANES)
        before = jnp.dot(sel_b, tri_ref[0], preferred_element_type=F32)
        total = jnp.dot(sel_b, tri_ref[1], preferred_element_type=F32)
        rank_all = (before + run_sc[...]).reshape(nv, SUBLANES, LANES)
        run_sc[...] = run_sc[...] + total
        denom = pick_scores[0]
        for kk in range(1, TOP_K):
            denom = denom + pick_scores[kk]
        e_out = jnp.zeros((SUBLANES, LANES), I32)
        w_out = jnp.zeros((SUBLANES, LANES), F32)
        r_out = jnp.zeros((SUBLANES, LANES), I32)
        for kk in range(TOP_K):
            rk = _sub_allreduce(jnp.sum(jnp.where(row == picks[kk], rank_all, 0.0), axis=0), jnp.add)
            e_out = jnp.where(sub == kk, picks[kk], e_out)
            w_out = jnp.where(sub == kk, pick_scores[kk] / denom * ROUTED_SCALE, w_out)
            r_out = jnp.where(sub == kk, rk.astype(I32), r_out)
        e_ref[:, lanes] = e_out
        w_ref[:, lanes] = w_out
        r_ref[:, lanes] = r_out

    lg_b[...] = _router_logits(wrt_ref, hxa_ref)
    for st in range(n_strips):
        topk_strip(lg_a, st, 0)
    lg_a[...] = _router_logits(wrt_ref, hxb_ref)
    for st in range(n_strips):
        topk_strip(lg_b, st, tb)
    cnt_ref[...] = run_sc[...].astype(I32)


def _route(hxf, w_rt3, bias, tb):
    _, n, d = hxf.shape
    nt = n // tb
    assert nt % 2 == 0
    iota_r = lax.broadcasted_iota(I32, (LANES, LANES), 0)
    iota_c = lax.broadcasted_iota(I32, (LANES, LANES), 1)
    tri = jnp.stack([(iota_r < iota_c), jnp.ones((LANES, LANES), jnp.bool_)]).astype(BF16)
    bias_b = jnp.broadcast_to(bias.reshape(N_EXPERTS, 1), (N_EXPERTS, LANES)).astype(F32)
    tokrow = lambda dt: jax.ShapeDtypeStruct((TOP_K, n), dt)
    out_blk = lambda: pl.BlockSpec((TOP_K, 2 * tb), lambda i: (0, i))
    return pl.pallas_call(
        _route_kernel,
        grid=(nt // 2,),
        in_specs=[pl.BlockSpec((3, tb, d), lambda i: (0, 0, 0)),
                  pl.BlockSpec((3, tb, d), lambda i: (0, 2 * i + 1, 0)),
                  pl.BlockSpec((3, tb, d), lambda i: (0, jnp.minimum(2 * i + 2, nt - 1), 0)),
                  pl.BlockSpec(w_rt3.shape, lambda i: (0, 0, 0)),
                  pl.BlockSpec((N_EXPERTS, LANES), lambda i: (0, 0)),
                  pl.BlockSpec((2, LANES, LANES), lambda i: (0, 0, 0))],
        out_specs=[out_blk(), out_blk(), out_blk(),
                   pl.BlockSpec((N_EXPERTS, LANES), lambda i: (0, 0))],
        out_shape=[tokrow(I32), tokrow(F32), tokrow(I32),
                   jax.ShapeDtypeStruct((N_EXPERTS, LANES), I32)],
        scratch_shapes=[pltpu.VMEM((N_EXPERTS, tb), F32), pltpu.VMEM((N_EXPERTS, tb), F32),
                        pltpu.VMEM((N_EXPERTS, LANES), F32)],
        compiler_params=_params(("arbitrary",)),
        name="route",
    )(hxf, hxf, hxf, w_rt3, bias_b, tri)


def _slots_kernel(pstart_ref, e_ref, r_ref, o_ref):
    e = e_ref[...]

    def body(x, acc):
        return acc + jnp.where(e == x, pstart_ref[x], 0)

    o_ref[...] = lax.fori_loop(0, N_EXPERTS, body, r_ref[...])


def _slots(pstart, e_idx, rank, tb):
    n = e_idx.shape[1]
    blk = lambda: pl.BlockSpec((TOP_K, tb), lambda i, ps: (0, i))
    return pl.pallas_call(
        _slots_kernel,
        grid_spec=pltpu.PrefetchScalarGridSpec(
            num_scalar_prefetch=1, grid=(n // tb,), in_specs=[blk(), blk()], out_specs=blk()),
        out_shape=jax.ShapeDtypeStruct((TOP_K, n), I32),
        compiler_params=_params(("arbitrary",)),
        name="slots",
    )(pstart, e_idx, rank)


def _sc_mesh():
    return plsc.VectorSubcoreMesh(core_axis_name="c", subcore_axis_name="s")


def _sc_worker():
    return lax.axis_index("c") * SC_SUBCORES + lax.axis_index("s")


def _sc_dispatch(hx, pos, cap):
    n, w = hx.shape
    per_worker = n // SC_WINDOW // SC_WORKERS
    assert per_worker * SC_WINDOW * SC_WORKERS == n

    @pl.kernel(out_type=jax.ShapeDtypeStruct((cap, w), hx.dtype), mesh=_sc_mesh(),
               scratch_types=[pltpu.VMEM((SC_WINDOW, w), hx.dtype), pltpu.VMEM((TOP_K, SC_WINDOW), I32),
                              pltpu.SemaphoreType.DMA])
    def scatter_rows(x_hbm, i_hbm, o_hbm, xbuf, ibuf, sem):
        wid = _sc_worker()

        @pl.loop(0, per_worker)
        def _(j):
            row0 = (wid * per_worker + j) * SC_WINDOW
            pltpu.sync_copy(x_hbm.at[pl.ds(row0, SC_WINDOW)], xbuf)
            pltpu.sync_copy(i_hbm.at[:, pl.ds(row0, SC_WINDOW)], ibuf)
            copies = [pltpu.async_copy(xbuf, o_hbm.at[ibuf.at[kk]], sem) for kk in range(TOP_K)]
            for cp in copies:
                cp.wait()

    return scatter_rows(hx, pos)


def _sc_gather(ys, pos):
    n = pos.shape[1]
    w = ys.shape[1]
    per_worker = n // SC_WINDOW // SC_WORKERS
    assert per_worker * SC_WINDOW * SC_WORKERS == n

    @pl.kernel(out_type=jax.ShapeDtypeStruct((TOP_K, n, w), ys.dtype), mesh=_sc_mesh(),
               scratch_types=[pltpu.VMEM((SC_WINDOW, w), ys.dtype), pltpu.VMEM((TOP_K, SC_WINDOW), I32)])
    def gather_rows(y_hbm, i_hbm, o_hbm, ybuf, ibuf):
        wid = _sc_worker()

        @pl.loop(0, per_worker)
        def _(j):
            row0 = (wid * per_worker + j) * SC_WINDOW
            pltpu.sync_copy(i_hbm.at[:, pl.ds(row0, SC_WINDOW)], ibuf)
            for kk in range(TOP_K):
                pltpu.sync_copy(y_hbm.at[ibuf.at[kk]], ybuf)
                pltpu.sync_copy(ybuf, o_hbm.at[kk, pl.ds(row0, SC_WINDOW)])

    return gather_rows(ys, pos)


def _experts_kernel(blk_e_ref, valid_ref, blk_in_ref, blk_out_ref, next_e_ref, slot_ref,
                    xs_ref, wgu_hbm, wdn_hbm, ys_ref, gu_buf, dn_buf, sem, wgu_sc, wdn_sc):
    del blk_in_ref, blk_out_ref

    def weight_copies(e, slot):
        return (pltpu.make_async_copy(wgu_hbm.at[e], gu_buf.at[slot], sem.at[0, slot]),
                pltpu.make_async_copy(wdn_hbm.at[e], dn_buf.at[slot], sem.at[1, slot]))

    def one_block(i, rsl):
        valid = valid_ref[i]

        @pl.when(valid > 0)
        def _():
            e = blk_e_ref[i]
            slot = slot_ref[i]
            prev = blk_e_ref[jnp.maximum(i - 1, 0)]

            @pl.when(i == 0)
            def _():
                for cp in weight_copies(e, slot):
                    cp.start()

            @pl.when((i == 0) | (e != prev))
            def _():
                for cp in weight_copies(e, slot):
                    cp.wait()
                nxt = next_e_ref[i]

                @pl.when(nxt >= 0)
                def _():
                    for cp in weight_copies(nxt, 1 - slot):
                        cp.start()

                wgu_sc[...] = gu_buf[slot].astype(BF16)
                wdn_sc[...] = dn_buf[slot].astype(BF16)

            rows = lax.broadcasted_iota(I32, (EXPERT_BLOCK, xs_ref.shape[1]), 0)
            xu = jnp.where(rows < valid, pltpu.bitcast(xs_ref[rsl, :], U32), jnp.uint32(0))
            lo, hi = _unpack_bf16_pair(xu)
            half = lo.shape[1]
            gu = (jnp.dot(lo.astype(BF16), wgu_sc[:half, :], preferred_element_type=F32)
                  + jnp.dot(hi.astype(BF16), wgu_sc[half:, :], preferred_element_type=F32))
            gt = gu[:, :EXPERT_HIDDEN]
            act = (gt * _sigmoid(gt) * gu[:, EXPERT_HIDDEN:]).astype(BF16)
            y = jnp.dot(act, wdn_sc[...], preferred_element_type=F32)
            ys_ref[rsl, :] = pltpu.bitcast(_pack_bf16_pair(y[:, :half], y[:, half:]), I32)

        @pl.when(valid <= 0)
        def _():
            ys_ref[rsl, :] = jnp.zeros((EXPERT_BLOCK, ys_ref.shape[1]), I32)

    step = pl.program_id(0)
    for sub in range(EXPERT_BLOCKS_PER_STEP):
        one_block(step * EXPERT_BLOCKS_PER_STEP + sub, slice(sub * EXPERT_BLOCK, (sub + 1) * EXPERT_BLOCK))


def _experts(blk_e, blk_valid, next_expert, expert_slot, xs, w_gu, w_dn):
    cap, w = xs.shape
    n_e, d, h2 = w_gu.shape
    n_blocks = cap // EXPERT_BLOCK
    assert n_blocks % EXPERT_BLOCKS_PER_STEP == 0
    n_steps = n_blocks // EXPERT_BLOCKS_PER_STEP
    step_rows = EXPERT_BLOCKS_PER_STEP * EXPERT_BLOCK
    blk_next = next_expert[blk_e].astype(I32)
    blk_slot = expert_slot[blk_e].astype(I32)
    step = jnp.arange(n_steps, dtype=I32)
    step_used = blk_valid[::EXPERT_BLOCKS_PER_STEP] > 0
    n_used = jnp.sum(step_used.astype(I32))
    blk_in = jnp.minimum(step, jnp.maximum(n_used - 1, 0)).astype(I32)
    blk_out = jnp.where(step_used, step, n_steps).astype(I32)
    return pl.pallas_call(
        _experts_kernel,
        grid_spec=pltpu.PrefetchScalarGridSpec(
            num_scalar_prefetch=6,
            grid=(n_steps,),
            in_specs=[pl.BlockSpec((step_rows, w), lambda i, be, bv, bi, bo, bn, bs: (bi[i], 0)),
                      pl.BlockSpec(memory_space=pl.ANY),
                      pl.BlockSpec(memory_space=pl.ANY)],
            out_specs=pl.BlockSpec((step_rows, w), lambda i, be, bv, bi, bo, bn, bs: (bo[i], 0)),
            scratch_shapes=[pltpu.VMEM((2, d, h2), F32), pltpu.VMEM((2, h2 // 2, d), F32),
                            pltpu.SemaphoreType.DMA((2, 2)),
                            pltpu.VMEM((d, h2), BF16), pltpu.VMEM((h2 // 2, d), BF16)]),
        out_shape=jax.ShapeDtypeStruct((cap + step_rows, w), I32),
        compiler_params=_params(("arbitrary",)),
        name="experts",
    )(blk_e, blk_valid, blk_in, blk_out, blk_next, blk_slot, xs, w_gu, w_dn)


def _shared_kernel(hx_ref, wsg_ref, wsd_ref, o_ref):
    lo, hi = _unpack_bf16_pair(pltpu.bitcast(hx_ref[...], U32))
    half = lo.shape[1]
    gu = (jnp.dot(lo.astype(BF16), wsg_ref[:half, :], preferred_element_type=F32)
          + jnp.dot(hi.astype(BF16), wsg_ref[half:, :], preferred_element_type=F32))
    gt = gu[:, :SHARED_HIDDEN]
    act = (gt * _sigmoid(gt) * gu[:, SHARED_HIDDEN:]).astype(BF16)
    y = jnp.dot(act, wsd_ref[...], preferred_element_type=F32)
    o_ref[...] = pltpu.bitcast(_pack_bf16_pair(y[:, :half], y[:, half:]), I32)


def _shared(hx, w_sg, w_sd, tm):
    n, w = hx.shape
    full = lambda a: pl.BlockSpec(a.shape, lambda i: (0,) * a.ndim)
    return pl.pallas_call(
        _shared_kernel,
        grid=(n // tm,),
        in_specs=[pl.BlockSpec((tm, w), lambda i: (i, 0)), full(w_sg), full(w_sd)],
        out_specs=pl.BlockSpec((tm, w), lambda i: (i, 0)),
        out_shape=jax.ShapeDtypeStruct((n, w), I32),
        compiler_params=_params(("arbitrary",)),
        name="shared",
    )(hx, w_sg, w_sd)


def _combine_kernel(yg_ref, wt_ref, ysh_ref, x1_ref, gt2_ref, fg_ref, *rest):
    o_ref = rest[-1]
    y_lo, y_hi = _unpack_bf16_pair(pltpu.bitcast(ysh_ref[...], U32))
    half = y_lo.shape[1]
    for kk in range(TOP_K):
        r_lo, r_hi = _unpack_bf16_pair(pltpu.bitcast(yg_ref[kk], U32))
        wk = wt_ref[:, kk:kk + 1]
        y_lo = y_lo + wk * r_lo
        y_hi = y_hi + wk * r_hi
    x2_lo = x1_ref[:, :half] + gt2_ref[0, :, :half] * y_lo
    x2_hi = x1_ref[:, half:] + gt2_ref[0, :, half:] * y_hi
    ms = (jnp.sum(x2_lo * x2_lo, axis=-1, keepdims=True)
          + jnp.sum(x2_hi * x2_hi, axis=-1, keepdims=True)) / (2 * half)
    inv = lax.rsqrt(ms + EPS)
    o_ref[:, :half] = x2_lo * inv * fg_ref[:, :half]
    o_ref[:, half:] = x2_hi * inv * fg_ref[:, half:]


def _combine(yg, wt, ysh, x1, gt2, fg, tm, tiles_per_batch, tile0, out_prev):
    n, w = ysh.shape
    d = 2 * w
    full = lambda a: pl.BlockSpec(a.shape, lambda i: (0,) * a.ndim)
    in_specs = [pl.BlockSpec((TOP_K, tm, w), lambda i: (0, i, 0)),
                pl.BlockSpec((tm, TOP_K), lambda i: (tile0 + i, 0)),
                pl.BlockSpec((tm, w), lambda i: (tile0 + i, 0)),
                pl.BlockSpec((tm, d), lambda i: (tile0 + i, 0)),
                pl.BlockSpec((1, 1, d), lambda i: ((tile0 + i) // tiles_per_batch, 0, 0)),
                full(fg)]
    args = [yg, wt, ysh, x1, gt2, fg]
    aliases = {}
    if out_prev is not None:
        in_specs.append(pl.BlockSpec(memory_space=pl.ANY))
        args.append(out_prev)
        aliases = {len(args) - 1: 0}
    return pl.pallas_call(
        _combine_kernel,
        grid=(yg.shape[1] // tm,),
        in_specs=in_specs,
        out_specs=pl.BlockSpec((tm, d), lambda i: (tile0 + i, 0)),
        out_shape=jax.ShapeDtypeStruct((n, d), F32),
        input_output_aliases=aliases,
        compiler_params=_params(("arbitrary",)),
        name="combine",
    )(*args)


def _rope_tables(seq):
    rows = seq // GRID_W
    pos_row = np.repeat(np.arange(rows, dtype=np.float32), GRID_W)
    pos_col = np.tile(np.arange(GRID_W, dtype=np.float32), rows)
    inv_freq = (ROPE_THETA ** (-np.arange(0, AXIS_DIM, 2, dtype=np.float32) / AXIS_DIM)).astype(np.float32)
    ar = pos_row[:, None] * inv_freq
    ac = pos_col[:, None] * inv_freq
    cos_t = np.concatenate([np.cos(ar), np.cos(ar), np.cos(ac), np.cos(ac)], axis=1)
    sin_t = np.concatenate([-np.sin(ar), np.sin(ar), -np.sin(ac), np.sin(ac)], axis=1)
    return jnp.asarray(cos_t, F32), jnp.asarray(sin_t, F32)


def _tile(n, want):
    t = min(n, want)
    assert n % t == 0, (n, want)
    return t


def kernel(x, c, ctx, c_ctx, w_mod, b_mod, norm1_g, w_in, q_norm_g, k_norm_g, w_dw, b_dw, conv_ln_g, conv_ln_b, w_attn_proj, w_conv_proj, w_out, norm2_g, w_router, router_bias, w_exp_gu, w_exp_dn, w_sh_gu, w_sh_dn, final_g):
    b, s, d = x.shape
    depth = w_mod.shape[0]
    assert depth == 1, "single-layer block"
    n = b * s
    row = lambda v: v.reshape(1, -1)

    cc = jnp.zeros((SUBLANES, d), F32).at[:b].set(c).at[b].set(c_ctx)
    mod = _modulation(cc, w_mod[0], row(b_mod[0]))
    mod_x = mod[:b].reshape(b, 1, 6, d)
    sh1, sc1, gt1, sh2, sc2, gt2 = [mod_x[:, :, j, :] for j in range(6)]
    mod_c = mod[b].reshape(6, d)
    csh1, csc1 = row(mod_c[0]), row(mod_c[1])

    w_in_b = w_in[0].astype(BF16)
    q_end, kv_end = ATTN_WIDTH, ATTN_WIDTH + 2 * KV_WIDTH
    k_end = q_end + KV_WIDTH
    w_vt = w_in_b[:, k_end:kv_end].T
    kc, vct = _ctx_kv(ctx, csh1, csc1, row(norm1_g[0]), w_in_b[:, q_end:k_end], w_vt, row(k_norm_g[0]))

    cos_t, sin_t = _rope_tables(s)
    q, kx, vxt, hglu, gates = _in_proj(x, sh1, sc1, row(norm1_g[0]), w_in_b, w_vt, row(q_norm_g[0]),
                                       row(k_norm_g[0]), cos_t, sin_t, _tile(s, 512))
    score_bound = (HEAD_DIM * ATTN_SCALE * LOG2E) * jnp.max(jnp.abs(q_norm_g[0])) * jnp.max(jnp.abs(k_norm_g[0]))
    bounded = (score_bound <= SAFE_EXP2_ARG).astype(I32).reshape(1)
    o, conv = _attention(bounded, q, kc, vct, kx, vxt, hglu, w_dw[0], row(b_dw[0]),
                         _tile(s, 1024), _tile(s // 2, 1024))

    top16 = lambda a: lax.bitcast_convert_type(lax.bitcast_convert_type(a, U32) & jnp.uint32(0xFFFF0000), F32)
    w_rt = w_router[0].T
    w_rt1 = top16(w_rt)
    w_rt2 = top16(w_rt - w_rt1)
    w_rt3 = w_rt - w_rt1 - w_rt2
    w_rt1, w_rt2, w_rt3 = w_rt1.astype(BF16), w_rt2.astype(BF16), w_rt3.astype(BF16)
    x1, hx, hxf = _merge(o, conv, gates, x, gt1, sh2, sc2, row(conv_ln_g[0]),
                         row(conv_ln_b[0]), w_attn_proj[0].astype(BF16), w_conv_proj[0].astype(BF16),
                         w_out[0].astype(BF16), row(norm2_g[0]), _tile(s, 512))

    e_idx, wts, rank, counts = _route(hxf, jnp.stack([w_rt1, w_rt2, w_rt3]), router_bias[0],
                                      _tile(n // 2, 512))

    cnt = counts[:, 0]
    padded = (cnt + EXPERT_BLOCK - 1) // EXPERT_BLOCK * EXPERT_BLOCK
    pends = jnp.cumsum(padded)
    pstart = (pends - padded).astype(I32)
    n_blocks = (n * TOP_K + N_EXPERTS * (EXPERT_BLOCK - 1)) // EXPERT_BLOCK
    n_blocks = -(-n_blocks // EXPERT_BLOCKS_PER_STEP) * EXPERT_BLOCKS_PER_STEP
    cap = n_blocks * EXPERT_BLOCK
    blk_row0 = jnp.arange(n_blocks, dtype=I32) * EXPERT_BLOCK
    blk_e = jnp.minimum(jnp.sum(pends[None, :] <= blk_row0[:, None], axis=1), N_EXPERTS - 1).astype(I32)
    blk_valid = jnp.clip(pstart[blk_e] + cnt[blk_e] - blk_row0, 0, EXPERT_BLOCK).astype(I32)

    pos = _slots(pstart, e_idx, rank, _tile(n, 4096))
    xs = _sc_dispatch(hx, pos, cap)
    ysh = _shared(hx, w_sh_gu[0].astype(BF16), w_sh_dn[0].astype(BF16), _tile(n, 512))
    has_rows = cnt > 0
    later = lax.cummin(jnp.where(has_rows, jnp.arange(N_EXPERTS, dtype=I32), N_EXPERTS), reverse=True)
    next_expert = jnp.concatenate([later[1:], jnp.full((1,), N_EXPERTS, I32)])
    next_expert = jnp.where(next_expert >= N_EXPERTS, -1, next_expert)
    expert_slot = (jnp.cumsum(has_rows.astype(I32)) - 1) % 2
    ys = _experts(blk_e, blk_valid, next_expert, expert_slot, xs, w_exp_gu[0], w_exp_dn[0])
    tm = _tile(n, 256)
    chunk = n // COMBINE_CHUNKS if n % (COMBINE_CHUNKS * SC_WINDOW * SC_WORKERS) == 0 else n
    wt_t, x1_2d = wts.T, x1.reshape(n, d)
    out = None
    for c0 in range(0, n, chunk):
        yg = _sc_gather(ys, pos[:, c0:c0 + chunk])
        out = _combine(yg, wt_t, ysh, x1_2d, gt2, row(final_g), tm, s // tm, c0 // tm, out)
    return out.reshape(b, s, d)
```
